```python
import math
import jax, jax.numpy as jnp
from jax import lax
import numpy as np

D_MODEL = 1024
BATCH = 8
SEQ = 2048
DEPTH = 1
DEC_BATCH = 32
DEC_SEQ = 1
PAST_LEN = 16384
PAGE_SIZE = 128

PLE_DIM = 256
GDN_HEADS = 8
GDN_DK = 64
GDN_DV = 64
GDN_WIDTH = GDN_HEADS * GDN_DV
GDN_QK_DIM = GDN_HEADS * GDN_DK
CONV_WIDTH = 4
CONV_DIM = 2 * GDN_QK_DIM + GDN_WIDTH
GDN_CHUNK = 64
MLA_HEADS = 8
MLA_NOPE = 64
MLA_ROPE = 32
MLA_VDIM = 64
MLA_QK = MLA_NOPE + MLA_ROPE
MLA_WIDTH = MLA_HEADS * MLA_VDIM
Q_LORA = 384
KV_LORA = 256
ROPE_THETA = 10000.0
ATTN_SCALE = MLA_QK ** -0.5
Q_BLOCK = 128
MIX_WIDTH = GDN_WIDTH + MLA_WIDTH
D_FF = -(-8 * D_MODEL // (3 * 256)) * 256
IN_SPLITS = (CONV_DIM, CONV_DIM + GDN_HEADS, CONV_DIM + 2 * GDN_HEADS,
             CONV_DIM + 2 * GDN_HEADS + GDN_WIDTH,
             CONV_DIM + 2 * GDN_HEADS + GDN_WIDTH + Q_LORA)
IN_DIM = IN_SPLITS[-1] + KV_LORA + MLA_ROPE
EPS = 1e-6

kernel_name = 'hybrid_gdn_mla_parallel_heads_step'


def rmsnorm(x, g):
    xf = x.astype(jnp.float32)
    y = xf * lax.rsqrt(jnp.mean(xf * xf, axis=-1, keepdims=True) + EPS)
    return (y * g.astype(jnp.float32)).astype(x.dtype)


def l2norm(x):
    xf = x.astype(jnp.float32)
    return xf * lax.rsqrt(jnp.sum(xf * xf, axis=-1, keepdims=True) + EPS)


def rope_tables(pos):
    half = MLA_ROPE // 2
    inv_freq = ROPE_THETA ** (-jnp.arange(half, dtype=jnp.float32) / half)
    ang = pos.astype(jnp.float32)[:, None] * inv_freq[None, :]
    return jnp.cos(ang), jnp.sin(ang)


def apply_rope(x, cos, sin):
    extra = x.ndim - 3
    c = cos.reshape(cos.shape[:1] + (1,) * extra + cos.shape[1:])
    s = sin.reshape(sin.shape[:1] + (1,) * extra + sin.shape[1:])
    xf = x.astype(jnp.float32)
    half = MLA_ROPE // 2
    x1, x2 = xf[..., :half], xf[..., half:]
    return jnp.concatenate([x1 * c - x2 * s, x1 * s + x2 * c], axis=-1).astype(x.dtype)


def split_in(z):
    return jnp.split(z, list(IN_SPLITS), axis=-1)


def causal_conv(ext, w):
    s = ext.shape[1] - (CONV_WIDTH - 1)
    y = ext[:, 0:s] * w[0]
    for j in range(1, CONV_WIDTH):
        y = y + ext[:, j:j + s] * w[j]
    return jax.nn.silu(y)


def gdn_prep(conv_out, a, b, a_log, dt_bias):
    bsz, s, _ = conv_out.shape
    q = l2norm(conv_out[..., :GDN_QK_DIM].reshape(bsz, s, GDN_HEADS, GDN_DK)) * (GDN_DK ** -0.5)
    k = l2norm(conv_out[..., GDN_QK_DIM:2 * GDN_QK_DIM].reshape(bsz, s, GDN_HEADS, GDN_DK))
    v = conv_out[..., 2 * GDN_QK_DIM:].reshape(bsz, s, GDN_HEADS, GDN_DV).astype(jnp.float32)
    g = -jnp.exp(a_log.astype(jnp.float32)) * jax.nn.softplus(a.astype(jnp.float32) + dt_bias.astype(jnp.float32))
    beta = jax.nn.sigmoid(b.astype(jnp.float32))
    return q, k, v, g, beta


def gdn_chunked(q, k, v, g, beta):
    bsz, s, h, dk = k.shape
    dv = v.shape[-1]
    c = GDN_CHUNK
    n = s // c

    def to_chunks(t):
        return jnp.moveaxis(t.reshape((bsz, n, c, h) + t.shape[3:]), 3, 1)

    q, k, v, g, beta = [to_chunks(t) for t in (q, k, v, g, beta)]
    gc = jnp.cumsum(g, axis=-1)
    tril = jnp.tril(jnp.ones((c, c), bool))
    strict = jnp.tril(jnp.ones((c, c), bool), -1)
    decay = jnp.exp(jnp.where(tril, gc[..., :, None] - gc[..., None, :], -jnp.inf))
    kb = k * beta[..., None]
    vb = v * beta[..., None]
    a_mat = jnp.where(strict, jnp.einsum('bhncd,bhnsd->bhncs', kb, k) * decay, 0.0)
    eye = jnp.eye(c, dtype=jnp.float32)
    t_inv = lax.linalg.triangular_solve(a_mat + eye, jnp.broadcast_to(eye, a_mat.shape), left_side=True, lower=True)
    u = t_inv @ vb
    w = t_inv @ (kb * jnp.exp(gc)[..., None])
    qk = jnp.einsum('bhncd,bhnsd->bhncs', q, k) * decay
    xs = tuple(jnp.moveaxis(t, 2, 0) for t in (q, k, u, w, gc, qk))

    def step(state, inp):
        q_c, k_c, u_c, w_c, g_c, qk_c = inp
        v_new = u_c - jnp.einsum('bhcd,bhde->bhce', w_c, state)
        o = jnp.einsum('bhcd,bhde->bhce', q_c * jnp.exp(g_c)[..., None], state) + jnp.einsum('bhcs,bhse->bhce', qk_c, v_new)
        g_last = g_c[..., -1]
        state = state * jnp.exp(g_last)[..., None, None] + jnp.einsum(
            'bhcd,bhce->bhde', k_c * jnp.exp(g_last[..., None] - g_c)[..., None], v_new)
        return state, o

    s0 = jnp.zeros((bsz, h, dk, dv), jnp.float32)
    s_fin, o = lax.scan(step, s0, xs)
    o = jnp.moveaxis(o, 0, 2).reshape(bsz, h, s, dv).transpose(0, 2, 1, 3)
    return o, s_fin


def gdn_recurrent(q, k, v, g, beta, s0):
    xs = tuple(jnp.moveaxis(t, 1, 0) for t in (q, k, v, g, beta))

    def step(state, inp):
        q_t, k_t, v_t, g_t, b_t = inp
        state = state * jnp.exp(g_t)[..., None, None]
        delta = (v_t - jnp.einsum('bhd,bhde->bhe', k_t, state)) * b_t[..., None]
        state = state + jnp.einsum('bhd,bhe->bhde', k_t, delta)
        return state, jnp.einsum('bhd,bhde->bhe', q_t, state)

    s_fin, o = lax.scan(step, s0.astype(jnp.float32), xs)
    return jnp.moveaxis(o, 0, 1), s_fin


def gdn_output(o, z, g_out):
    bsz, s = o.shape[:2]
    gate = jax.nn.silu(z.astype(jnp.float32).reshape(bsz, s, GDN_HEADS, GDN_DV))
    return (rmsnorm(o, g_out) * gate).reshape(bsz, s, GDN_WIDTH)


def mla_project(qa, kva, cos, sin, g_q_a, w_q_b, g_q_nope, g_q_rope, g_kv_a, g_k_rope):
    bsz, s = qa.shape[:2]
    q = (rmsnorm(qa, g_q_a) @ w_q_b).reshape(bsz, s, MLA_HEADS, MLA_QK)
    q_nope = rmsnorm(q[..., :MLA_NOPE], g_q_nope)
    q_rope = apply_rope(rmsnorm(q[..., MLA_NOPE:], g_q_rope), cos, sin)
    c_kv = rmsnorm(kva[..., :KV_LORA], g_kv_a)
    k_rope = apply_rope(rmsnorm(kva[..., KV_LORA:], g_k_rope), cos, sin)
    return q_nope, q_rope, c_kv, k_rope


def mla_expand(c_kv, w_kv_b, g_k_nope):
    kv = (c_kv @ w_kv_b).reshape(c_kv.shape[:-1] + (MLA_HEADS, MLA_NOPE + MLA_VDIM))
    return rmsnorm(kv[..., :MLA_NOPE], g_k_nope), kv[..., MLA_NOPE:]


def mla_prompt_attention(q_nope, q_rope, k_nope, k_rope, v):
    bsz, s, h, _ = q_nope.shape
    nb = s // Q_BLOCK
    qn = jnp.moveaxis(q_nope.reshape(bsz, nb, Q_BLOCK, h, MLA_NOPE), 1, 0)
    qr = jnp.moveaxis(q_rope.reshape(bsz, nb, Q_BLOCK, h, MLA_ROPE), 1, 0)
    key_pos = jnp.arange(s)
    kn = k_nope.astype(jnp.float32)
    kr = k_rope.astype(jnp.float32)
    vf = v.astype(jnp.float32)

    def block(args):
        qn_b, qr_b, i = args
        sc = (jnp.einsum('bqhd,bkhd->bhqk', qn_b.astype(jnp.float32), kn)
              + jnp.einsum('bqhd,bkd->bhqk', qr_b.astype(jnp.float32), kr)) * ATTN_SCALE
        qpos = i * Q_BLOCK + jnp.arange(Q_BLOCK)
        sc = jnp.where(qpos[:, None] >= key_pos[None, :], sc, -jnp.inf)
        p = jax.nn.softmax(sc, axis=-1)
        return jnp.einsum('bhqk,bkhd->bqhd', p, vf)

    o = lax.map(block, (qn, qr, jnp.arange(nb)))
    return jnp.moveaxis(o, 0, 1).reshape(bsz, s, h * MLA_VDIM)


def mla_sample_attention(q_nope, q_rope, ckv_new, kr_new, ckv_pool, kr_pool, page_table, w_kv_b, g_k_nope):
    bsz, t, h, _ = q_nope.shape
    qn = q_nope.astype(jnp.float32)
    qr = q_rope.astype(jnp.float32)

    def scores(k_nope, k_rope):
        return (jnp.einsum('bqhd,bkhd->bhqk', qn, k_nope.astype(jnp.float32))
                + jnp.einsum('bqhd,bkd->bhqk', qr, k_rope.astype(jnp.float32))) * ATTN_SCALE

    kn, vv = mla_expand(ckv_new, w_kv_b, g_k_nope)
    causal = jnp.tril(jnp.ones((t, t), bool))
    sc = jnp.where(causal, scores(kn, kr_new), -jnp.inf)
    m = sc.max(axis=-1)
    p = jnp.exp(sc - m[..., None])
    l = p.sum(axis=-1)
    acc = jnp.einsum('bhqk,bkhd->bhqd', p, vv.astype(jnp.float32))

    def page_step(carry, pages):
        m, l, acc = carry
        kn_p, v_p = mla_expand(ckv_pool[pages], w_kv_b, g_k_nope)
        sc_p = scores(kn_p, kr_pool[pages])
        m_new = jnp.maximum(m, sc_p.max(axis=-1))
        alpha = jnp.exp(m - m_new)
        p_p = jnp.exp(sc_p - m_new[..., None])
        l = l * alpha + p_p.sum(axis=-1)
        acc = acc * alpha[..., None] + jnp.einsum('bhqk,bkhd->bhqd', p_p, v_p.astype(jnp.float32))
        return (m_new, l, acc), None

    (m, l, acc), _ = lax.scan(page_step, (m, l, acc), page_table.T)
    o = acc / l[..., None]
    return o.transpose(0, 2, 1, 3).reshape(bsz, t, h * MLA_VDIM)


def layer_tail(h, o_mix, p, w_o, g_ffn, w_ffn_gate, w_ffn_up, w_ffn_down, g_ple, w_ple_gate, w_ple_proj):
    h = h + o_mix.astype(h.dtype) @ w_o
    u = rmsnorm(h, g_ffn)
    h = h + (jax.nn.silu(u @ w_ffn_gate) * (u @ w_ffn_up)) @ w_ffn_down
    gate = jax.nn.sigmoid(rmsnorm(h, g_ple) @ w_ple_gate)
    return h + (p.astype(h.dtype) @ w_ple_proj) * gate


def setup_inputs(seed: int = 0) -> dict:
    key = jax.random.key(seed)
    k = jax.random.split(key, 32)

    def nrm(i, shape, scale):
        return jax.random.normal(k[i], shape, jnp.float32) * scale

    def gain(i, n):
        return 1.0 + 0.02 * jax.random.normal(k[i], (DEPTH, n), jnp.float32)

    n_pages = PAST_LEN // PAGE_SIZE
    n_used = DEC_BATCH * n_pages
    n_phys = n_used + (n_used + 3) // 4
    page_table = jax.random.permutation(k[6], n_phys)[:n_used].reshape(DEC_BATCH, n_pages).astype(jnp.int32)
    a_log = jnp.log(jax.random.uniform(k[12], (DEPTH, GDN_HEADS), jnp.float32, 1.0, 16.0))
    dt = jnp.exp(jax.random.uniform(k[13], (DEPTH, GDN_HEADS), jnp.float32, math.log(1e-3), math.log(1e-1)))
    dt_bias = dt + jnp.log(-jnp.expm1(-dt))
    return {
        'x_prompt': nrm(0, (BATCH, SEQ, D_MODEL), 1.0),
        'x_sample': nrm(1, (DEC_BATCH, DEC_SEQ, D_MODEL), 1.0),
        'cache_ckv': nrm(2, (DEPTH, n_phys, PAGE_SIZE, KV_LORA), 1.0),
        'cache_krope': nrm(3, (DEPTH, n_phys, PAGE_SIZE, MLA_ROPE), 1.0),
        'state_gdn': nrm(4, (DEPTH, DEC_BATCH, GDN_HEADS, GDN_DK, GDN_DV), 0.1),
        'state_conv': nrm(5, (DEPTH, DEC_BATCH, CONV_WIDTH - 1, CONV_DIM), 1.0),
        'page_table': page_table,
        'p_prompt': nrm(7, (DEPTH, BATCH, SEQ, PLE_DIM), 1.0),
        'p_sample': nrm(8, (DEPTH, DEC_BATCH, DEC_SEQ, PLE_DIM), 1.0),
        'g_attn': gain(9, D_MODEL),
        'w_in': nrm(10, (DEPTH, D_MODEL, IN_DIM), D_MODEL ** -0.5),
        'w_conv': nrm(11, (DEPTH, CONV_WIDTH, CONV_DIM), CONV_WIDTH ** -0.5),
        'gdn_a_log': a_log,
        'gdn_dt_bias': dt_bias,
        'g_gdn_out': gain(14, GDN_DV),
        'g_q_a': gain(15, Q_LORA),
        'w_q_b': nrm(16, (DEPTH, Q_LORA, MLA_HEADS * MLA_QK), Q_LORA ** -0.5),
        'g_q_nope': gain(17, MLA_NOPE),
        'g_q_rope': gain(18, MLA_ROPE),
        'g_kv_a': gain(19, KV_LORA),
        'g_k_rope': gain(20, MLA_ROPE),
        'w_kv_b': nrm(21, (DEPTH, KV_LORA, MLA_HEADS * (MLA_NOPE + MLA_VDIM)), KV_LORA ** -0.5),
        'g_k_nope': gain(22, MLA_NOPE),
        'w_o': nrm(23, (DEPTH, MIX_WIDTH, D_MODEL), MIX_WIDTH ** -0.5),
        'g_ffn': gain(24, D_MODEL),
        'w_ffn_gate': nrm(25, (DEPTH, D_MODEL, D_FF), D_MODEL ** -0.5),
        'w_ffn_up': nrm(26, (DEPTH, D_MODEL, D_FF), D_MODEL ** -0.5),
        'w_ffn_down': nrm(27, (DEPTH, D_FF, D_MODEL), D_FF ** -0.5),
        'g_ple': gain(28, D_MODEL),
        'w_ple_gate': nrm(29, (DEPTH, D_MODEL, D_MODEL), D_MODEL ** -0.5),
        'w_ple_proj': nrm(30, (DEPTH, PLE_DIM, D_MODEL), PLE_DIM ** -0.5),
    }


def reference(x_prompt, x_sample, cache_ckv, cache_krope, state_gdn, state_conv, page_table,
              p_prompt, p_sample, g_attn, w_in, w_conv, gdn_a_log, gdn_dt_bias, g_gdn_out,
              g_q_a, w_q_b, g_q_nope, g_q_rope, g_kv_a, g_k_rope, w_kv_b, g_k_nope, w_o,
              g_ffn, w_ffn_gate, w_ffn_up, w_ffn_down, g_ple, w_ple_gate, w_ple_proj):
    bp, seq, _ = x_prompt.shape
    bs, dec_seq, _ = x_sample.shape
    past = page_table.shape[1] * cache_ckv.shape[2]
    cos_p, sin_p = rope_tables(jnp.arange(seq))
    cos_s, sin_s = rope_tables(past + jnp.arange(dec_seq))
    hp, hs = x_prompt, x_sample
    ckv_p_l, kr_p_l, gdn_p_l, conv_p_l = [], [], [], []
    ckv_s_l, kr_s_l, gdn_s_l, conv_s_l = [], [], [], []
    for i in range(DEPTH):
        conv_in, a, b, z, qa, kva = split_in(rmsnorm(hp, g_attn[i]) @ w_in[i])
        ext = jnp.concatenate([jnp.zeros((bp, CONV_WIDTH - 1, CONV_DIM), conv_in.dtype), conv_in], axis=1)
        q, k, v, g, beta = gdn_prep(causal_conv(ext, w_conv[i]), a, b, gdn_a_log[i], gdn_dt_bias[i])
        o_g, s_fin = gdn_chunked(q, k, v, g, beta)
        o_gdn = gdn_output(o_g, z, g_gdn_out[i])
        q_nope, q_rope, ckv, kr = mla_project(qa, kva, cos_p, sin_p, g_q_a[i], w_q_b[i], g_q_nope[i],
                                              g_q_rope[i], g_kv_a[i], g_k_rope[i])
        k_nope, v_mla = mla_expand(ckv, w_kv_b[i], g_k_nope[i])
        o_mla = mla_prompt_attention(q_nope, q_rope, k_nope, kr, v_mla)
        hp = layer_tail(hp, jnp.concatenate([o_gdn, o_mla.astype(o_gdn.dtype)], axis=-1), p_prompt[i], w_o[i],
                        g_ffn[i], w_ffn_gate[i], w_ffn_up[i], w_ffn_down[i], g_ple[i], w_ple_gate[i], w_ple_proj[i])
        ckv_p_l.append(ckv)
        kr_p_l.append(kr)
        gdn_p_l.append(s_fin)
        conv_p_l.append(ext[:, -(CONV_WIDTH - 1):])
        conv_in, a, b, z, qa, kva = split_in(rmsnorm(hs, g_attn[i]) @ w_in[i])
        ext = jnp.concatenate([state_conv[i].astype(conv_in.dtype), conv_in], axis=1)
        q, k, v, g, beta = gdn_prep(causal_conv(ext, w_conv[i]), a, b, gdn_a_log[i], gdn_dt_bias[i])
        o_g, s_fin = gdn_recurrent(q, k, v, g, beta, state_gdn[i])
        o_gdn = gdn_output(o_g, z, g_gdn_out[i])
        q_nope, q_rope, ckv, kr = mla_project(qa, kva, cos_s, sin_s, g_q_a[i], w_q_b[i], g_q_nope[i],
                                              g_q_rope[i], g_kv_a[i], g_k_rope[i])
        o_mla = mla_sample_attention(q_nope, q_rope, ckv, kr, cache_ckv[i], cache_krope[i], page_table,
                                     w_kv_b[i], g_k_nope[i])
        hs = layer_tail(hs, jnp.concatenate([o_gdn, o_mla.astype(o_gdn.dtype)], axis=-1), p_sample[i], w_o[i],
                        g_ffn[i], w_ffn_gate[i], w_ffn_up[i], w_ffn_down[i], g_ple[i], w_ple_gate[i], w_ple_proj[i])
        ckv_s_l.append(ckv)
        kr_s_l.append(kr)
        gdn_s_l.append(s_fin)
        conv_s_l.append(ext[:, -(CONV_WIDTH - 1):])
    y_prompt = hp
    y_sample = hs
    ckv_prompt = jnp.stack(ckv_p_l)
    krope_prompt = jnp.stack(kr_p_l)
    gdn_state_prompt = jnp.stack(gdn_p_l)
    conv_state_prompt = jnp.stack(conv_p_l)
    ckv_sample = jnp.stack(ckv_s_l)
    krope_sample = jnp.stack(kr_s_l)
    gdn_state_sample = jnp.stack(gdn_s_l)
    conv_state_sample = jnp.stack(conv_s_l)
    return (y_prompt, y_sample, ckv_prompt, krope_prompt, gdn_state_prompt, conv_state_prompt,
            ckv_sample, krope_sample, gdn_state_sample, conv_state_sample)
```

```python
import functools
import math

import numpy as np
import jax
import jax.numpy as jnp
from jax import lax
from jax.experimental import pallas as pl
from jax.experimental.pallas import tpu as pltpu

F32 = jnp.float32
BF16 = jnp.bfloat16

D_MODEL = 1024
PLE_DIM = 256
GDN_HEADS = 8
GDN_DK = 64
GDN_DV = 64
GDN_WIDTH = GDN_HEADS * GDN_DV
GDN_QK_DIM = GDN_HEADS * GDN_DK
CONV_WIDTH = 4
CONV_DIM = 2 * GDN_QK_DIM + GDN_WIDTH
GDN_CHUNK = 64
MLA_HEADS = 8
MLA_NOPE = 64
MLA_ROPE = 32
MLA_VDIM = 64
MLA_QK = MLA_NOPE + MLA_ROPE
Q_LORA = 384
KV_LORA = 256
ROPE_THETA = 10000.0
ATTN_SCALE = MLA_QK ** -0.5
D_FF = 2816
EPS = 1e-6

LANES = 128
SLAB = 128
HALF = MLA_ROPE // 2
ROPE_LO = MLA_NOPE
ROPE_HI = MLA_NOPE + HALF
QK_SLAB_W = MLA_HEADS * SLAB

OFF_CONV = 0
OFF_GB = OFF_CONV + CONV_DIM
OFF_Z = OFF_GB + LANES
OFF_QA = OFF_Z + GDN_WIDTH
OFF_CKV = OFF_QA + Q_LORA
OFF_KR = OFF_CKV + KV_LORA
IN_PAD = OFF_KR + LANES
BETA_LANE0 = 32

VMEM_LIMIT = 56 * 1024 * 1024


def _dot(a, b):
    return jnp.dot(a, b, preferred_element_type=F32)


def _dot_nt(a, b):
    return lax.dot_general(a, b, (((1,), (1,)), ((), ())), preferred_element_type=F32)


def _dot_tn(a, b):
    return lax.dot_general(a, b, (((0,), (0,)), ((), ())), preferred_element_type=F32)


def _split2(x):
    hi = x.astype(BF16)
    lo = (x - hi.astype(F32)).astype(BF16)
    return hi, lo


def _split3(x):
    hi = x.astype(BF16)
    r1 = x - hi.astype(F32)
    mid = r1.astype(BF16)
    lo = (r1 - mid.astype(F32)).astype(BF16)
    return hi, mid, lo


def _sigmoid(x):
    return 1.0 / (1.0 + jnp.exp(-x))


def _silu(x):
    return x * _sigmoid(x)


def _softplus(x):
    return jnp.maximum(x, 0.0) + jnp.log1p(jnp.exp(-jnp.abs(x)))


def _rmsnorm(x, g):
    return x * lax.rsqrt(jnp.mean(x * x, axis=-1, keepdims=True) + EPS) * g


def _seg_rsqrt(x, ered, eexp, nseg):
    hi, lo = _split2(x * x)
    red = _dot(hi, ered) + _dot(lo, ered)
    r = lax.rsqrt(red + EPS)
    p0, p1, p2 = _split3(r)
    lane = lax.broadcasted_iota(jnp.int32, r.shape, 1)
    piece = jnp.where(lane < nseg, p0, jnp.where(lane < 2 * nseg, p1, p2))
    return _dot(piece, eexp)


def _rope(x, c, s_up, s_dn):
    w = x.shape[-1]
    up = pltpu.roll(x, w - HALF, axis=1)
    dn = pltpu.roll(x, HALF, axis=1)
    return x * c + up * s_up + dn * s_dn


def _tile_lanes(x, n):
    return jnp.concatenate([x] * n, axis=1)


def _const_spec(arr, ngrid):
    nd = arr.ndim
    return pl.BlockSpec(arr.shape, lambda *a, _nd=nd: (0,) * _nd, pipeline_mode=pl.Buffered(1))


MIXER_CONSTS = ("g_attn", "w_in", "w_conv", "alog", "dtb", "ered_g", "eexp_g", "g_q_a", "w_qb", "gq", "ered_q",
                "eexp_q", "g_kv_a", "gkr", "ered_r", "eexp_r", "wk", "gk", "ered_k", "eexp_k")


def _mixer_rows(x, c, conv_fn, rc, rsu, rsd):
    xn = _rmsnorm(x, c["g_attn"][...]).astype(BF16)
    w_in = c["w_in"]

    conv_in = _dot(xn, w_in[:, OFF_CONV:OFF_GB])
    y = _silu(conv_fn(conv_in))
    qk = y[:, :2 * GDN_QK_DIM]
    qk = qk * _seg_rsqrt(qk, c["ered_g"][...], c["eexp_g"][...], 2 * GDN_HEADS)
    q_g = qk[:, :GDN_QK_DIM] * (GDN_DK ** -0.5)
    k_g = qk[:, GDN_QK_DIM:]
    v_g = y[:, 2 * GDN_QK_DIM:]

    ab = _dot(xn, w_in[:, OFF_GB:OFF_Z])
    lane = lax.broadcasted_iota(jnp.int32, ab.shape, 1)
    g_log = -jnp.exp(c["alog"][...]) * _softplus(ab + c["dtb"][...])
    gb = jnp.where(lane < BETA_LANE0, g_log, _sigmoid(ab))

    z = _dot(xn, w_in[:, OFF_Z:OFF_QA])

    qa = _dot(xn, w_in[:, OFF_QA:OFF_CKV])
    qs = _dot(_rmsnorm(qa, c["g_q_a"][...]).astype(BF16), c["w_qb"][...])
    qs = qs * _seg_rsqrt(qs, c["ered_q"][...], c["eexp_q"][...], 2 * MLA_HEADS) * c["gq"][...]
    q_mla = _rope(qs, _tile_lanes(rc, MLA_HEADS), _tile_lanes(rsu, MLA_HEADS), _tile_lanes(rsd, MLA_HEADS)) * ATTN_SCALE

    ckv = _rmsnorm(_dot(xn, w_in[:, OFF_CKV:OFF_KR]), c["g_kv_a"][...])
    ckv_bf = ckv.astype(BF16)
    kr = _dot(xn, w_in[:, OFF_KR:IN_PAD])
    kr = kr * _seg_rsqrt(kr, c["ered_r"][...], c["eexp_r"][...], 1) * c["gkr"][...]
    kr = _rope(kr, rc, rsu, rsd)
    kk = _dot(ckv_bf, c["wk"][...])
    kk = kk * _seg_rsqrt(kk, c["ered_k"][...], c["eexp_k"][...], MLA_HEADS) * c["gk"][...]
    k_mla = kk + _tile_lanes(kr, MLA_HEADS)
    return dict(conv_in=conv_in, q_g=q_g, k_g=k_g, v_g=v_g, gb=gb, z=z, q_mla=q_mla, k_mla=k_mla,
                ckv=ckv, ckv_bf=ckv_bf, kr=kr)


def _prompt_mixer_kernel(tiles_per_seq, tm, x_ref, rc_ref, rsu_ref, rsd_ref, *refs):
    nc = len(MIXER_CONSTS)
    c = dict(zip(MIXER_CONSTS, refs[:nc]))
    wvt_ref = refs[nc]
    (qg_ref, kg_ref, vg_ref, gb_ref, z_ref, qm_ref, km_ref, vt_ref, ckv_ref, kro_ref, cs_ref, ext_ref) = refs[nc + 1:]
    i = pl.program_id(0)

    @pl.when(i % tiles_per_seq == 0)
    def _():
        ext_ref[5:8, :] = jnp.zeros((3, CONV_DIM), F32)

    def conv_fn(conv_in):
        w = c["w_conv"]
        ext_ref[8:8 + tm, :] = conv_in
        return (conv_in * w[3:4, :] + ext_ref[7:7 + tm, :] * w[2:3, :] + ext_ref[6:6 + tm, :] * w[1:2, :]
                + ext_ref[5:5 + tm, :] * w[0:1, :])

    r = _mixer_rows(x_ref[...], c, conv_fn, rc_ref[...], rsu_ref[...], rsd_ref[...])
    last3 = ext_ref[tm + 5:tm + 8, :]
    ext_ref[5:8, :] = last3
    cs_ref[0] = last3
    qg_ref[...] = r["q_g"]
    kg_ref[...] = r["k_g"]
    vg_ref[...] = r["v_g"]
    gb_ref[...] = r["gb"]
    z_ref[...] = r["z"]
    qm_ref[...] = r["q_mla"].astype(BF16)
    km_ref[...] = r["k_mla"].astype(BF16)
    vt_ref[...] = _dot_nt(wvt_ref[...], r["ckv_bf"]).astype(BF16)
    ckv_ref[...] = r["ckv"]
    kro_ref[...] = r["kr"][:, ROPE_LO:ROPE_LO + MLA_ROPE]


def _sample_mixer_kernel(x_ref, rc_ref, rsu_ref, rsd_ref, hist_ref, *refs):
    nc = len(MIXER_CONSTS)
    c = dict(zip(MIXER_CONSTS, refs[:nc]))
    (qg_ref, kg_ref, vg_ref, gb_ref, z_ref, qm_ref, km_ref, ckv_ref, kro_ref, cin_ref) = refs[nc:]

    def conv_fn(conv_in):
        w = c["w_conv"]
        return conv_in * w[3:4, :] + hist_ref[2] * w[2:3, :] + hist_ref[1] * w[1:2, :] + hist_ref[0] * w[0:1, :]

    r = _mixer_rows(x_ref[...], c, conv_fn, rc_ref[...], rsu_ref[...], rsd_ref[...])
    qg_ref[...] = r["q_g"]
    kg_ref[...] = r["k_g"]
    vg_ref[...] = r["v_g"]
    gb_ref[...] = r["gb"]
    z_ref[...] = r["z"]
    qm_ref[...] = r["q_mla"]
    km_ref[...] = r["k_mla"]
    ckv_ref[...] = r["ckv"]
    kro_ref[...] = r["kr"][:, ROPE_LO:ROPE_LO + MLA_ROPE]
    cin_ref[...] = r["conv_in"]


GROUP_HEADS = 4
GROUP_W = GROUP_HEADS * GDN_DK


def _cumsum_rows(x, period):
    row = lax.broadcasted_iota(jnp.int32, x.shape, 0) % period
    s = 1
    while s < period:
        x = x + jnp.where(row >= s, pltpu.roll(x, s, axis=0), 0.0)
        s *= 2
    return x


def _gdn_group(q, k, v, g_r, b_r, s_ref, grp, masks):
    bd_mask, tril_cat, strict_cat, eye_cat = masks
    c = GDN_CHUNK

    def bd(y):
        return jnp.where(bd_mask, jnp.concatenate([y] * GROUP_HEADS, axis=0), 0.0)

    kb = k * b_r
    vb = v * b_r
    k_bd = bd(k)
    aq = _dot_nt(jnp.concatenate([kb, q], axis=0), k_bd)
    g_col = jnp.sum(jnp.where(eye_cat, g_r, 0.0), axis=0, keepdims=True)
    decay = jnp.exp(jnp.where(tril_cat, g_r - g_col, -jnp.inf))
    a = jnp.where(strict_cat, aq[:c] * decay, 0.0)
    qk = aq[c:] * decay
    p = -a
    s_inv = jnp.where(eye_cat, 1.0, 0.0) + p
    p = _dot(p, bd(p))
    for lvl in range(5):
        s_bd = bd(s_inv)
        if lvl < 4:
            p_bd = bd(p)
            s_inv = s_inv + _dot(p, s_bd)
            p = _dot(p, p_bd)
        else:
            s_inv = s_inv + _dot(p, s_bd)
    eg = jnp.exp(g_r)
    u = _dot(s_inv, bd(vb))
    w = _dot(s_inv, bd(kb * eg))
    state = s_ref[grp]
    ws = _dot(jnp.concatenate([w, q * eg], axis=0), state)
    v_new = u - ws[:c]
    o = ws[c:] + _dot(qk, bd(v_new))
    g_last = g_r[c - 1:c, :]
    kd = k * jnp.exp(g_last - g_r)
    upd = _dot_tn(kd, v_new)
    s_ref[grp] = state * jnp.exp(g_last) + jnp.where(bd_mask, upd, 0.0)
    return o


def _gdn_kernel(nblk, tb, q_ref, k_ref, v_ref, gb_ref, z_ref, e6_ref, esum_ref, gout_ref, o_ref, sfin_ref, s_ref):
    j = pl.program_id(1)

    @pl.when(j == 0)
    def _():
        s_ref[...] = jnp.zeros(s_ref.shape, F32)

    c = GDN_CHUNK
    r_bd = lax.broadcasted_iota(jnp.int32, (GROUP_W, GROUP_W), 0) // c
    c_bd = lax.broadcasted_iota(jnp.int32, (GROUP_W, GROUP_W), 1) // c
    bd_mask = r_bd == c_bd
    ri = lax.broadcasted_iota(jnp.int32, (c, GROUP_W), 0)
    cj = lax.broadcasted_iota(jnp.int32, (c, GROUP_W), 1) % c
    masks = (bd_mask, ri >= cj, ri > cj, ri == cj)
    lane = lax.broadcasted_iota(jnp.int32, (c, LANES), 1)
    copy_id = (lane % BETA_LANE0) // GDN_HEADS

    def chunk(ci, carry):
        r0 = pl.multiple_of(ci * c, c)
        rows = pl.ds(r0, c)
        gb = gb_ref[rows, :]
        sc = jnp.where(lane < BETA_LANE0, _cumsum_rows(gb, c), gb)
        p0, p1, p2 = _split3(sc)
        piece = jnp.where(copy_id == 0, p0, jnp.where(copy_id == 1, p1, p2))
        ex = _dot(piece, e6_ref[...])
        q = q_ref[rows, :]
        k = k_ref[rows, :]
        v = v_ref[rows, :]
        outs = []
        for grp in range(GDN_HEADS // GROUP_HEADS):
            sl = slice(grp * GROUP_W, (grp + 1) * GROUP_W)
            sb = slice(GDN_WIDTH + grp * GROUP_W, GDN_WIDTH + (grp + 1) * GROUP_W)
            outs.append(_gdn_group(q[:, sl], k[:, sl], v[:, sl], ex[:, sl], ex[:, sb], s_ref, grp, masks))
        o = jnp.concatenate(outs, axis=1)
        hi, lo = _split2(o * o)
        ms = _dot(hi, esum_ref[...]) + _dot(lo, esum_ref[...])
        zz = z_ref[rows, :]
        o_ref[rows, :] = (o * lax.rsqrt(ms + EPS) * gout_ref[...] * _silu(zz)).astype(o_ref.dtype)
        return carry

    lax.fori_loop(0, tb // c, chunk, 0)

    @pl.when(j == nblk - 1)
    def _():
        for grp in range(GDN_HEADS // GROUP_HEADS):
            for h in range(GROUP_HEADS):
                sfin_ref[0, grp * GROUP_HEADS + h] = s_ref[grp, h * c:(h + 1) * c, h * c:(h + 1) * c]


ATT_T = 256


def _attn_kernel(q_ref, k_ref, vt_ref, o_ref):
    qi = pl.program_id(2)
    t = ATT_T
    outs = []
    for hh in range(2):
        q = q_ref[:, hh * SLAB:(hh + 1) * SLAB]

        def tile(j, carry, masked, q=q, hh=hh):
            m, l, acc = carry
            k0 = pl.multiple_of(j * t, t)
            kt = k_ref[pl.ds(k0, t), hh * SLAB:(hh + 1) * SLAB]
            s = _dot_nt(kt, q)
            if masked:
                kp = lax.broadcasted_iota(jnp.int32, (t, t), 0)
                qp = lax.broadcasted_iota(jnp.int32, (t, t), 1)
                s = jnp.where(qp >= kp, s, -jnp.inf)
            m_new = jnp.maximum(m, jnp.max(s, axis=0, keepdims=True))
            alpha = jnp.exp(m - m_new)
            p = jnp.exp(s - m_new)
            l = l * alpha + jnp.sum(p, axis=0, keepdims=True)
            vt = vt_ref[hh * MLA_VDIM:(hh + 1) * MLA_VDIM, pl.ds(k0, t)]
            acc = acc * alpha + _dot(vt, p.astype(BF16))
            return m_new, l, acc

        init = (jnp.full((1, t), -jnp.inf, F32), jnp.zeros((1, t), F32), jnp.zeros((MLA_VDIM, t), F32))
        carry = lax.fori_loop(0, qi, functools.partial(tile, masked=False), init)
        m, l, acc = tile(qi, carry, True)
        outs.append(acc / l)
    o_ref[...] = jnp.concatenate(outs, axis=0).T.astype(o_ref.dtype)


def _tail_kernel(x_ref, og_ref, om_ref, p_ref, wo_ref, gffn_ref, wg_ref, wu_ref, wd_ref, gple_ref, wpg_ref, wpp_ref, y_ref):
    h = x_ref[...] + _dot(og_ref[...], wo_ref[:GDN_WIDTH, :]) + _dot(om_ref[...], wo_ref[GDN_WIDTH:, :])
    u = _rmsnorm(h, gffn_ref[...]).astype(BF16)
    act = (_silu(_dot(u, wg_ref[...])) * _dot(u, wu_ref[...])).astype(BF16)
    acc = h + _dot(act, wd_ref[...])
    gate = _sigmoid(_dot(_rmsnorm(acc, gple_ref[...]).astype(BF16), wpg_ref[...]))
    y_ref[...] = acc + _dot(p_ref[...].astype(BF16), wpp_ref[...]) * gate


PAGES_PER_STEP = 8


def _sample_prep_kernel(qm_ref, km_ref, gk_ref, wk_ref, qabs_ref, qr_ref, sself_ref):
    qm = qm_ref[...]
    prod = qm * km_ref[...]
    qg = (qm * gk_ref[...]).astype(BF16)
    for h in range(MLA_HEADS):
        sl = slice(h * SLAB, (h + 1) * SLAB)
        qabs_ref[h] = _dot_nt(qg[:, sl], wk_ref[:, sl])
        qr_ref[h] = qm[:, h * SLAB + ROPE_LO:h * SLAB + ROPE_LO + MLA_ROPE]
        sself_ref[h] = jnp.sum(prod[:, sl], axis=1, keepdims=True)


def _paged_kernel(nsteps, pt_ref, *refs):
    pp = PAGES_PER_STEP
    ckv_refs = refs[:pp]
    kr_refs = refs[pp:2 * pp]
    qabs_ref, qr_ref, sself_ref, ckvn_ref, wkt_ref, o_ref, m_ref, l_ref, acc_ref = refs[2 * pp:]
    g = pl.program_id(1)

    @pl.when(g == 0)
    def _():
        m_ref[...] = sself_ref[...]
        l_ref[...] = jnp.ones(l_ref.shape, F32)
        acc_ref[...] = jnp.broadcast_to(ckvn_ref[...], acc_ref.shape)

    qa = qabs_ref[...].astype(BF16)
    qr = qr_ref[...].astype(BF16)
    wkt = wkt_ref[...]
    m, l, acc = m_ref[...], l_ref[...], acc_ref[...]
    for i in range(pp // 2):
        c2 = jnp.concatenate([ckv_refs[2 * i][...], ckv_refs[2 * i + 1][...]], axis=0).astype(BF16)
        k2 = jnp.concatenate([kr_refs[2 * i][...], kr_refs[2 * i + 1][...]], axis=0).astype(BF16)
        kv = _dot_nt(wkt, c2)
        ss = jnp.sum((kv * kv).reshape(MLA_HEADS, MLA_NOPE, kv.shape[1]), axis=1)
        s = _dot_nt(qa, c2) * lax.rsqrt(ss * (1.0 / MLA_NOPE) + EPS) + _dot_nt(qr, k2)
        m_new = jnp.maximum(m, jnp.max(s, axis=1, keepdims=True))
        alpha = jnp.exp(m - m_new)
        p = jnp.exp(s - m_new)
        l = l * alpha + jnp.sum(p, axis=1, keepdims=True)
        acc = acc * alpha + _dot(p.astype(BF16), c2)
        m = m_new
    m_ref[...] = m
    l_ref[...] = l
    acc_ref[...] = acc

    @pl.when(g == nsteps - 1)
    def _():
        o_ref[...] = acc / l


def _sample_gdn_kernel(q_ref, k_ref, v_ref, gb_ref, s_ref, snew_ref, o_ref):
    hk = GDN_HEADS * GDN_DK
    hrow = lax.broadcasted_iota(jnp.int32, (GDN_HEADS, hk), 0)
    hlane = lax.broadcasted_iota(jnp.int32, (GDN_HEADS, hk), 1) // GDN_DK
    hm = hrow == hlane
    kmask = jnp.where(hm, jnp.broadcast_to(k_ref[...], (GDN_HEADS, hk)), 0.0)
    qmask = jnp.where(hm, jnp.broadcast_to(q_ref[...], (GDN_HEADS, hk)), 0.0)
    gbb = jnp.broadcast_to(gb_ref[...], (GDN_HEADS, LANES))
    r8 = lax.broadcasted_iota(jnp.int32, (GDN_HEADS, LANES), 0)
    l8 = lax.broadcasted_iota(jnp.int32, (GDN_HEADS, LANES), 1)
    g_col = jnp.sum(jnp.where(l8 == r8, gbb, 0.0), axis=1, keepdims=True)
    b_col = jnp.sum(jnp.where(l8 == r8 + BETA_LANE0, gbb, 0.0), axis=1, keepdims=True)
    ones = jnp.where(hm, 1.0, 0.0).astype(BF16)

    def expand_rows(col):
        e0, e1, e2 = _split3(jnp.broadcast_to(col, (GDN_HEADS, GDN_DV)))
        return _dot_tn(jnp.concatenate([ones, ones, ones], axis=0), jnp.concatenate([e0, e1, e2], axis=0))

    def head_dot(xmask, mat):
        xh, xl = _split2(xmask)
        mh, ml = _split2(mat)
        return _dot(jnp.concatenate([xh, xh, xl], axis=1), jnp.concatenate([mh, ml, mh], axis=0))

    state = s_ref[...] * expand_rows(jnp.exp(g_col))
    delta = (v_ref[...] - head_dot(kmask, state)) * b_col
    kh, kl = _split2(kmask)
    dh, dl = _split2(delta)
    state = state + _dot_tn(jnp.concatenate([kh, kh, kl], axis=0), jnp.concatenate([dh, dl, dh], axis=0))
    snew_ref[...] = state
    o_ref[...] = head_dot(qmask, state)


def _sample_mix_kernel(o_ref, z_ref, olat_ref, wvbd_ref, esum_ref, gout_ref, og_ref, om_ref):
    o = o_ref[...]
    hi, lo = _split2(o * o)
    ms = _dot(hi, esum_ref[...]) + _dot(lo, esum_ref[...])
    og_ref[...] = (o * lax.rsqrt(ms + EPS) * gout_ref[...] * _silu(z_ref[...])).astype(og_ref.dtype)
    om_ref[...] = _dot(olat_ref[...].astype(BF16), wvbd_ref[...]).astype(om_ref.dtype)


def _seg_tables(width, segs, mean):
    n = len(segs)
    ered = np.zeros((width, LANES), np.float32)
    eexp = np.zeros((LANES, width), np.float32)
    for copy in range(3):
        for s, (a, b) in enumerate(segs):
            ered[a:b, copy * n + s] = 1.0 / (b - a) if mean else 1.0
            eexp[copy * n + s, a:b] = 1.0
    return jnp.asarray(ered, BF16), jnp.asarray(eexp, BF16)


def _rope_slabs(pos):
    inv_freq = ROPE_THETA ** (-jnp.arange(HALF, dtype=F32) / HALF)
    ang = pos.astype(F32)[:, None] * inv_freq[None, :]
    cos, sin = jnp.cos(ang), jnp.sin(ang)
    n = pos.shape[0]
    zeros = lambda w: jnp.zeros((n, w), F32)
    rc = jnp.concatenate([jnp.ones((n, MLA_NOPE), F32), cos, cos, zeros(SLAB - MLA_QK)], axis=1)
    rsu = jnp.concatenate([zeros(MLA_NOPE), -sin, zeros(SLAB - ROPE_HI)], axis=1)
    rsd = jnp.concatenate([zeros(ROPE_HI), sin, zeros(SLAB - MLA_QK)], axis=1)
    return rc, rsu, rsd


def _prepare_weights(g_attn, w_in, w_conv, gdn_a_log, gdn_dt_bias, g_gdn_out, g_q_a, w_q_b, g_q_nope, g_q_rope,
                     g_kv_a, g_k_rope, w_kv_b, g_k_nope):
    row = lambda v: v.reshape(1, -1).astype(F32)
    a = w_in[:, CONV_DIM:CONV_DIM + GDN_HEADS]
    b = w_in[:, CONV_DIM + GDN_HEADS:CONV_DIM + 2 * GDN_HEADS]
    o_z = CONV_DIM + 2 * GDN_HEADS
    o_qa = o_z + GDN_WIDTH
    o_kv = o_qa + Q_LORA
    zc = lambda w: jnp.zeros((D_MODEL, w), F32)
    gb_slab = jnp.concatenate([a, a, a, zc(BETA_LANE0 - 3 * GDN_HEADS), b, b, b, zc(LANES - BETA_LANE0 - 3 * GDN_HEADS)], axis=1)
    kr_slab = jnp.concatenate([zc(ROPE_LO), w_in[:, o_kv + KV_LORA:], zc(SLAB - MLA_QK)], axis=1)
    w_in_p = jnp.concatenate([w_in[:, :CONV_DIM], gb_slab, w_in[:, o_z:o_qa], w_in[:, o_qa:o_kv],
                              w_in[:, o_kv:o_kv + KV_LORA], kr_slab], axis=1).astype(BF16)

    def scalar_slab(v):
        z8 = jnp.zeros((BETA_LANE0 - 3 * GDN_HEADS,), F32)
        return jnp.concatenate([v, v, v, z8, jnp.zeros((LANES - BETA_LANE0,), F32)]).reshape(1, LANES)

    wq = w_q_b.reshape(Q_LORA, MLA_HEADS, MLA_QK)
    w_qb = jnp.concatenate([wq, jnp.zeros((Q_LORA, MLA_HEADS, SLAB - MLA_QK), F32)], axis=2).reshape(Q_LORA, QK_SLAB_W).astype(BF16)
    gq = jnp.tile(jnp.concatenate([g_q_nope, g_q_rope, jnp.zeros((SLAB - MLA_QK,), F32)]), MLA_HEADS).reshape(1, QK_SLAB_W)
    wkv = w_kv_b.reshape(KV_LORA, MLA_HEADS, MLA_NOPE + MLA_VDIM)
    wk_part, wv_part = wkv[:, :, :MLA_NOPE], wkv[:, :, MLA_NOPE:]
    wk = jnp.concatenate([wk_part, jnp.zeros((KV_LORA, MLA_HEADS, SLAB - MLA_NOPE), F32)], axis=2).reshape(KV_LORA, QK_SLAB_W).astype(BF16)
    gk = jnp.tile(jnp.concatenate([g_k_nope, jnp.zeros((SLAB - MLA_NOPE,), F32)]), MLA_HEADS).reshape(1, QK_SLAB_W)
    gkr = jnp.concatenate([jnp.zeros((ROPE_LO,), F32), g_k_rope, jnp.zeros((SLAB - MLA_QK,), F32)]).reshape(1, SLAB)
    wvt = wv_part.reshape(KV_LORA, MLA_HEADS * MLA_VDIM).T.astype(BF16)
    wkt = wk_part.reshape(KV_LORA, MLA_HEADS * MLA_NOPE).T.astype(BF16)
    wv_bd = jnp.zeros((MLA_HEADS, KV_LORA, MLA_HEADS, MLA_VDIM), F32)
    wv_bd = wv_bd.at[jnp.arange(MLA_HEADS), :, jnp.arange(MLA_HEADS), :].set(jnp.moveaxis(wv_part, 1, 0))
    wv_bd = wv_bd.reshape(MLA_HEADS * KV_LORA, MLA_HEADS * MLA_VDIM).astype(BF16)

    ered_g, eexp_g = _seg_tables(2 * GDN_QK_DIM, [(h * GDN_DK, (h + 1) * GDN_DK) for h in range(2 * GDN_HEADS)], False)
    q_segs = ([(h * SLAB, h * SLAB + MLA_NOPE) for h in range(MLA_HEADS)]
              + [(h * SLAB + ROPE_LO, h * SLAB + MLA_QK) for h in range(MLA_HEADS)])
    ered_q, eexp_q = _seg_tables(QK_SLAB_W, q_segs, True)
    ered_k, eexp_k = _seg_tables(QK_SLAB_W, q_segs[:MLA_HEADS], True)
    ered_r, eexp_r = _seg_tables(SLAB, [(ROPE_LO, MLA_QK)], True)
    e6 = np.zeros((LANES, 2 * GDN_WIDTH), np.float32)
    for copy in range(3):
        for h in range(GDN_HEADS):
            e6[copy * GDN_HEADS + h, h * GDN_DV:(h + 1) * GDN_DV] = 1.0
            e6[BETA_LANE0 + copy * GDN_HEADS + h, GDN_WIDTH + h * GDN_DV:GDN_WIDTH + (h + 1) * GDN_DV] = 1.0
    esum = np.kron(np.eye(GDN_HEADS, dtype=np.float32), np.full((GDN_DV, GDN_DV), 1.0 / GDN_DV, np.float32))

    mixer = dict(g_attn=row(g_attn), w_in=w_in_p, w_conv=w_conv.astype(F32), alog=scalar_slab(gdn_a_log),
                 dtb=scalar_slab(gdn_dt_bias), ered_g=ered_g, eexp_g=eexp_g, g_q_a=row(g_q_a), w_qb=w_qb, gq=gq,
                 ered_q=ered_q, eexp_q=eexp_q, g_kv_a=row(g_kv_a), gkr=gkr, ered_r=ered_r, eexp_r=eexp_r, wk=wk, gk=gk,
                 ered_k=ered_k, eexp_k=eexp_k)
    extra = dict(wvt=wvt, wkt=wkt, wv_bd=wv_bd, e6=jnp.asarray(e6, BF16), esum=jnp.asarray(esum, BF16),
                 gout=jnp.tile(g_gdn_out, GDN_HEADS).reshape(1, GDN_WIDTH).astype(F32))
    return mixer, extra


def _params(sem=None):
    return pltpu.CompilerParams(dimension_semantics=sem, vmem_limit_bytes=VMEM_LIMIT)


def _prompt_mixer(x2, slabs, mixer, wvt, bsz, seq, tm):
    n = x2.shape[0]
    tps = seq // tm
    consts = [mixer[k] for k in MIXER_CONSTS] + [wvt]
    row_spec = lambda w: pl.BlockSpec((tm, w), lambda i: (i, 0))
    rope_spec = pl.BlockSpec((tm, SLAB), lambda i: (i % tps, 0))
    out_shapes = [
        jax.ShapeDtypeStruct((n, GDN_QK_DIM), F32), jax.ShapeDtypeStruct((n, GDN_QK_DIM), F32),
        jax.ShapeDtypeStruct((n, GDN_WIDTH), F32), jax.ShapeDtypeStruct((n, LANES), F32),
        jax.ShapeDtypeStruct((n, GDN_WIDTH), F32), jax.ShapeDtypeStruct((n, QK_SLAB_W), BF16),
        jax.ShapeDtypeStruct((n, QK_SLAB_W), BF16), jax.ShapeDtypeStruct((MLA_HEADS * MLA_VDIM, n), BF16),
        jax.ShapeDtypeStruct((n, KV_LORA), F32), jax.ShapeDtypeStruct((n, MLA_ROPE), F32),
        jax.ShapeDtypeStruct((bsz, CONV_WIDTH - 1, CONV_DIM), F32)]
    out_specs = [row_spec(GDN_QK_DIM), row_spec(GDN_QK_DIM), row_spec(GDN_WIDTH), row_spec(LANES), row_spec(GDN_WIDTH),
                 row_spec(QK_SLAB_W), row_spec(QK_SLAB_W), pl.BlockSpec((MLA_HEADS * MLA_VDIM, tm), lambda i: (0, i)),
                 row_spec(KV_LORA), row_spec(MLA_ROPE),
                 pl.BlockSpec((1, CONV_WIDTH - 1, CONV_DIM), lambda i: (i // tps, 0, 0))]
    return pl.pallas_call(
        functools.partial(_prompt_mixer_kernel, tps, tm),
        grid=(n // tm,),
        in_specs=[row_spec(D_MODEL), rope_spec, rope_spec, rope_spec] + [_const_spec(a, 1) for a in consts],
        out_specs=out_specs, out_shape=out_shapes,
        scratch_shapes=[pltpu.VMEM((tm + 8, CONV_DIM), F32)],
        compiler_params=_params(("arbitrary",)), name="prompt_mixer",
    )(x2, *slabs, *consts)


def _sample_mixer(x2, slabs, hist, mixer):
    n = x2.shape[0]
    consts = [mixer[k] for k in MIXER_CONSTS]
    sd = lambda w: jax.ShapeDtypeStruct((n, w), F32)
    return pl.pallas_call(
        _sample_mixer_kernel,
        out_shape=[sd(GDN_QK_DIM), sd(GDN_QK_DIM), sd(GDN_WIDTH), sd(LANES), sd(GDN_WIDTH), sd(QK_SLAB_W), sd(QK_SLAB_W),
                   sd(KV_LORA), sd(MLA_ROPE), sd(CONV_DIM)],
        compiler_params=_params(), name="sample_mixer",
    )(x2, *slabs, hist, *consts)


def _gdn_prompt(qg, kg, vg, gb, z, extra, bsz, seq, tb):
    n = qg.shape[0]
    nblk = seq // tb
    row_spec = lambda w: pl.BlockSpec((tb, w), lambda b, j: (b * nblk + j, 0))
    consts = [extra["e6"], extra["esum"], extra["gout"]]
    return pl.pallas_call(
        functools.partial(_gdn_kernel, nblk, tb),
        grid=(bsz, nblk),
        in_specs=[row_spec(GDN_QK_DIM), row_spec(GDN_QK_DIM), row_spec(GDN_WIDTH), row_spec(LANES), row_spec(GDN_WIDTH)]
        + [_const_spec(a, 2) for a in consts],
        out_specs=[row_spec(GDN_WIDTH), pl.BlockSpec((1, GDN_HEADS, GDN_DK, GDN_DV), lambda b, j: (b, 0, 0, 0))],
        out_shape=[jax.ShapeDtypeStruct((n, GDN_WIDTH), BF16), jax.ShapeDtypeStruct((bsz, GDN_HEADS, GDN_DK, GDN_DV), F32)],
        scratch_shapes=[pltpu.VMEM((GDN_HEADS // GROUP_HEADS, GROUP_W, GROUP_W), F32)],
        compiler_params=_params(("arbitrary", "arbitrary")), name="gdn_chunked",
    )(qg, kg, vg, gb, z, *consts)


def _attn_prompt(qm, km, vt, bsz, seq):
    n = qm.shape[0]
    t = ATT_T
    nq = seq // t
    return pl.pallas_call(
        _attn_kernel,
        grid=(bsz, MLA_HEADS // 2, nq),
        in_specs=[pl.BlockSpec((t, 2 * SLAB), lambda b, hp, qi: (b * nq + qi, hp)),
                  pl.BlockSpec((seq, 2 * SLAB), lambda b, hp, qi: (b, hp)),
                  pl.BlockSpec((2 * MLA_VDIM, seq), lambda b, hp, qi: (hp, b))],
        out_specs=pl.BlockSpec((t, 2 * MLA_VDIM), lambda b, hp, qi: (b * nq + qi, hp)),
        out_shape=jax.ShapeDtypeStruct((n, MLA_HEADS * MLA_VDIM), BF16),
        compiler_params=_params(("arbitrary", "arbitrary", "arbitrary")), name="mla_prompt_attention",
    )(qm, km, vt)


def _tail(x2, og, om, p2, tailw, tm):
    n = x2.shape[0]
    row_spec = lambda w: pl.BlockSpec((tm, w), lambda i: (i, 0))
    return pl.pallas_call(
        _tail_kernel,
        grid=(n // tm,),
        in_specs=[row_spec(D_MODEL), row_spec(GDN_WIDTH), row_spec(GDN_WIDTH), row_spec(PLE_DIM)]
        + [_const_spec(a, 1) for a in tailw],
        out_specs=row_spec(D_MODEL), out_shape=jax.ShapeDtypeStruct((n, D_MODEL), F32),
        compiler_params=_params(("arbitrary",)), name="layer_tail",
    )(x2, og, om, p2, *tailw)


def _sample_prep(qm, km, gk, wk):
    n = qm.shape[0]
    return pl.pallas_call(
        _sample_prep_kernel,
        out_shape=[jax.ShapeDtypeStruct((MLA_HEADS, n, KV_LORA), F32), jax.ShapeDtypeStruct((MLA_HEADS, n, MLA_ROPE), F32),
                   jax.ShapeDtypeStruct((MLA_HEADS, n, 1), F32)],
        compiler_params=_params(), name="sample_prep",
    )(qm, km, gk, wk)


def _paged_attention(page_table, pool_ckv, pool_kr, qabs, qr, sself, ckv_new, wkt):
    bs, n_pages = page_table.shape
    page = pool_ckv.shape[1]
    pp = PAGES_PER_STEP
    nsteps = n_pages // pp
    ckv_specs = [pl.BlockSpec((None, page, KV_LORA), lambda b, g, pt, j=j: (pt[b, g * pp + j], 0, 0)) for j in range(pp)]
    kr_specs = [pl.BlockSpec((None, page, MLA_ROPE), lambda b, g, pt, j=j: (pt[b, g * pp + j], 0, 0)) for j in range(pp)]
    per_b = lambda w: pl.BlockSpec((None, MLA_HEADS, w), lambda b, g, pt: (b, 0, 0))
    grid_spec = pltpu.PrefetchScalarGridSpec(
        num_scalar_prefetch=1, grid=(bs, nsteps),
        in_specs=ckv_specs + kr_specs + [per_b(KV_LORA), per_b(MLA_ROPE), per_b(1),
                                         pl.BlockSpec((None, 1, KV_LORA), lambda b, g, pt: (b, 0, 0)),
                                         pl.BlockSpec(wkt.shape, lambda b, g, pt: (0, 0))],
        out_specs=pl.BlockSpec((None, MLA_HEADS, KV_LORA), lambda b, g, pt: (b, 0, 0)),
        scratch_shapes=[pltpu.VMEM((MLA_HEADS, 1), F32), pltpu.VMEM((MLA_HEADS, 1), F32), pltpu.VMEM((MLA_HEADS, KV_LORA), F32)])
    return pl.pallas_call(
        functools.partial(_paged_kernel, nsteps), grid_spec=grid_spec,
        out_shape=jax.ShapeDtypeStruct((bs, MLA_HEADS, KV_LORA), F32),
        compiler_params=_params(("arbitrary", "arbitrary")), name="mla_paged_decode",
    )(page_table, *([pool_ckv] * pp), *([pool_kr] * pp), qabs, qr, sself, ckv_new, wkt)


def _sample_gdn(q3, k3, v3, gb3, state):
    bs = q3.shape[0]
    hk = GDN_HEADS * GDN_DK
    b3 = lambda r, w: pl.BlockSpec((None, r, w), lambda b: (b, 0, 0))
    return pl.pallas_call(
        _sample_gdn_kernel, grid=(bs,),
        in_specs=[b3(1, hk), b3(1, hk), b3(GDN_HEADS, GDN_DV), b3(1, LANES), b3(hk, GDN_DV)],
        out_specs=[b3(hk, GDN_DV), b3(GDN_HEADS, GDN_DV)],
        out_shape=[jax.ShapeDtypeStruct((bs, hk, GDN_DV), F32), jax.ShapeDtypeStruct((bs, GDN_HEADS, GDN_DV), F32)],
        compiler_params=_params(("arbitrary",)), name="gdn_recurrent_step",
    )(q3, k3, v3, gb3, state)


def _sample_mix(o2, z, olat2, extra):
    n = o2.shape[0]
    return pl.pallas_call(
        _sample_mix_kernel,
        out_shape=[jax.ShapeDtypeStruct((n, GDN_WIDTH), BF16), jax.ShapeDtypeStruct((n, MLA_HEADS * MLA_VDIM), BF16)],
        compiler_params=_params(), name="sample_mix",
    )(o2, z, olat2, extra["wv_bd"], extra["esum"], extra["gout"])


def _pick(n, prefs):
    for t in prefs:
        if n % t == 0:
            return t
    return n


def kernel(x_prompt, x_sample, cache_ckv, cache_krope, state_gdn, state_conv, page_table, p_prompt, p_sample, g_attn, w_in, w_conv, gdn_a_log, gdn_dt_bias, g_gdn_out, g_q_a, w_q_b, g_q_nope, g_q_rope, g_kv_a, g_k_rope, w_kv_b, g_k_nope, w_o, g_ffn, w_ffn_gate, w_ffn_up, w_ffn_down, g_ple, w_ple_gate, w_ple_proj):
    depth = g_attn.shape[0]
    assert depth == 1 and x_sample.shape[1] == 1, "single layer, single new token per sample row"
    bp, seq, _ = x_prompt.shape
    bs = x_sample.shape[0]
    past = page_table.shape[1] * cache_ckv.shape[2]
    li = 0
    mixer, extra = _prepare_weights(g_attn[li], w_in[li], w_conv[li], gdn_a_log[li], gdn_dt_bias[li], g_gdn_out[li],
                                    g_q_a[li], w_q_b[li], g_q_nope[li], g_q_rope[li], g_kv_a[li], g_k_rope[li],
                                    w_kv_b[li], g_k_nope[li])
    row = lambda v: v.reshape(1, -1).astype(F32)
    tailw = [w_o[li].astype(BF16), row(g_ffn[li]), w_ffn_gate[li].astype(BF16), w_ffn_up[li].astype(BF16),
             w_ffn_down[li].astype(BF16), row(g_ple[li]), w_ple_gate[li].astype(BF16), w_ple_proj[li].astype(BF16)]

    n = bp * seq
    xp2 = x_prompt.reshape(n, D_MODEL)
    tm = _pick(seq, (256, 128, 64, 32, 16, 8))
    (qg, kg, vg, gb, z, qm, km, vt, ckv_p, kr_p, conv_p) = _prompt_mixer(
        xp2, _rope_slabs(jnp.arange(seq)), mixer, extra["wvt"], bp, seq, tm)
    tb = _pick(seq, (512, 256, 128, 64))
    og, gdn_p = _gdn_prompt(qg, kg, vg, gb, z, extra, bp, seq, tb)
    om = _attn_prompt(qm, km, vt, bp, seq)
    y_prompt = _tail(xp2, og, om, p_prompt[li].reshape(n, PLE_DIM), tailw, tm).reshape(bp, seq, D_MODEL)

    xs2 = x_sample.reshape(bs, D_MODEL)
    hist = jnp.moveaxis(state_conv[li], 1, 0)
    (qg_s, kg_s, vg_s, gb_s, z_s, qm_s, km_s, ckv_s, kr_s, cin_s) = _sample_mixer(
        xs2, _rope_slabs(past + jnp.arange(1)), hist, mixer)
    qabs, qr, sself = _sample_prep(qm_s, km_s, mixer["gk"], mixer["wk"])
    per_row = lambda t: jnp.swapaxes(t, 0, 1)
    olat = _paged_attention(page_table, cache_ckv[li], cache_krope[li], per_row(qabs), per_row(qr), per_row(sself),
                            ckv_s.reshape(bs, 1, KV_LORA), extra["wkt"])
    hk = GDN_HEADS * GDN_DK
    s_new, o_s = _sample_gdn(qg_s.reshape(bs, 1, hk), kg_s.reshape(bs, 1, hk), vg_s.reshape(bs, GDN_HEADS, GDN_DV),
                             gb_s.reshape(bs, 1, LANES), state_gdn[li].reshape(bs, hk, GDN_DV))
    og_s, om_s = _sample_mix(o_s.reshape(bs, GDN_WIDTH), z_s, olat.reshape(bs, MLA_HEADS * KV_LORA), extra)
    y_sample = _tail(xs2, og_s, om_s, p_sample[li].reshape(bs, PLE_DIM), tailw, bs).reshape(bs, 1, D_MODEL)
    conv_s = jnp.concatenate([state_conv[li][:, 1:], cin_s[:, None, :]], axis=1)

    return (y_prompt, y_sample,
            ckv_p.reshape(1, bp, seq, KV_LORA), kr_p.reshape(1, bp, seq, MLA_ROPE),
            gdn_p[None], conv_p[None],
            ckv_s.reshape(1, bs, 1, KV_LORA), kr_s.reshape(1, bs, 1, MLA_ROPE),
            s_new.reshape(1, bs, GDN_HEADS, GDN_DK, GDN_DV), conv_s[None])
```

```python
import functools
import math

import numpy as np
import jax
import jax.numpy as jnp
from jax import lax
from jax.experimental import pallas as pl
from jax.experimental.pallas import tpu as pltpu

F32 = jnp.float32
BF16 = jnp.bfloat16

D_MODEL = 1024
PLE_DIM = 256
GDN_HEADS = 8
GDN_DK = 64
GDN_DV = 64
GDN_WIDTH = GDN_HEADS * GDN_DV
GDN_QK_DIM = GDN_HEADS * GDN_DK
CONV_WIDTH = 4
CONV_DIM = 2 * GDN_QK_DIM + GDN_WIDTH
GDN_CHUNK = 64
MLA_HEADS = 8
MLA_NOPE = 64
MLA_ROPE = 32
MLA_VDIM = 64
MLA_QK = MLA_NOPE + MLA_ROPE
Q_LORA = 384
KV_LORA = 256
ROPE_THETA = 10000.0
ATTN_SCALE = MLA_QK ** -0.5
D_FF = 2816
EPS = 1e-6

LANES = 128
SLAB = 128
HALF = MLA_ROPE // 2
ROPE_LO = MLA_NOPE
ROPE_HI = MLA_NOPE + HALF
QK_SLAB_W = MLA_HEADS * SLAB

OFF_CONV = 0
OFF_GB = OFF_CONV + CONV_DIM
OFF_Z = OFF_GB + LANES
OFF_QA = OFF_Z + GDN_WIDTH
OFF_CKV = OFF_QA + Q_LORA
OFF_KR = OFF_CKV + KV_LORA
IN_PAD = OFF_KR + LANES
BETA_LANE0 = 32

VMEM_LIMIT = 56 * 1024 * 1024


def _dot(a, b):
    return jnp.dot(a, b, preferred_element_type=F32)


def _dot_nt(a, b):
    return lax.dot_general(a, b, (((1,), (1,)), ((), ())), preferred_element_type=F32)


def _dot_tn(a, b):
    return lax.dot_general(a, b, (((0,), (0,)), ((), ())), preferred_element_type=F32)


def _split2(x):
    hi = x.astype(BF16)
    lo = (x - hi.astype(F32)).astype(BF16)
    return hi, lo


def _split3(x):
    hi = x.astype(BF16)
    r1 = x - hi.astype(F32)
    mid = r1.astype(BF16)
    lo = (r1 - mid.astype(F32)).astype(BF16)
    return hi, mid, lo


def _sigmoid(x):
    return 1.0 / (1.0 + jnp.exp(-x))


def _silu(x):
    return x * _sigmoid(x)


def _softplus(x):
    return jnp.maximum(x, 0.0) + jnp.log1p(jnp.exp(-jnp.abs(x)))


def _rmsnorm(x, g):
    return x * lax.rsqrt(jnp.mean(x * x, axis=-1, keepdims=True) + EPS) * g


def _seg_rsqrt(x, ered, eexp, nseg):
    hi, lo = _split2(x * x)
    red = _dot(hi, ered) + _dot(lo, ered)
    r = lax.rsqrt(red + EPS)
    p0, p1, p2 = _split3(r)
    lane = lax.broadcasted_iota(jnp.int32, r.shape, 1)
    piece = jnp.where(lane < nseg, p0, jnp.where(lane < 2 * nseg, p1, p2))
    return _dot(piece, eexp)


def _rope(x, c, s_up, s_dn):
    w = x.shape[-1]
    up = pltpu.roll(x, w - HALF, axis=1)
    dn = pltpu.roll(x, HALF, axis=1)
    return x * c + up * s_up + dn * s_dn


def _tile_lanes(x, n):
    return jnp.concatenate([x] * n, axis=1)


def _const_spec(arr, ngrid):
    nd = arr.ndim
    return pl.BlockSpec(arr.shape, lambda *a, _nd=nd: (0,) * _nd, pipeline_mode=pl.Buffered(1))


MIXER_CONSTS = ("g_attn", "w_in", "w_conv", "alog", "dtb", "ered_g", "eexp_g", "g_q_a", "w_qb", "gq", "ered_q",
                "eexp_q", "g_kv_a", "gkr", "ered_r", "eexp_r", "wk", "gk", "ered_k", "eexp_k")


def _mixer_rows(x, c, conv_fn, rc, rsu, rsd):
    xn = _rmsnorm(x, c["g_attn"][...]).astype(BF16)
    w_in = c["w_in"]

    conv_in = _dot(xn, w_in[:, OFF_CONV:OFF_GB])
    y = _silu(conv_fn(conv_in))
    qk = y[:, :2 * GDN_QK_DIM]
    qk = qk * _seg_rsqrt(qk, c["ered_g"][...], c["eexp_g"][...], 2 * GDN_HEADS)
    q_g = qk[:, :GDN_QK_DIM] * (GDN_DK ** -0.5)
    k_g = qk[:, GDN_QK_DIM:]
    v_g = y[:, 2 * GDN_QK_DIM:]

    ab = _dot(xn, w_in[:, OFF_GB:OFF_Z])
    lane = lax.broadcasted_iota(jnp.int32, ab.shape, 1)
    g_log = -jnp.exp(c["alog"][...]) * _softplus(ab + c["dtb"][...])
    gb = jnp.where(lane < BETA_LANE0, g_log, _sigmoid(ab))

    z = _dot(xn, w_in[:, OFF_Z:OFF_QA])

    qa = _dot(xn, w_in[:, OFF_QA:OFF_CKV])
    qs = _dot(_rmsnorm(qa, c["g_q_a"][...]).astype(BF16), c["w_qb"][...])
    qs = qs * _seg_rsqrt(qs, c["ered_q"][...], c["eexp_q"][...], 2 * MLA_HEADS) * c["gq"][...]
    q_mla = _rope(qs, _tile_lanes(rc, MLA_HEADS), _tile_lanes(rsu, MLA_HEADS), _tile_lanes(rsd, MLA_HEADS)) * ATTN_SCALE

    ckv = _rmsnorm(_dot(xn, w_in[:, OFF_CKV:OFF_KR]), c["g_kv_a"][...])
    ckv_bf = ckv.astype(BF16)
    kr = _dot(xn, w_in[:, OFF_KR:IN_PAD])
    kr = kr * _seg_rsqrt(kr, c["ered_r"][...], c["eexp_r"][...], 1) * c["gkr"][...]
    kr = _rope(kr, rc, rsu, rsd)
    kk = _dot(ckv_bf, c["wk"][...])
    kk = kk * _seg_rsqrt(kk, c["ered_k"][...], c["eexp_k"][...], MLA_HEADS) * c["gk"][...]
    k_mla = kk + _tile_lanes(kr, MLA_HEADS)
    return dict(conv_in=conv_in, q_g=q_g, k_g=k_g, v_g=v_g, gb=gb, z=z, q_mla=q_mla, k_mla=k_mla,
                ckv=ckv, ckv_bf=ckv_bf, kr=kr)


def _prompt_mixer_kernel(tiles_per_seq, tm, x_ref, rc_ref, rsu_ref, rsd_ref, *refs):
    nc = len(MIXER_CONSTS)
    c = dict(zip(MIXER_CONSTS, refs[:nc]))
    wvt_ref = refs[nc]
    (qg_ref, kg_ref, vg_ref, gb_ref, z_ref, qm_ref, km_ref, vt_ref, ckv_ref, kro_ref, cs_ref, ext_ref) = refs[nc + 1:]
    i = pl.program_id(0)

    @pl.when(i % tiles_per_seq == 0)
    def _():
        ext_ref[5:8, :] = jnp.zeros((3, CONV_DIM), F32)

    def conv_fn(conv_in):
        w = c["w_conv"]
        ext_ref[8:8 + tm, :] = conv_in
        return (conv_in * w[3:4, :] + ext_ref[7:7 + tm, :] * w[2:3, :] + ext_ref[6:6 + tm, :] * w[1:2, :]
                + ext_ref[5:5 + tm, :] * w[0:1, :])

    r = _mixer_rows(x_ref[...], c, conv_fn, rc_ref[...], rsu_ref[...], rsd_ref[...])
    last3 = ext_ref[tm + 5:tm + 8, :]
    ext_ref[5:8, :] = last3
    cs_ref[0] = last3
    qg_ref[...] = r["q_g"]
    kg_ref[...] = r["k_g"]
    vg_ref[...] = r["v_g"]
    gb_ref[...] = r["gb"]
    z_ref[...] = r["z"]
    qm_ref[...] = r["q_mla"].astype(BF16)
    km_ref[...] = r["k_mla"].astype(BF16)
    vt_ref[...] = _dot_nt(wvt_ref[...], r["ckv_bf"]).astype(BF16)
    ckv_ref[...] = r["ckv"]
    kro_ref[...] = r["kr"][:, ROPE_LO:ROPE_LO + MLA_ROPE]


def _sample_mixer_kernel(x_ref, rc_ref, rsu_ref, rsd_ref, hist_ref, *refs):
    nc = len(MIXER_CONSTS)
    c = dict(zip(MIXER_CONSTS, refs[:nc]))
    (qg_ref, kg_ref, vg_ref, gb_ref, z_ref, qm_ref, km_ref, ckv_ref, kro_ref, cin_ref) = refs[nc:]

    def conv_fn(conv_in):
        w = c["w_conv"]
        return conv_in * w[3:4, :] + hist_ref[2] * w[2:3, :] + hist_ref[1] * w[1:2, :] + hist_ref[0] * w[0:1, :]

    r = _mixer_rows(x_ref[...], c, conv_fn, rc_ref[...], rsu_ref[...], rsd_ref[...])
    qg_ref[...] = r["q_g"]
    kg_ref[...] = r["k_g"]
    vg_ref[...] = r["v_g"]
    gb_ref[...] = r["gb"]
    z_ref[...] = r["z"]
    qm_ref[...] = r["q_mla"]
    km_ref[...] = r["k_mla"]
    ckv_ref[...] = r["ckv"]
    kro_ref[...] = r["kr"][:, ROPE_LO:ROPE_LO + MLA_ROPE]
    cin_ref[...] = r["conv_in"]


GROUP_HEADS = 4
GROUP_W = GROUP_HEADS * GDN_DK


def _cumsum_rows(x, period):
    row = lax.broadcasted_iota(jnp.int32, x.shape, 0) % period
    s = 1
    while s < period:
        x = x + jnp.where(row >= s, pltpu.roll(x, s, axis=0), 0.0)
        s *= 2
    return x


def _gdn_group(q, k, v, g_r, b_r, s_ref, grp, masks):
    bd_mask, tril_cat, strict_cat, eye_cat = masks
    c = GDN_CHUNK

    def bd(y):
        return jnp.where(bd_mask, jnp.concatenate([y] * GROUP_HEADS, axis=0), 0.0)

    kb = k * b_r
    vb = v * b_r
    k_bd = bd(k)
    aq = _dot_nt(jnp.concatenate([kb, q], axis=0), k_bd)
    g_col = jnp.sum(jnp.where(eye_cat, g_r, 0.0), axis=0, keepdims=True)
    decay = jnp.exp(jnp.where(tril_cat, g_r - g_col, -jnp.inf))
    a = jnp.where(strict_cat, aq[:c] * decay, 0.0)
    qk = aq[c:] * decay
    p = -a
    s_inv = jnp.where(eye_cat, 1.0, 0.0) + p
    p = _dot(p, bd(p))
    for lvl in range(5):
        s_bd = bd(s_inv)
        if lvl < 4:
            p_bd = bd(p)
            s_inv = s_inv + _dot(p, s_bd)
            p = _dot(p, p_bd)
        else:
            s_inv = s_inv + _dot(p, s_bd)
    eg = jnp.exp(g_r)
    u = _dot(s_inv, bd(vb))
    w = _dot(s_inv, bd(kb * eg))
    state = s_ref[grp]
    ws = _dot(jnp.concatenate([w, q * eg], axis=0), state)
    v_new = u - ws[:c]
    o = ws[c:] + _dot(qk, bd(v_new))
    g_last = g_r[c - 1:c, :]
    kd = k * jnp.exp(g_last - g_r)
    upd = _dot_tn(kd, v_new)
    s_ref[grp] = state * jnp.exp(g_last) + jnp.where(bd_mask, upd, 0.0)
    return o


def _gdn_kernel(nblk, tb, q_ref, k_ref, v_ref, gb_ref, z_ref, e6_ref, esum_ref, gout_ref, o_ref, sfin_ref, s_ref):
    j = pl.program_id(1)

    @pl.when(j == 0)
    def _():
        s_ref[...] = jnp.zeros(s_ref.shape, F32)

    c = GDN_CHUNK
    r_bd = lax.broadcasted_iota(jnp.int32, (GROUP_W, GROUP_W), 0) // c
    c_bd = lax.broadcasted_iota(jnp.int32, (GROUP_W, GROUP_W), 1) // c
    bd_mask = r_bd == c_bd
    ri = lax.broadcasted_iota(jnp.int32, (c, GROUP_W), 0)
    cj = lax.broadcasted_iota(jnp.int32, (c, GROUP_W), 1) % c
    masks = (bd_mask, ri >= cj, ri > cj, ri == cj)
    lane = lax.broadcasted_iota(jnp.int32, (c, LANES), 1)
    copy_id = (lane % BETA_LANE0) // GDN_HEADS

    def chunk(ci, carry):
        r0 = pl.multiple_of(ci * c, c)
        rows = pl.ds(r0, c)
        gb = gb_ref[rows, :]
        sc = jnp.where(lane < BETA_LANE0, _cumsum_rows(gb, c), gb)
        p0, p1, p2 = _split3(sc)
        piece = jnp.where(copy_id == 0, p0, jnp.where(copy_id == 1, p1, p2))
        ex = _dot(piece, e6_ref[...])
        q = q_ref[rows, :]
        k = k_ref[rows, :]
        v = v_ref[rows, :]
        outs = []
        for grp in range(GDN_HEADS // GROUP_HEADS):
            sl = slice(grp * GROUP_W, (grp + 1) * GROUP_W)
            sb = slice(GDN_WIDTH + grp * GROUP_W, GDN_WIDTH + (grp + 1) * GROUP_W)
            outs.append(_gdn_group(q[:, sl], k[:, sl], v[:, sl], ex[:, sl], ex[:, sb], s_ref, grp, masks))
        o = jnp.concatenate(outs, axis=1)
        hi, lo = _split2(o * o)
        ms = _dot(hi, esum_ref[...]) + _dot(lo, esum_ref[...])
        zz = z_ref[rows, :]
        o_ref[rows, :] = (o * lax.rsqrt(ms + EPS) * gout_ref[...] * _silu(zz)).astype(o_ref.dtype)
        return carry

    lax.fori_loop(0, tb // c, chunk, 0)

    @pl.when(j == nblk - 1)
    def _():
        for grp in range(GDN_HEADS // GROUP_HEADS):
            for h in range(GROUP_HEADS):
                sfin_ref[0, grp * GROUP_HEADS + h] = s_ref[grp, h * c:(h + 1) * c, h * c:(h + 1) * c]


ATT_T = 256
ATT_HEADS = 4


def _attn_kernel(q_ref, k_ref, vt_ref, o_ref):
    qi = pl.program_id(2)
    t = ATT_T
    qs = [q_ref[:, hh * SLAB:(hh + 1) * SLAB] for hh in range(ATT_HEADS)]

    def tile(j, carry, masked):
        k0 = pl.multiple_of(j * t, t)
        out = []
        for hh in range(ATT_HEADS):
            m, l, acc = carry[hh]
            kt = k_ref[pl.ds(k0, t), hh * SLAB:(hh + 1) * SLAB]
            s = _dot_nt(kt, qs[hh])
            if masked:
                kp = lax.broadcasted_iota(jnp.int32, (t, t), 0)
                qp = lax.broadcasted_iota(jnp.int32, (t, t), 1)
                s = jnp.where(qp >= kp, s, -jnp.inf)
            m_new = jnp.maximum(m, jnp.max(s, axis=0, keepdims=True))
            alpha = jnp.exp(m - m_new)
            p = jnp.exp(s - m_new)
            l = l * alpha + jnp.sum(p, axis=0, keepdims=True)
            vt = vt_ref[hh * MLA_VDIM:(hh + 1) * MLA_VDIM, pl.ds(k0, t)]
            out.append((m_new, l, acc * alpha + _dot(vt, p.astype(BF16))))
        return tuple(out)

    init = tuple((jnp.full((1, t), -jnp.inf, F32), jnp.zeros((1, t), F32), jnp.zeros((MLA_VDIM, t), F32))
                 for _ in range(ATT_HEADS))
    carry = lax.fori_loop(0, qi, functools.partial(tile, masked=False), init)
    carry = tile(qi, carry, True)
    outs = [acc / l for (_, l, acc) in carry]
    o_ref[...] = jnp.concatenate(outs, axis=0).T.astype(o_ref.dtype)


def _tail_kernel(x_ref, og_ref, om_ref, p_ref, wo_ref, gffn_ref, wg_ref, wu_ref, wd_ref, gple_ref, wpg_ref, wpp_ref, y_ref):
    h = x_ref[...] + _dot(og_ref[...], wo_ref[:GDN_WIDTH, :]) + _dot(om_ref[...], wo_ref[GDN_WIDTH:, :])
    u = _rmsnorm(h, gffn_ref[...]).astype(BF16)
    act = (_silu(_dot(u, wg_ref[...])) * _dot(u, wu_ref[...])).astype(BF16)
    acc = h + _dot(act, wd_ref[...])
    gate = _sigmoid(_dot(_rmsnorm(acc, gple_ref[...]).astype(BF16), wpg_ref[...]))
    y_ref[...] = acc + _dot(p_ref[...].astype(BF16), wpp_ref[...]) * gate


PAGES_PER_STEP = 16
PAGES_PER_SUB = 8


def _sample_prep_kernel(qm_ref, km_ref, gk_ref, wk_ref, qabs_ref, qr_ref, sself_ref):
    qm = qm_ref[...]
    prod = qm * km_ref[...]
    qg = (qm * gk_ref[...]).astype(BF16)
    for h in range(MLA_HEADS):
        sl = slice(h * SLAB, (h + 1) * SLAB)
        qabs_ref[h] = _dot_nt(qg[:, sl], wk_ref[:, sl])
        qr_ref[h] = qm[:, h * SLAB + ROPE_LO:h * SLAB + ROPE_LO + MLA_ROPE]
        sself_ref[h] = jnp.sum(prod[:, sl], axis=1, keepdims=True)


def _paged_kernel(nsteps, pt_ref, *refs):
    pp = PAGES_PER_STEP
    ckv_refs = refs[:pp]
    kr_refs = refs[pp:2 * pp]
    qabs_ref, qr_ref, sself_ref, ckvn_ref, wkt_ref, o_ref, m_ref, l_ref, acc_ref = refs[2 * pp:]
    g = pl.program_id(1)

    @pl.when(g == 0)
    def _():
        m_ref[...] = sself_ref[...]
        l_ref[...] = jnp.ones(l_ref.shape, F32)
        acc_ref[...] = jnp.broadcast_to(ckvn_ref[...], acc_ref.shape)

    qa = qabs_ref[...].astype(BF16)
    qr = qr_ref[...].astype(BF16)
    wkt = wkt_ref[...]
    m, l, acc = m_ref[...], l_ref[...], acc_ref[...]
    cs, ss_list = [], []
    for i in range(pp // PAGES_PER_SUB):
        sub = range(i * PAGES_PER_SUB, (i + 1) * PAGES_PER_SUB)
        c = jnp.concatenate([ckv_refs[j][...] for j in sub], axis=0).astype(BF16)
        krt = jnp.concatenate([kr_refs[j][...] for j in sub], axis=1).astype(BF16)
        kv = _dot_nt(wkt, c)
        ss = jnp.sum((kv * kv).reshape(MLA_HEADS, MLA_NOPE, kv.shape[1]), axis=1)
        ss_list.append(_dot_nt(qa, c) * lax.rsqrt(ss * (1.0 / MLA_NOPE) + EPS) + _dot(qr, krt))
        cs.append(c)
    s = jnp.concatenate(ss_list, axis=1)
    m_new = jnp.maximum(m, jnp.max(s, axis=1, keepdims=True))
    alpha = jnp.exp(m - m_new)
    p = jnp.exp(s - m_new)
    l = l * alpha + jnp.sum(p, axis=1, keepdims=True)
    acc = acc * alpha + _dot(p.astype(BF16), jnp.concatenate(cs, axis=0))
    m_ref[...] = m_new
    l_ref[...] = l
    acc_ref[...] = acc

    @pl.when(g == nsteps - 1)
    def _():
        o_ref[...] = acc / l


def _sample_gdn_kernel(q_ref, k_ref, v_ref, gb_ref, s_ref, snew_ref, o_ref):
    hk = GDN_HEADS * GDN_DK
    hrow = lax.broadcasted_iota(jnp.int32, (GDN_HEADS, hk), 0)
    hlane = lax.broadcasted_iota(jnp.int32, (GDN_HEADS, hk), 1) // GDN_DK
    hm = hrow == hlane
    kmask = jnp.where(hm, jnp.broadcast_to(k_ref[...], (GDN_HEADS, hk)), 0.0)
    qmask = jnp.where(hm, jnp.broadcast_to(q_ref[...], (GDN_HEADS, hk)), 0.0)
    gbb = jnp.broadcast_to(gb_ref[...], (GDN_HEADS, LANES))
    r8 = lax.broadcasted_iota(jnp.int32, (GDN_HEADS, LANES), 0)
    l8 = lax.broadcasted_iota(jnp.int32, (GDN_HEADS, LANES), 1)
    g_col = jnp.sum(jnp.where(l8 == r8, gbb, 0.0), axis=1, keepdims=True)
    b_col = jnp.sum(jnp.where(l8 == r8 + BETA_LANE0, gbb, 0.0), axis=1, keepdims=True)
    ones = jnp.where(hm, 1.0, 0.0).astype(BF16)

    def expand_rows(col):
        e0, e1, e2 = _split3(jnp.broadcast_to(col, (GDN_HEADS, GDN_DV)))
        return _dot_tn(jnp.concatenate([ones, ones, ones], axis=0), jnp.concatenate([e0, e1, e2], axis=0))

    def head_dot(xmask, mat):
        xh, xl = _split2(xmask)
        mh, ml = _split2(mat)
        return _dot(jnp.concatenate([xh, xh, xl], axis=1), jnp.concatenate([mh, ml, mh], axis=0))

    state = s_ref[...] * expand_rows(jnp.exp(g_col))
    delta = (v_ref[...] - head_dot(kmask, state)) * b_col
    kh, kl = _split2(kmask)
    dh, dl = _split2(delta)
    state = state + _dot_tn(jnp.concatenate([kh, kh, kl], axis=0), jnp.concatenate([dh, dl, dh], axis=0))
    snew_ref[...] = state
    o_ref[...] = head_dot(qmask, state)


def _sample_mix_kernel(o_ref, z_ref, olat_ref, wvbd_ref, esum_ref, gout_ref, og_ref, om_ref):
    o = o_ref[...]
    hi, lo = _split2(o * o)
    ms = _dot(hi, esum_ref[...]) + _dot(lo, esum_ref[...])
    og_ref[...] = (o * lax.rsqrt(ms + EPS) * gout_ref[...] * _silu(z_ref[...])).astype(og_ref.dtype)
    om_ref[...] = _dot(olat_ref[...].astype(BF16), wvbd_ref[...]).astype(om_ref.dtype)


def _seg_tables(width, segs, mean):
    n = len(segs)
    ered = np.zeros((width, LANES), np.float32)
    eexp = np.zeros((LANES, width), np.float32)
    for copy in range(3):
        for s, (a, b) in enumerate(segs):
            ered[a:b, copy * n + s] = 1.0 / (b - a) if mean else 1.0
            eexp[copy * n + s, a:b] = 1.0
    return jnp.asarray(ered, BF16), jnp.asarray(eexp, BF16)


def _rope_slabs(pos):
    inv_freq = ROPE_THETA ** (-jnp.arange(HALF, dtype=F32) / HALF)
    ang = pos.astype(F32)[:, None] * inv_freq[None, :]
    cos, sin = jnp.cos(ang), jnp.sin(ang)
    n = pos.shape[0]
    zeros = lambda w: jnp.zeros((n, w), F32)
    rc = jnp.concatenate([jnp.ones((n, MLA_NOPE), F32), cos, cos, zeros(SLAB - MLA_QK)], axis=1)
    rsu = jnp.concatenate([zeros(MLA_NOPE), -sin, zeros(SLAB - ROPE_HI)], axis=1)
    rsd = jnp.concatenate([zeros(ROPE_HI), sin, zeros(SLAB - MLA_QK)], axis=1)
    return rc, rsu, rsd


def _prepare_weights(g_attn, w_in, w_conv, gdn_a_log, gdn_dt_bias, g_gdn_out, g_q_a, w_q_b, g_q_nope, g_q_rope,
                     g_kv_a, g_k_rope, w_kv_b, g_k_nope):
    row = lambda v: v.reshape(1, -1).astype(F32)
    a = w_in[:, CONV_DIM:CONV_DIM + GDN_HEADS]
    b = w_in[:, CONV_DIM + GDN_HEADS:CONV_DIM + 2 * GDN_HEADS]
    o_z = CONV_DIM + 2 * GDN_HEADS
    o_qa = o_z + GDN_WIDTH
    o_kv = o_qa + Q_LORA
    zc = lambda w: jnp.zeros((D_MODEL, w), F32)
    gb_slab = jnp.concatenate([a, a, a, zc(BETA_LANE0 - 3 * GDN_HEADS), b, b, b, zc(LANES - BETA_LANE0 - 3 * GDN_HEADS)], axis=1)
    kr_slab = jnp.concatenate([zc(ROPE_LO), w_in[:, o_kv + KV_LORA:], zc(SLAB - MLA_QK)], axis=1)
    w_in_p = jnp.concatenate([w_in[:, :CONV_DIM], gb_slab, w_in[:, o_z:o_qa], w_in[:, o_qa:o_kv],
                              w_in[:, o_kv:o_kv + KV_LORA], kr_slab], axis=1).astype(BF16)

    def scalar_slab(v):
        z8 = jnp.zeros((BETA_LANE0 - 3 * GDN_HEADS,), F32)
        return jnp.concatenate([v, v, v, z8, jnp.zeros((LANES - BETA_LANE0,), F32)]).reshape(1, LANES)

    wq = w_q_b.reshape(Q_LORA, MLA_HEADS, MLA_QK)
    w_qb = jnp.concatenate([wq, jnp.zeros((Q_LORA, MLA_HEADS, SLAB - MLA_QK), F32)], axis=2).reshape(Q_LORA, QK_SLAB_W).astype(BF16)
    gq = jnp.tile(jnp.concatenate([g_q_nope, g_q_rope, jnp.zeros((SLAB - MLA_QK,), F32)]), MLA_HEADS).reshape(1, QK_SLAB_W)
    wkv = w_kv_b.reshape(KV_LORA, MLA_HEADS, MLA_NOPE + MLA_VDIM)
    wk_part, wv_part = wkv[:, :, :MLA_NOPE], wkv[:, :, MLA_NOPE:]
    wk = jnp.concatenate([wk_part, jnp.zeros((KV_LORA, MLA_HEADS, SLAB - MLA_NOPE), F32)], axis=2).reshape(KV_LORA, QK_SLAB_W).astype(BF16)
    gk = jnp.tile(jnp.concatenate([g_k_nope, jnp.zeros((SLAB - MLA_NOPE,), F32)]), MLA_HEADS).reshape(1, QK_SLAB_W)
    gkr = jnp.concatenate([jnp.zeros((ROPE_LO,), F32), g_k_rope, jnp.zeros((SLAB - MLA_QK,), F32)]).reshape(1, SLAB)
    wvt = wv_part.reshape(KV_LORA, MLA_HEADS * MLA_VDIM).T.astype(BF16)
    wkt = wk_part.reshape(KV_LORA, MLA_HEADS * MLA_NOPE).T.astype(BF16)
    wv_bd = jnp.zeros((MLA_HEADS, KV_LORA, MLA_HEADS, MLA_VDIM), F32)
    wv_bd = wv_bd.at[jnp.arange(MLA_HEADS), :, jnp.arange(MLA_HEADS), :].set(jnp.moveaxis(wv_part, 1, 0))
    wv_bd = wv_bd.reshape(MLA_HEADS * KV_LORA, MLA_HEADS * MLA_VDIM).astype(BF16)

    ered_g, eexp_g = _seg_tables(2 * GDN_QK_DIM, [(h * GDN_DK, (h + 1) * GDN_DK) for h in range(2 * GDN_HEADS)], False)
    q_segs = ([(h * SLAB, h * SLAB + MLA_NOPE) for h in range(MLA_HEADS)]
              + [(h * SLAB + ROPE_LO, h * SLAB + MLA_QK) for h in range(MLA_HEADS)])
    ered_q, eexp_q = _seg_tables(QK_SLAB_W, q_segs, True)
    ered_k, eexp_k = _seg_tables(QK_SLAB_W, q_segs[:MLA_HEADS], True)
    ered_r, eexp_r = _seg_tables(SLAB, [(ROPE_LO, MLA_QK)], True)
    e6 = np.zeros((LANES, 2 * GDN_WIDTH), np.float32)
    for copy in range(3):
        for h in range(GDN_HEADS):
            e6[copy * GDN_HEADS + h, h * GDN_DV:(h + 1) * GDN_DV] = 1.0
            e6[BETA_LANE0 + copy * GDN_HEADS + h, GDN_WIDTH + h * GDN_DV:GDN_WIDTH + (h + 1) * GDN_DV] = 1.0
    esum = np.kron(np.eye(GDN_HEADS, dtype=np.float32), np.full((GDN_DV, GDN_DV), 1.0 / GDN_DV, np.float32))

    mixer = dict(g_attn=row(g_attn), w_in=w_in_p, w_conv=w_conv.astype(F32), alog=scalar_slab(gdn_a_log),
                 dtb=scalar_slab(gdn_dt_bias), ered_g=ered_g, eexp_g=eexp_g, g_q_a=row(g_q_a), w_qb=w_qb, gq=gq,
                 ered_q=ered_q, eexp_q=eexp_q, g_kv_a=row(g_kv_a), gkr=gkr, ered_r=ered_r, eexp_r=eexp_r, wk=wk, gk=gk,
                 ered_k=ered_k, eexp_k=eexp_k)
    extra = dict(wvt=wvt, wkt=wkt, wv_bd=wv_bd, e6=jnp.asarray(e6, BF16), esum=jnp.asarray(esum, BF16),
                 gout=jnp.tile(g_gdn_out, GDN_HEADS).reshape(1, GDN_WIDTH).astype(F32))
    return mixer, extra


def _params(sem=None):
    return pltpu.CompilerParams(dimension_semantics=sem, vmem_limit_bytes=VMEM_LIMIT)


def _prompt_mixer(x2, slabs, mixer, wvt, bsz, seq, tm):
    n = x2.shape[0]
    tps = seq // tm
    consts = [mixer[k] for k in MIXER_CONSTS] + [wvt]
    row_spec = lambda w: pl.BlockSpec((tm, w), lambda i: (i, 0))
    rope_spec = pl.BlockSpec((tm, SLAB), lambda i: (i % tps, 0))
    out_shapes = [
        jax.ShapeDtypeStruct((n, GDN_QK_DIM), F32), jax.ShapeDtypeStruct((n, GDN_QK_DIM), F32),
        jax.ShapeDtypeStruct((n, GDN_WIDTH), F32), jax.ShapeDtypeStruct((n, LANES), F32),
        jax.ShapeDtypeStruct((n, GDN_WIDTH), F32), jax.ShapeDtypeStruct((n, QK_SLAB_W), BF16),
        jax.ShapeDtypeStruct((n, QK_SLAB_W), BF16), jax.ShapeDtypeStruct((MLA_HEADS * MLA_VDIM, n), BF16),
        jax.ShapeDtypeStruct((n, KV_LORA), F32), jax.ShapeDtypeStruct((n, MLA_ROPE), F32),
        jax.ShapeDtypeStruct((bsz, CONV_WIDTH - 1, CONV_DIM), F32)]
    out_specs = [row_spec(GDN_QK_DIM), row_spec(GDN_QK_DIM), row_spec(GDN_WIDTH), row_spec(LANES), row_spec(GDN_WIDTH),
                 row_spec(QK_SLAB_W), row_spec(QK_SLAB_W), pl.BlockSpec((MLA_HEADS * MLA_VDIM, tm), lambda i: (0, i)),
                 row_spec(KV_LORA), row_spec(MLA_ROPE),
                 pl.BlockSpec((1, CONV_WIDTH - 1, CONV_DIM), lambda i: (i // tps, 0, 0))]
    return pl.pallas_call(
        functools.partial(_prompt_mixer_kernel, tps, tm),
        grid=(n // tm,),
        in_specs=[row_spec(D_MODEL), rope_spec, rope_spec, rope_spec] + [_const_spec(a, 1) for a in consts],
        out_specs=out_specs, out_shape=out_shapes,
        scratch_shapes=[pltpu.VMEM((tm + 8, CONV_DIM), F32)],
        compiler_params=_params(("arbitrary",)), name="prompt_mixer",
    )(x2, *slabs, *consts)


def _sample_mixer(x2, slabs, hist, mixer):
    n = x2.shape[0]
    consts = [mixer[k] for k in MIXER_CONSTS]
    sd = lambda w: jax.ShapeDtypeStruct((n, w), F32)
    return pl.pallas_call(
        _sample_mixer_kernel,
        out_shape=[sd(GDN_QK_DIM), sd(GDN_QK_DIM), sd(GDN_WIDTH), sd(LANES), sd(GDN_WIDTH), sd(QK_SLAB_W), sd(QK_SLAB_W),
                   sd(KV_LORA), sd(MLA_ROPE), sd(CONV_DIM)],
        compiler_params=_params(), name="sample_mixer",
    )(x2, *slabs, hist, *consts)


def _gdn_prompt(qg, kg, vg, gb, z, extra, bsz, seq, tb):
    n = qg.shape[0]
    nblk = seq // tb
    row_spec = lambda w: pl.BlockSpec((tb, w), lambda b, j: (b * nblk + j, 0))
    consts = [extra["e6"], extra["esum"], extra["gout"]]
    return pl.pallas_call(
        functools.partial(_gdn_kernel, nblk, tb),
        grid=(bsz, nblk),
        in_specs=[row_spec(GDN_QK_DIM), row_spec(GDN_QK_DIM), row_spec(GDN_WIDTH), row_spec(LANES), row_spec(GDN_WIDTH)]
        + [_const_spec(a, 2) for a in consts],
        out_specs=[row_spec(GDN_WIDTH), pl.BlockSpec((1, GDN_HEADS, GDN_DK, GDN_DV), lambda b, j: (b, 0, 0, 0))],
        out_shape=[jax.ShapeDtypeStruct((n, GDN_WIDTH), BF16), jax.ShapeDtypeStruct((bsz, GDN_HEADS, GDN_DK, GDN_DV), F32)],
        scratch_shapes=[pltpu.VMEM((GDN_HEADS // GROUP_HEADS, GROUP_W, GROUP_W), F32)],
        compiler_params=_params(("arbitrary", "arbitrary")), name="gdn_chunked",
    )(qg, kg, vg, gb, z, *consts)


def _attn_prompt(qm, km, vt, bsz, seq):
    n = qm.shape[0]
    t = ATT_T
    nq = seq // t
    return pl.pallas_call(
        _attn_kernel,
        grid=(bsz, MLA_HEADS // ATT_HEADS, nq),
        in_specs=[pl.BlockSpec((t, ATT_HEADS * SLAB), lambda b, hp, qi: (b * nq + qi, hp)),
                  pl.BlockSpec((seq, ATT_HEADS * SLAB), lambda b, hp, qi: (b, hp)),
                  pl.BlockSpec((ATT_HEADS * MLA_VDIM, seq), lambda b, hp, qi: (hp, b))],
        out_specs=pl.BlockSpec((t, ATT_HEADS * MLA_VDIM), lambda b, hp, qi: (b * nq + qi, hp)),
        out_shape=jax.ShapeDtypeStruct((n, MLA_HEADS * MLA_VDIM), BF16),
        compiler_params=_params(("arbitrary", "arbitrary", "arbitrary")), name="mla_prompt_attention",
    )(qm, km, vt)


def _tail(x2, og, om, p2, tailw, tm):
    n = x2.shape[0]
    row_spec = lambda w: pl.BlockSpec((tm, w), lambda i: (i, 0))
    return pl.pallas_call(
        _tail_kernel,
        grid=(n // tm,),
        in_specs=[row_spec(D_MODEL), row_spec(GDN_WIDTH), row_spec(GDN_WIDTH), row_spec(PLE_DIM)]
        + [_const_spec(a, 1) for a in tailw],
        out_specs=row_spec(D_MODEL), out_shape=jax.ShapeDtypeStruct((n, D_MODEL), F32),
        compiler_params=_params(("arbitrary",)), name="layer_tail",
    )(x2, og, om, p2, *tailw)


def _sample_prep(qm, km, gk, wk):
    n = qm.shape[0]
    return pl.pallas_call(
        _sample_prep_kernel,
        out_shape=[jax.ShapeDtypeStruct((MLA_HEADS, n, KV_LORA), F32), jax.ShapeDtypeStruct((MLA_HEADS, n, MLA_ROPE), F32),
                   jax.ShapeDtypeStruct((MLA_HEADS, n, 1), F32)],
        compiler_params=_params(), name="sample_prep",
    )(qm, km, gk, wk)


def _paged_attention(page_table, pool_ckv, pool_kr, qabs, qr, sself, ckv_new, wkt):
    bs, n_pages = page_table.shape
    page = pool_ckv.shape[1]
    pp = PAGES_PER_STEP
    nsteps = n_pages // pp
    ckv_specs = [pl.BlockSpec((None, page, KV_LORA), lambda b, g, pt, j=j: (pt[b, g * pp + j], 0, 0)) for j in range(pp)]
    kr_specs = [pl.BlockSpec((None, MLA_ROPE, page), lambda b, g, pt, j=j: (pt[b, g * pp + j], 0, 0)) for j in range(pp)]
    per_b = lambda w: pl.BlockSpec((None, MLA_HEADS, w), lambda b, g, pt: (b, 0, 0))
    grid_spec = pltpu.PrefetchScalarGridSpec(
        num_scalar_prefetch=1, grid=(bs, nsteps),
        in_specs=ckv_specs + kr_specs + [per_b(KV_LORA), per_b(MLA_ROPE), per_b(1),
                                         pl.BlockSpec((None, 1, KV_LORA), lambda b, g, pt: (b, 0, 0)),
                                         pl.BlockSpec(wkt.shape, lambda b, g, pt: (0, 0))],
        out_specs=pl.BlockSpec((None, MLA_HEADS, KV_LORA), lambda b, g, pt: (b, 0, 0)),
        scratch_shapes=[pltpu.VMEM((MLA_HEADS, 1), F32), pltpu.VMEM((MLA_HEADS, 1), F32), pltpu.VMEM((MLA_HEADS, KV_LORA), F32)])
    return pl.pallas_call(
        functools.partial(_paged_kernel, nsteps), grid_spec=grid_spec,
        out_shape=jax.ShapeDtypeStruct((bs, MLA_HEADS, KV_LORA), F32),
        compiler_params=_params(("arbitrary", "arbitrary")), name="mla_paged_decode",
    )(page_table, *([pool_ckv] * pp), *([pool_kr] * pp), qabs, qr, sself, ckv_new, wkt)


def _sample_gdn(q3, k3, v3, gb3, state):
    bs = q3.shape[0]
    hk = GDN_HEADS * GDN_DK
    b3 = lambda r, w: pl.BlockSpec((None, r, w), lambda b: (b, 0, 0))
    return pl.pallas_call(
        _sample_gdn_kernel, grid=(bs,),
        in_specs=[b3(1, hk), b3(1, hk), b3(GDN_HEADS, GDN_DV), b3(1, LANES), b3(hk, GDN_DV)],
        out_specs=[b3(hk, GDN_DV), b3(GDN_HEADS, GDN_DV)],
        out_shape=[jax.ShapeDtypeStruct((bs, hk, GDN_DV), F32), jax.ShapeDtypeStruct((bs, GDN_HEADS, GDN_DV), F32)],
        compiler_params=_params(("arbitrary",)), name="gdn_recurrent_step",
    )(q3, k3, v3, gb3, state)


def _sample_mix(o2, z, olat2, extra):
    n = o2.shape[0]
    return pl.pallas_call(
        _sample_mix_kernel,
        out_shape=[jax.ShapeDtypeStruct((n, GDN_WIDTH), BF16), jax.ShapeDtypeStruct((n, MLA_HEADS * MLA_VDIM), BF16)],
        compiler_params=_params(), name="sample_mix",
    )(o2, z, olat2, extra["wv_bd"], extra["esum"], extra["gout"])


def _pick(n, prefs):
    for t in prefs:
        if n % t == 0:
            return t
    return n


def kernel(x_prompt, x_sample, cache_ckv, cache_krope, state_gdn, state_conv, page_table, p_prompt, p_sample, g_attn, w_in, w_conv, gdn_a_log, gdn_dt_bias, g_gdn_out, g_q_a, w_q_b, g_q_nope, g_q_rope, g_kv_a, g_k_rope, w_kv_b, g_k_nope, w_o, g_ffn, w_ffn_gate, w_ffn_up, w_ffn_down, g_ple, w_ple_gate, w_ple_proj):
    depth = g_attn.shape[0]
    assert depth == 1 and x_sample.shape[1] == 1, "single layer, single new token per sample row"
    bp, seq, _ = x_prompt.shape
    bs = x_sample.shape[0]
    past = page_table.shape[1] * cache_ckv.shape[2]
    li = 0
    mixer, extra = _prepare_weights(g_attn[li], w_in[li], w_conv[li], gdn_a_log[li], gdn_dt_bias[li], g_gdn_out[li],
                                    g_q_a[li], w_q_b[li], g_q_nope[li], g_q_rope[li], g_kv_a[li], g_k_rope[li],
                                    w_kv_b[li], g_k_nope[li])
    row = lambda v: v.reshape(1, -1).astype(F32)
    tailw = [w_o[li].astype(BF16), row(g_ffn[li]), w_ffn_gate[li].astype(BF16), w_ffn_up[li].astype(BF16),
             w_ffn_down[li].astype(BF16), row(g_ple[li]), w_ple_gate[li].astype(BF16), w_ple_proj[li].astype(BF16)]

    n = bp * seq
    xp2 = x_prompt.reshape(n, D_MODEL)
    tm = _pick(seq, (256, 128, 64, 32, 16, 8))
    (qg, kg, vg, gb, z, qm, km, vt, ckv_p, kr_p, conv_p) = _prompt_mixer(
        xp2, _rope_slabs(jnp.arange(seq)), mixer, extra["wvt"], bp, seq, tm)
    tb = _pick(seq, (512, 256, 128, 64))
    og, gdn_p = _gdn_prompt(qg, kg, vg, gb, z, extra, bp, seq, tb)
    om = _attn_prompt(qm, km, vt, bp, seq)
    y_prompt = _tail(xp2, og, om, p_prompt[li].reshape(n, PLE_DIM), tailw, tm).reshape(bp, seq, D_MODEL)

    xs2 = x_sample.reshape(bs, D_MODEL)
    hist = jnp.moveaxis(state_conv[li], 1, 0)
    (qg_s, kg_s, vg_s, gb_s, z_s, qm_s, km_s, ckv_s, kr_s, cin_s) = _sample_mixer(
        xs2, _rope_slabs(past + jnp.arange(1)), hist, mixer)
    qabs, qr, sself = _sample_prep(qm_s, km_s, mixer["gk"], mixer["wk"])
    per_row = lambda t: jnp.swapaxes(t, 0, 1)
    pool_krt = jnp.swapaxes(cache_krope[li], 1, 2)
    olat = _paged_attention(page_table, cache_ckv[li], pool_krt, per_row(qabs), per_row(qr), per_row(sself),
                            ckv_s.reshape(bs, 1, KV_LORA), extra["wkt"])
    hk = GDN_HEADS * GDN_DK
    s_new, o_s = _sample_gdn(qg_s.reshape(bs, 1, hk), kg_s.reshape(bs, 1, hk), vg_s.reshape(bs, GDN_HEADS, GDN_DV),
                             gb_s.reshape(bs, 1, LANES), state_gdn[li].reshape(bs, hk, GDN_DV))
    og_s, om_s = _sample_mix(o_s.reshape(bs, GDN_WIDTH), z_s, olat.reshape(bs, MLA_HEADS * KV_LORA), extra)
    y_sample = _tail(xs2, og_s, om_s, p_sample[li].reshape(bs, PLE_DIM), tailw, bs).reshape(bs, 1, D_MODEL)
    conv_s = jnp.concatenate([state_conv[li][:, 1:], cin_s[:, None, :]], axis=1)

    return (y_prompt, y_sample,
            ckv_p.reshape(1, bp, seq, KV_LORA), kr_p.reshape(1, bp, seq, MLA_ROPE),
            gdn_p[None], conv_p[None],
            ckv_s.reshape(1, bs, 1, KV_LORA), kr_s.reshape(1, bs, 1, MLA_ROPE),
            s_new.reshape(1, bs, GDN_HEADS, GDN_DK, GDN_DV), conv_s[None])
```

```python
import functools
import math

import numpy as np
import jax
import jax.numpy as jnp
from jax import lax
from jax.experimental import pallas as pl
from jax.experimental.pallas import tpu as pltpu

F32 = jnp.float32
BF16 = jnp.bfloat16

D_MODEL = 1024
PLE_DIM = 256
GDN_HEADS = 8
GDN_DK = 64
GDN_DV = 64
GDN_WIDTH = GDN_HEADS * GDN_DV
GDN_QK_DIM = GDN_HEADS * GDN_DK
CONV_WIDTH = 4
CONV_DIM = 2 * GDN_QK_DIM + GDN_WIDTH
GDN_CHUNK = 64
MLA_HEADS = 8
MLA_NOPE = 64
MLA_ROPE = 32
MLA_VDIM = 64
MLA_QK = MLA_NOPE + MLA_ROPE
Q_LORA = 384
KV_LORA = 256
ROPE_THETA = 10000.0
ATTN_SCALE = MLA_QK ** -0.5
LOG2E = math.log2(math.e)
D_FF = 2816
EPS = 1e-6

LANES = 128
SLAB = 128
HALF = MLA_ROPE // 2
ROPE_LO = MLA_NOPE
ROPE_HI = MLA_NOPE + HALF
QK_SLAB_W = MLA_HEADS * SLAB

OFF_CONV = 0
OFF_GB = OFF_CONV + CONV_DIM
OFF_Z = OFF_GB + LANES
OFF_QA = OFF_Z + GDN_WIDTH
OFF_CKV = OFF_QA + Q_LORA
OFF_KR = OFF_CKV + KV_LORA
IN_PAD = OFF_KR + LANES
BETA_LANE0 = 32

VMEM_LIMIT = 56 * 1024 * 1024


def _dot(a, b):
    return jnp.dot(a, b, preferred_element_type=F32)


def _dot_nt(a, b):
    return lax.dot_general(a, b, (((1,), (1,)), ((), ())), preferred_element_type=F32)


def _dot_tn(a, b):
    return lax.dot_general(a, b, (((0,), (0,)), ((), ())), preferred_element_type=F32)


def _split2(x):
    hi = x.astype(BF16)
    lo = (x - hi.astype(F32)).astype(BF16)
    return hi, lo


def _split3(x):
    hi = x.astype(BF16)
    r1 = x - hi.astype(F32)
    mid = r1.astype(BF16)
    lo = (r1 - mid.astype(F32)).astype(BF16)
    return hi, mid, lo


def _sigmoid(x):
    return 1.0 / (1.0 + jnp.exp(-x))


def _silu(x):
    return x * _sigmoid(x)


def _softplus(x):
    return jnp.maximum(x, 0.0) + jnp.log1p(jnp.exp(-jnp.abs(x)))


def _rmsnorm(x, g):
    return x * lax.rsqrt(jnp.mean(x * x, axis=-1, keepdims=True) + EPS) * g


def _seg_rsqrt(x, ered, eexp, nseg):
    hi, lo = _split2(x * x)
    red = _dot(hi, ered) + _dot(lo, ered)
    r = lax.rsqrt(red + EPS)
    p0, p1, p2 = _split3(r)
    lane = lax.broadcasted_iota(jnp.int32, r.shape, 1)
    piece = jnp.where(lane < nseg, p0, jnp.where(lane < 2 * nseg, p1, p2))
    return _dot(piece, eexp)


def _rope(x, c, s_up, s_dn):
    w = x.shape[-1]
    up = pltpu.roll(x, w - HALF, axis=1)
    dn = pltpu.roll(x, HALF, axis=1)
    return x * c + up * s_up + dn * s_dn


def _tile_lanes(x, n):
    return jnp.concatenate([x] * n, axis=1)


def _const_spec(arr, ngrid):
    nd = arr.ndim
    return pl.BlockSpec(arr.shape, lambda *a, _nd=nd: (0,) * _nd, pipeline_mode=pl.Buffered(1))


MIXER_CONSTS = ("g_attn", "w_in", "w_conv", "alog", "dtb", "ered_g", "eexp_g", "g_q_a", "w_qb", "gq", "ered_q",
                "eexp_q", "g_kv_a", "gkr", "ered_r", "eexp_r", "wk", "gk", "ered_k", "eexp_k")


def _mixer_rows(x, c, conv_fn, rc, rsu, rsd, q_scale):
    xn = _rmsnorm(x, c["g_attn"][...]).astype(BF16)
    w_in = c["w_in"]

    conv_in = _dot(xn, w_in[:, OFF_CONV:OFF_GB])
    y = _silu(conv_fn(conv_in))
    qk = y[:, :2 * GDN_QK_DIM]
    qk = qk * _seg_rsqrt(qk, c["ered_g"][...], c["eexp_g"][...], 2 * GDN_HEADS)
    q_g = qk[:, :GDN_QK_DIM] * (GDN_DK ** -0.5)
    k_g = qk[:, GDN_QK_DIM:]
    v_g = y[:, 2 * GDN_QK_DIM:]

    ab = _dot(xn, w_in[:, OFF_GB:OFF_Z])
    lane = lax.broadcasted_iota(jnp.int32, ab.shape, 1)
    g_log = -jnp.exp(c["alog"][...]) * _softplus(ab + c["dtb"][...])
    gb = jnp.where(lane < BETA_LANE0, g_log, _sigmoid(ab))

    z = _dot(xn, w_in[:, OFF_Z:OFF_QA])

    qa = _dot(xn, w_in[:, OFF_QA:OFF_CKV])
    qs = _dot(_rmsnorm(qa, c["g_q_a"][...]).astype(BF16), c["w_qb"][...])
    qs = qs * _seg_rsqrt(qs, c["ered_q"][...], c["eexp_q"][...], 2 * MLA_HEADS) * c["gq"][...]
    q_mla = _rope(qs, _tile_lanes(rc, MLA_HEADS), _tile_lanes(rsu, MLA_HEADS), _tile_lanes(rsd, MLA_HEADS)) * q_scale

    ckv = _rmsnorm(_dot(xn, w_in[:, OFF_CKV:OFF_KR]), c["g_kv_a"][...])
    ckv_bf = ckv.astype(BF16)
    kr = _dot(xn, w_in[:, OFF_KR:IN_PAD])
    kr = kr * _seg_rsqrt(kr, c["ered_r"][...], c["eexp_r"][...], 1) * c["gkr"][...]
    kr = _rope(kr, rc, rsu, rsd)
    kk = _dot(ckv_bf, c["wk"][...])
    kk = kk * _seg_rsqrt(kk, c["ered_k"][...], c["eexp_k"][...], MLA_HEADS) * c["gk"][...]
    k_mla = kk + _tile_lanes(kr, MLA_HEADS)
    return dict(conv_in=conv_in, q_g=q_g, k_g=k_g, v_g=v_g, gb=gb, z=z, q_mla=q_mla, k_mla=k_mla,
                ckv=ckv, ckv_bf=ckv_bf, kr=kr)


def _prompt_mixer_kernel(tiles_per_seq, tm, x_ref, rc_ref, rsu_ref, rsd_ref, *refs):
    nc = len(MIXER_CONSTS)
    c = dict(zip(MIXER_CONSTS, refs[:nc]))
    wvt_ref = refs[nc]
    (qg_ref, kg_ref, vg_ref, gb_ref, z_ref, qm_ref, km_ref, vt_ref, ckv_ref, kro_ref, cs_ref, ext_ref) = refs[nc + 1:]
    i = pl.program_id(0)

    @pl.when(i % tiles_per_seq == 0)
    def _():
        ext_ref[5:8, :] = jnp.zeros((3, CONV_DIM), F32)

    def conv_fn(conv_in):
        w = c["w_conv"]
        ext_ref[8:8 + tm, :] = conv_in
        return (conv_in * w[3:4, :] + ext_ref[7:7 + tm, :] * w[2:3, :] + ext_ref[6:6 + tm, :] * w[1:2, :]
                + ext_ref[5:5 + tm, :] * w[0:1, :])

    r = _mixer_rows(x_ref[...], c, conv_fn, rc_ref[...], rsu_ref[...], rsd_ref[...], ATTN_SCALE * LOG2E)
    last3 = ext_ref[tm + 5:tm + 8, :]
    ext_ref[5:8, :] = last3
    cs_ref[0] = last3
    qg_ref[...] = r["q_g"]
    kg_ref[...] = r["k_g"]
    vg_ref[...] = r["v_g"]
    gb_ref[...] = r["gb"]
    z_ref[...] = r["z"]
    qm_ref[...] = r["q_mla"].astype(BF16)
    km_ref[...] = r["k_mla"].astype(BF16)
    vt_ref[...] = _dot_nt(wvt_ref[...], r["ckv_bf"]).astype(BF16)
    ckv_ref[...] = r["ckv"]
    kro_ref[...] = r["kr"][:, ROPE_LO:ROPE_LO + MLA_ROPE]


def _sample_mixer_kernel(x_ref, rc_ref, rsu_ref, rsd_ref, hist_ref, *refs):
    nc = len(MIXER_CONSTS)
    c = dict(zip(MIXER_CONSTS, refs[:nc]))
    (qg_ref, kg_ref, vg_ref, gb_ref, z_ref, qm_ref, km_ref, ckv_ref, kro_ref, cin_ref) = refs[nc:]

    def conv_fn(conv_in):
        w = c["w_conv"]
        return conv_in * w[3:4, :] + hist_ref[2] * w[2:3, :] + hist_ref[1] * w[1:2, :] + hist_ref[0] * w[0:1, :]

    r = _mixer_rows(x_ref[...], c, conv_fn, rc_ref[...], rsu_ref[...], rsd_ref[...], ATTN_SCALE)
    qg_ref[...] = r["q_g"]
    kg_ref[...] = r["k_g"]
    vg_ref[...] = r["v_g"]
    gb_ref[...] = r["gb"]
    z_ref[...] = r["z"]
    qm_ref[...] = r["q_mla"]
    km_ref[...] = r["k_mla"]
    ckv_ref[...] = r["ckv"]
    kro_ref[...] = r["kr"][:, ROPE_LO:ROPE_LO + MLA_ROPE]
    cin_ref[...] = r["conv_in"]


GROUP_HEADS = 4
GROUP_W = GROUP_HEADS * GDN_DK


def _cumsum_rows(x, period):
    row = lax.broadcasted_iota(jnp.int32, x.shape, 0) % period
    s = 1
    while s < period:
        x = x + jnp.where(row >= s, pltpu.roll(x, s, axis=0), 0.0)
        s *= 2
    return x


GDN_ITER_CHUNKS = 2


def _bd(y, bd_mask):
    return jnp.where(bd_mask, jnp.concatenate([y] * GROUP_HEADS, axis=0), 0.0)


def _gdn_solve(chains, masks):
    bd_mask, tril_cat, strict_cat, eye_cat = masks
    c = GDN_CHUNK
    n = range(len(chains))
    bd = lambda y: _bd(y, bd_mask)
    kb = [ch["k"] * ch["b_r"] for ch in chains]
    aq = [_dot_nt(jnp.concatenate([kb[i], chains[i]["q"]], axis=0), bd(chains[i]["k"])) for i in n]
    decay = []
    for ch in chains:
        g_col = jnp.sum(jnp.where(eye_cat, ch["g_r"], 0.0), axis=0, keepdims=True)
        decay.append(jnp.exp(jnp.where(tril_cat, ch["g_r"] - g_col, -jnp.inf)))
    qk = [aq[i][c:] * decay[i] for i in n]
    p = [-jnp.where(strict_cat, aq[i][:c] * decay[i], 0.0) for i in n]
    s_inv = [jnp.where(eye_cat, 1.0, 0.0) + p[i] for i in n]
    p = [_dot(p[i], bd(p[i])) for i in n]
    for lvl in range(5):
        s_next = [s_inv[i] + _dot(p[i], bd(s_inv[i])) for i in n]
        if lvl < 4:
            p = [_dot(p[i], bd(p[i])) for i in n]
        s_inv = s_next
    eg = [jnp.exp(ch["g_r"]) for ch in chains]
    u = [_dot(s_inv[i], bd(chains[i]["v"] * chains[i]["b_r"])) for i in n]
    w = [_dot(s_inv[i], bd(kb[i] * eg[i])) for i in n]
    sols = []
    for i in n:
        g_r = chains[i]["g_r"]
        g_last = g_r[c - 1:c, :]
        sols.append(dict(u=u[i], wq=jnp.concatenate([w[i], chains[i]["q"] * eg[i]], axis=0), qk=qk[i],
                         kd=chains[i]["k"] * jnp.exp(g_last - g_r), e_last=jnp.exp(g_last)))
    return sols


def _gdn_apply(sols, states, bd_mask):
    c = GDN_CHUNK
    n = range(len(sols))
    ws = [_dot(sols[i]["wq"], states[i]) for i in n]
    v_new = [sols[i]["u"] - ws[i][:c] for i in n]
    o = [ws[i][c:] + _dot(sols[i]["qk"], _bd(v_new[i], bd_mask)) for i in n]
    upd = [_dot_tn(sols[i]["kd"], v_new[i]) for i in n]
    new_states = [states[i] * sols[i]["e_last"] + jnp.where(bd_mask, upd[i], 0.0) for i in n]
    return jnp.concatenate(o, axis=1), new_states


def _gdn_kernel(nblk, tb, q_ref, k_ref, v_ref, gb_ref, z_ref, e6_ref, esum_ref, gout_ref, o_ref, sfin_ref, s_ref):
    j = pl.program_id(1)

    @pl.when(j == 0)
    def _():
        s_ref[...] = jnp.zeros(s_ref.shape, F32)

    c = GDN_CHUNK
    ngrp = GDN_HEADS // GROUP_HEADS
    rows_it = GDN_ITER_CHUNKS * c
    r_bd = lax.broadcasted_iota(jnp.int32, (GROUP_W, GROUP_W), 0) // c
    c_bd = lax.broadcasted_iota(jnp.int32, (GROUP_W, GROUP_W), 1) // c
    bd_mask = r_bd == c_bd
    ri = lax.broadcasted_iota(jnp.int32, (c, GROUP_W), 0)
    cj = lax.broadcasted_iota(jnp.int32, (c, GROUP_W), 1) % c
    masks = (bd_mask, ri >= cj, ri > cj, ri == cj)
    lane = lax.broadcasted_iota(jnp.int32, (rows_it, LANES), 1)
    copy_id = (lane % BETA_LANE0) // GDN_HEADS

    def body(it, carry):
        rows = pl.ds(pl.multiple_of(it * rows_it, rows_it), rows_it)
        gb = gb_ref[rows, :]
        sc = jnp.where(lane < BETA_LANE0, _cumsum_rows(gb, c), gb)
        p0, p1, p2 = _split3(sc)
        piece = jnp.where(copy_id == 0, p0, jnp.where(copy_id == 1, p1, p2))
        ex = _dot(piece, e6_ref[...])
        q = q_ref[rows, :]
        k = k_ref[rows, :]
        v = v_ref[rows, :]
        chains = []
        for t in range(GDN_ITER_CHUNKS):
            rs = slice(t * c, (t + 1) * c)
            for grp in range(ngrp):
                sl = slice(grp * GROUP_W, (grp + 1) * GROUP_W)
                sb = slice(GDN_WIDTH + grp * GROUP_W, GDN_WIDTH + (grp + 1) * GROUP_W)
                chains.append(dict(q=q[rs, sl], k=k[rs, sl], v=v[rs, sl], g_r=ex[rs, sl], b_r=ex[rs, sb]))
        sols = _gdn_solve(chains, masks)
        states = [s_ref[grp] for grp in range(ngrp)]
        outs = []
        for t in range(GDN_ITER_CHUNKS):
            o_t, states = _gdn_apply(sols[t * ngrp:(t + 1) * ngrp], states, bd_mask)
            outs.append(o_t)
        for grp in range(ngrp):
            s_ref[grp] = states[grp]
        o = jnp.concatenate(outs, axis=0)
        hi, lo = _split2(o * o)
        ms = _dot(hi, esum_ref[...]) + _dot(lo, esum_ref[...])
        o_ref[rows, :] = (o * lax.rsqrt(ms + EPS) * gout_ref[...] * _silu(z_ref[rows, :])).astype(o_ref.dtype)
        return carry

    lax.fori_loop(0, tb // rows_it, body, 0)

    @pl.when(j == nblk - 1)
    def _():
        for grp in range(ngrp):
            for h in range(GROUP_HEADS):
                sfin_ref[0, grp * GROUP_HEADS + h] = s_ref[grp, h * c:(h + 1) * c, h * c:(h + 1) * c]


ATT_T = 256
ATT_HEADS = 4
ATT_SUM_ROWS = 16


def _attn_kernel(q_ref, k_ref, vt_ref, o_ref):
    qi = pl.program_id(2)
    t = ATT_T
    qs = [q_ref[:, hh * SLAB:(hh + 1) * SLAB] for hh in range(ATT_HEADS)]

    def tile(j, carry, masked):
        k0 = pl.multiple_of(j * t, t)
        scores = [_dot_nt(k_ref[pl.ds(k0, t), hh * SLAB:(hh + 1) * SLAB], qs[hh]) for hh in range(ATT_HEADS)]
        stats = []
        for hh in range(ATT_HEADS):
            m, _ = carry[hh]
            s = scores[hh]
            if masked:
                kp = lax.broadcasted_iota(jnp.int32, (t, t), 0)
                qp = lax.broadcasted_iota(jnp.int32, (t, t), 1)
                s = jnp.where(qp >= kp, s, -jnp.inf)
            m_new = jnp.maximum(m, jnp.max(s, axis=0, keepdims=True))
            stats.append((m_new, jnp.exp2(m - m_new), jnp.exp2(s - m_new).astype(BF16)))
        out = []
        for hh in range(ATT_HEADS):
            m_new, alpha, p = stats[hh]
            vt = vt_ref[hh * MLA_VDIM:(hh + 1) * MLA_VDIM, pl.ds(k0, t)]
            out.append((m_new, carry[hh][1] * alpha + _dot(jnp.concatenate([vt, ones], axis=0), p)))
        return tuple(out)

    ones = jnp.ones((ATT_SUM_ROWS, t), BF16)
    init = tuple((jnp.full((1, t), -jnp.inf, F32), jnp.zeros((MLA_VDIM + ATT_SUM_ROWS, t), F32)) for _ in range(ATT_HEADS))
    carry = lax.fori_loop(0, qi, functools.partial(tile, masked=False), init)
    carry = tile(qi, carry, True)
    outs = [acc[:MLA_VDIM] / acc[MLA_VDIM:MLA_VDIM + 1] for (_, acc) in carry]
    o_ref[...] = jnp.concatenate(outs, axis=0).T.astype(o_ref.dtype)


def _tail_kernel(x_ref, og_ref, om_ref, p_ref, wo_ref, gffn_ref, wg_ref, wu_ref, wd_ref, gple_ref, wpg_ref, wpp_ref, y_ref):
    h = x_ref[...] + _dot(og_ref[...], wo_ref[:GDN_WIDTH, :]) + _dot(om_ref[...], wo_ref[GDN_WIDTH:, :])
    u = _rmsnorm(h, gffn_ref[...]).astype(BF16)
    act = (_silu(_dot(u, wg_ref[...])) * _dot(u, wu_ref[...])).astype(BF16)
    acc = h + _dot(act, wd_ref[...])
    gate = _sigmoid(_dot(_rmsnorm(acc, gple_ref[...]).astype(BF16), wpg_ref[...]))
    y_ref[...] = acc + _dot(p_ref[...].astype(BF16), wpp_ref[...]) * gate


PAGES_PER_STEP = 16
PAGES_PER_SUB = 8


def _sample_prep_kernel(qm_ref, km_ref, gk_ref, wk_ref, qabs_ref, qr_ref, sself_ref):
    qm = qm_ref[...]
    prod = qm * km_ref[...]
    qg = (qm * gk_ref[...]).astype(BF16)
    for h in range(MLA_HEADS):
        sl = slice(h * SLAB, (h + 1) * SLAB)
        qabs_ref[h] = _dot_nt(qg[:, sl], wk_ref[:, sl])
        qr_ref[h] = qm[:, h * SLAB + ROPE_LO:h * SLAB + ROPE_LO + MLA_ROPE]
        sself_ref[h] = jnp.sum(prod[:, sl], axis=1, keepdims=True)


def _paged_kernel(nsteps, pt_ref, *refs):
    pp = PAGES_PER_STEP
    ckv_refs = refs[:pp]
    kr_refs = refs[pp:2 * pp]
    qabs_ref, qr_ref, sself_ref, ckvn_ref, wkt_ref, o_ref, m_ref, l_ref, acc_ref = refs[2 * pp:]
    g = pl.program_id(1)

    @pl.when(g == 0)
    def _():
        m_ref[...] = sself_ref[...]
        l_ref[...] = jnp.ones(l_ref.shape, F32)
        acc_ref[...] = jnp.broadcast_to(ckvn_ref[...], acc_ref.shape)

    qa = qabs_ref[...].astype(BF16)
    qr = qr_ref[...].astype(BF16)
    wkt = wkt_ref[...]
    m, l, acc = m_ref[...], l_ref[...], acc_ref[...]
    cs, ss_list = [], []
    for i in range(pp // PAGES_PER_SUB):
        sub = range(i * PAGES_PER_SUB, (i + 1) * PAGES_PER_SUB)
        c = jnp.concatenate([ckv_refs[j][...] for j in sub], axis=0).astype(BF16)
        krt = jnp.concatenate([kr_refs[j][...] for j in sub], axis=1).astype(BF16)
        kv = _dot_nt(wkt, c)
        ss = jnp.sum((kv * kv).reshape(MLA_HEADS, MLA_NOPE, kv.shape[1]), axis=1)
        ss_list.append(_dot_nt(qa, c) * lax.rsqrt(ss * (1.0 / MLA_NOPE) + EPS) + _dot(qr, krt))
        cs.append(c)
    s = jnp.concatenate(ss_list, axis=1)
    m_new = jnp.maximum(m, jnp.max(s, axis=1, keepdims=True))
    alpha = jnp.exp(m - m_new)
    p = jnp.exp(s - m_new)
    l = l * alpha + jnp.sum(p, axis=1, keepdims=True)
    acc = acc * alpha + _dot(p.astype(BF16), jnp.concatenate(cs, axis=0))
    m_ref[...] = m_new
    l_ref[...] = l
    acc_ref[...] = acc

    @pl.when(g == nsteps - 1)
    def _():
        o_ref[...] = acc / l


def _sample_gdn_kernel(q_ref, k_ref, v_ref, gb_ref, s_ref, snew_ref, o_ref):
    hk = GDN_HEADS * GDN_DK
    hrow = lax.broadcasted_iota(jnp.int32, (GDN_HEADS, hk), 0)
    hlane = lax.broadcasted_iota(jnp.int32, (GDN_HEADS, hk), 1) // GDN_DK
    hm = hrow == hlane
    kmask = jnp.where(hm, jnp.broadcast_to(k_ref[...], (GDN_HEADS, hk)), 0.0)
    qmask = jnp.where(hm, jnp.broadcast_to(q_ref[...], (GDN_HEADS, hk)), 0.0)
    gbb = jnp.broadcast_to(gb_ref[...], (GDN_HEADS, LANES))
    r8 = lax.broadcasted_iota(jnp.int32, (GDN_HEADS, LANES), 0)
    l8 = lax.broadcasted_iota(jnp.int32, (GDN_HEADS, LANES), 1)
    g_col = jnp.sum(jnp.where(l8 == r8, gbb, 0.0), axis=1, keepdims=True)
    b_col = jnp.sum(jnp.where(l8 == r8 + BETA_LANE0, gbb, 0.0), axis=1, keepdims=True)
    ones = jnp.where(hm, 1.0, 0.0).astype(BF16)

    def expand_rows(col):
        e0, e1, e2 = _split3(jnp.broadcast_to(col, (GDN_HEADS, GDN_DV)))
        return _dot_tn(jnp.concatenate([ones, ones, ones], axis=0), jnp.concatenate([e0, e1, e2], axis=0))

    def head_dot(xmask, mat):
        xh, xl = _split2(xmask)
        mh, ml = _split2(mat)
        return _dot(jnp.concatenate([xh, xh, xl], axis=1), jnp.concatenate([mh, ml, mh], axis=0))

    state = s_ref[...] * expand_rows(jnp.exp(g_col))
    delta = (v_ref[...] - head_dot(kmask, state)) * b_col
    kh, kl = _split2(kmask)
    dh, dl = _split2(delta)
    state = state + _dot_tn(jnp.concatenate([kh, kh, kl], axis=0), jnp.concatenate([dh, dl, dh], axis=0))
    snew_ref[...] = state
    o_ref[...] = head_dot(qmask, state)


def _sample_mix_kernel(o_ref, z_ref, olat_ref, wvbd_ref, esum_ref, gout_ref, og_ref, om_ref):
    o = o_ref[...]
    hi, lo = _split2(o * o)
    ms = _dot(hi, esum_ref[...]) + _dot(lo, esum_ref[...])
    og_ref[...] = (o * lax.rsqrt(ms + EPS) * gout_ref[...] * _silu(z_ref[...])).astype(og_ref.dtype)
    om_ref[...] = _dot(olat_ref[...].astype(BF16), wvbd_ref[...]).astype(om_ref.dtype)


def _seg_tables(width, segs, mean):
    n = len(segs)
    ered = np.zeros((width, LANES), np.float32)
    eexp = np.zeros((LANES, width), np.float32)
    for copy in range(3):
        for s, (a, b) in enumerate(segs):
            ered[a:b, copy * n + s] = 1.0 / (b - a) if mean else 1.0
            eexp[copy * n + s, a:b] = 1.0
    return jnp.asarray(ered, BF16), jnp.asarray(eexp, BF16)


def _rope_slabs(pos):
    inv_freq = ROPE_THETA ** (-jnp.arange(HALF, dtype=F32) / HALF)
    ang = pos.astype(F32)[:, None] * inv_freq[None, :]
    cos, sin = jnp.cos(ang), jnp.sin(ang)
    n = pos.shape[0]
    zeros = lambda w: jnp.zeros((n, w), F32)
    rc = jnp.concatenate([jnp.ones((n, MLA_NOPE), F32), cos, cos, zeros(SLAB - MLA_QK)], axis=1)
    rsu = jnp.concatenate([zeros(MLA_NOPE), -sin, zeros(SLAB - ROPE_HI)], axis=1)
    rsd = jnp.concatenate([zeros(ROPE_HI), sin, zeros(SLAB - MLA_QK)], axis=1)
    return rc, rsu, rsd


def _prepare_weights(g_attn, w_in, w_conv, gdn_a_log, gdn_dt_bias, g_gdn_out, g_q_a, w_q_b, g_q_nope, g_q_rope,
                     g_kv_a, g_k_rope, w_kv_b, g_k_nope):
    row = lambda v: v.reshape(1, -1).astype(F32)
    a = w_in[:, CONV_DIM:CONV_DIM + GDN_HEADS]
    b = w_in[:, CONV_DIM + GDN_HEADS:CONV_DIM + 2 * GDN_HEADS]
    o_z = CONV_DIM + 2 * GDN_HEADS
    o_qa = o_z + GDN_WIDTH
    o_kv = o_qa + Q_LORA
    zc = lambda w: jnp.zeros((D_MODEL, w), F32)
    gb_slab = jnp.concatenate([a, a, a, zc(BETA_LANE0 - 3 * GDN_HEADS), b, b, b, zc(LANES - BETA_LANE0 - 3 * GDN_HEADS)], axis=1)
    kr_slab = jnp.concatenate([zc(ROPE_LO), w_in[:, o_kv + KV_LORA:], zc(SLAB - MLA_QK)], axis=1)
    w_in_p = jnp.concatenate([w_in[:, :CONV_DIM], gb_slab, w_in[:, o_z:o_qa], w_in[:, o_qa:o_kv],
                              w_in[:, o_kv:o_kv + KV_LORA], kr_slab], axis=1).astype(BF16)

    def scalar_slab(v):
        z8 = jnp.zeros((BETA_LANE0 - 3 * GDN_HEADS,), F32)
        return jnp.concatenate([v, v, v, z8, jnp.zeros((LANES - BETA_LANE0,), F32)]).reshape(1, LANES)

    wq = w_q_b.reshape(Q_LORA, MLA_HEADS, MLA_QK)
    w_qb = jnp.concatenate([wq, jnp.zeros((Q_LORA, MLA_HEADS, SLAB - MLA_QK), F32)], axis=2).reshape(Q_LORA, QK_SLAB_W).astype(BF16)
    gq = jnp.tile(jnp.concatenate([g_q_nope, g_q_rope, jnp.zeros((SLAB - MLA_QK,), F32)]), MLA_HEADS).reshape(1, QK_SLAB_W)
    wkv = w_kv_b.reshape(KV_LORA, MLA_HEADS, MLA_NOPE + MLA_VDIM)
    wk_part, wv_part = wkv[:, :, :MLA_NOPE], wkv[:, :, MLA_NOPE:]
    wk = jnp.concatenate([wk_part, jnp.zeros((KV_LORA, MLA_HEADS, SLAB - MLA_NOPE), F32)], axis=2).reshape(KV_LORA, QK_SLAB_W).astype(BF16)
    gk = jnp.tile(jnp.concatenate([g_k_nope, jnp.zeros((SLAB - MLA_NOPE,), F32)]), MLA_HEADS).reshape(1, QK_SLAB_W)
    gkr = jnp.concatenate([jnp.zeros((ROPE_LO,), F32), g_k_rope, jnp.zeros((SLAB - MLA_QK,), F32)]).reshape(1, SLAB)
    wvt = wv_part.reshape(KV_LORA, MLA_HEADS * MLA_VDIM).T.astype(BF16)
    wkt = wk_part.reshape(KV_LORA, MLA_HEADS * MLA_NOPE).T.astype(BF16)
    wv_bd = jnp.zeros((MLA_HEADS, KV_LORA, MLA_HEADS, MLA_VDIM), F32)
    wv_bd = wv_bd.at[jnp.arange(MLA_HEADS), :, jnp.arange(MLA_HEADS), :].set(jnp.moveaxis(wv_part, 1, 0))
    wv_bd = wv_bd.reshape(MLA_HEADS * KV_LORA, MLA_HEADS * MLA_VDIM).astype(BF16)

    ered_g, eexp_g = _seg_tables(2 * GDN_QK_DIM, [(h * GDN_DK, (h + 1) * GDN_DK) for h in range(2 * GDN_HEADS)], False)
    q_segs = ([(h * SLAB, h * SLAB + MLA_NOPE) for h in range(MLA_HEADS)]
              + [(h * SLAB + ROPE_LO, h * SLAB + MLA_QK) for h in range(MLA_HEADS)])
    ered_q, eexp_q = _seg_tables(QK_SLAB_W, q_segs, True)
    ered_k, eexp_k = _seg_tables(QK_SLAB_W, q_segs[:MLA_HEADS], True)
    ered_r, eexp_r = _seg_tables(SLAB, [(ROPE_LO, MLA_QK)], True)
    e6 = np.zeros((LANES, 2 * GDN_WIDTH), np.float32)
    for copy in range(3):
        for h in range(GDN_HEADS):
            e6[copy * GDN_HEADS + h, h * GDN_DV:(h + 1) * GDN_DV] = 1.0
            e6[BETA_LANE0 + copy * GDN_HEADS + h, GDN_WIDTH + h * GDN_DV:GDN_WIDTH + (h + 1) * GDN_DV] = 1.0
    esum = np.kron(np.eye(GDN_HEADS, dtype=np.float32), np.full((GDN_DV, GDN_DV), 1.0 / GDN_DV, np.float32))

    mixer = dict(g_attn=row(g_attn), w_in=w_in_p, w_conv=w_conv.astype(F32), alog=scalar_slab(gdn_a_log),
                 dtb=scalar_slab(gdn_dt_bias), ered_g=ered_g, eexp_g=eexp_g, g_q_a=row(g_q_a), w_qb=w_qb, gq=gq,
                 ered_q=ered_q, eexp_q=eexp_q, g_kv_a=row(g_kv_a), gkr=gkr, ered_r=ered_r, eexp_r=eexp_r, wk=wk, gk=gk,
                 ered_k=ered_k, eexp_k=eexp_k)
    extra = dict(wvt=wvt, wkt=wkt, wv_bd=wv_bd, e6=jnp.asarray(e6, BF16), esum=jnp.asarray(esum, BF16),
                 gout=jnp.tile(g_gdn_out, GDN_HEADS).reshape(1, GDN_WIDTH).astype(F32))
    return mixer, extra


def _params(sem=None):
    return pltpu.CompilerParams(dimension_semantics=sem, vmem_limit_bytes=VMEM_LIMIT)


def _prompt_mixer(x2, slabs, mixer, wvt, bsz, seq, tm):
    n = x2.shape[0]
    tps = seq // tm
    consts = [mixer[k] for k in MIXER_CONSTS] + [wvt]
    row_spec = lambda w: pl.BlockSpec((tm, w), lambda i: (i, 0))
    rope_spec = pl.BlockSpec((tm, SLAB), lambda i: (i % tps, 0))
    out_shapes = [
        jax.ShapeDtypeStruct((n, GDN_QK_DIM), F32), jax.ShapeDtypeStruct((n, GDN_QK_DIM), F32),
        jax.ShapeDtypeStruct((n, GDN_WIDTH), F32), jax.ShapeDtypeStruct((n, LANES), F32),
        jax.ShapeDtypeStruct((n, GDN_WIDTH), F32), jax.ShapeDtypeStruct((n, QK_SLAB_W), BF16),
        jax.ShapeDtypeStruct((n, QK_SLAB_W), BF16), jax.ShapeDtypeStruct((MLA_HEADS * MLA_VDIM, n), BF16),
        jax.ShapeDtypeStruct((n, KV_LORA), F32), jax.ShapeDtypeStruct((n, MLA_ROPE), F32),
        jax.ShapeDtypeStruct((bsz, CONV_WIDTH - 1, CONV_DIM), F32)]
    out_specs = [row_spec(GDN_QK_DIM), row_spec(GDN_QK_DIM), row_spec(GDN_WIDTH), row_spec(LANES), row_spec(GDN_WIDTH),
                 row_spec(QK_SLAB_W), row_spec(QK_SLAB_W), pl.BlockSpec((MLA_HEADS * MLA_VDIM, tm), lambda i: (0, i)),
                 row_spec(KV_LORA), row_spec(MLA_ROPE),
                 pl.BlockSpec((1, CONV_WIDTH - 1, CONV_DIM), lambda i: (i // tps, 0, 0))]
    return pl.pallas_call(
        functools.partial(_prompt_mixer_kernel, tps, tm),
        grid=(n // tm,),
        in_specs=[row_spec(D_MODEL), rope_spec, rope_spec, rope_spec] + [_const_spec(a, 1) for a in consts],
        out_specs=out_specs, out_shape=out_shapes,
        scratch_shapes=[pltpu.VMEM((tm + 8, CONV_DIM), F32)],
        compiler_params=_params(("arbitrary",)), name="prompt_mixer",
    )(x2, *slabs, *consts)


def _sample_mixer(x2, slabs, hist, mixer):
    n = x2.shape[0]
    consts = [mixer[k] for k in MIXER_CONSTS]
    sd = lambda w: jax.ShapeDtypeStruct((n, w), F32)
    return pl.pallas_call(
        _sample_mixer_kernel,
        out_shape=[sd(GDN_QK_DIM), sd(GDN_QK_DIM), sd(GDN_WIDTH), sd(LANES), sd(GDN_WIDTH), sd(QK_SLAB_W), sd(QK_SLAB_W),
                   sd(KV_LORA), sd(MLA_ROPE), sd(CONV_DIM)],
        compiler_params=_params(), name="sample_mixer",
    )(x2, *slabs, hist, *consts)


def _gdn_prompt(qg, kg, vg, gb, z, extra, bsz, seq, tb):
    n = qg.shape[0]
    nblk = seq // tb
    row_spec = lambda w: pl.BlockSpec((tb, w), lambda b, j: (b * nblk + j, 0))
    consts = [extra["e6"], extra["esum"], extra["gout"]]
    return pl.pallas_call(
        functools.partial(_gdn_kernel, nblk, tb),
        grid=(bsz, nblk),
        in_specs=[row_spec(GDN_QK_DIM), row_spec(GDN_QK_DIM), row_spec(GDN_WIDTH), row_spec(LANES), row_spec(GDN_WIDTH)]
        + [_const_spec(a, 2) for a in consts],
        out_specs=[row_spec(GDN_WIDTH), pl.BlockSpec((1, GDN_HEADS, GDN_DK, GDN_DV), lambda b, j: (b, 0, 0, 0))],
        out_shape=[jax.ShapeDtypeStruct((n, GDN_WIDTH), BF16), jax.ShapeDtypeStruct((bsz, GDN_HEADS, GDN_DK, GDN_DV), F32)],
        scratch_shapes=[pltpu.VMEM((GDN_HEADS // GROUP_HEADS, GROUP_W, GROUP_W), F32)],
        compiler_params=_params(("arbitrary", "arbitrary")), name="gdn_chunked",
    )(qg, kg, vg, gb, z, *consts)


def _attn_prompt(qm, km, vt, bsz, seq):
    n = qm.shape[0]
    t = ATT_T
    nq = seq // t
    return pl.pallas_call(
        _attn_kernel,
        grid=(bsz, MLA_HEADS // ATT_HEADS, nq),
        in_specs=[pl.BlockSpec((t, ATT_HEADS * SLAB), lambda b, hp, qi: (b * nq + qi, hp)),
                  pl.BlockSpec((seq, ATT_HEADS * SLAB), lambda b, hp, qi: (b, hp)),
                  pl.BlockSpec((ATT_HEADS * MLA_VDIM, seq), lambda b, hp, qi: (hp, b))],
        out_specs=pl.BlockSpec((t, ATT_HEADS * MLA_VDIM), lambda b, hp, qi: (b * nq + qi, hp)),
        out_shape=jax.ShapeDtypeStruct((n, MLA_HEADS * MLA_VDIM), BF16),
        compiler_params=_params(("arbitrary", "arbitrary", "arbitrary")), name="mla_prompt_attention",
    )(qm, km, vt)


def _tail(x2, og, om, p2, tailw, tm):
    n = x2.shape[0]
    row_spec = lambda w: pl.BlockSpec((tm, w), lambda i: (i, 0))
    return pl.pallas_call(
        _tail_kernel,
        grid=(n // tm,),
        in_specs=[row_spec(D_MODEL), row_spec(GDN_WIDTH), row_spec(GDN_WIDTH), row_spec(PLE_DIM)]
        + [_const_spec(a, 1) for a in tailw],
        out_specs=row_spec(D_MODEL), out_shape=jax.ShapeDtypeStruct((n, D_MODEL), F32),
        compiler_params=_params(("arbitrary",)), name="layer_tail",
    )(x2, og, om, p2, *tailw)


def _sample_prep(qm, km, gk, wk):
    n = qm.shape[0]
    return pl.pallas_call(
        _sample_prep_kernel,
        out_shape=[jax.ShapeDtypeStruct((MLA_HEADS, n, KV_LORA), F32), jax.ShapeDtypeStruct((MLA_HEADS, n, MLA_ROPE), F32),
                   jax.ShapeDtypeStruct((MLA_HEADS, n, 1), F32)],
        compiler_params=_params(), name="sample_prep",
    )(qm, km, gk, wk)


def _paged_attention(page_table, pool_ckv, pool_kr, qabs, qr, sself, ckv_new, wkt):
    bs, n_pages = page_table.shape
    page = pool_ckv.shape[1]
    pp = PAGES_PER_STEP
    nsteps = n_pages // pp
    ckv_specs = [pl.BlockSpec((None, page, KV_LORA), lambda b, g, pt, j=j: (pt[b, g * pp + j], 0, 0)) for j in range(pp)]
    kr_specs = [pl.BlockSpec((None, MLA_ROPE, page), lambda b, g, pt, j=j: (pt[b, g * pp + j], 0, 0)) for j in range(pp)]
    per_b = lambda w: pl.BlockSpec((None, MLA_HEADS, w), lambda b, g, pt: (b, 0, 0))
    grid_spec = pltpu.PrefetchScalarGridSpec(
        num_scalar_prefetch=1, grid=(bs, nsteps),
        in_specs=ckv_specs + kr_specs + [per_b(KV_LORA), per_b(MLA_ROPE), per_b(1),
                                         pl.BlockSpec((None, 1, KV_LORA), lambda b, g, pt: (b, 0, 0)),
                                         pl.BlockSpec(wkt.shape, lambda b, g, pt: (0, 0))],
        out_specs=pl.BlockSpec((None, MLA_HEADS, KV_LORA), lambda b, g, pt: (b, 0, 0)),
        scratch_shapes=[pltpu.VMEM((MLA_HEADS, 1), F32), pltpu.VMEM((MLA_HEADS, 1), F32), pltpu.VMEM((MLA_HEADS, KV_LORA), F32)])
    return pl.pallas_call(
        functools.partial(_paged_kernel, nsteps), grid_spec=grid_spec,
        out_shape=jax.ShapeDtypeStruct((bs, MLA_HEADS, KV_LORA), F32),
        compiler_params=_params(("arbitrary", "arbitrary")), name="mla_paged_decode",
    )(page_table, *([pool_ckv] * pp), *([pool_kr] * pp), qabs, qr, sself, ckv_new, wkt)


def _sample_gdn(q3, k3, v3, gb3, state):
    bs = q3.shape[0]
    hk = GDN_HEADS * GDN_DK
    b3 = lambda r, w: pl.BlockSpec((None, r, w), lambda b: (b, 0, 0))
    return pl.pallas_call(
        _sample_gdn_kernel, grid=(bs,),
        in_specs=[b3(1, hk), b3(1, hk), b3(GDN_HEADS, GDN_DV), b3(1, LANES), b3(hk, GDN_DV)],
        out_specs=[b3(hk, GDN_DV), b3(GDN_HEADS, GDN_DV)],
        out_shape=[jax.ShapeDtypeStruct((bs, hk, GDN_DV), F32), jax.ShapeDtypeStruct((bs, GDN_HEADS, GDN_DV), F32)],
        compiler_params=_params(("arbitrary",)), name="gdn_recurrent_step",
    )(q3, k3, v3, gb3, state)


def _sample_mix(o2, z, olat2, extra):
    n = o2.shape[0]
    return pl.pallas_call(
        _sample_mix_kernel,
        out_shape=[jax.ShapeDtypeStruct((n, GDN_WIDTH), BF16), jax.ShapeDtypeStruct((n, MLA_HEADS * MLA_VDIM), BF16)],
        compiler_params=_params(), name="sample_mix",
    )(o2, z, olat2, extra["wv_bd"], extra["esum"], extra["gout"])


def _pick(n, prefs):
    for t in prefs:
        if n % t == 0:
            return t
    return n


def kernel(x_prompt, x_sample, cache_ckv, cache_krope, state_gdn, state_conv, page_table, p_prompt, p_sample, g_attn, w_in, w_conv, gdn_a_log, gdn_dt_bias, g_gdn_out, g_q_a, w_q_b, g_q_nope, g_q_rope, g_kv_a, g_k_rope, w_kv_b, g_k_nope, w_o, g_ffn, w_ffn_gate, w_ffn_up, w_ffn_down, g_ple, w_ple_gate, w_ple_proj):
    depth = g_attn.shape[0]
    assert depth == 1 and x_sample.shape[1] == 1, "single layer, single new token per sample row"
    bp, seq, _ = x_prompt.shape
    bs = x_sample.shape[0]
    past = page_table.shape[1] * cache_ckv.shape[2]
    li = 0
    mixer, extra = _prepare_weights(g_attn[li], w_in[li], w_conv[li], gdn_a_log[li], gdn_dt_bias[li], g_gdn_out[li],
                                    g_q_a[li], w_q_b[li], g_q_nope[li], g_q_rope[li], g_kv_a[li], g_k_rope[li],
                                    w_kv_b[li], g_k_nope[li])
    row = lambda v: v.reshape(1, -1).astype(F32)
    tailw = [w_o[li].astype(BF16), row(g_ffn[li]), w_ffn_gate[li].astype(BF16), w_ffn_up[li].astype(BF16),
             w_ffn_down[li].astype(BF16), row(g_ple[li]), w_ple_gate[li].astype(BF16), w_ple_proj[li].astype(BF16)]

    n = bp * seq
    xp2 = x_prompt.reshape(n, D_MODEL)
    tm = _pick(seq, (256, 128, 64, 32, 16, 8))
    (qg, kg, vg, gb, z, qm, km, vt, ckv_p, kr_p, conv_p) = _prompt_mixer(
        xp2, _rope_slabs(jnp.arange(seq)), mixer, extra["wvt"], bp, seq, tm)
    tb = _pick(seq, (512, 256, 128, 64))
    og, gdn_p = _gdn_prompt(qg, kg, vg, gb, z, extra, bp, seq, tb)
    om = _attn_prompt(qm, km, vt, bp, seq)
    y_prompt = _tail(xp2, og, om, p_prompt[li].reshape(n, PLE_DIM), tailw, tm).reshape(bp, seq, D_MODEL)

    xs2 = x_sample.reshape(bs, D_MODEL)
    hist = jnp.moveaxis(state_conv[li], 1, 0)
    (qg_s, kg_s, vg_s, gb_s, z_s, qm_s, km_s, ckv_s, kr_s, cin_s) = _sample_mixer(
        xs2, _rope_slabs(past + jnp.arange(1)), hist, mixer)
    qabs, qr, sself = _sample_prep(qm_s, km_s, mixer["gk"], mixer["wk"])
    per_row = lambda t: jnp.swapaxes(t, 0, 1)
    pool_krt = jnp.swapaxes(cache_krope[li], 1, 2)
    olat = _paged_attention(page_table, cache_ckv[li], pool_krt, per_row(qabs), per_row(qr), per_row(sself),
                            ckv_s.reshape(bs, 1, KV_LORA), extra["wkt"])
    hk = GDN_HEADS * GDN_DK
    s_new, o_s = _sample_gdn(qg_s.reshape(bs, 1, hk), kg_s.reshape(bs, 1, hk), vg_s.reshape(bs, GDN_HEADS, GDN_DV),
                             gb_s.reshape(bs, 1, LANES), state_gdn[li].reshape(bs, hk, GDN_DV))
    og_s, om_s = _sample_mix(o_s.reshape(bs, GDN_WIDTH), z_s, olat.reshape(bs, MLA_HEADS * KV_LORA), extra)
    y_sample = _tail(xs2, og_s, om_s, p_sample[li].reshape(bs, PLE_DIM), tailw, bs).reshape(bs, 1, D_MODEL)
    conv_s = jnp.concatenate([state_conv[li][:, 1:], cin_s[:, None, :]], axis=1)

    return (y_prompt, y_sample,
            ckv_p.reshape(1, bp, seq, KV_LORA), kr_p.reshape(1, bp, seq, MLA_ROPE),
            gdn_p[None], conv_p[None],
            ckv_s.reshape(1, bs, 1, KV_LORA), kr_s.reshape(1, bs, 1, MLA_ROPE),
            s_new.reshape(1, bs, GDN_HEADS, GDN_DK, GDN_DV), conv_s[None])
```

```python
import functools
import math

import numpy as np
import jax
import jax.numpy as jnp
from jax import lax
from jax.experimental import pallas as pl
from jax.experimental.pallas import tpu as pltpu

F32 = jnp.float32
BF16 = jnp.bfloat16

D_MODEL = 1024
PLE_DIM = 256
GDN_HEADS = 8
GDN_DK = 64
GDN_DV = 64
GDN_WIDTH = GDN_HEADS * GDN_DV
GDN_QK_DIM = GDN_HEADS * GDN_DK
CONV_WIDTH = 4
CONV_DIM = 2 * GDN_QK_DIM + GDN_WIDTH
GDN_CHUNK = 64
MLA_HEADS = 8
MLA_NOPE = 64
MLA_ROPE = 32
MLA_VDIM = 64
MLA_QK = MLA_NOPE + MLA_ROPE
Q_LORA = 384
KV_LORA = 256
ROPE_THETA = 10000.0
ATTN_SCALE = MLA_QK ** -0.5
LOG2E = math.log2(math.e)
D_FF = 2816
EPS = 1e-6

LANES = 128
SLAB = 128
HALF = MLA_ROPE // 2
ROPE_LO = MLA_NOPE
ROPE_HI = MLA_NOPE + HALF
QK_SLAB_W = MLA_HEADS * SLAB

OFF_CONV = 0
OFF_GB = OFF_CONV + CONV_DIM
OFF_Z = OFF_GB + LANES
OFF_QA = OFF_Z + GDN_WIDTH
OFF_CKV = OFF_QA + Q_LORA
OFF_KR = OFF_CKV + KV_LORA
IN_PAD = OFF_KR + LANES
BETA_LANE0 = 32

VMEM_LIMIT = 56 * 1024 * 1024


def _dot(a, b):
    return jnp.dot(a, b, preferred_element_type=F32)


def _dot_nt(a, b):
    return lax.dot_general(a, b, (((1,), (1,)), ((), ())), preferred_element_type=F32)


def _dot_tn(a, b):
    return lax.dot_general(a, b, (((0,), (0,)), ((), ())), preferred_element_type=F32)


def _split2(x):
    hi = x.astype(BF16)
    lo = (x - hi.astype(F32)).astype(BF16)
    return hi, lo


def _split3(x):
    hi = x.astype(BF16)
    r1 = x - hi.astype(F32)
    mid = r1.astype(BF16)
    lo = (r1 - mid.astype(F32)).astype(BF16)
    return hi, mid, lo


def _sigmoid(x):
    return 1.0 / (1.0 + jnp.exp(-x))


def _silu(x):
    return x * _sigmoid(x)


def _softplus(x):
    return jnp.maximum(x, 0.0) + jnp.log1p(jnp.exp(-jnp.abs(x)))


def _rmsnorm(x, g):
    return x * lax.rsqrt(jnp.mean(x * x, axis=-1, keepdims=True) + EPS) * g


def _seg_rsqrt(x, ered, eexp, nseg):
    red = _dot((x * x).astype(BF16), ered)
    r = lax.rsqrt(red + EPS)
    p0, p1, p2 = _split3(r)
    lane = lax.broadcasted_iota(jnp.int32, r.shape, 1)
    piece = jnp.where(lane < nseg, p0, jnp.where(lane < 2 * nseg, p1, p2))
    return _dot(piece, eexp)


def _rope(x, c, s_up, s_dn):
    w = x.shape[-1]
    up = pltpu.roll(x, w - HALF, axis=1)
    dn = pltpu.roll(x, HALF, axis=1)
    return x * c + up * s_up + dn * s_dn


def _tile_lanes(x, n):
    return jnp.concatenate([x] * n, axis=1)


def _const_spec(arr, ngrid):
    nd = arr.ndim
    return pl.BlockSpec(arr.shape, lambda *a, _nd=nd: (0,) * _nd, pipeline_mode=pl.Buffered(1))


MIXER_CONSTS = ("g_attn", "w_in", "w_conv", "alog", "dtb", "ered_g", "eexp_g", "g_q_a", "w_qb", "gq", "ered_q",
                "eexp_q", "g_kv_a", "gkr", "ered_r", "eexp_r", "wk", "gk", "ered_k", "eexp_k")


def _mixer_rows(x, c, conv_fn, rc, rsu, rsd, q_scale):
    xn = _rmsnorm(x, c["g_attn"][...]).astype(BF16)
    w_in = c["w_in"]

    conv_in = _dot(xn, w_in[:, OFF_CONV:OFF_GB])
    y = _silu(conv_fn(conv_in))
    qk = y[:, :2 * GDN_QK_DIM]
    qk = qk * _seg_rsqrt(qk, c["ered_g"][...], c["eexp_g"][...], 2 * GDN_HEADS)
    q_g = qk[:, :GDN_QK_DIM] * (GDN_DK ** -0.5)
    k_g = qk[:, GDN_QK_DIM:]
    v_g = y[:, 2 * GDN_QK_DIM:]

    ab = _dot(xn, w_in[:, OFF_GB:OFF_Z])
    lane = lax.broadcasted_iota(jnp.int32, ab.shape, 1)
    g_log = -jnp.exp(c["alog"][...]) * _softplus(ab + c["dtb"][...])
    gb = jnp.where(lane < BETA_LANE0, g_log, _sigmoid(ab))

    z = _dot(xn, w_in[:, OFF_Z:OFF_QA])

    qa = _dot(xn, w_in[:, OFF_QA:OFF_CKV])
    qs = _dot(_rmsnorm(qa, c["g_q_a"][...]).astype(BF16), c["w_qb"][...])
    qs = qs * _seg_rsqrt(qs, c["ered_q"][...], c["eexp_q"][...], 2 * MLA_HEADS) * c["gq"][...]
    q_mla = _rope(qs, _tile_lanes(rc, MLA_HEADS), _tile_lanes(rsu, MLA_HEADS), _tile_lanes(rsd, MLA_HEADS)) * q_scale

    ckv = _rmsnorm(_dot(xn, w_in[:, OFF_CKV:OFF_KR]), c["g_kv_a"][...])
    ckv_bf = ckv.astype(BF16)
    kr = _dot(xn, w_in[:, OFF_KR:IN_PAD])
    kr = kr * _seg_rsqrt(kr, c["ered_r"][...], c["eexp_r"][...], 1) * c["gkr"][...]
    kr = _rope(kr, rc, rsu, rsd)
    kk = _dot(ckv_bf, c["wk"][...])
    kk = kk * _seg_rsqrt(kk, c["ered_k"][...], c["eexp_k"][...], MLA_HEADS) * c["gk"][...]
    k_mla = kk + _tile_lanes(kr, MLA_HEADS)
    return dict(conv_in=conv_in, q_g=q_g, k_g=k_g, v_g=v_g, gb=gb, z=z, q_mla=q_mla, k_mla=k_mla,
                ckv=ckv, ckv_bf=ckv_bf, kr=kr)


def _prompt_mixer_kernel(tiles_per_seq, tm, x_ref, rc_ref, rsu_ref, rsd_ref, *refs):
    nc = len(MIXER_CONSTS)
    c = dict(zip(MIXER_CONSTS, refs[:nc]))
    wvt_ref = refs[nc]
    (qg_ref, kg_ref, vg_ref, gb_ref, z_ref, qm_ref, km_ref, vt_ref, ckv_ref, kro_ref, cs_ref, ext_ref) = refs[nc + 1:]
    i = pl.program_id(0)

    @pl.when(i % tiles_per_seq == 0)
    def _():
        ext_ref[5:8, :] = jnp.zeros((3, CONV_DIM), F32)

    def conv_fn(conv_in):
        w = c["w_conv"]
        ext_ref[8:8 + tm, :] = conv_in
        return (conv_in * w[3:4, :] + ext_ref[7:7 + tm, :] * w[2:3, :] + ext_ref[6:6 + tm, :] * w[1:2, :]
                + ext_ref[5:5 + tm, :] * w[0:1, :])

    r = _mixer_rows(x_ref[...], c, conv_fn, rc_ref[...], rsu_ref[...], rsd_ref[...], ATTN_SCALE * LOG2E)
    last3 = ext_ref[tm + 5:tm + 8, :]
    ext_ref[5:8, :] = last3
    cs_ref[0] = last3
    qg_ref[...] = r["q_g"]
    kg_ref[...] = r["k_g"]
    vg_ref[...] = r["v_g"]
    gb_ref[...] = r["gb"]
    z_ref[...] = r["z"]
    qm_ref[...] = r["q_mla"].astype(BF16)
    km_ref[...] = r["k_mla"].astype(BF16)
    vt_ref[...] = _dot_nt(wvt_ref[...], r["ckv_bf"]).astype(BF16)
    ckv_ref[...] = r["ckv"]
    kro_ref[...] = r["kr"][:, ROPE_LO:ROPE_LO + MLA_ROPE]


def _sample_mixer_kernel(x_ref, rc_ref, rsu_ref, rsd_ref, hist_ref, *refs):
    nc = len(MIXER_CONSTS)
    c = dict(zip(MIXER_CONSTS, refs[:nc]))
    (qg_ref, kg_ref, vg_ref, gb_ref, z_ref, qm_ref, km_ref, ckv_ref, kro_ref, cin_ref) = refs[nc:]

    def conv_fn(conv_in):
        w = c["w_conv"]
        return conv_in * w[3:4, :] + hist_ref[2] * w[2:3, :] + hist_ref[1] * w[1:2, :] + hist_ref[0] * w[0:1, :]

    r = _mixer_rows(x_ref[...], c, conv_fn, rc_ref[...], rsu_ref[...], rsd_ref[...], ATTN_SCALE)
    qg_ref[...] = r["q_g"]
    kg_ref[...] = r["k_g"]
    vg_ref[...] = r["v_g"]
    gb_ref[...] = r["gb"]
    z_ref[...] = r["z"]
    qm_ref[...] = r["q_mla"]
    km_ref[...] = r["k_mla"]
    ckv_ref[...] = r["ckv"]
    kro_ref[...] = r["kr"][:, ROPE_LO:ROPE_LO + MLA_ROPE]
    cin_ref[...] = r["conv_in"]


GROUP_HEADS = 4
GROUP_W = GROUP_HEADS * GDN_DK


def _cumsum_rows(x, period):
    row = lax.broadcasted_iota(jnp.int32, x.shape, 0) % period
    s = 1
    while s < period:
        x = x + jnp.where(row >= s, pltpu.roll(x, s, axis=0), 0.0)
        s *= 2
    return x


GDN_ITER_CHUNKS = 4


def _bd(y, bd_mask):
    return jnp.where(bd_mask, jnp.concatenate([y] * GROUP_HEADS, axis=0), 0.0)


def _gdn_solve(chains, masks):
    bd_mask, tril_cat, strict_cat, eye_cat = masks
    c = GDN_CHUNK
    n = range(len(chains))
    bd = lambda y: _bd(y, bd_mask)
    kb = [ch["k"] * ch["b_r"] for ch in chains]
    aq = [_dot_nt(jnp.concatenate([kb[i], chains[i]["q"]], axis=0), bd(chains[i]["k"])) for i in n]
    decay = []
    for ch in chains:
        g_col = jnp.sum(jnp.where(eye_cat, ch["g_r"], 0.0), axis=0, keepdims=True)
        decay.append(jnp.exp(jnp.where(tril_cat, ch["g_r"] - g_col, -jnp.inf)))
    qk = [aq[i][c:] * decay[i] for i in n]
    p = [-jnp.where(strict_cat, aq[i][:c] * decay[i], 0.0) for i in n]
    s_inv = [jnp.where(eye_cat, 1.0, 0.0) + p[i] for i in n]
    p = [_dot(p[i], bd(p[i])) for i in n]
    for lvl in range(5):
        s_next = [s_inv[i] + _dot(p[i], bd(s_inv[i])) for i in n]
        if lvl < 4:
            p = [_dot(p[i], bd(p[i])) for i in n]
        s_inv = s_next
    eg = [jnp.exp(ch["g_r"]) for ch in chains]
    u = [_dot(s_inv[i], bd(chains[i]["v"] * chains[i]["b_r"])) for i in n]
    w = [_dot(s_inv[i], bd(kb[i] * eg[i])) for i in n]
    sols = []
    for i in n:
        g_r = chains[i]["g_r"]
        g_last = g_r[c - 1:c, :]
        sols.append(dict(u=u[i], wq=jnp.concatenate([w[i], chains[i]["q"] * eg[i]], axis=0), qk=qk[i],
                         kd=chains[i]["k"] * jnp.exp(g_last - g_r), e_last=jnp.exp(g_last)))
    return sols


def _gdn_apply(sols, states, bd_mask):
    c = GDN_CHUNK
    n = range(len(sols))
    ws = [_dot(sols[i]["wq"], states[i]) for i in n]
    v_new = [sols[i]["u"] - ws[i][:c] for i in n]
    o = [ws[i][c:] + _dot(sols[i]["qk"], _bd(v_new[i], bd_mask)) for i in n]
    upd = [_dot_tn(sols[i]["kd"], v_new[i]) for i in n]
    new_states = [states[i] * sols[i]["e_last"] + jnp.where(bd_mask, upd[i], 0.0) for i in n]
    return jnp.concatenate(o, axis=1), new_states


def _gdn_kernel(nblk, tb, q_ref, k_ref, v_ref, gb_ref, z_ref, e6_ref, esum_ref, gout_ref, o_ref, sfin_ref, s_ref):
    j = pl.program_id(1)

    @pl.when(j == 0)
    def _():
        s_ref[...] = jnp.zeros(s_ref.shape, F32)

    c = GDN_CHUNK
    ngrp = GDN_HEADS // GROUP_HEADS
    rows_it = GDN_ITER_CHUNKS * c
    r_bd = lax.broadcasted_iota(jnp.int32, (GROUP_W, GROUP_W), 0) // c
    c_bd = lax.broadcasted_iota(jnp.int32, (GROUP_W, GROUP_W), 1) // c
    bd_mask = r_bd == c_bd
    ri = lax.broadcasted_iota(jnp.int32, (c, GROUP_W), 0)
    cj = lax.broadcasted_iota(jnp.int32, (c, GROUP_W), 1) % c
    masks = (bd_mask, ri >= cj, ri > cj, ri == cj)
    lane = lax.broadcasted_iota(jnp.int32, (rows_it, LANES), 1)
    copy_id = (lane % BETA_LANE0) // GDN_HEADS

    def body(it, carry):
        rows = pl.ds(pl.multiple_of(it * rows_it, rows_it), rows_it)
        gb = gb_ref[rows, :]
        sc = jnp.where(lane < BETA_LANE0, _cumsum_rows(gb, c), gb)
        p0, p1, p2 = _split3(sc)
        piece = jnp.where(copy_id == 0, p0, jnp.where(copy_id == 1, p1, p2))
        ex = _dot(piece, e6_ref[...])
        q = q_ref[rows, :]
        k = k_ref[rows, :]
        v = v_ref[rows, :]
        chains = []
        for t in range(GDN_ITER_CHUNKS):
            rs = slice(t * c, (t + 1) * c)
            for grp in range(ngrp):
                sl = slice(grp * GROUP_W, (grp + 1) * GROUP_W)
                sb = slice(GDN_WIDTH + grp * GROUP_W, GDN_WIDTH + (grp + 1) * GROUP_W)
                chains.append(dict(q=q[rs, sl], k=k[rs, sl], v=v[rs, sl], g_r=ex[rs, sl], b_r=ex[rs, sb]))
        sols = _gdn_solve(chains, masks)
        states = [s_ref[grp] for grp in range(ngrp)]
        outs = []
        for t in range(GDN_ITER_CHUNKS):
            o_t, states = _gdn_apply(sols[t * ngrp:(t + 1) * ngrp], states, bd_mask)
            outs.append(o_t)
        for grp in range(ngrp):
            s_ref[grp] = states[grp]
        o = jnp.concatenate(outs, axis=0)
        hi, lo = _split2(o * o)
        ms = _dot(hi, esum_ref[...]) + _dot(lo, esum_ref[...])
        o_ref[rows, :] = (o * lax.rsqrt(ms + EPS) * gout_ref[...] * _silu(z_ref[rows, :])).astype(o_ref.dtype)
        return carry

    lax.fori_loop(0, tb // rows_it, body, 0)

    @pl.when(j == nblk - 1)
    def _():
        for grp in range(ngrp):
            for h in range(GROUP_HEADS):
                sfin_ref[0, grp * GROUP_HEADS + h] = s_ref[grp, h * c:(h + 1) * c, h * c:(h + 1) * c]


ATT_T = 256
ATT_HEADS = 8
ATT_SUM_ROWS = 16


def _attn_kernel(q_ref, k_ref, vt_ref, o_ref):
    qi = pl.program_id(2)
    t = ATT_T
    qs = [q_ref[:, hh * SLAB:(hh + 1) * SLAB] for hh in range(ATT_HEADS)]

    def tile(j, carry, masked):
        k0 = pl.multiple_of(j * t, t)
        scores = [_dot_nt(k_ref[pl.ds(k0, t), hh * SLAB:(hh + 1) * SLAB], qs[hh]) for hh in range(ATT_HEADS)]
        stats = []
        for hh in range(ATT_HEADS):
            m, _ = carry[hh]
            s = scores[hh]
            if masked:
                kp = lax.broadcasted_iota(jnp.int32, (t, t), 0)
                qp = lax.broadcasted_iota(jnp.int32, (t, t), 1)
                s = jnp.where(qp >= kp, s, -jnp.inf)
            m_new = jnp.maximum(m, jnp.max(s, axis=0, keepdims=True))
            stats.append((m_new, jnp.exp2(m - m_new), jnp.exp2(s - m_new).astype(BF16)))
        out = []
        for hh in range(ATT_HEADS):
            m_new, alpha, p = stats[hh]
            vt = vt_ref[hh * MLA_VDIM:(hh + 1) * MLA_VDIM, pl.ds(k0, t)]
            out.append((m_new, carry[hh][1] * alpha + _dot(jnp.concatenate([vt, ones], axis=0), p)))
        return tuple(out)

    ones = jnp.ones((ATT_SUM_ROWS, t), BF16)
    init = tuple((jnp.full((1, t), -jnp.inf, F32), jnp.zeros((MLA_VDIM + ATT_SUM_ROWS, t), F32)) for _ in range(ATT_HEADS))
    carry = lax.fori_loop(0, qi, functools.partial(tile, masked=False), init)
    carry = tile(qi, carry, True)
    outs = [acc[:MLA_VDIM] / acc[MLA_VDIM:MLA_VDIM + 1] for (_, acc) in carry]
    o_ref[...] = jnp.concatenate(outs, axis=0).T.astype(o_ref.dtype)


def _tail_kernel(x_ref, og_ref, om_ref, p_ref, wo_ref, gffn_ref, wg_ref, wu_ref, wd_ref, gple_ref, wpg_ref, wpp_ref, y_ref):
    h = x_ref[...] + _dot(og_ref[...], wo_ref[:GDN_WIDTH, :]) + _dot(om_ref[...], wo_ref[GDN_WIDTH:, :])
    u = _rmsnorm(h, gffn_ref[...]).astype(BF16)
    act = (_silu(_dot(u, wg_ref[...])) * _dot(u, wu_ref[...])).astype(BF16)
    acc = h + _dot(act, wd_ref[...])
    gate = _sigmoid(_dot(_rmsnorm(acc, gple_ref[...]).astype(BF16), wpg_ref[...]))
    y_ref[...] = acc + _dot(p_ref[...].astype(BF16), wpp_ref[...]) * gate


PAGES_PER_STEP = 16
PAGES_PER_SUB = 8


def _sample_prep_kernel(qm_ref, km_ref, gk_ref, wk_ref, qabs_ref, qr_ref, sself_ref):
    qm = qm_ref[...]
    prod = qm * km_ref[...]
    qg = (qm * gk_ref[...]).astype(BF16)
    for h in range(MLA_HEADS):
        sl = slice(h * SLAB, (h + 1) * SLAB)
        qabs_ref[h] = _dot_nt(qg[:, sl], wk_ref[:, sl])
        qr_ref[h] = qm[:, h * SLAB + ROPE_LO:h * SLAB + ROPE_LO + MLA_ROPE]
        sself_ref[h] = jnp.sum(prod[:, sl], axis=1, keepdims=True)


def _paged_kernel(nsteps, page, pt_ref, qabs_ref, qr_ref, sself_ref, ckvn_ref, wkt_ref, ckv_hbm, krt_hbm, o_ref,
                  cbuf, kbuf, sems):
    pp = PAGES_PER_STEP
    b = pl.program_id(0)
    nb = pl.num_programs(0)

    def copies(row, g, slot):
        out = []
        for j in range(pp):
            pg = pt_ref[row, g * pp + j]
            out.append(pltpu.make_async_copy(ckv_hbm.at[pg], cbuf.at[slot, pl.ds(j * page, page), :], sems.at[slot, 0]))
            out.append(pltpu.make_async_copy(krt_hbm.at[pg], kbuf.at[slot, :, pl.ds(j * page, page)], sems.at[slot, 1]))
        return out

    @pl.when(b == 0)
    def _():
        for cp in copies(0, 0, 0):
            cp.start()

    qa = qabs_ref[...].astype(BF16)
    qr = qr_ref[...].astype(BF16)
    wkt = wkt_ref[...]
    m = sself_ref[...]
    l = jnp.ones(m.shape, F32)
    acc = jnp.broadcast_to(ckvn_ref[...], (MLA_HEADS, KV_LORA))
    sub_tok = PAGES_PER_SUB * page
    for g in range(nsteps):
        slot = g % 2
        for cp in copies(b, g, slot):
            cp.wait()
        nrow, ng = (b, g + 1) if g + 1 < nsteps else (jnp.minimum(b + 1, nb - 1), 0)
        for cp in copies(nrow, ng, 1 - slot):
            cp.start()
        cs, ss_list = [], []
        for i in range(pp // PAGES_PER_SUB):
            c = cbuf[slot, i * sub_tok:(i + 1) * sub_tok, :].astype(BF16)
            krt = kbuf[slot, :, i * sub_tok:(i + 1) * sub_tok].astype(BF16)
            kv = _dot_nt(wkt, c)
            ss = jnp.sum((kv * kv).reshape(MLA_HEADS, MLA_NOPE, kv.shape[1]), axis=1)
            ss_list.append(_dot_nt(qa, c) * lax.rsqrt(ss * (1.0 / MLA_NOPE) + EPS) + _dot(qr, krt))
            cs.append(c)
        s = jnp.concatenate(ss_list, axis=1)
        m_new = jnp.maximum(m, jnp.max(s, axis=1, keepdims=True))
        alpha = jnp.exp(m - m_new)
        p = jnp.exp(s - m_new)
        l = l * alpha + jnp.sum(p, axis=1, keepdims=True)
        acc = acc * alpha + _dot(p.astype(BF16), jnp.concatenate(cs, axis=0))
        m = m_new
    o_ref[...] = acc / l

    @pl.when(b == nb - 1)
    def _():
        for cp in copies(b, 0, 0):
            cp.wait()


def _sample_gdn_kernel(q_ref, k_ref, v_ref, gb_ref, s_ref, snew_ref, o_ref):
    hk = GDN_HEADS * GDN_DK
    hrow = lax.broadcasted_iota(jnp.int32, (GDN_HEADS, hk), 0)
    hlane = lax.broadcasted_iota(jnp.int32, (GDN_HEADS, hk), 1) // GDN_DK
    hm = hrow == hlane
    kmask = jnp.where(hm, jnp.broadcast_to(k_ref[...], (GDN_HEADS, hk)), 0.0)
    qmask = jnp.where(hm, jnp.broadcast_to(q_ref[...], (GDN_HEADS, hk)), 0.0)
    gbb = jnp.broadcast_to(gb_ref[...], (GDN_HEADS, LANES))
    r8 = lax.broadcasted_iota(jnp.int32, (GDN_HEADS, LANES), 0)
    l8 = lax.broadcasted_iota(jnp.int32, (GDN_HEADS, LANES), 1)
    g_col = jnp.sum(jnp.where(l8 == r8, gbb, 0.0), axis=1, keepdims=True)
    b_col = jnp.sum(jnp.where(l8 == r8 + BETA_LANE0, gbb, 0.0), axis=1, keepdims=True)
    ones = jnp.where(hm, 1.0, 0.0).astype(BF16)

    def expand_rows(col):
        e0, e1, e2 = _split3(jnp.broadcast_to(col, (GDN_HEADS, GDN_DV)))
        return _dot_tn(jnp.concatenate([ones, ones, ones], axis=0), jnp.concatenate([e0, e1, e2], axis=0))

    def head_dot(xmask, mat):
        xh, xl = _split2(xmask)
        mh, ml = _split2(mat)
        return _dot(jnp.concatenate([xh, xh, xl], axis=1), jnp.concatenate([mh, ml, mh], axis=0))

    state = s_ref[...] * expand_rows(jnp.exp(g_col))
    delta = (v_ref[...] - head_dot(kmask, state)) * b_col
    kh, kl = _split2(kmask)
    dh, dl = _split2(delta)
    state = state + _dot_tn(jnp.concatenate([kh, kh, kl], axis=0), jnp.concatenate([dh, dl, dh], axis=0))
    snew_ref[...] = state
    o_ref[...] = head_dot(qmask, state)


def _sample_mix_kernel(o_ref, z_ref, olat_ref, wvbd_ref, esum_ref, gout_ref, og_ref, om_ref):
    o = o_ref[...]
    hi, lo = _split2(o * o)
    ms = _dot(hi, esum_ref[...]) + _dot(lo, esum_ref[...])
    og_ref[...] = (o * lax.rsqrt(ms + EPS) * gout_ref[...] * _silu(z_ref[...])).astype(og_ref.dtype)
    om_ref[...] = _dot(olat_ref[...].astype(BF16), wvbd_ref[...]).astype(om_ref.dtype)


def _seg_tables(width, segs, mean):
    n = len(segs)
    ered = np.zeros((width, LANES), np.float32)
    eexp = np.zeros((LANES, width), np.float32)
    for copy in range(3):
        for s, (a, b) in enumerate(segs):
            ered[a:b, copy * n + s] = 1.0 / (b - a) if mean else 1.0
            eexp[copy * n + s, a:b] = 1.0
    return jnp.asarray(ered, BF16), jnp.asarray(eexp, BF16)


def _rope_slabs(pos):
    inv_freq = ROPE_THETA ** (-jnp.arange(HALF, dtype=F32) / HALF)
    ang = pos.astype(F32)[:, None] * inv_freq[None, :]
    cos, sin = jnp.cos(ang), jnp.sin(ang)
    n = pos.shape[0]
    zeros = lambda w: jnp.zeros((n, w), F32)
    rc = jnp.concatenate([jnp.ones((n, MLA_NOPE), F32), cos, cos, zeros(SLAB - MLA_QK)], axis=1)
    rsu = jnp.concatenate([zeros(MLA_NOPE), -sin, zeros(SLAB - ROPE_HI)], axis=1)
    rsd = jnp.concatenate([zeros(ROPE_HI), sin, zeros(SLAB - MLA_QK)], axis=1)
    return rc, rsu, rsd


def _prepare_weights(g_attn, w_in, w_conv, gdn_a_log, gdn_dt_bias, g_gdn_out, g_q_a, w_q_b, g_q_nope, g_q_rope,
                     g_kv_a, g_k_rope, w_kv_b, g_k_nope):
    row = lambda v: v.reshape(1, -1).astype(F32)
    a = w_in[:, CONV_DIM:CONV_DIM + GDN_HEADS]
    b = w_in[:, CONV_DIM + GDN_HEADS:CONV_DIM + 2 * GDN_HEADS]
    o_z = CONV_DIM + 2 * GDN_HEADS
    o_qa = o_z + GDN_WIDTH
    o_kv = o_qa + Q_LORA
    zc = lambda w: jnp.zeros((D_MODEL, w), F32)
    gb_slab = jnp.concatenate([a, a, a, zc(BETA_LANE0 - 3 * GDN_HEADS), b, b, b, zc(LANES - BETA_LANE0 - 3 * GDN_HEADS)], axis=1)
    kr_slab = jnp.concatenate([zc(ROPE_LO), w_in[:, o_kv + KV_LORA:], zc(SLAB - MLA_QK)], axis=1)
    w_in_p = jnp.concatenate([w_in[:, :CONV_DIM], gb_slab, w_in[:, o_z:o_qa], w_in[:, o_qa:o_kv],
                              w_in[:, o_kv:o_kv + KV_LORA], kr_slab], axis=1).astype(BF16)

    def scalar_slab(v):
        z8 = jnp.zeros((BETA_LANE0 - 3 * GDN_HEADS,), F32)
        return jnp.concatenate([v, v, v, z8, jnp.zeros((LANES - BETA_LANE0,), F32)]).reshape(1, LANES)

    wq = w_q_b.reshape(Q_LORA, MLA_HEADS, MLA_QK)
    w_qb = jnp.concatenate([wq, jnp.zeros((Q_LORA, MLA_HEADS, SLAB - MLA_QK), F32)], axis=2).reshape(Q_LORA, QK_SLAB_W).astype(BF16)
    gq = jnp.tile(jnp.concatenate([g_q_nope, g_q_rope, jnp.zeros((SLAB - MLA_QK,), F32)]), MLA_HEADS).reshape(1, QK_SLAB_W)
    wkv = w_kv_b.reshape(KV_LORA, MLA_HEADS, MLA_NOPE + MLA_VDIM)
    wk_part, wv_part = wkv[:, :, :MLA_NOPE], wkv[:, :, MLA_NOPE:]
    wk = jnp.concatenate([wk_part, jnp.zeros((KV_LORA, MLA_HEADS, SLAB - MLA_NOPE), F32)], axis=2).reshape(KV_LORA, QK_SLAB_W).astype(BF16)
    gk = jnp.tile(jnp.concatenate([g_k_nope, jnp.zeros((SLAB - MLA_NOPE,), F32)]), MLA_HEADS).reshape(1, QK_SLAB_W)
    gkr = jnp.concatenate([jnp.zeros((ROPE_LO,), F32), g_k_rope, jnp.zeros((SLAB - MLA_QK,), F32)]).reshape(1, SLAB)
    wvt = wv_part.reshape(KV_LORA, MLA_HEADS * MLA_VDIM).T.astype(BF16)
    wkt = wk_part.reshape(KV_LORA, MLA_HEADS * MLA_NOPE).T.astype(BF16)
    wv_bd = jnp.zeros((MLA_HEADS, KV_LORA, MLA_HEADS, MLA_VDIM), F32)
    wv_bd = wv_bd.at[jnp.arange(MLA_HEADS), :, jnp.arange(MLA_HEADS), :].set(jnp.moveaxis(wv_part, 1, 0))
    wv_bd = wv_bd.reshape(MLA_HEADS * KV_LORA, MLA_HEADS * MLA_VDIM).astype(BF16)

    ered_g, eexp_g = _seg_tables(2 * GDN_QK_DIM, [(h * GDN_DK, (h + 1) * GDN_DK) for h in range(2 * GDN_HEADS)], False)
    q_segs = ([(h * SLAB, h * SLAB + MLA_NOPE) for h in range(MLA_HEADS)]
              + [(h * SLAB + ROPE_LO, h * SLAB + MLA_QK) for h in range(MLA_HEADS)])
    ered_q, eexp_q = _seg_tables(QK_SLAB_W, q_segs, True)
    ered_k, eexp_k = _seg_tables(QK_SLAB_W, q_segs[:MLA_HEADS], True)
    ered_r, eexp_r = _seg_tables(SLAB, [(ROPE_LO, MLA_QK)], True)
    e6 = np.zeros((LANES, 2 * GDN_WIDTH), np.float32)
    for copy in range(3):
        for h in range(GDN_HEADS):
            e6[copy * GDN_HEADS + h, h * GDN_DV:(h + 1) * GDN_DV] = 1.0
            e6[BETA_LANE0 + copy * GDN_HEADS + h, GDN_WIDTH + h * GDN_DV:GDN_WIDTH + (h + 1) * GDN_DV] = 1.0
    esum = np.kron(np.eye(GDN_HEADS, dtype=np.float32), np.full((GDN_DV, GDN_DV), 1.0 / GDN_DV, np.float32))

    mixer = dict(g_attn=row(g_attn), w_in=w_in_p, w_conv=w_conv.astype(F32), alog=scalar_slab(gdn_a_log),
                 dtb=scalar_slab(gdn_dt_bias), ered_g=ered_g, eexp_g=eexp_g, g_q_a=row(g_q_a), w_qb=w_qb, gq=gq,
                 ered_q=ered_q, eexp_q=eexp_q, g_kv_a=row(g_kv_a), gkr=gkr, ered_r=ered_r, eexp_r=eexp_r, wk=wk, gk=gk,
                 ered_k=ered_k, eexp_k=eexp_k)
    extra = dict(wvt=wvt, wkt=wkt, wv_bd=wv_bd, e6=jnp.asarray(e6, BF16), esum=jnp.asarray(esum, BF16),
                 gout=jnp.tile(g_gdn_out, GDN_HEADS).reshape(1, GDN_WIDTH).astype(F32))
    return mixer, extra


def _params(sem=None):
    return pltpu.CompilerParams(dimension_semantics=sem, vmem_limit_bytes=VMEM_LIMIT)


def _prompt_mixer(x2, slabs, mixer, wvt, bsz, seq, tm):
    n = x2.shape[0]
    tps = seq // tm
    consts = [mixer[k] for k in MIXER_CONSTS] + [wvt]
    row_spec = lambda w: pl.BlockSpec((tm, w), lambda i: (i, 0))
    rope_spec = pl.BlockSpec((tm, SLAB), lambda i: (i % tps, 0))
    out_shapes = [
        jax.ShapeDtypeStruct((n, GDN_QK_DIM), F32), jax.ShapeDtypeStruct((n, GDN_QK_DIM), F32),
        jax.ShapeDtypeStruct((n, GDN_WIDTH), F32), jax.ShapeDtypeStruct((n, LANES), F32),
        jax.ShapeDtypeStruct((n, GDN_WIDTH), F32), jax.ShapeDtypeStruct((n, QK_SLAB_W), BF16),
        jax.ShapeDtypeStruct((n, QK_SLAB_W), BF16), jax.ShapeDtypeStruct((MLA_HEADS * MLA_VDIM, n), BF16),
        jax.ShapeDtypeStruct((n, KV_LORA), F32), jax.ShapeDtypeStruct((n, MLA_ROPE), F32),
        jax.ShapeDtypeStruct((bsz, CONV_WIDTH - 1, CONV_DIM), F32)]
    out_specs = [row_spec(GDN_QK_DIM), row_spec(GDN_QK_DIM), row_spec(GDN_WIDTH), row_spec(LANES), row_spec(GDN_WIDTH),
                 row_spec(QK_SLAB_W), row_spec(QK_SLAB_W), pl.BlockSpec((MLA_HEADS * MLA_VDIM, tm), lambda i: (0, i)),
                 row_spec(KV_LORA), row_spec(MLA_ROPE),
                 pl.BlockSpec((1, CONV_WIDTH - 1, CONV_DIM), lambda i: (i // tps, 0, 0))]
    return pl.pallas_call(
        functools.partial(_prompt_mixer_kernel, tps, tm),
        grid=(n // tm,),
        in_specs=[row_spec(D_MODEL), rope_spec, rope_spec, rope_spec] + [_const_spec(a, 1) for a in consts],
        out_specs=out_specs, out_shape=out_shapes,
        scratch_shapes=[pltpu.VMEM((tm + 8, CONV_DIM), F32)],
        compiler_params=_params(("arbitrary",)), name="prompt_mixer",
    )(x2, *slabs, *consts)


def _sample_mixer(x2, slabs, hist, mixer):
    n = x2.shape[0]
    consts = [mixer[k] for k in MIXER_CONSTS]
    sd = lambda w: jax.ShapeDtypeStruct((n, w), F32)
    return pl.pallas_call(
        _sample_mixer_kernel,
        out_shape=[sd(GDN_QK_DIM), sd(GDN_QK_DIM), sd(GDN_WIDTH), sd(LANES), sd(GDN_WIDTH), sd(QK_SLAB_W), sd(QK_SLAB_W),
                   sd(KV_LORA), sd(MLA_ROPE), sd(CONV_DIM)],
        compiler_params=_params(), name="sample_mixer",
    )(x2, *slabs, hist, *consts)


def _gdn_prompt(qg, kg, vg, gb, z, extra, bsz, seq, tb):
    n = qg.shape[0]
    nblk = seq // tb
    row_spec = lambda w: pl.BlockSpec((tb, w), lambda b, j: (b * nblk + j, 0))
    consts = [extra["e6"], extra["esum"], extra["gout"]]
    return pl.pallas_call(
        functools.partial(_gdn_kernel, nblk, tb),
        grid=(bsz, nblk),
        in_specs=[row_spec(GDN_QK_DIM), row_spec(GDN_QK_DIM), row_spec(GDN_WIDTH), row_spec(LANES), row_spec(GDN_WIDTH)]
        + [_const_spec(a, 2) for a in consts],
        out_specs=[row_spec(GDN_WIDTH), pl.BlockSpec((1, GDN_HEADS, GDN_DK, GDN_DV), lambda b, j: (b, 0, 0, 0))],
        out_shape=[jax.ShapeDtypeStruct((n, GDN_WIDTH), BF16), jax.ShapeDtypeStruct((bsz, GDN_HEADS, GDN_DK, GDN_DV), F32)],
        scratch_shapes=[pltpu.VMEM((GDN_HEADS // GROUP_HEADS, GROUP_W, GROUP_W), F32)],
        compiler_params=_params(("arbitrary", "arbitrary")), name="gdn_chunked",
    )(qg, kg, vg, gb, z, *consts)


def _attn_prompt(qm, km, vt, bsz, seq):
    n = qm.shape[0]
    t = ATT_T
    nq = seq // t
    return pl.pallas_call(
        _attn_kernel,
        grid=(bsz, MLA_HEADS // ATT_HEADS, nq),
        in_specs=[pl.BlockSpec((t, ATT_HEADS * SLAB), lambda b, hp, qi: (b * nq + qi, hp)),
                  pl.BlockSpec((seq, ATT_HEADS * SLAB), lambda b, hp, qi: (b, hp)),
                  pl.BlockSpec((ATT_HEADS * MLA_VDIM, seq), lambda b, hp, qi: (hp, b))],
        out_specs=pl.BlockSpec((t, ATT_HEADS * MLA_VDIM), lambda b, hp, qi: (b * nq + qi, hp)),
        out_shape=jax.ShapeDtypeStruct((n, MLA_HEADS * MLA_VDIM), BF16),
        compiler_params=_params(("arbitrary", "arbitrary", "arbitrary")), name="mla_prompt_attention",
    )(qm, km, vt)


def _tail(x2, og, om, p2, tailw, tm):
    n = x2.shape[0]
    row_spec = lambda w: pl.BlockSpec((tm, w), lambda i: (i, 0))
    return pl.pallas_call(
        _tail_kernel,
        grid=(n // tm,),
        in_specs=[row_spec(D_MODEL), row_spec(GDN_WIDTH), row_spec(GDN_WIDTH), row_spec(PLE_DIM)]
        + [_const_spec(a, 1) for a in tailw],
        out_specs=row_spec(D_MODEL), out_shape=jax.ShapeDtypeStruct((n, D_MODEL), F32),
        compiler_params=_params(("arbitrary",)), name="layer_tail",
    )(x2, og, om, p2, *tailw)


def _sample_prep(qm, km, gk, wk):
    n = qm.shape[0]
    return pl.pallas_call(
        _sample_prep_kernel,
        out_shape=[jax.ShapeDtypeStruct((MLA_HEADS, n, KV_LORA), F32), jax.ShapeDtypeStruct((MLA_HEADS, n, MLA_ROPE), F32),
                   jax.ShapeDtypeStruct((MLA_HEADS, n, 1), F32)],
        compiler_params=_params(), name="sample_prep",
    )(qm, km, gk, wk)


def _paged_attention(page_table, pool_ckv, pool_kr, qabs, qr, sself, ckv_new, wkt):
    bs, n_pages = page_table.shape
    page = pool_ckv.shape[1]
    pp = PAGES_PER_STEP
    nsteps = n_pages // pp
    assert n_pages % pp == 0 and nsteps % 2 == 0, "page groups must fill whole double-buffer rounds"
    per_b = lambda w: pl.BlockSpec((None, MLA_HEADS, w), lambda b, pt: (b, 0, 0))
    grid_spec = pltpu.PrefetchScalarGridSpec(
        num_scalar_prefetch=1, grid=(bs,),
        in_specs=[per_b(KV_LORA), per_b(MLA_ROPE), per_b(1),
                  pl.BlockSpec((None, 1, KV_LORA), lambda b, pt: (b, 0, 0)),
                  pl.BlockSpec(wkt.shape, lambda b, pt: (0, 0)),
                  pl.BlockSpec(memory_space=pl.ANY), pl.BlockSpec(memory_space=pl.ANY)],
        out_specs=pl.BlockSpec((None, MLA_HEADS, KV_LORA), lambda b, pt: (b, 0, 0)),
        scratch_shapes=[pltpu.VMEM((2, pp * page, KV_LORA), F32), pltpu.VMEM((2, MLA_ROPE, pp * page), F32),
                        pltpu.SemaphoreType.DMA((2, 2))])
    return pl.pallas_call(
        functools.partial(_paged_kernel, nsteps, page), grid_spec=grid_spec,
        out_shape=jax.ShapeDtypeStruct((bs, MLA_HEADS, KV_LORA), F32),
        compiler_params=_params(("arbitrary",)), name="mla_paged_decode",
    )(page_table, qabs, qr, sself, ckv_new, wkt, pool_ckv, pool_kr)


def _sample_gdn(q3, k3, v3, gb3, state):
    bs = q3.shape[0]
    hk = GDN_HEADS * GDN_DK
    b3 = lambda r, w: pl.BlockSpec((None, r, w), lambda b: (b, 0, 0))
    return pl.pallas_call(
        _sample_gdn_kernel, grid=(bs,),
        in_specs=[b3(1, hk), b3(1, hk), b3(GDN_HEADS, GDN_DV), b3(1, LANES), b3(hk, GDN_DV)],
        out_specs=[b3(hk, GDN_DV), b3(GDN_HEADS, GDN_DV)],
        out_shape=[jax.ShapeDtypeStruct((bs, hk, GDN_DV), F32), jax.ShapeDtypeStruct((bs, GDN_HEADS, GDN_DV), F32)],
        compiler_params=_params(("arbitrary",)), name="gdn_recurrent_step",
    )(q3, k3, v3, gb3, state)


def _sample_mix(o2, z, olat2, extra):
    n = o2.shape[0]
    return pl.pallas_call(
        _sample_mix_kernel,
        out_shape=[jax.ShapeDtypeStruct((n, GDN_WIDTH), BF16), jax.ShapeDtypeStruct((n, MLA_HEADS * MLA_VDIM), BF16)],
        compiler_params=_params(), name="sample_mix",
    )(o2, z, olat2, extra["wv_bd"], extra["esum"], extra["gout"])


def _pick(n, prefs):
    for t in prefs:
        if n % t == 0:
            return t
    return n


def kernel(x_prompt, x_sample, cache_ckv, cache_krope, state_gdn, state_conv, page_table, p_prompt, p_sample, g_attn, w_in, w_conv, gdn_a_log, gdn_dt_bias, g_gdn_out, g_q_a, w_q_b, g_q_nope, g_q_rope, g_kv_a, g_k_rope, w_kv_b, g_k_nope, w_o, g_ffn, w_ffn_gate, w_ffn_up, w_ffn_down, g_ple, w_ple_gate, w_ple_proj):
    depth = g_attn.shape[0]
    assert depth == 1 and x_sample.shape[1] == 1, "single layer, single new token per sample row"
    bp, seq, _ = x_prompt.shape
    bs = x_sample.shape[0]
    past = page_table.shape[1] * cache_ckv.shape[2]
    li = 0
    mixer, extra = _prepare_weights(g_attn[li], w_in[li], w_conv[li], gdn_a_log[li], gdn_dt_bias[li], g_gdn_out[li],
                                    g_q_a[li], w_q_b[li], g_q_nope[li], g_q_rope[li], g_kv_a[li], g_k_rope[li],
                                    w_kv_b[li], g_k_nope[li])
    row = lambda v: v.reshape(1, -1).astype(F32)
    tailw = [w_o[li].astype(BF16), row(g_ffn[li]), w_ffn_gate[li].astype(BF16), w_ffn_up[li].astype(BF16),
             w_ffn_down[li].astype(BF16), row(g_ple[li]), w_ple_gate[li].astype(BF16), w_ple_proj[li].astype(BF16)]

    n = bp * seq
    xp2 = x_prompt.reshape(n, D_MODEL)
    tm = _pick(seq, (256, 128, 64, 32, 16, 8))
    (qg, kg, vg, gb, z, qm, km, vt, ckv_p, kr_p, conv_p) = _prompt_mixer(
        xp2, _rope_slabs(jnp.arange(seq)), mixer, extra["wvt"], bp, seq, tm)
    tb = _pick(seq, (512, 256, 128, 64))
    og, gdn_p = _gdn_prompt(qg, kg, vg, gb, z, extra, bp, seq, tb)
    om = _attn_prompt(qm, km, vt, bp, seq)
    y_prompt = _tail(xp2, og, om, p_prompt[li].reshape(n, PLE_DIM), tailw, tm).reshape(bp, seq, D_MODEL)

    xs2 = x_sample.reshape(bs, D_MODEL)
    hist = jnp.moveaxis(state_conv[li], 1, 0)
    (qg_s, kg_s, vg_s, gb_s, z_s, qm_s, km_s, ckv_s, kr_s, cin_s) = _sample_mixer(
        xs2, _rope_slabs(past + jnp.arange(1)), hist, mixer)
    qabs, qr, sself = _sample_prep(qm_s, km_s, mixer["gk"], mixer["wk"])
    per_row = lambda t: jnp.swapaxes(t, 0, 1)
    pool_krt = jnp.swapaxes(cache_krope[li], 1, 2)
    olat = _paged_attention(page_table, cache_ckv[li], pool_krt, per_row(qabs), per_row(qr), per_row(sself),
                            ckv_s.reshape(bs, 1, KV_LORA), extra["wkt"])
    hk = GDN_HEADS * GDN_DK
    s_new, o_s = _sample_gdn(qg_s.reshape(bs, 1, hk), kg_s.reshape(bs, 1, hk), vg_s.reshape(bs, GDN_HEADS, GDN_DV),
                             gb_s.reshape(bs, 1, LANES), state_gdn[li].reshape(bs, hk, GDN_DV))
    og_s, om_s = _sample_mix(o_s.reshape(bs, GDN_WIDTH), z_s, olat.reshape(bs, MLA_HEADS * KV_LORA), extra)
    y_sample = _tail(xs2, og_s, om_s, p_sample[li].reshape(bs, PLE_DIM), tailw, bs).reshape(bs, 1, D_MODEL)
    conv_s = jnp.concatenate([state_conv[li][:, 1:], cin_s[:, None, :]], axis=1)

    return (y_prompt, y_sample,
            ckv_p.reshape(1, bp, seq, KV_LORA), kr_p.reshape(1, bp, seq, MLA_ROPE),
            gdn_p[None], conv_p[None],
            ckv_s.reshape(1, bs, 1, KV_LORA), kr_s.reshape(1, bs, 1, MLA_ROPE),
            s_new.reshape(1, bs, GDN_HEADS, GDN_DK, GDN_DV), conv_s[None])
```

```python
import functools
import math

import numpy as np
import jax
import jax.numpy as jnp
from jax import lax
from jax.experimental import pallas as pl
from jax.experimental.pallas import tpu as pltpu

F32 = jnp.float32
BF16 = jnp.bfloat16

D_MODEL = 1024
PLE_DIM = 256
GDN_HEADS = 8
GDN_DK = 64
GDN_DV = 64
GDN_WIDTH = GDN_HEADS * GDN_DV
GDN_QK_DIM = GDN_HEADS * GDN_DK
CONV_WIDTH = 4
CONV_DIM = 2 * GDN_QK_DIM + GDN_WIDTH
GDN_CHUNK = 64
MLA_HEADS = 8
MLA_NOPE = 64
MLA_ROPE = 32
MLA_VDIM = 64
MLA_QK = MLA_NOPE + MLA_ROPE
Q_LORA = 384
KV_LORA = 256
ROPE_THETA = 10000.0
ATTN_SCALE = MLA_QK ** -0.5
LOG2E = math.log2(math.e)
D_FF = 2816
EPS = 1e-6

LANES = 128
SLAB = 128
HALF = MLA_ROPE // 2
ROPE_LO = MLA_NOPE
ROPE_HI = MLA_NOPE + HALF
QK_SLAB_W = MLA_HEADS * SLAB

OFF_CONV = 0
OFF_GB = OFF_CONV + CONV_DIM
OFF_Z = OFF_GB + LANES
OFF_QA = OFF_Z + GDN_WIDTH
OFF_CKV = OFF_QA + Q_LORA
OFF_KR = OFF_CKV + KV_LORA
IN_PAD = OFF_KR + LANES
BETA_LANE0 = 32

VMEM_LIMIT = 56 * 1024 * 1024


def _dot(a, b):
    return jnp.dot(a, b, preferred_element_type=F32)


def _dot_nt(a, b):
    return lax.dot_general(a, b, (((1,), (1,)), ((), ())), preferred_element_type=F32)


def _dot_tn(a, b):
    return lax.dot_general(a, b, (((0,), (0,)), ((), ())), preferred_element_type=F32)


def _split2(x):
    hi = x.astype(BF16)
    lo = (x - hi.astype(F32)).astype(BF16)
    return hi, lo


def _split3(x):
    hi = x.astype(BF16)
    r1 = x - hi.astype(F32)
    mid = r1.astype(BF16)
    lo = (r1 - mid.astype(F32)).astype(BF16)
    return hi, mid, lo


def _sigmoid(x):
    return 1.0 / (1.0 + jnp.exp(-x))


def _silu(x):
    return x * _sigmoid(x)


def _softplus(x):
    return jnp.maximum(x, 0.0) + jnp.log1p(jnp.exp(-jnp.abs(x)))


def _rmsnorm(x, g):
    return x * lax.rsqrt(jnp.mean(x * x, axis=-1, keepdims=True) + EPS) * g


def _seg_rsqrt(x, ered, eexp, nseg):
    red = _dot((x * x).astype(BF16), ered)
    r = lax.rsqrt(red + EPS)
    p0, p1, p2 = _split3(r)
    lane = lax.broadcasted_iota(jnp.int32, r.shape, 1)
    piece = jnp.where(lane < nseg, p0, jnp.where(lane < 2 * nseg, p1, p2))
    return _dot(piece, eexp)


def _rope(x, c, s_up, s_dn):
    w = x.shape[-1]
    up = pltpu.roll(x, w - HALF, axis=1)
    dn = pltpu.roll(x, HALF, axis=1)
    return x * c + up * s_up + dn * s_dn


def _tile_lanes(x, n):
    return jnp.concatenate([x] * n, axis=1)


def _const_spec(arr, ngrid):
    nd = arr.ndim
    return pl.BlockSpec(arr.shape, lambda *a, _nd=nd: (0,) * _nd, pipeline_mode=pl.Buffered(1))


MIXER_CONSTS = ("g_attn", "w_in", "w_conv", "alog", "dtb", "ered_g", "eexp_g", "g_q_a", "w_qb", "gq", "ered_q",
                "eexp_q", "g_kv_a", "gkr", "ered_r", "eexp_r", "wk", "gk", "ered_k", "eexp_k")


def _mixer_rows(x, c, conv_fn, rc, rsu, rsd, q_scale):
    xn = _rmsnorm(x, c["g_attn"][...]).astype(BF16)
    w_in = c["w_in"]

    conv_in = _dot(xn, w_in[:, OFF_CONV:OFF_GB])
    y = _silu(conv_fn(conv_in))
    qk = y[:, :2 * GDN_QK_DIM]
    qk = qk * _seg_rsqrt(qk, c["ered_g"][...], c["eexp_g"][...], 2 * GDN_HEADS)
    q_g = qk[:, :GDN_QK_DIM] * (GDN_DK ** -0.5)
    k_g = qk[:, GDN_QK_DIM:]
    v_g = y[:, 2 * GDN_QK_DIM:]

    ab = _dot(xn, w_in[:, OFF_GB:OFF_Z])
    lane = lax.broadcasted_iota(jnp.int32, ab.shape, 1)
    g_log = -jnp.exp(c["alog"][...]) * _softplus(ab + c["dtb"][...])
    gb = jnp.where(lane < BETA_LANE0, g_log, _sigmoid(ab))

    z = _dot(xn, w_in[:, OFF_Z:OFF_QA])

    qa = _dot(xn, w_in[:, OFF_QA:OFF_CKV])
    qs = _dot(_rmsnorm(qa, c["g_q_a"][...]).astype(BF16), c["w_qb"][...])
    qs = qs * _seg_rsqrt(qs, c["ered_q"][...], c["eexp_q"][...], 2 * MLA_HEADS) * c["gq"][...]
    q_mla = _rope(qs, _tile_lanes(rc, MLA_HEADS), _tile_lanes(rsu, MLA_HEADS), _tile_lanes(rsd, MLA_HEADS)) * q_scale

    ckv = _rmsnorm(_dot(xn, w_in[:, OFF_CKV:OFF_KR]), c["g_kv_a"][...])
    ckv_bf = ckv.astype(BF16)
    kr = _dot(xn, w_in[:, OFF_KR:IN_PAD])
    kr = kr * _seg_rsqrt(kr, c["ered_r"][...], c["eexp_r"][...], 1) * c["gkr"][...]
    kr = _rope(kr, rc, rsu, rsd)
    kk = _dot(ckv_bf, c["wk"][...])
    kk = kk * _seg_rsqrt(kk, c["ered_k"][...], c["eexp_k"][...], MLA_HEADS) * c["gk"][...]
    k_mla = kk + _tile_lanes(kr, MLA_HEADS)
    return dict(conv_in=conv_in, q_g=q_g, k_g=k_g, v_g=v_g, gb=gb, z=z, q_mla=q_mla, k_mla=k_mla,
                ckv=ckv, ckv_bf=ckv_bf, kr=kr)


def _prompt_mixer_kernel(tiles_per_seq, tm, x_ref, rc_ref, rsu_ref, rsd_ref, *refs):
    nc = len(MIXER_CONSTS)
    c = dict(zip(MIXER_CONSTS, refs[:nc]))
    wvt_ref = refs[nc]
    (qg_ref, kg_ref, vg_ref, gb_ref, z_ref, qm_ref, km_ref, vt_ref, ckv_ref, kro_ref, cs_ref, ext_ref) = refs[nc + 1:]
    i = pl.program_id(0)

    @pl.when(i % tiles_per_seq == 0)
    def _():
        ext_ref[5:8, :] = jnp.zeros((3, CONV_DIM), F32)

    def conv_fn(conv_in):
        w = c["w_conv"]
        ext_ref[8:8 + tm, :] = conv_in
        return (conv_in * w[3:4, :] + ext_ref[7:7 + tm, :] * w[2:3, :] + ext_ref[6:6 + tm, :] * w[1:2, :]
                + ext_ref[5:5 + tm, :] * w[0:1, :])

    r = _mixer_rows(x_ref[...], c, conv_fn, rc_ref[...], rsu_ref[...], rsd_ref[...], ATTN_SCALE * LOG2E)
    last3 = ext_ref[tm + 5:tm + 8, :]
    ext_ref[5:8, :] = last3
    cs_ref[0] = last3
    qg_ref[...] = r["q_g"]
    kg_ref[...] = r["k_g"]
    vg_ref[...] = r["v_g"]
    gb_ref[...] = r["gb"]
    z_ref[...] = r["z"]
    qm_ref[...] = r["q_mla"].astype(BF16)
    km_ref[...] = r["k_mla"].astype(BF16)
    vt_ref[...] = _dot_nt(wvt_ref[...], r["ckv_bf"]).astype(BF16)
    ckv_ref[...] = r["ckv"]
    kro_ref[...] = r["kr"][:, ROPE_LO:ROPE_LO + MLA_ROPE]


def _sample_mixer_kernel(x_ref, rc_ref, rsu_ref, rsd_ref, hist_ref, *refs):
    nc = len(MIXER_CONSTS)
    c = dict(zip(MIXER_CONSTS, refs[:nc]))
    (qg_ref, kg_ref, vg_ref, gb_ref, z_ref, qm_ref, km_ref, ckv_ref, kro_ref, cin_ref) = refs[nc:]

    def conv_fn(conv_in):
        w = c["w_conv"]
        return conv_in * w[3:4, :] + hist_ref[2] * w[2:3, :] + hist_ref[1] * w[1:2, :] + hist_ref[0] * w[0:1, :]

    r = _mixer_rows(x_ref[...], c, conv_fn, rc_ref[...], rsu_ref[...], rsd_ref[...], ATTN_SCALE)
    qg_ref[...] = r["q_g"]
    kg_ref[...] = r["k_g"]
    vg_ref[...] = r["v_g"]
    gb_ref[...] = r["gb"]
    z_ref[...] = r["z"]
    qm_ref[...] = r["q_mla"]
    km_ref[...] = r["k_mla"]
    ckv_ref[...] = r["ckv"]
    kro_ref[...] = r["kr"][:, ROPE_LO:ROPE_LO + MLA_ROPE]
    cin_ref[...] = r["conv_in"]


GROUP_HEADS = 4
GROUP_W = GROUP_HEADS * GDN_DK


def _cumsum_rows(x, period):
    row = lax.broadcasted_iota(jnp.int32, x.shape, 0) % period
    s = 1
    while s < period:
        x = x + jnp.where(row >= s, pltpu.roll(x, s, axis=0), 0.0)
        s *= 2
    return x


GDN_ITER_CHUNKS = 4


def _bd(y, bd_mask):
    return jnp.where(bd_mask, jnp.concatenate([y] * GROUP_HEADS, axis=0), 0.0)


def _gdn_solve(chains, masks):
    bd_mask, tril_cat, strict_cat, eye_cat = masks
    c = GDN_CHUNK
    n = range(len(chains))
    bd = lambda y: _bd(y, bd_mask)
    kb = [ch["k"] * ch["b_r"] for ch in chains]
    aq = [_dot_nt(jnp.concatenate([kb[i], chains[i]["q"]], axis=0), bd(chains[i]["k"])) for i in n]
    decay = []
    for ch in chains:
        g_col = jnp.sum(jnp.where(eye_cat, ch["g_r"], 0.0), axis=0, keepdims=True)
        decay.append(jnp.exp(jnp.where(tril_cat, ch["g_r"] - g_col, -jnp.inf)))
    qk = [aq[i][c:] * decay[i] for i in n]
    p = [-jnp.where(strict_cat, aq[i][:c] * decay[i], 0.0) for i in n]
    s_inv = [jnp.where(eye_cat, 1.0, 0.0) + p[i] for i in n]
    p = [_dot(p[i], bd(p[i])) for i in n]
    for lvl in range(5):
        s_next = [s_inv[i] + _dot(p[i], bd(s_inv[i])) for i in n]
        if lvl < 4:
            p = [_dot(p[i], bd(p[i])) for i in n]
        s_inv = s_next
    eg = [jnp.exp(ch["g_r"]) for ch in chains]
    u = [_dot(s_inv[i], bd(chains[i]["v"] * chains[i]["b_r"])) for i in n]
    w = [_dot(s_inv[i], bd(kb[i] * eg[i])) for i in n]
    sols = []
    for i in n:
        g_r = chains[i]["g_r"]
        g_last = g_r[c - 1:c, :]
        sols.append(dict(u=u[i], wq=jnp.concatenate([w[i], chains[i]["q"] * eg[i]], axis=0), qk=qk[i],
                         kd=chains[i]["k"] * jnp.exp(g_last - g_r), e_last=jnp.exp(g_last)))
    return sols


def _gdn_apply(sols, states, bd_mask):
    c = GDN_CHUNK
    n = range(len(sols))
    ws = [_dot(sols[i]["wq"], states[i]) for i in n]
    v_new = [sols[i]["u"] - ws[i][:c] for i in n]
    o = [ws[i][c:] + _dot(sols[i]["qk"], _bd(v_new[i], bd_mask)) for i in n]
    upd = [_dot_tn(sols[i]["kd"], v_new[i]) for i in n]
    new_states = [states[i] * sols[i]["e_last"] + jnp.where(bd_mask, upd[i], 0.0) for i in n]
    return jnp.concatenate(o, axis=1), new_states


def _gdn_kernel(nblk, tb, q_ref, k_ref, v_ref, gb_ref, z_ref, e6_ref, esum_ref, gout_ref, o_ref, sfin_ref, s_ref):
    j = pl.program_id(1)

    @pl.when(j == 0)
    def _():
        s_ref[...] = jnp.zeros(s_ref.shape, F32)

    c = GDN_CHUNK
    ngrp = GDN_HEADS // GROUP_HEADS
    rows_it = GDN_ITER_CHUNKS * c
    r_bd = lax.broadcasted_iota(jnp.int32, (GROUP_W, GROUP_W), 0) // c
    c_bd = lax.broadcasted_iota(jnp.int32, (GROUP_W, GROUP_W), 1) // c
    bd_mask = r_bd == c_bd
    ri = lax.broadcasted_iota(jnp.int32, (c, GROUP_W), 0)
    cj = lax.broadcasted_iota(jnp.int32, (c, GROUP_W), 1) % c
    masks = (bd_mask, ri >= cj, ri > cj, ri == cj)
    lane = lax.broadcasted_iota(jnp.int32, (rows_it, LANES), 1)
    copy_id = (lane % BETA_LANE0) // GDN_HEADS

    def body(it, carry):
        rows = pl.ds(pl.multiple_of(it * rows_it, rows_it), rows_it)
        gb = gb_ref[rows, :]
        sc = jnp.where(lane < BETA_LANE0, _cumsum_rows(gb, c), gb)
        p0, p1, p2 = _split3(sc)
        piece = jnp.where(copy_id == 0, p0, jnp.where(copy_id == 1, p1, p2))
        ex = _dot(piece, e6_ref[...])
        q = q_ref[rows, :]
        k = k_ref[rows, :]
        v = v_ref[rows, :]
        chains = []
        for t in range(GDN_ITER_CHUNKS):
            rs = slice(t * c, (t + 1) * c)
            for grp in range(ngrp):
                sl = slice(grp * GROUP_W, (grp + 1) * GROUP_W)
                sb = slice(GDN_WIDTH + grp * GROUP_W, GDN_WIDTH + (grp + 1) * GROUP_W)
                chains.append(dict(q=q[rs, sl], k=k[rs, sl], v=v[rs, sl], g_r=ex[rs, sl], b_r=ex[rs, sb]))
        sols = _gdn_solve(chains, masks)
        states = [s_ref[grp] for grp in range(ngrp)]
        outs = []
        for t in range(GDN_ITER_CHUNKS):
            o_t, states = _gdn_apply(sols[t * ngrp:(t + 1) * ngrp], states, bd_mask)
            outs.append(o_t)
        for grp in range(ngrp):
            s_ref[grp] = states[grp]
        o = jnp.concatenate(outs, axis=0)
        hi, lo = _split2(o * o)
        ms = _dot(hi, esum_ref[...]) + _dot(lo, esum_ref[...])
        o_ref[rows, :] = (o * lax.rsqrt(ms + EPS) * gout_ref[...] * _silu(z_ref[rows, :])).astype(o_ref.dtype)
        return carry

    lax.fori_loop(0, tb // rows_it, body, 0)

    @pl.when(j == nblk - 1)
    def _():
        for grp in range(ngrp):
            for h in range(GROUP_HEADS):
                sfin_ref[0, grp * GROUP_HEADS + h] = s_ref[grp, h * c:(h + 1) * c, h * c:(h + 1) * c]


ATT_T = 256
ATT_HEADS = 8
ATT_SUM_ROWS = 16


def _attn_kernel(q_ref, k_ref, vt_ref, o_ref):
    qi = pl.program_id(2)
    t = ATT_T
    qs = [q_ref[:, hh * SLAB:(hh + 1) * SLAB] for hh in range(ATT_HEADS)]

    def tile(j, carry, masked):
        k0 = pl.multiple_of(j * t, t)
        scores = [_dot_nt(k_ref[pl.ds(k0, t), hh * SLAB:(hh + 1) * SLAB], qs[hh]) for hh in range(ATT_HEADS)]
        stats = []
        for hh in range(ATT_HEADS):
            m, _ = carry[hh]
            s = scores[hh]
            if masked:
                kp = lax.broadcasted_iota(jnp.int32, (t, t), 0)
                qp = lax.broadcasted_iota(jnp.int32, (t, t), 1)
                s = jnp.where(qp >= kp, s, -jnp.inf)
            m_new = jnp.maximum(m, jnp.max(s, axis=0, keepdims=True))
            stats.append((m_new, jnp.exp2(m - m_new), jnp.exp2(s - m_new).astype(BF16)))
        out = []
        for hh in range(ATT_HEADS):
            m_new, alpha, p = stats[hh]
            vt = vt_ref[hh * MLA_VDIM:(hh + 1) * MLA_VDIM, pl.ds(k0, t)]
            out.append((m_new, carry[hh][1] * alpha + _dot(jnp.concatenate([vt, ones], axis=0), p)))
        return tuple(out)

    ones = jnp.ones((ATT_SUM_ROWS, t), BF16)
    init = tuple((jnp.full((1, t), -jnp.inf, F32), jnp.zeros((MLA_VDIM + ATT_SUM_ROWS, t), F32)) for _ in range(ATT_HEADS))
    carry = lax.fori_loop(0, qi, functools.partial(tile, masked=False), init)
    carry = tile(qi, carry, True)
    outs = [acc[:MLA_VDIM] / acc[MLA_VDIM:MLA_VDIM + 1] for (_, acc) in carry]
    o_ref[...] = jnp.concatenate(outs, axis=0).T.astype(o_ref.dtype)


def _tail_kernel(x_ref, og_ref, om_ref, p_ref, wo_ref, gffn_ref, wg_ref, wu_ref, wd_ref, gple_ref, wpg_ref, wpp_ref, y_ref):
    h = x_ref[...] + _dot(og_ref[...], wo_ref[:GDN_WIDTH, :]) + _dot(om_ref[...], wo_ref[GDN_WIDTH:, :])
    u = _rmsnorm(h, gffn_ref[...]).astype(BF16)
    act = (_silu(_dot(u, wg_ref[...])) * _dot(u, wu_ref[...])).astype(BF16)
    acc = h + _dot(act, wd_ref[...])
    gate = _sigmoid(_dot(_rmsnorm(acc, gple_ref[...]).astype(BF16), wpg_ref[...]))
    y_ref[...] = acc + _dot(p_ref[...].astype(BF16), wpp_ref[...]) * gate


PAGES_PER_STEP = 16
PAGES_PER_SUB = 8
DECODE_SLOTS = 4
DECODE_AHEAD = 2


def _sample_prep_kernel(qm_ref, km_ref, gk_ref, wk_ref, qabs_ref, qr_ref, sself_ref):
    qm = qm_ref[...]
    prod = qm * km_ref[...]
    qg = (qm * gk_ref[...]).astype(BF16)
    for h in range(MLA_HEADS):
        sl = slice(h * SLAB, (h + 1) * SLAB)
        qabs_ref[h] = _dot_nt(qg[:, sl], wk_ref[:, sl])
        qr_ref[h] = qm[:, h * SLAB + ROPE_LO:h * SLAB + ROPE_LO + MLA_ROPE]
        sself_ref[h] = jnp.sum(prod[:, sl], axis=1, keepdims=True)


def _paged_kernel(nsteps, page, pt_ref, qabs_ref, qr_ref, sself_ref, ckvn_ref, wkt_ref, ckv_hbm, krt_hbm, o_ref,
                  cbuf, kbuf, sems):
    pp = PAGES_PER_STEP
    b = pl.program_id(0)
    nb = pl.num_programs(0)

    def copies(row, g):
        slot = g % DECODE_SLOTS
        out = []
        for j in range(pp):
            pg = pt_ref[row, g * pp + j]
            out.append(pltpu.make_async_copy(ckv_hbm.at[pg], cbuf.at[slot, pl.ds(j * page, page), :], sems.at[slot, 0]))
            out.append(pltpu.make_async_copy(krt_hbm.at[pg], kbuf.at[slot, :, pl.ds(j * page, page)], sems.at[slot, 1]))
        return out

    @pl.when(b == 0)
    def _():
        for g in range(DECODE_AHEAD):
            for cp in copies(0, g):
                cp.start()

    qa = qabs_ref[...].astype(BF16)
    lhs = jnp.concatenate([wkt_ref[...], qa, jnp.zeros_like(qa)], axis=0)
    nk = MLA_HEADS * MLA_NOPE
    qr = qr_ref[...].astype(BF16)
    m = sself_ref[...]
    l = jnp.ones(m.shape, F32)
    acc = jnp.broadcast_to(ckvn_ref[...], (MLA_HEADS, KV_LORA))
    sub_tok = PAGES_PER_SUB * page

    def update(s, cs, m, l, acc):
        m_new = jnp.maximum(m, jnp.max(s, axis=1, keepdims=True))
        alpha = jnp.exp(m - m_new)
        p = jnp.exp(s - m_new)
        l = l * alpha + jnp.sum(p, axis=1, keepdims=True)
        return m_new, l, acc * alpha + _dot(p.astype(BF16), jnp.concatenate(cs, axis=0))

    pending = None
    for g in range(nsteps):
        slot = g % DECODE_SLOTS
        for cp in copies(b, g):
            cp.wait()
        ng = g + DECODE_AHEAD
        nrow, ng = (b, ng) if ng < nsteps else (jnp.minimum(b + 1, nb - 1), ng - nsteps)
        for cp in copies(nrow, ng):
            cp.start()
        cs, ss_list = [], []
        for i in range(pp // PAGES_PER_SUB):
            c = cbuf[slot, i * sub_tok:(i + 1) * sub_tok, :].astype(BF16)
            krt = kbuf[slot, :, i * sub_tok:(i + 1) * sub_tok].astype(BF16)
            kvq = _dot_nt(lhs, c)
            kv = kvq[:nk]
            ss = jnp.sum((kv * kv).reshape(MLA_HEADS, MLA_NOPE, kv.shape[1]), axis=1)
            ss_list.append(kvq[nk:nk + MLA_HEADS] * lax.rsqrt(ss * (1.0 / MLA_NOPE) + EPS) + _dot(qr, krt))
            cs.append(c)
        if pending is not None:
            m, l, acc = update(*pending, m, l, acc)
        pending = (jnp.concatenate(ss_list, axis=1), cs)
    m, l, acc = update(*pending, m, l, acc)
    o_ref[...] = acc / l

    @pl.when(b == nb - 1)
    def _():
        for g in range(DECODE_AHEAD):
            for cp in copies(b, g):
                cp.wait()


SAMPLE_GDN_ROWS = 4


def _sample_gdn_kernel(q_ref, k_ref, v_ref, gb_ref, s_ref, snew_ref, o_ref):
    hk = GDN_HEADS * GDN_DK
    hrow = lax.broadcasted_iota(jnp.int32, (GDN_HEADS, hk), 0)
    hlane = lax.broadcasted_iota(jnp.int32, (GDN_HEADS, hk), 1) // GDN_DK
    hm = hrow == hlane
    r8 = lax.broadcasted_iota(jnp.int32, (GDN_HEADS, LANES), 0)
    l8 = lax.broadcasted_iota(jnp.int32, (GDN_HEADS, LANES), 1)
    ones = jnp.where(hm, 1.0, 0.0).astype(BF16)
    rows = range(q_ref.shape[0])
    kmask = [jnp.where(hm, jnp.broadcast_to(k_ref[r], (GDN_HEADS, hk)), 0.0) for r in rows]
    qmask = [jnp.where(hm, jnp.broadcast_to(q_ref[r], (GDN_HEADS, hk)), 0.0) for r in rows]
    gbb = [jnp.broadcast_to(gb_ref[r], (GDN_HEADS, LANES)) for r in rows]
    g_col = [jnp.sum(jnp.where(l8 == r8, gbb[r], 0.0), axis=1, keepdims=True) for r in rows]
    b_col = [jnp.sum(jnp.where(l8 == r8 + BETA_LANE0, gbb[r], 0.0), axis=1, keepdims=True) for r in rows]

    def expand_rows(col):
        e0, e1, e2 = _split3(jnp.broadcast_to(col, (GDN_HEADS, GDN_DV)))
        return _dot_tn(jnp.concatenate([ones, ones, ones], axis=0), jnp.concatenate([e0, e1, e2], axis=0))

    def head_dot(xmask, mat):
        xh, xl = _split2(xmask)
        mh, ml = _split2(mat)
        return _dot(jnp.concatenate([xh, xh, xl], axis=1), jnp.concatenate([mh, ml, mh], axis=0))

    def outer(kmask_r, delta_r):
        kh, kl = _split2(kmask_r)
        dh, dl = _split2(delta_r)
        return _dot_tn(jnp.concatenate([kh, kh, kl], axis=0), jnp.concatenate([dh, dl, dh], axis=0))

    decay = [expand_rows(jnp.exp(g_col[r])) for r in rows]
    state = [s_ref[r] * decay[r] for r in rows]
    ks = [head_dot(kmask[r], state[r]) for r in rows]
    delta = [(v_ref[r] - ks[r]) * b_col[r] for r in rows]
    upd = [outer(kmask[r], delta[r]) for r in rows]
    state = [state[r] + upd[r] for r in rows]
    outs = [head_dot(qmask[r], state[r]) for r in rows]
    for r in rows:
        snew_ref[r] = state[r]
        o_ref[r] = outs[r]


def _sample_mix_kernel(o_ref, z_ref, olat_ref, wvbd_ref, esum_ref, gout_ref, og_ref, om_ref):
    o = o_ref[...]
    hi, lo = _split2(o * o)
    ms = _dot(hi, esum_ref[...]) + _dot(lo, esum_ref[...])
    og_ref[...] = (o * lax.rsqrt(ms + EPS) * gout_ref[...] * _silu(z_ref[...])).astype(og_ref.dtype)
    om_ref[...] = _dot(olat_ref[...].astype(BF16), wvbd_ref[...]).astype(om_ref.dtype)


def _seg_tables(width, segs, mean):
    n = len(segs)
    ered = np.zeros((width, LANES), np.float32)
    eexp = np.zeros((LANES, width), np.float32)
    for copy in range(3):
        for s, (a, b) in enumerate(segs):
            ered[a:b, copy * n + s] = 1.0 / (b - a) if mean else 1.0
            eexp[copy * n + s, a:b] = 1.0
    return jnp.asarray(ered, BF16), jnp.asarray(eexp, BF16)


def _rope_slabs(pos):
    inv_freq = ROPE_THETA ** (-jnp.arange(HALF, dtype=F32) / HALF)
    ang = pos.astype(F32)[:, None] * inv_freq[None, :]
    cos, sin = jnp.cos(ang), jnp.sin(ang)
    n = pos.shape[0]
    zeros = lambda w: jnp.zeros((n, w), F32)
    rc = jnp.concatenate([jnp.ones((n, MLA_NOPE), F32), cos, cos, zeros(SLAB - MLA_QK)], axis=1)
    rsu = jnp.concatenate([zeros(MLA_NOPE), -sin, zeros(SLAB - ROPE_HI)], axis=1)
    rsd = jnp.concatenate([zeros(ROPE_HI), sin, zeros(SLAB - MLA_QK)], axis=1)
    return rc, rsu, rsd


def _prepare_weights(g_attn, w_in, w_conv, gdn_a_log, gdn_dt_bias, g_gdn_out, g_q_a, w_q_b, g_q_nope, g_q_rope,
                     g_kv_a, g_k_rope, w_kv_b, g_k_nope):
    row = lambda v: v.reshape(1, -1).astype(F32)
    a = w_in[:, CONV_DIM:CONV_DIM + GDN_HEADS]
    b = w_in[:, CONV_DIM + GDN_HEADS:CONV_DIM + 2 * GDN_HEADS]
    o_z = CONV_DIM + 2 * GDN_HEADS
    o_qa = o_z + GDN_WIDTH
    o_kv = o_qa + Q_LORA
    zc = lambda w: jnp.zeros((D_MODEL, w), F32)
    gb_slab = jnp.concatenate([a, a, a, zc(BETA_LANE0 - 3 * GDN_HEADS), b, b, b, zc(LANES - BETA_LANE0 - 3 * GDN_HEADS)], axis=1)
    kr_slab = jnp.concatenate([zc(ROPE_LO), w_in[:, o_kv + KV_LORA:], zc(SLAB - MLA_QK)], axis=1)
    w_in_p = jnp.concatenate([w_in[:, :CONV_DIM], gb_slab, w_in[:, o_z:o_qa], w_in[:, o_qa:o_kv],
                              w_in[:, o_kv:o_kv + KV_LORA], kr_slab], axis=1).astype(BF16)

    def scalar_slab(v):
        z8 = jnp.zeros((BETA_LANE0 - 3 * GDN_HEADS,), F32)
        return jnp.concatenate([v, v, v, z8, jnp.zeros((LANES - BETA_LANE0,), F32)]).reshape(1, LANES)

    wq = w_q_b.reshape(Q_LORA, MLA_HEADS, MLA_QK)
    w_qb = jnp.concatenate([wq, jnp.zeros((Q_LORA, MLA_HEADS, SLAB - MLA_QK), F32)], axis=2).reshape(Q_LORA, QK_SLAB_W).astype(BF16)
    gq = jnp.tile(jnp.concatenate([g_q_nope, g_q_rope, jnp.zeros((SLAB - MLA_QK,), F32)]), MLA_HEADS).reshape(1, QK_SLAB_W)
    wkv = w_kv_b.reshape(KV_LORA, MLA_HEADS, MLA_NOPE + MLA_VDIM)
    wk_part, wv_part = wkv[:, :, :MLA_NOPE], wkv[:, :, MLA_NOPE:]
    wk = jnp.concatenate([wk_part, jnp.zeros((KV_LORA, MLA_HEADS, SLAB - MLA_NOPE), F32)], axis=2).reshape(KV_LORA, QK_SLAB_W).astype(BF16)
    gk = jnp.tile(jnp.concatenate([g_k_nope, jnp.zeros((SLAB - MLA_NOPE,), F32)]), MLA_HEADS).reshape(1, QK_SLAB_W)
    gkr = jnp.concatenate([jnp.zeros((ROPE_LO,), F32), g_k_rope, jnp.zeros((SLAB - MLA_QK,), F32)]).reshape(1, SLAB)
    wvt = wv_part.reshape(KV_LORA, MLA_HEADS * MLA_VDIM).T.astype(BF16)
    wkt = wk_part.reshape(KV_LORA, MLA_HEADS * MLA_NOPE).T.astype(BF16)
    wv_bd = jnp.zeros((MLA_HEADS, KV_LORA, MLA_HEADS, MLA_VDIM), F32)
    wv_bd = wv_bd.at[jnp.arange(MLA_HEADS), :, jnp.arange(MLA_HEADS), :].set(jnp.moveaxis(wv_part, 1, 0))
    wv_bd = wv_bd.reshape(MLA_HEADS * KV_LORA, MLA_HEADS * MLA_VDIM).astype(BF16)

    ered_g, eexp_g = _seg_tables(2 * GDN_QK_DIM, [(h * GDN_DK, (h + 1) * GDN_DK) for h in range(2 * GDN_HEADS)], False)
    q_segs = ([(h * SLAB, h * SLAB + MLA_NOPE) for h in range(MLA_HEADS)]
              + [(h * SLAB + ROPE_LO, h * SLAB + MLA_QK) for h in range(MLA_HEADS)])
    ered_q, eexp_q = _seg_tables(QK_SLAB_W, q_segs, True)
    ered_k, eexp_k = _seg_tables(QK_SLAB_W, q_segs[:MLA_HEADS], True)
    ered_r, eexp_r = _seg_tables(SLAB, [(ROPE_LO, MLA_QK)], True)
    e6 = np.zeros((LANES, 2 * GDN_WIDTH), np.float32)
    for copy in range(3):
        for h in range(GDN_HEADS):
            e6[copy * GDN_HEADS + h, h * GDN_DV:(h + 1) * GDN_DV] = 1.0
            e6[BETA_LANE0 + copy * GDN_HEADS + h, GDN_WIDTH + h * GDN_DV:GDN_WIDTH + (h + 1) * GDN_DV] = 1.0
    esum = np.kron(np.eye(GDN_HEADS, dtype=np.float32), np.full((GDN_DV, GDN_DV), 1.0 / GDN_DV, np.float32))

    mixer = dict(g_attn=row(g_attn), w_in=w_in_p, w_conv=w_conv.astype(F32), alog=scalar_slab(gdn_a_log),
                 dtb=scalar_slab(gdn_dt_bias), ered_g=ered_g, eexp_g=eexp_g, g_q_a=row(g_q_a), w_qb=w_qb, gq=gq,
                 ered_q=ered_q, eexp_q=eexp_q, g_kv_a=row(g_kv_a), gkr=gkr, ered_r=ered_r, eexp_r=eexp_r, wk=wk, gk=gk,
                 ered_k=ered_k, eexp_k=eexp_k)
    extra = dict(wvt=wvt, wkt=wkt, wv_bd=wv_bd, e6=jnp.asarray(e6, BF16), esum=jnp.asarray(esum, BF16),
                 gout=jnp.tile(g_gdn_out, GDN_HEADS).reshape(1, GDN_WIDTH).astype(F32))
    return mixer, extra


def _params(sem=None):
    return pltpu.CompilerParams(dimension_semantics=sem, vmem_limit_bytes=VMEM_LIMIT)


def _prompt_mixer(x2, slabs, mixer, wvt, bsz, seq, tm):
    n = x2.shape[0]
    tps = seq // tm
    consts = [mixer[k] for k in MIXER_CONSTS] + [wvt]
    row_spec = lambda w: pl.BlockSpec((tm, w), lambda i: (i, 0))
    rope_spec = pl.BlockSpec((tm, SLAB), lambda i: (i % tps, 0))
    out_shapes = [
        jax.ShapeDtypeStruct((n, GDN_QK_DIM), F32), jax.ShapeDtypeStruct((n, GDN_QK_DIM), F32),
        jax.ShapeDtypeStruct((n, GDN_WIDTH), F32), jax.ShapeDtypeStruct((n, LANES), F32),
        jax.ShapeDtypeStruct((n, GDN_WIDTH), F32), jax.ShapeDtypeStruct((n, QK_SLAB_W), BF16),
        jax.ShapeDtypeStruct((n, QK_SLAB_W), BF16), jax.ShapeDtypeStruct((MLA_HEADS * MLA_VDIM, n), BF16),
        jax.ShapeDtypeStruct((n, KV_LORA), F32), jax.ShapeDtypeStruct((n, MLA_ROPE), F32),
        jax.ShapeDtypeStruct((bsz, CONV_WIDTH - 1, CONV_DIM), F32)]
    out_specs = [row_spec(GDN_QK_DIM), row_spec(GDN_QK_DIM), row_spec(GDN_WIDTH), row_spec(LANES), row_spec(GDN_WIDTH),
                 row_spec(QK_SLAB_W), row_spec(QK_SLAB_W), pl.BlockSpec((MLA_HEADS * MLA_VDIM, tm), lambda i: (0, i)),
                 row_spec(KV_LORA), row_spec(MLA_ROPE),
                 pl.BlockSpec((1, CONV_WIDTH - 1, CONV_DIM), lambda i: (i // tps, 0, 0))]
    return pl.pallas_call(
        functools.partial(_prompt_mixer_kernel, tps, tm),
        grid=(n // tm,),
        in_specs=[row_spec(D_MODEL), rope_spec, rope_spec, rope_spec] + [_const_spec(a, 1) for a in consts],
        out_specs=out_specs, out_shape=out_shapes,
        scratch_shapes=[pltpu.VMEM((tm + 8, CONV_DIM), F32)],
        compiler_params=_params(("arbitrary",)), name="prompt_mixer",
    )(x2, *slabs, *consts)


def _sample_mixer(x2, slabs, hist, mixer):
    n = x2.shape[0]
    consts = [mixer[k] for k in MIXER_CONSTS]
    sd = lambda w: jax.ShapeDtypeStruct((n, w), F32)
    return pl.pallas_call(
        _sample_mixer_kernel,
        out_shape=[sd(GDN_QK_DIM), sd(GDN_QK_DIM), sd(GDN_WIDTH), sd(LANES), sd(GDN_WIDTH), sd(QK_SLAB_W), sd(QK_SLAB_W),
                   sd(KV_LORA), sd(MLA_ROPE), sd(CONV_DIM)],
        compiler_params=_params(), name="sample_mixer",
    )(x2, *slabs, hist, *consts)


def _gdn_prompt(qg, kg, vg, gb, z, extra, bsz, seq, tb):
    n = qg.shape[0]
    nblk = seq // tb
    row_spec = lambda w: pl.BlockSpec((tb, w), lambda b, j: (b * nblk + j, 0))
    consts = [extra["e6"], extra["esum"], extra["gout"]]
    return pl.pallas_call(
        functools.partial(_gdn_kernel, nblk, tb),
        grid=(bsz, nblk),
        in_specs=[row_spec(GDN_QK_DIM), row_spec(GDN_QK_DIM), row_spec(GDN_WIDTH), row_spec(LANES), row_spec(GDN_WIDTH)]
        + [_const_spec(a, 2) for a in consts],
        out_specs=[row_spec(GDN_WIDTH), pl.BlockSpec((1, GDN_HEADS, GDN_DK, GDN_DV), lambda b, j: (b, 0, 0, 0))],
        out_shape=[jax.ShapeDtypeStruct((n, GDN_WIDTH), BF16), jax.ShapeDtypeStruct((bsz, GDN_HEADS, GDN_DK, GDN_DV), F32)],
        scratch_shapes=[pltpu.VMEM((GDN_HEADS // GROUP_HEADS, GROUP_W, GROUP_W), F32)],
        compiler_params=_params(("arbitrary", "arbitrary")), name="gdn_chunked",
    )(qg, kg, vg, gb, z, *consts)


def _attn_prompt(qm, km, vt, bsz, seq):
    n = qm.shape[0]
    t = ATT_T
    nq = seq // t
    return pl.pallas_call(
        _attn_kernel,
        grid=(bsz, MLA_HEADS // ATT_HEADS, nq),
        in_specs=[pl.BlockSpec((t, ATT_HEADS * SLAB), lambda b, hp, qi: (b * nq + qi, hp)),
                  pl.BlockSpec((seq, ATT_HEADS * SLAB), lambda b, hp, qi: (b, hp)),
                  pl.BlockSpec((ATT_HEADS * MLA_VDIM, seq), lambda b, hp, qi: (hp, b))],
        out_specs=pl.BlockSpec((t, ATT_HEADS * MLA_VDIM), lambda b, hp, qi: (b * nq + qi, hp)),
        out_shape=jax.ShapeDtypeStruct((n, MLA_HEADS * MLA_VDIM), BF16),
        compiler_params=_params(("arbitrary", "arbitrary", "arbitrary")), name="mla_prompt_attention",
    )(qm, km, vt)


def _tail(x2, og, om, p2, tailw, tm):
    n = x2.shape[0]
    row_spec = lambda w: pl.BlockSpec((tm, w), lambda i: (i, 0))
    return pl.pallas_call(
        _tail_kernel,
        grid=(n // tm,),
        in_specs=[row_spec(D_MODEL), row_spec(GDN_WIDTH), row_spec(GDN_WIDTH), row_spec(PLE_DIM)]
        + [_const_spec(a, 1) for a in tailw],
        out_specs=row_spec(D_MODEL), out_shape=jax.ShapeDtypeStruct((n, D_MODEL), F32),
        compiler_params=_params(("arbitrary",)), name="layer_tail",
    )(x2, og, om, p2, *tailw)


def _sample_prep(qm, km, gk, wk):
    n = qm.shape[0]
    return pl.pallas_call(
        _sample_prep_kernel,
        out_shape=[jax.ShapeDtypeStruct((MLA_HEADS, n, KV_LORA), F32), jax.ShapeDtypeStruct((MLA_HEADS, n, MLA_ROPE), F32),
                   jax.ShapeDtypeStruct((MLA_HEADS, n, 1), F32)],
        compiler_params=_params(), name="sample_prep",
    )(qm, km, gk, wk)


def _paged_attention(page_table, pool_ckv, pool_kr, qabs, qr, sself, ckv_new, wkt):
    bs, n_pages = page_table.shape
    page = pool_ckv.shape[1]
    pp = PAGES_PER_STEP
    nsteps = n_pages // pp
    assert n_pages % pp == 0 and nsteps % DECODE_SLOTS == 0 and DECODE_AHEAD < DECODE_SLOTS, "page groups must fill whole buffer rings"
    per_b = lambda w: pl.BlockSpec((None, MLA_HEADS, w), lambda b, pt: (b, 0, 0))
    grid_spec = pltpu.PrefetchScalarGridSpec(
        num_scalar_prefetch=1, grid=(bs,),
        in_specs=[per_b(KV_LORA), per_b(MLA_ROPE), per_b(1),
                  pl.BlockSpec((None, 1, KV_LORA), lambda b, pt: (b, 0, 0)),
                  pl.BlockSpec(wkt.shape, lambda b, pt: (0, 0)),
                  pl.BlockSpec(memory_space=pl.ANY), pl.BlockSpec(memory_space=pl.ANY)],
        out_specs=pl.BlockSpec((None, MLA_HEADS, KV_LORA), lambda b, pt: (b, 0, 0)),
        scratch_shapes=[pltpu.VMEM((DECODE_SLOTS, pp * page, KV_LORA), F32), pltpu.VMEM((DECODE_SLOTS, MLA_ROPE, pp * page), F32),
                        pltpu.SemaphoreType.DMA((DECODE_SLOTS, 2))])
    return pl.pallas_call(
        functools.partial(_paged_kernel, nsteps, page), grid_spec=grid_spec,
        out_shape=jax.ShapeDtypeStruct((bs, MLA_HEADS, KV_LORA), F32),
        compiler_params=_params(("arbitrary",)), name="mla_paged_decode",
    )(page_table, qabs, qr, sself, ckv_new, wkt, pool_ckv, pool_kr)


def _sample_gdn(q3, k3, v3, gb3, state):
    bs = q3.shape[0]
    hk = GDN_HEADS * GDN_DK
    rb = _pick(bs, (SAMPLE_GDN_ROWS, 2, 1))
    b3 = lambda r, w: pl.BlockSpec((rb, r, w), lambda b: (b, 0, 0))
    return pl.pallas_call(
        _sample_gdn_kernel, grid=(bs // rb,),
        in_specs=[b3(1, hk), b3(1, hk), b3(GDN_HEADS, GDN_DV), b3(1, LANES), b3(hk, GDN_DV)],
        out_specs=[b3(hk, GDN_DV), b3(GDN_HEADS, GDN_DV)],
        out_shape=[jax.ShapeDtypeStruct((bs, hk, GDN_DV), F32), jax.ShapeDtypeStruct((bs, GDN_HEADS, GDN_DV), F32)],
        compiler_params=_params(("arbitrary",)), name="gdn_recurrent_step",
    )(q3, k3, v3, gb3, state)


def _sample_mix(o2, z, olat2, extra):
    n = o2.shape[0]
    return pl.pallas_call(
        _sample_mix_kernel,
        out_shape=[jax.ShapeDtypeStruct((n, GDN_WIDTH), BF16), jax.ShapeDtypeStruct((n, MLA_HEADS * MLA_VDIM), BF16)],
        compiler_params=_params(), name="sample_mix",
    )(o2, z, olat2, extra["wv_bd"], extra["esum"], extra["gout"])


def _pick(n, prefs):
    for t in prefs:
        if n % t == 0:
            return t
    return n


def kernel(x_prompt, x_sample, cache_ckv, cache_krope, state_gdn, state_conv, page_table, p_prompt, p_sample, g_attn, w_in, w_conv, gdn_a_log, gdn_dt_bias, g_gdn_out, g_q_a, w_q_b, g_q_nope, g_q_rope, g_kv_a, g_k_rope, w_kv_b, g_k_nope, w_o, g_ffn, w_ffn_gate, w_ffn_up, w_ffn_down, g_ple, w_ple_gate, w_ple_proj):
    depth = g_attn.shape[0]
    assert depth == 1 and x_sample.shape[1] == 1, "single layer, single new token per sample row"
    bp, seq, _ = x_prompt.shape
    bs = x_sample.shape[0]
    past = page_table.shape[1] * cache_ckv.shape[2]
    li = 0
    mixer, extra = _prepare_weights(g_attn[li], w_in[li], w_conv[li], gdn_a_log[li], gdn_dt_bias[li], g_gdn_out[li],
                                    g_q_a[li], w_q_b[li], g_q_nope[li], g_q_rope[li], g_kv_a[li], g_k_rope[li],
                                    w_kv_b[li], g_k_nope[li])
    row = lambda v: v.reshape(1, -1).astype(F32)
    tailw = [w_o[li].astype(BF16), row(g_ffn[li]), w_ffn_gate[li].astype(BF16), w_ffn_up[li].astype(BF16),
             w_ffn_down[li].astype(BF16), row(g_ple[li]), w_ple_gate[li].astype(BF16), w_ple_proj[li].astype(BF16)]

    n = bp * seq
    xp2 = x_prompt.reshape(n, D_MODEL)
    tm = _pick(seq, (256, 128, 64, 32, 16, 8))
    (qg, kg, vg, gb, z, qm, km, vt, ckv_p, kr_p, conv_p) = _prompt_mixer(
        xp2, _rope_slabs(jnp.arange(seq)), mixer, extra["wvt"], bp, seq, tm)
    tb = _pick(seq, (512, 256, 128, 64))
    og, gdn_p = _gdn_prompt(qg, kg, vg, gb, z, extra, bp, seq, tb)
    om = _attn_prompt(qm, km, vt, bp, seq)
    y_prompt = _tail(xp2, og, om, p_prompt[li].reshape(n, PLE_DIM), tailw, tm).reshape(bp, seq, D_MODEL)

    xs2 = x_sample.reshape(bs, D_MODEL)
    hist = jnp.moveaxis(state_conv[li], 1, 0)
    (qg_s, kg_s, vg_s, gb_s, z_s, qm_s, km_s, ckv_s, kr_s, cin_s) = _sample_mixer(
        xs2, _rope_slabs(past + jnp.arange(1)), hist, mixer)
    qabs, qr, sself = _sample_prep(qm_s, km_s, mixer["gk"], mixer["wk"])
    per_row = lambda t: jnp.swapaxes(t, 0, 1)
    pool_krt = jnp.swapaxes(cache_krope[li], 1, 2)
    olat = _paged_attention(page_table, cache_ckv[li], pool_krt, per_row(qabs), per_row(qr), per_row(sself),
                            ckv_s.reshape(bs, 1, KV_LORA), extra["wkt"])
    hk = GDN_HEADS * GDN_DK
    s_new, o_s = _sample_gdn(qg_s.reshape(bs, 1, hk), kg_s.reshape(bs, 1, hk), vg_s.reshape(bs, GDN_HEADS, GDN_DV),
                             gb_s.reshape(bs, 1, LANES), state_gdn[li].reshape(bs, hk, GDN_DV))
    og_s, om_s = _sample_mix(o_s.reshape(bs, GDN_WIDTH), z_s, olat.reshape(bs, MLA_HEADS * KV_LORA), extra)
    y_sample = _tail(xs2, og_s, om_s, p_sample[li].reshape(bs, PLE_DIM), tailw, bs).reshape(bs, 1, D_MODEL)
    conv_s = jnp.concatenate([state_conv[li][:, 1:], cin_s[:, None, :]], axis=1)

    return (y_prompt, y_sample,
            ckv_p.reshape(1, bp, seq, KV_LORA), kr_p.reshape(1, bp, seq, MLA_ROPE),
            gdn_p[None], conv_p[None],
            ckv_s.reshape(1, bs, 1, KV_LORA), kr_s.reshape(1, bs, 1, MLA_ROPE),
            s_new.reshape(1, bs, GDN_HEADS, GDN_DK, GDN_DV), conv_s[None])
```

```python
import functools
import math

import numpy as np
import jax
import jax.numpy as jnp
from jax import lax
from jax.experimental import pallas as pl
from jax.experimental.pallas import tpu as pltpu

F32 = jnp.float32
BF16 = jnp.bfloat16

D_MODEL = 1024
PLE_DIM = 256
GDN_HEADS = 8
GDN_DK = 64
GDN_DV = 64
GDN_WIDTH = GDN_HEADS * GDN_DV
GDN_QK_DIM = GDN_HEADS * GDN_DK
CONV_WIDTH = 4
CONV_DIM = 2 * GDN_QK_DIM + GDN_WIDTH
GDN_CHUNK = 64
MLA_HEADS = 8
MLA_NOPE = 64
MLA_ROPE = 32
MLA_VDIM = 64
MLA_QK = MLA_NOPE + MLA_ROPE
Q_LORA = 384
KV_LORA = 256
ROPE_THETA = 10000.0
ATTN_SCALE = MLA_QK ** -0.5
LOG2E = math.log2(math.e)
D_FF = 2816
EPS = 1e-6

LANES = 128
SLAB = 128
HALF = MLA_ROPE // 2
ROPE_LO = MLA_NOPE
ROPE_HI = MLA_NOPE + HALF
QK_SLAB_W = MLA_HEADS * SLAB

OFF_CONV = 0
OFF_GB = OFF_CONV + CONV_DIM
OFF_Z = OFF_GB + LANES
OFF_QA = OFF_Z + GDN_WIDTH
OFF_CKV = OFF_QA + Q_LORA
OFF_KR = OFF_CKV + KV_LORA
IN_PAD = OFF_KR + LANES
BETA_LANE0 = 32

VMEM_LIMIT = 56 * 1024 * 1024


def _dot(a, b):
    return jnp.dot(a, b, preferred_element_type=F32)


def _dot_nt(a, b):
    return lax.dot_general(a, b, (((1,), (1,)), ((), ())), preferred_element_type=F32)


def _dot_tn(a, b):
    return lax.dot_general(a, b, (((0,), (0,)), ((), ())), preferred_element_type=F32)


def _split2(x):
    hi = x.astype(BF16)
    lo = (x - hi.astype(F32)).astype(BF16)
    return hi, lo


def _split3(x):
    hi = x.astype(BF16)
    r1 = x - hi.astype(F32)
    mid = r1.astype(BF16)
    lo = (r1 - mid.astype(F32)).astype(BF16)
    return hi, mid, lo


def _sigmoid(x):
    return 1.0 / (1.0 + jnp.exp(-x))


def _silu(x):
    return x * _sigmoid(x)


def _softplus(x):
    return jnp.maximum(x, 0.0) + jnp.log1p(jnp.exp(-jnp.abs(x)))


def _rmsnorm(x, g):
    return x * lax.rsqrt(jnp.mean(x * x, axis=-1, keepdims=True) + EPS) * g


def _seg_rsqrt(x, ered, eexp, nseg):
    red = _dot((x * x).astype(BF16), ered)
    r = lax.rsqrt(red + EPS)
    p0, p1, p2 = _split3(r)
    lane = lax.broadcasted_iota(jnp.int32, r.shape, 1)
    piece = jnp.where(lane < nseg, p0, jnp.where(lane < 2 * nseg, p1, p2))
    return _dot(piece, eexp)


def _rope(x, c, s_up, s_dn):
    w = x.shape[-1]
    up = pltpu.roll(x, w - HALF, axis=1)
    dn = pltpu.roll(x, HALF, axis=1)
    return x * c + up * s_up + dn * s_dn


def _tile_lanes(x, n):
    return jnp.concatenate([x] * n, axis=1)


def _const_spec(arr, ngrid):
    nd = arr.ndim
    return pl.BlockSpec(arr.shape, lambda *a, _nd=nd: (0,) * _nd, pipeline_mode=pl.Buffered(1))


MIXER_BLOCKS = 2
MIXER_CONSTS = ("g_attn", "w_in", "w_conv", "alog", "dtb", "ered_g", "eexp_g", "g_q_a", "w_qb", "gq", "ered_q",
                "eexp_q", "g_kv_a", "gkr", "ered_r", "eexp_r", "wk", "gk", "ered_k", "eexp_k")


def _mixer_rows(xs, c, conv_fn, ropes, q_scale):
    n = range(len(xs))
    w_in = c["w_in"]
    xn = [_rmsnorm(x, c["g_attn"][...]).astype(BF16) for x in xs]

    conv_in = [_dot(xn[i], w_in[:, OFF_CONV:OFF_GB]) for i in n]
    y = [_silu(v) for v in conv_fn(conv_in)]
    qk = [y[i][:, :2 * GDN_QK_DIM] for i in n]
    qk = [qk[i] * _seg_rsqrt(qk[i], c["ered_g"][...], c["eexp_g"][...], 2 * GDN_HEADS) for i in n]

    ab = [_dot(xn[i], w_in[:, OFF_GB:OFF_Z]) for i in n]
    gb = []
    for i in n:
        lane = lax.broadcasted_iota(jnp.int32, ab[i].shape, 1)
        g_log = -jnp.exp(c["alog"][...]) * _softplus(ab[i] + c["dtb"][...])
        gb.append(jnp.where(lane < BETA_LANE0, g_log, _sigmoid(ab[i])))

    z = [_dot(xn[i], w_in[:, OFF_Z:OFF_QA]) for i in n]

    qa = [_dot(xn[i], w_in[:, OFF_QA:OFF_CKV]) for i in n]
    qs = [_dot(_rmsnorm(qa[i], c["g_q_a"][...]).astype(BF16), c["w_qb"][...]) for i in n]
    qs = [qs[i] * _seg_rsqrt(qs[i], c["ered_q"][...], c["eexp_q"][...], 2 * MLA_HEADS) * c["gq"][...] for i in n]
    q_mla = [_rope(qs[i], *[_tile_lanes(t, MLA_HEADS) for t in ropes[i]]) * q_scale for i in n]

    ckv = [_rmsnorm(_dot(xn[i], w_in[:, OFF_CKV:OFF_KR]), c["g_kv_a"][...]) for i in n]
    ckv_bf = [v.astype(BF16) for v in ckv]
    kr = [_dot(xn[i], w_in[:, OFF_KR:IN_PAD]) for i in n]
    kr = [kr[i] * _seg_rsqrt(kr[i], c["ered_r"][...], c["eexp_r"][...], 1) * c["gkr"][...] for i in n]
    kr = [_rope(kr[i], *ropes[i]) for i in n]
    kk = [_dot(ckv_bf[i], c["wk"][...]) for i in n]
    kk = [kk[i] * _seg_rsqrt(kk[i], c["ered_k"][...], c["eexp_k"][...], MLA_HEADS) * c["gk"][...] for i in n]
    return [dict(conv_in=conv_in[i], q_g=qk[i][:, :GDN_QK_DIM] * (GDN_DK ** -0.5), k_g=qk[i][:, GDN_QK_DIM:],
                 v_g=y[i][:, 2 * GDN_QK_DIM:], gb=gb[i], z=z[i], q_mla=q_mla[i],
                 k_mla=kk[i] + _tile_lanes(kr[i], MLA_HEADS), ckv=ckv[i], ckv_bf=ckv_bf[i], kr=kr[i]) for i in n]


def _prompt_mixer_kernel(tiles_per_seq, tm, x_ref, rc_ref, rsu_ref, rsd_ref, *refs):
    nc = len(MIXER_CONSTS)
    c = dict(zip(MIXER_CONSTS, refs[:nc]))
    wvt_ref = refs[nc]
    (qg_ref, kg_ref, vg_ref, gb_ref, z_ref, qm_ref, km_ref, vt_ref, ckv_ref, kro_ref, cs_ref, ext_ref) = refs[nc + 1:]
    i = pl.program_id(0)
    hm = tm // MIXER_BLOCKS
    blocks = [slice(j * hm, (j + 1) * hm) for j in range(MIXER_BLOCKS)]

    @pl.when(i % tiles_per_seq == 0)
    def _():
        ext_ref[5:8, :] = jnp.zeros((3, CONV_DIM), F32)

    def conv_fn(conv_ins):
        w = c["w_conv"]
        for sl, v in zip(blocks, conv_ins):
            ext_ref[8 + sl.start:8 + sl.stop, :] = v
        return [v * w[3:4, :] + ext_ref[7 + sl.start:7 + sl.stop, :] * w[2:3, :]
                + ext_ref[6 + sl.start:6 + sl.stop, :] * w[1:2, :] + ext_ref[5 + sl.start:5 + sl.stop, :] * w[0:1, :]
                for sl, v in zip(blocks, conv_ins)]

    res = _mixer_rows([x_ref[sl, :] for sl in blocks], c, conv_fn,
                      [(rc_ref[sl, :], rsu_ref[sl, :], rsd_ref[sl, :]) for sl in blocks], ATTN_SCALE * LOG2E)
    last3 = ext_ref[tm + 5:tm + 8, :]
    ext_ref[5:8, :] = last3
    cs_ref[0] = last3
    for sl, r in zip(blocks, res):
        qg_ref[sl, :] = r["q_g"]
        kg_ref[sl, :] = r["k_g"]
        vg_ref[sl, :] = r["v_g"]
        gb_ref[sl, :] = r["gb"]
        z_ref[sl, :] = r["z"]
        qm_ref[sl, :] = r["q_mla"].astype(BF16)
        km_ref[sl, :] = r["k_mla"].astype(BF16)
        vt_ref[:, sl] = _dot_nt(wvt_ref[...], r["ckv_bf"]).astype(BF16)
        ckv_ref[sl, :] = r["ckv"]
        kro_ref[sl, :] = r["kr"][:, ROPE_LO:ROPE_LO + MLA_ROPE]


def _sample_mixer_kernel(x_ref, rc_ref, rsu_ref, rsd_ref, hist_ref, *refs):
    nc = len(MIXER_CONSTS)
    c = dict(zip(MIXER_CONSTS, refs[:nc]))
    (qg_ref, kg_ref, vg_ref, gb_ref, z_ref, qm_ref, km_ref, ckv_ref, kro_ref, cin_ref) = refs[nc:]

    def conv_fn(conv_ins):
        w = c["w_conv"]
        return [v * w[3:4, :] + hist_ref[2] * w[2:3, :] + hist_ref[1] * w[1:2, :] + hist_ref[0] * w[0:1, :] for v in conv_ins]

    (r,) = _mixer_rows([x_ref[...]], c, conv_fn, [(rc_ref[...], rsu_ref[...], rsd_ref[...])], ATTN_SCALE)
    qg_ref[...] = r["q_g"]
    kg_ref[...] = r["k_g"]
    vg_ref[...] = r["v_g"]
    gb_ref[...] = r["gb"]
    z_ref[...] = r["z"]
    qm_ref[...] = r["q_mla"]
    km_ref[...] = r["k_mla"]
    ckv_ref[...] = r["ckv"]
    kro_ref[...] = r["kr"][:, ROPE_LO:ROPE_LO + MLA_ROPE]
    cin_ref[...] = r["conv_in"]


GROUP_HEADS = 4
GROUP_W = GROUP_HEADS * GDN_DK


def _cumsum_rows(x, period):
    row = lax.broadcasted_iota(jnp.int32, x.shape, 0) % period
    s = 1
    while s < period:
        x = x + jnp.where(row >= s, pltpu.roll(x, s, axis=0), 0.0)
        s *= 2
    return x


GDN_ITER_CHUNKS = 4


def _bd(y, bd_mask):
    return jnp.where(bd_mask, jnp.concatenate([y] * GROUP_HEADS, axis=0), 0.0)


def _gdn_solve_stages(chains, masks, sols):
    bd_mask, tril_cat, strict_cat, eye_cat = masks
    c = GDN_CHUNK
    n = range(len(chains))
    bd = lambda y: _bd(y, bd_mask)
    kb = [ch["k"] * ch["b_r"] for ch in chains]
    aq = [_dot_nt(jnp.concatenate([kb[i], chains[i]["q"]], axis=0), bd(chains[i]["k"])) for i in n]
    yield
    decay = []
    for ch in chains:
        g_col = jnp.sum(jnp.where(eye_cat, ch["g_r"], 0.0), axis=0, keepdims=True)
        decay.append(jnp.exp(jnp.where(tril_cat, ch["g_r"] - g_col, -jnp.inf)))
    qk = [aq[i][c:] * decay[i] for i in n]
    p = [-jnp.where(strict_cat, aq[i][:c] * decay[i], 0.0) for i in n]
    s_inv = [jnp.where(eye_cat, 1.0, 0.0) + p[i] for i in n]
    p = [_dot(p[i], bd(p[i])) for i in n]
    yield
    for lvl in range(5):
        s_next = [s_inv[i] + _dot(p[i], bd(s_inv[i])) for i in n]
        yield
        if lvl < 4:
            p = [_dot(p[i], bd(p[i])) for i in n]
            yield
        s_inv = s_next
    eg = [jnp.exp(ch["g_r"]) for ch in chains]
    u = [_dot(s_inv[i], bd(chains[i]["v"] * chains[i]["b_r"])) for i in n]
    yield
    w = [_dot(s_inv[i], bd(kb[i] * eg[i])) for i in n]
    for i in n:
        g_r = chains[i]["g_r"]
        g_last = g_r[c - 1:c, :]
        sols.append(dict(u=u[i], wq=jnp.concatenate([w[i], chains[i]["q"] * eg[i]], axis=0), qk=qk[i],
                         kd=chains[i]["k"] * jnp.exp(g_last - g_r), e_last=jnp.exp(g_last)))


def _gdn_apply_stages(sols, states, bd_mask, box):
    c = GDN_CHUNK
    ngrp = len(states)
    n = range(ngrp)
    outs = []
    for t in range(len(sols) // ngrp):
        sl = sols[t * ngrp:(t + 1) * ngrp]
        ws = [_dot(sl[i]["wq"], states[i]) for i in n]
        yield
        v_new = [sl[i]["u"] - ws[i][:c] for i in n]
        o = [ws[i][c:] + _dot(sl[i]["qk"], _bd(v_new[i], bd_mask)) for i in n]
        upd = [_dot_tn(sl[i]["kd"], v_new[i]) for i in n]
        yield
        states = [states[i] * sl[i]["e_last"] + jnp.where(bd_mask, upd[i], 0.0) for i in n]
        outs.append(jnp.concatenate(o, axis=1))
    box["o"] = jnp.concatenate(outs, axis=0)
    box["states"] = states


def _interleave(stage_generators):
    live = list(stage_generators)
    while live:
        for g in list(live):
            try:
                next(g)
            except StopIteration:
                live.remove(g)


def _gdn_kernel(nblk, tb, q_ref, k_ref, v_ref, gb_ref, z_ref, e6_ref, esum_ref, gout_ref, o_ref, sfin_ref, s_ref):
    j = pl.program_id(1)

    @pl.when(j == 0)
    def _():
        s_ref[...] = jnp.zeros(s_ref.shape, F32)

    c = GDN_CHUNK
    ngrp = GDN_HEADS // GROUP_HEADS
    rows_it = GDN_ITER_CHUNKS * c
    r_bd = lax.broadcasted_iota(jnp.int32, (GROUP_W, GROUP_W), 0) // c
    c_bd = lax.broadcasted_iota(jnp.int32, (GROUP_W, GROUP_W), 1) // c
    bd_mask = r_bd == c_bd
    ri = lax.broadcasted_iota(jnp.int32, (c, GROUP_W), 0)
    cj = lax.broadcasted_iota(jnp.int32, (c, GROUP_W), 1) % c
    masks = (bd_mask, ri >= cj, ri > cj, ri == cj)
    lane = lax.broadcasted_iota(jnp.int32, (rows_it, LANES), 1)
    copy_id = (lane % BETA_LANE0) // GDN_HEADS

    def chains_of(rows):
        gb = gb_ref[rows, :]
        sc = jnp.where(lane < BETA_LANE0, _cumsum_rows(gb, c), gb)
        p0, p1, p2 = _split3(sc)
        piece = jnp.where(copy_id == 0, p0, jnp.where(copy_id == 1, p1, p2))
        ex = _dot(piece, e6_ref[...])
        q = q_ref[rows, :]
        k = k_ref[rows, :]
        v = v_ref[rows, :]
        chains = []
        for t in range(GDN_ITER_CHUNKS):
            rs = slice(t * c, (t + 1) * c)
            for grp in range(ngrp):
                sl = slice(grp * GROUP_W, (grp + 1) * GROUP_W)
                sb = slice(GDN_WIDTH + grp * GROUP_W, GDN_WIDTH + (grp + 1) * GROUP_W)
                chains.append(dict(q=q[rs, sl], k=k[rs, sl], v=v[rs, sl], g_r=ex[rs, sl], b_r=ex[rs, sb]))
        return chains

    def finish(o, rows):
        hi, lo = _split2(o * o)
        ms = _dot(hi, esum_ref[...]) + _dot(lo, esum_ref[...])
        o_ref[rows, :] = (o * lax.rsqrt(ms + EPS) * gout_ref[...] * _silu(z_ref[rows, :])).astype(o_ref.dtype)

    states = [s_ref[grp] for grp in range(ngrp)]
    pending = None
    for gi in range(tb // rows_it):
        rows = slice(gi * rows_it, (gi + 1) * rows_it)
        sols, box = [], {}
        stages = [_gdn_solve_stages(chains_of(rows), masks, sols)]
        if pending is not None:
            stages.append(_gdn_apply_stages(pending[0], states, bd_mask, box))
        _interleave(stages)
        if pending is not None:
            states = box["states"]
            finish(box["o"], pending[1])
        pending = (sols, rows)
    box = {}
    _interleave([_gdn_apply_stages(pending[0], states, bd_mask, box)])
    finish(box["o"], pending[1])
    for grp in range(ngrp):
        s_ref[grp] = box["states"][grp]

    @pl.when(j == nblk - 1)
    def _():
        for grp in range(ngrp):
            for h in range(GROUP_HEADS):
                sfin_ref[0, grp * GROUP_HEADS + h] = s_ref[grp, h * c:(h + 1) * c, h * c:(h + 1) * c]


ATT_T = 256
ATT_HEADS = 8
ATT_SUM_ROWS = 16
ATT_WIDE = 2


def _attn_kernel(q_ref, k_ref, vt_ref, o_ref):
    qi = pl.program_id(2)
    t = ATT_T
    qs = [q_ref[:, hh * SLAB:(hh + 1) * SLAB] for hh in range(ATT_HEADS)]

    def score_tile(k0, tk):
        return tuple(_dot_nt(k_ref[pl.ds(k0, tk), hh * SLAB:(hh + 1) * SLAB], qs[hh]) for hh in range(ATT_HEADS))

    def consume(k0, tk, carry, masked):
        scores = score_tile(k0, tk)
        stats = []
        for hh in range(ATT_HEADS):
            m, _ = carry[hh]
            s = scores[hh]
            if masked:
                kp = lax.broadcasted_iota(jnp.int32, (tk, t), 0)
                qp = lax.broadcasted_iota(jnp.int32, (tk, t), 1)
                s = jnp.where(qp >= kp, s, -jnp.inf)
            m_new = jnp.maximum(m, jnp.max(s, axis=0, keepdims=True))
            stats.append((m_new, jnp.exp2(m - m_new), jnp.exp2(s - m_new).astype(BF16)))
        out = []
        for hh in range(ATT_HEADS):
            m_new, alpha, p = stats[hh]
            vt = vt_ref[hh * MLA_VDIM:(hh + 1) * MLA_VDIM, pl.ds(k0, tk)]
            ones = jnp.ones((ATT_SUM_ROWS, tk), BF16)
            out.append((m_new, carry[hh][1] * alpha + _dot(jnp.concatenate([vt, ones], axis=0), p)))
        return tuple(out)

    init = tuple((jnp.full((1, t), -jnp.inf, F32), jnp.zeros((MLA_VDIM + ATT_SUM_ROWS, t), F32)) for _ in range(ATT_HEADS))
    wide = ATT_WIDE * t
    nwide = qi // ATT_WIDE
    carry = lax.fori_loop(0, nwide, lambda j, c: consume(pl.multiple_of(j * wide, wide), wide, c, False), init)
    carry = lax.fori_loop(nwide * ATT_WIDE, qi, lambda j, c: consume(pl.multiple_of(j * t, t), t, c, False), carry)
    carry = consume(pl.multiple_of(qi * t, t), t, carry, True)
    outs = [acc[:MLA_VDIM] / acc[MLA_VDIM:MLA_VDIM + 1] for (_, acc) in carry]
    o_ref[...] = jnp.concatenate(outs, axis=0).T.astype(o_ref.dtype)


def _tail_kernel(x_ref, og_ref, om_ref, p_ref, wo_ref, gffn_ref, wg_ref, wu_ref, wd_ref, gple_ref, wpg_ref, wpp_ref, y_ref):
    h = x_ref[...] + _dot(og_ref[...], wo_ref[:GDN_WIDTH, :]) + _dot(om_ref[...], wo_ref[GDN_WIDTH:, :])
    u = _rmsnorm(h, gffn_ref[...]).astype(BF16)
    act = (_silu(_dot(u, wg_ref[...])) * _dot(u, wu_ref[...])).astype(BF16)
    acc = h + _dot(act, wd_ref[...])
    gate = _sigmoid(_dot(_rmsnorm(acc, gple_ref[...]).astype(BF16), wpg_ref[...]))
    y_ref[...] = acc + _dot(p_ref[...].astype(BF16), wpp_ref[...]) * gate


PAGES_PER_STEP = 16
PAGES_PER_SUB = 8
DECODE_SLOTS = 4
DECODE_AHEAD = 2


def _sample_prep_kernel(qm_ref, km_ref, gk_ref, wk_ref, qabs_ref, qr_ref, sself_ref):
    qm = qm_ref[...]
    prod = qm * km_ref[...]
    qg = (qm * gk_ref[...]).astype(BF16)
    for h in range(MLA_HEADS):
        sl = slice(h * SLAB, (h + 1) * SLAB)
        qabs_ref[h] = _dot_nt(qg[:, sl], wk_ref[:, sl])
        qr_ref[h] = qm[:, h * SLAB + ROPE_LO:h * SLAB + ROPE_LO + MLA_ROPE]
        sself_ref[h] = jnp.sum(prod[:, sl], axis=1, keepdims=True)


def _paged_kernel(nsteps, page, pt_ref, qabs_ref, qr_ref, sself_ref, ckvn_ref, wkt_ref, ckv_hbm, krt_hbm, o_ref,
                  cbuf, kbuf, sems):
    pp = PAGES_PER_STEP
    b = pl.program_id(0)
    nb = pl.num_programs(0)

    def copies(row, g):
        slot = g % DECODE_SLOTS
        out = []
        for j in range(pp):
            pg = pt_ref[row, g * pp + j]
            out.append(pltpu.make_async_copy(ckv_hbm.at[pg], cbuf.at[slot, pl.ds(j * page, page), :], sems.at[slot, 0]))
            out.append(pltpu.make_async_copy(krt_hbm.at[pg], kbuf.at[slot, :, pl.ds(j * page, page)], sems.at[slot, 1]))
        return out

    @pl.when(b == 0)
    def _():
        for g in range(DECODE_AHEAD):
            for cp in copies(0, g):
                cp.start()

    qa = qabs_ref[...].astype(BF16)
    lhs = jnp.concatenate([wkt_ref[...], qa, jnp.zeros_like(qa)], axis=0)
    nk = MLA_HEADS * MLA_NOPE
    qr = qr_ref[...].astype(BF16)
    m = sself_ref[...]
    l = jnp.ones(m.shape, F32)
    acc = jnp.broadcast_to(ckvn_ref[...], (MLA_HEADS, KV_LORA))
    sub_tok = PAGES_PER_SUB * page

    def update(s, cs, m, l, acc):
        m_new = jnp.maximum(m, jnp.max(s, axis=1, keepdims=True))
        alpha = jnp.exp(m - m_new)
        p = jnp.exp(s - m_new)
        l = l * alpha + jnp.sum(p, axis=1, keepdims=True)
        return m_new, l, acc * alpha + _dot(p.astype(BF16), jnp.concatenate(cs, axis=0))

    pending = None
    for g in range(nsteps):
        slot = g % DECODE_SLOTS
        for cp in copies(b, g):
            cp.wait()
        ng = g + DECODE_AHEAD
        nrow, ng = (b, ng) if ng < nsteps else (jnp.minimum(b + 1, nb - 1), ng - nsteps)
        for cp in copies(nrow, ng):
            cp.start()
        cs, ss_list = [], []
        for i in range(pp // PAGES_PER_SUB):
            c = cbuf[slot, i * sub_tok:(i + 1) * sub_tok, :].astype(BF16)
            krt = kbuf[slot, :, i * sub_tok:(i + 1) * sub_tok].astype(BF16)
            kvq = _dot_nt(lhs, c)
            kv = kvq[:nk]
            ss = jnp.sum((kv * kv).reshape(MLA_HEADS, MLA_NOPE, kv.shape[1]), axis=1)
            ss_list.append(kvq[nk:nk + MLA_HEADS] * lax.rsqrt(ss * (1.0 / MLA_NOPE) + EPS) + _dot(qr, krt))
            cs.append(c)
        if pending is not None:
            m, l, acc = update(*pending, m, l, acc)
        pending = (jnp.concatenate(ss_list, axis=1), cs)
    m, l, acc = update(*pending, m, l, acc)
    o_ref[...] = acc / l

    @pl.when(b == nb - 1)
    def _():
        for g in range(DECODE_AHEAD):
            for cp in copies(b, g):
                cp.wait()


SAMPLE_GDN_ROWS = 4


def _sample_gdn_kernel(q_ref, k_ref, v_ref, gb_ref, s_ref, snew_ref, o_ref):
    hk = GDN_HEADS * GDN_DK
    hrow = lax.broadcasted_iota(jnp.int32, (GDN_HEADS, hk), 0)
    hlane = lax.broadcasted_iota(jnp.int32, (GDN_HEADS, hk), 1) // GDN_DK
    hm = hrow == hlane
    r8 = lax.broadcasted_iota(jnp.int32, (GDN_HEADS, LANES), 0)
    l8 = lax.broadcasted_iota(jnp.int32, (GDN_HEADS, LANES), 1)
    ones = jnp.where(hm, 1.0, 0.0).astype(BF16)
    rows = range(q_ref.shape[0])
    kmask = [jnp.where(hm, jnp.broadcast_to(k_ref[r], (GDN_HEADS, hk)), 0.0) for r in rows]
    qmask = [jnp.where(hm, jnp.broadcast_to(q_ref[r], (GDN_HEADS, hk)), 0.0) for r in rows]
    gbb = [jnp.broadcast_to(gb_ref[r], (GDN_HEADS, LANES)) for r in rows]
    g_col = [jnp.sum(jnp.where(l8 == r8, gbb[r], 0.0), axis=1, keepdims=True) for r in rows]
    b_col = [jnp.sum(jnp.where(l8 == r8 + BETA_LANE0, gbb[r], 0.0), axis=1, keepdims=True) for r in rows]

    def expand_rows(col):
        e0, e1, e2 = _split3(jnp.broadcast_to(col, (GDN_HEADS, GDN_DV)))
        return _dot_tn(jnp.concatenate([ones, ones, ones], axis=0), jnp.concatenate([e0, e1, e2], axis=0))

    def head_dot(xmask, mat):
        xh, xl = _split2(xmask)
        mh, ml = _split2(mat)
        return _dot(jnp.concatenate([xh, xh, xl], axis=1), jnp.concatenate([mh, ml, mh], axis=0))

    def outer(kmask_r, delta_r):
        kh, kl = _split2(kmask_r)
        dh, dl = _split2(delta_r)
        return _dot_tn(jnp.concatenate([kh, kh, kl], axis=0), jnp.concatenate([dh, dl, dh], axis=0))

    decay = [expand_rows(jnp.exp(g_col[r])) for r in rows]
    state = [s_ref[r] * decay[r] for r in rows]
    ks = [head_dot(kmask[r], state[r]) for r in rows]
    delta = [(v_ref[r] - ks[r]) * b_col[r] for r in rows]
    upd = [outer(kmask[r], delta[r]) for r in rows]
    state = [state[r] + upd[r] for r in rows]
    outs = [head_dot(qmask[r], state[r]) for r in rows]
    for r in rows:
        snew_ref[r] = state[r]
        o_ref[r] = outs[r]


def _sample_mix_kernel(o_ref, z_ref, olat_ref, wvbd_ref, esum_ref, gout_ref, og_ref, om_ref):
    o = o_ref[...]
    hi, lo = _split2(o * o)
    ms = _dot(hi, esum_ref[...]) + _dot(lo, esum_ref[...])
    og_ref[...] = (o * lax.rsqrt(ms + EPS) * gout_ref[...] * _silu(z_ref[...])).astype(og_ref.dtype)
    om_ref[...] = _dot(olat_ref[...].astype(BF16), wvbd_ref[...]).astype(om_ref.dtype)


def _seg_tables(width, segs, mean):
    n = len(segs)
    ered = np.zeros((width, LANES), np.float32)
    eexp = np.zeros((LANES, width), np.float32)
    for copy in range(3):
        for s, (a, b) in enumerate(segs):
            ered[a:b, copy * n + s] = 1.0 / (b - a) if mean else 1.0
            eexp[copy * n + s, a:b] = 1.0
    return jnp.asarray(ered, BF16), jnp.asarray(eexp, BF16)


def _rope_slabs(pos):
    inv_freq = ROPE_THETA ** (-jnp.arange(HALF, dtype=F32) / HALF)
    ang = pos.astype(F32)[:, None] * inv_freq[None, :]
    cos, sin = jnp.cos(ang), jnp.sin(ang)
    n = pos.shape[0]
    zeros = lambda w: jnp.zeros((n, w), F32)
    rc = jnp.concatenate([jnp.ones((n, MLA_NOPE), F32), cos, cos, zeros(SLAB - MLA_QK)], axis=1)
    rsu = jnp.concatenate([zeros(MLA_NOPE), -sin, zeros(SLAB - ROPE_HI)], axis=1)
    rsd = jnp.concatenate([zeros(ROPE_HI), sin, zeros(SLAB - MLA_QK)], axis=1)
    return rc, rsu, rsd


def _prepare_weights(g_attn, w_in, w_conv, gdn_a_log, gdn_dt_bias, g_gdn_out, g_q_a, w_q_b, g_q_nope, g_q_rope,
                     g_kv_a, g_k_rope, w_kv_b, g_k_nope):
    row = lambda v: v.reshape(1, -1).astype(F32)
    a = w_in[:, CONV_DIM:CONV_DIM + GDN_HEADS]
    b = w_in[:, CONV_DIM + GDN_HEADS:CONV_DIM + 2 * GDN_HEADS]
    o_z = CONV_DIM + 2 * GDN_HEADS
    o_qa = o_z + GDN_WIDTH
    o_kv = o_qa + Q_LORA
    zc = lambda w: jnp.zeros((D_MODEL, w), F32)
    gb_slab = jnp.concatenate([a, a, a, zc(BETA_LANE0 - 3 * GDN_HEADS), b, b, b, zc(LANES - BETA_LANE0 - 3 * GDN_HEADS)], axis=1)
    kr_slab = jnp.concatenate([zc(ROPE_LO), w_in[:, o_kv + KV_LORA:], zc(SLAB - MLA_QK)], axis=1)
    w_in_p = jnp.concatenate([w_in[:, :CONV_DIM], gb_slab, w_in[:, o_z:o_qa], w_in[:, o_qa:o_kv],
                              w_in[:, o_kv:o_kv + KV_LORA], kr_slab], axis=1).astype(BF16)

    def scalar_slab(v):
        z8 = jnp.zeros((BETA_LANE0 - 3 * GDN_HEADS,), F32)
        return jnp.concatenate([v, v, v, z8, jnp.zeros((LANES - BETA_LANE0,), F32)]).reshape(1, LANES)

    wq = w_q_b.reshape(Q_LORA, MLA_HEADS, MLA_QK)
    w_qb = jnp.concatenate([wq, jnp.zeros((Q_LORA, MLA_HEADS, SLAB - MLA_QK), F32)], axis=2).reshape(Q_LORA, QK_SLAB_W).astype(BF16)
    gq = jnp.tile(jnp.concatenate([g_q_nope, g_q_rope, jnp.zeros((SLAB - MLA_QK,), F32)]), MLA_HEADS).reshape(1, QK_SLAB_W)
    wkv = w_kv_b.reshape(KV_LORA, MLA_HEADS, MLA_NOPE + MLA_VDIM)
    wk_part, wv_part = wkv[:, :, :MLA_NOPE], wkv[:, :, MLA_NOPE:]
    wk = jnp.concatenate([wk_part, jnp.zeros((KV_LORA, MLA_HEADS, SLAB - MLA_NOPE), F32)], axis=2).reshape(KV_LORA, QK_SLAB_W).astype(BF16)
    gk = jnp.tile(jnp.concatenate([g_k_nope, jnp.zeros((SLAB - MLA_NOPE,), F32)]), MLA_HEADS).reshape(1, QK_SLAB_W)
    gkr = jnp.concatenate([jnp.zeros((ROPE_LO,), F32), g_k_rope, jnp.zeros((SLAB - MLA_QK,), F32)]).reshape(1, SLAB)
    wvt = wv_part.reshape(KV_LORA, MLA_HEADS * MLA_VDIM).T.astype(BF16)
    wkt = wk_part.reshape(KV_LORA, MLA_HEADS * MLA_NOPE).T.astype(BF16)
    wv_bd = jnp.zeros((MLA_HEADS, KV_LORA, MLA_HEADS, MLA_VDIM), F32)
    wv_bd = wv_bd.at[jnp.arange(MLA_HEADS), :, jnp.arange(MLA_HEADS), :].set(jnp.moveaxis(wv_part, 1, 0))
    wv_bd = wv_bd.reshape(MLA_HEADS * KV_LORA, MLA_HEADS * MLA_VDIM).astype(BF16)

    ered_g, eexp_g = _seg_tables(2 * GDN_QK_DIM, [(h * GDN_DK, (h + 1) * GDN_DK) for h in range(2 * GDN_HEADS)], False)
    q_segs = ([(h * SLAB, h * SLAB + MLA_NOPE) for h in range(MLA_HEADS)]
              + [(h * SLAB + ROPE_LO, h * SLAB + MLA_QK) for h in range(MLA_HEADS)])
    ered_q, eexp_q = _seg_tables(QK_SLAB_W, q_segs, True)
    ered_k, eexp_k = _seg_tables(QK_SLAB_W, q_segs[:MLA_HEADS], True)
    ered_r, eexp_r = _seg_tables(SLAB, [(ROPE_LO, MLA_QK)], True)
    e6 = np.zeros((LANES, 2 * GDN_WIDTH), np.float32)
    for copy in range(3):
        for h in range(GDN_HEADS):
            e6[copy * GDN_HEADS + h, h * GDN_DV:(h + 1) * GDN_DV] = 1.0
            e6[BETA_LANE0 + copy * GDN_HEADS + h, GDN_WIDTH + h * GDN_DV:GDN_WIDTH + (h + 1) * GDN_DV] = 1.0
    esum = np.kron(np.eye(GDN_HEADS, dtype=np.float32), np.full((GDN_DV, GDN_DV), 1.0 / GDN_DV, np.float32))

    mixer = dict(g_attn=row(g_attn), w_in=w_in_p, w_conv=w_conv.astype(F32), alog=scalar_slab(gdn_a_log),
                 dtb=scalar_slab(gdn_dt_bias), ered_g=ered_g, eexp_g=eexp_g, g_q_a=row(g_q_a), w_qb=w_qb, gq=gq,
                 ered_q=ered_q, eexp_q=eexp_q, g_kv_a=row(g_kv_a), gkr=gkr, ered_r=ered_r, eexp_r=eexp_r, wk=wk, gk=gk,
                 ered_k=ered_k, eexp_k=eexp_k)
    extra = dict(wvt=wvt, wkt=wkt, wv_bd=wv_bd, e6=jnp.asarray(e6, BF16), esum=jnp.asarray(esum, BF16),
                 gout=jnp.tile(g_gdn_out, GDN_HEADS).reshape(1, GDN_WIDTH).astype(F32))
    return mixer, extra


def _params(sem=None):
    return pltpu.CompilerParams(dimension_semantics=sem, vmem_limit_bytes=VMEM_LIMIT)


def _prompt_mixer(x2, slabs, mixer, wvt, bsz, seq, tm):
    n = x2.shape[0]
    tps = seq // tm
    consts = [mixer[k] for k in MIXER_CONSTS] + [wvt]
    row_spec = lambda w: pl.BlockSpec((tm, w), lambda i: (i, 0))
    rope_spec = pl.BlockSpec((tm, SLAB), lambda i: (i % tps, 0))
    out_shapes = [
        jax.ShapeDtypeStruct((n, GDN_QK_DIM), F32), jax.ShapeDtypeStruct((n, GDN_QK_DIM), F32),
        jax.ShapeDtypeStruct((n, GDN_WIDTH), F32), jax.ShapeDtypeStruct((n, LANES), F32),
        jax.ShapeDtypeStruct((n, GDN_WIDTH), F32), jax.ShapeDtypeStruct((n, QK_SLAB_W), BF16),
        jax.ShapeDtypeStruct((n, QK_SLAB_W), BF16), jax.ShapeDtypeStruct((MLA_HEADS * MLA_VDIM, n), BF16),
        jax.ShapeDtypeStruct((n, KV_LORA), F32), jax.ShapeDtypeStruct((n, MLA_ROPE), F32),
        jax.ShapeDtypeStruct((bsz, CONV_WIDTH - 1, CONV_DIM), F32)]
    out_specs = [row_spec(GDN_QK_DIM), row_spec(GDN_QK_DIM), row_spec(GDN_WIDTH), row_spec(LANES), row_spec(GDN_WIDTH),
                 row_spec(QK_SLAB_W), row_spec(QK_SLAB_W), pl.BlockSpec((MLA_HEADS * MLA_VDIM, tm), lambda i: (0, i)),
                 row_spec(KV_LORA), row_spec(MLA_ROPE),
                 pl.BlockSpec((1, CONV_WIDTH - 1, CONV_DIM), lambda i: (i // tps, 0, 0))]
    return pl.pallas_call(
        functools.partial(_prompt_mixer_kernel, tps, tm),
        grid=(n // tm,),
        in_specs=[row_spec(D_MODEL), rope_spec, rope_spec, rope_spec] + [_const_spec(a, 1) for a in consts],
        out_specs=out_specs, out_shape=out_shapes,
        scratch_shapes=[pltpu.VMEM((tm + 8, CONV_DIM), F32)],
        compiler_params=_params(("arbitrary",)), name="prompt_mixer",
    )(x2, *slabs, *consts)


def _sample_mixer(x2, slabs, hist, mixer):
    n = x2.shape[0]
    consts = [mixer[k] for k in MIXER_CONSTS]
    sd = lambda w: jax.ShapeDtypeStruct((n, w), F32)
    return pl.pallas_call(
        _sample_mixer_kernel,
        out_shape=[sd(GDN_QK_DIM), sd(GDN_QK_DIM), sd(GDN_WIDTH), sd(LANES), sd(GDN_WIDTH), sd(QK_SLAB_W), sd(QK_SLAB_W),
                   sd(KV_LORA), sd(MLA_ROPE), sd(CONV_DIM)],
        compiler_params=_params(), name="sample_mixer",
    )(x2, *slabs, hist, *consts)


def _gdn_prompt(qg, kg, vg, gb, z, extra, bsz, seq, tb):
    n = qg.shape[0]
    nblk = seq // tb
    row_spec = lambda w: pl.BlockSpec((tb, w), lambda b, j: (b * nblk + j, 0))
    consts = [extra["e6"], extra["esum"], extra["gout"]]
    return pl.pallas_call(
        functools.partial(_gdn_kernel, nblk, tb),
        grid=(bsz, nblk),
        in_specs=[row_spec(GDN_QK_DIM), row_spec(GDN_QK_DIM), row_spec(GDN_WIDTH), row_spec(LANES), row_spec(GDN_WIDTH)]
        + [_const_spec(a, 2) for a in consts],
        out_specs=[row_spec(GDN_WIDTH), pl.BlockSpec((1, GDN_HEADS, GDN_DK, GDN_DV), lambda b, j: (b, 0, 0, 0))],
        out_shape=[jax.ShapeDtypeStruct((n, GDN_WIDTH), BF16), jax.ShapeDtypeStruct((bsz, GDN_HEADS, GDN_DK, GDN_DV), F32)],
        scratch_shapes=[pltpu.VMEM((GDN_HEADS // GROUP_HEADS, GROUP_W, GROUP_W), F32)],
        compiler_params=_params(("arbitrary", "arbitrary")), name="gdn_chunked",
    )(qg, kg, vg, gb, z, *consts)


def _attn_prompt(qm, km, vt, bsz, seq):
    n = qm.shape[0]
    t = ATT_T
    nq = seq // t
    return pl.pallas_call(
        _attn_kernel,
        grid=(bsz, MLA_HEADS // ATT_HEADS, nq),
        in_specs=[pl.BlockSpec((t, ATT_HEADS * SLAB), lambda b, hp, qi: (b * nq + qi, hp)),
                  pl.BlockSpec((seq, ATT_HEADS * SLAB), lambda b, hp, qi: (b, hp)),
                  pl.BlockSpec((ATT_HEADS * MLA_VDIM, seq), lambda b, hp, qi: (hp, b))],
        out_specs=pl.BlockSpec((t, ATT_HEADS * MLA_VDIM), lambda b, hp, qi: (b * nq + qi, hp)),
        out_shape=jax.ShapeDtypeStruct((n, MLA_HEADS * MLA_VDIM), BF16),
        compiler_params=_params(("arbitrary", "arbitrary", "arbitrary")), name="mla_prompt_attention",
    )(qm, km, vt)


def _tail(x2, og, om, p2, tailw, tm):
    n = x2.shape[0]
    row_spec = lambda w: pl.BlockSpec((tm, w), lambda i: (i, 0))
    return pl.pallas_call(
        _tail_kernel,
        grid=(n // tm,),
        in_specs=[row_spec(D_MODEL), row_spec(GDN_WIDTH), row_spec(GDN_WIDTH), row_spec(PLE_DIM)]
        + [_const_spec(a, 1) for a in tailw],
        out_specs=row_spec(D_MODEL), out_shape=jax.ShapeDtypeStruct((n, D_MODEL), F32),
        compiler_params=_params(("arbitrary",)), name="layer_tail",
    )(x2, og, om, p2, *tailw)


def _sample_prep(qm, km, gk, wk):
    n = qm.shape[0]
    return pl.pallas_call(
        _sample_prep_kernel,
        out_shape=[jax.ShapeDtypeStruct((MLA_HEADS, n, KV_LORA), F32), jax.ShapeDtypeStruct((MLA_HEADS, n, MLA_ROPE), F32),
                   jax.ShapeDtypeStruct((MLA_HEADS, n, 1), F32)],
        compiler_params=_params(), name="sample_prep",
    )(qm, km, gk, wk)


def _paged_attention(page_table, pool_ckv, pool_kr, qabs, qr, sself, ckv_new, wkt):
    bs, n_pages = page_table.shape
    page = pool_ckv.shape[1]
    pp = PAGES_PER_STEP
    nsteps = n_pages // pp
    assert n_pages % pp == 0 and nsteps % DECODE_SLOTS == 0 and DECODE_AHEAD < DECODE_SLOTS, "page groups must fill whole buffer rings"
    per_b = lambda w: pl.BlockSpec((None, MLA_HEADS, w), lambda b, pt: (b, 0, 0))
    grid_spec = pltpu.PrefetchScalarGridSpec(
        num_scalar_prefetch=1, grid=(bs,),
        in_specs=[per_b(KV_LORA), per_b(MLA_ROPE), per_b(1),
                  pl.BlockSpec((None, 1, KV_LORA), lambda b, pt: (b, 0, 0)),
                  pl.BlockSpec(wkt.shape, lambda b, pt: (0, 0)),
                  pl.BlockSpec(memory_space=pl.ANY), pl.BlockSpec(memory_space=pl.ANY)],
        out_specs=pl.BlockSpec((None, MLA_HEADS, KV_LORA), lambda b, pt: (b, 0, 0)),
        scratch_shapes=[pltpu.VMEM((DECODE_SLOTS, pp * page, KV_LORA), F32), pltpu.VMEM((DECODE_SLOTS, MLA_ROPE, pp * page), F32),
                        pltpu.SemaphoreType.DMA((DECODE_SLOTS, 2))])
    return pl.pallas_call(
        functools.partial(_paged_kernel, nsteps, page), grid_spec=grid_spec,
        out_shape=jax.ShapeDtypeStruct((bs, MLA_HEADS, KV_LORA), F32),
        compiler_params=_params(("arbitrary",)), name="mla_paged_decode",
    )(page_table, qabs, qr, sself, ckv_new, wkt, pool_ckv, pool_kr)


def _sample_gdn(q3, k3, v3, gb3, state):
    bs = q3.shape[0]
    hk = GDN_HEADS * GDN_DK
    rb = _pick(bs, (SAMPLE_GDN_ROWS, 2, 1))
    b3 = lambda r, w: pl.BlockSpec((rb, r, w), lambda b: (b, 0, 0))
    return pl.pallas_call(
        _sample_gdn_kernel, grid=(bs // rb,),
        in_specs=[b3(1, hk), b3(1, hk), b3(GDN_HEADS, GDN_DV), b3(1, LANES), b3(hk, GDN_DV)],
        out_specs=[b3(hk, GDN_DV), b3(GDN_HEADS, GDN_DV)],
        out_shape=[jax.ShapeDtypeStruct((bs, hk, GDN_DV), F32), jax.ShapeDtypeStruct((bs, GDN_HEADS, GDN_DV), F32)],
        compiler_params=_params(("arbitrary",)), name="gdn_recurrent_step",
    )(q3, k3, v3, gb3, state)


def _sample_mix(o2, z, olat2, extra):
    n = o2.shape[0]
    return pl.pallas_call(
        _sample_mix_kernel,
        out_shape=[jax.ShapeDtypeStruct((n, GDN_WIDTH), BF16), jax.ShapeDtypeStruct((n, MLA_HEADS * MLA_VDIM), BF16)],
        compiler_params=_params(), name="sample_mix",
    )(o2, z, olat2, extra["wv_bd"], extra["esum"], extra["gout"])


def _pick(n, prefs):
    for t in prefs:
        if n % t == 0:
            return t
    return n


def kernel(x_prompt, x_sample, cache_ckv, cache_krope, state_gdn, state_conv, page_table, p_prompt, p_sample, g_attn, w_in, w_conv, gdn_a_log, gdn_dt_bias, g_gdn_out, g_q_a, w_q_b, g_q_nope, g_q_rope, g_kv_a, g_k_rope, w_kv_b, g_k_nope, w_o, g_ffn, w_ffn_gate, w_ffn_up, w_ffn_down, g_ple, w_ple_gate, w_ple_proj):
    depth = g_attn.shape[0]
    assert depth == 1 and x_sample.shape[1] == 1, "single layer, single new token per sample row"
    bp, seq, _ = x_prompt.shape
    bs = x_sample.shape[0]
    past = page_table.shape[1] * cache_ckv.shape[2]
    li = 0
    mixer, extra = _prepare_weights(g_attn[li], w_in[li], w_conv[li], gdn_a_log[li], gdn_dt_bias[li], g_gdn_out[li],
                                    g_q_a[li], w_q_b[li], g_q_nope[li], g_q_rope[li], g_kv_a[li], g_k_rope[li],
                                    w_kv_b[li], g_k_nope[li])
    row = lambda v: v.reshape(1, -1).astype(F32)
    tailw = [w_o[li].astype(BF16), row(g_ffn[li]), w_ffn_gate[li].astype(BF16), w_ffn_up[li].astype(BF16),
             w_ffn_down[li].astype(BF16), row(g_ple[li]), w_ple_gate[li].astype(BF16), w_ple_proj[li].astype(BF16)]

    n = bp * seq
    xp2 = x_prompt.reshape(n, D_MODEL)
    tm = _pick(seq, (256, 128, 64, 32, 16, 8))
    (qg, kg, vg, gb, z, qm, km, vt, ckv_p, kr_p, conv_p) = _prompt_mixer(
        xp2, _rope_slabs(jnp.arange(seq)), mixer, extra["wvt"], bp, seq, MIXER_BLOCKS * tm)
    tb = _pick(seq, (1024, 512, 256))
    og, gdn_p = _gdn_prompt(qg, kg, vg, gb, z, extra, bp, seq, tb)
    om = _attn_prompt(qm, km, vt, bp, seq)
    y_prompt = _tail(xp2, og, om, p_prompt[li].reshape(n, PLE_DIM), tailw, tm).reshape(bp, seq, D_MODEL)

    xs2 = x_sample.reshape(bs, D_MODEL)
    hist = jnp.moveaxis(state_conv[li], 1, 0)
    (qg_s, kg_s, vg_s, gb_s, z_s, qm_s, km_s, ckv_s, kr_s, cin_s) = _sample_mixer(
        xs2, _rope_slabs(past + jnp.arange(1)), hist, mixer)
    qabs, qr, sself = _sample_prep(qm_s, km_s, mixer["gk"], mixer["wk"])
    per_row = lambda t: jnp.swapaxes(t, 0, 1)
    pool_krt = jnp.swapaxes(cache_krope[li], 1, 2)
    olat = _paged_attention(page_table, cache_ckv[li], pool_krt, per_row(qabs), per_row(qr), per_row(sself),
                            ckv_s.reshape(bs, 1, KV_LORA), extra["wkt"])
    hk = GDN_HEADS * GDN_DK
    s_new, o_s = _sample_gdn(qg_s.reshape(bs, 1, hk), kg_s.reshape(bs, 1, hk), vg_s.reshape(bs, GDN_HEADS, GDN_DV),
                             gb_s.reshape(bs, 1, LANES), state_gdn[li].reshape(bs, hk, GDN_DV))
    og_s, om_s = _sample_mix(o_s.reshape(bs, GDN_WIDTH), z_s, olat.reshape(bs, MLA_HEADS * KV_LORA), extra)
    y_sample = _tail(xs2, og_s, om_s, p_sample[li].reshape(bs, PLE_DIM), tailw, bs).reshape(bs, 1, D_MODEL)
    conv_s = jnp.concatenate([state_conv[li][:, 1:], cin_s[:, None, :]], axis=1)

    return (y_prompt, y_sample,
            ckv_p.reshape(1, bp, seq, KV_LORA), kr_p.reshape(1, bp, seq, MLA_ROPE),
            gdn_p[None], conv_p[None],
            ckv_s.reshape(1, bs, 1, KV_LORA), kr_s.reshape(1, bs, 1, MLA_ROPE),
            s_new.reshape(1, bs, GDN_HEADS, GDN_DK, GDN_DV), conv_s[None])
```

```python
import functools
import math

import numpy as np
import jax
import jax.numpy as jnp
from jax import lax
from jax.experimental import pallas as pl
from jax.experimental.pallas import tpu as pltpu

F32 = jnp.float32
BF16 = jnp.bfloat16

D_MODEL = 1024
PLE_DIM = 256
GDN_HEADS = 8
GDN_DK = 64
GDN_DV = 64
GDN_WIDTH = GDN_HEADS * GDN_DV
GDN_QK_DIM = GDN_HEADS * GDN_DK
CONV_WIDTH = 4
CONV_DIM = 2 * GDN_QK_DIM + GDN_WIDTH
GDN_CHUNK = 64
MLA_HEADS = 8
MLA_NOPE = 64
MLA_ROPE = 32
MLA_VDIM = 64
MLA_QK = MLA_NOPE + MLA_ROPE
Q_LORA = 384
KV_LORA = 256
ROPE_THETA = 10000.0
ATTN_SCALE = MLA_QK ** -0.5
LOG2E = math.log2(math.e)
D_FF = 2816
EPS = 1e-6

LANES = 128
SLAB = 128
HALF = MLA_ROPE // 2
ROPE_LO = MLA_NOPE
ROPE_HI = MLA_NOPE + HALF
QK_SLAB_W = MLA_HEADS * SLAB

OFF_CONV = 0
OFF_GB = OFF_CONV + CONV_DIM
OFF_Z = OFF_GB + LANES
OFF_QA = OFF_Z + GDN_WIDTH
OFF_CKV = OFF_QA + Q_LORA
OFF_KR = OFF_CKV + KV_LORA
IN_PAD = OFF_KR + LANES
BETA_LANE0 = 32

VMEM_LIMIT = 56 * 1024 * 1024


def _dot(a, b):
    return jnp.dot(a, b, preferred_element_type=F32)


def _dot_nt(a, b):
    return lax.dot_general(a, b, (((1,), (1,)), ((), ())), preferred_element_type=F32)


def _dot_tn(a, b):
    return lax.dot_general(a, b, (((0,), (0,)), ((), ())), preferred_element_type=F32)


def _split2(x):
    hi = x.astype(BF16)
    lo = (x - hi.astype(F32)).astype(BF16)
    return hi, lo


def _split3(x):
    hi = x.astype(BF16)
    r1 = x - hi.astype(F32)
    mid = r1.astype(BF16)
    lo = (r1 - mid.astype(F32)).astype(BF16)
    return hi, mid, lo


def _sigmoid(x):
    return 1.0 / (1.0 + jnp.exp(-x))


def _silu(x):
    return x * _sigmoid(x)


def _softplus(x):
    return jnp.maximum(x, 0.0) + jnp.log1p(jnp.exp(-jnp.abs(x)))


def _rmsnorm(x, g):
    return x * lax.rsqrt(jnp.mean(x * x, axis=-1, keepdims=True) + EPS) * g


def _seg_rsqrt(x, ered, eexp, nseg):
    red = _dot((x * x).astype(BF16), ered)
    r = lax.rsqrt(red + EPS)
    p0, p1, p2 = _split3(r)
    lane = lax.broadcasted_iota(jnp.int32, r.shape, 1)
    piece = jnp.where(lane < nseg, p0, jnp.where(lane < 2 * nseg, p1, p2))
    return _dot(piece, eexp)


def _rope(x, c, s_up, s_dn):
    w = x.shape[-1]
    up = pltpu.roll(x, w - HALF, axis=1)
    dn = pltpu.roll(x, HALF, axis=1)
    return x * c + up * s_up + dn * s_dn


def _tile_lanes(x, n):
    return jnp.concatenate([x] * n, axis=1)


def _const_spec(arr, ngrid):
    nd = arr.ndim
    return pl.BlockSpec(arr.shape, lambda *a, _nd=nd: (0,) * _nd, pipeline_mode=pl.Buffered(1))


MIXER_BLOCKS = 2
MIXER_CONSTS = ("g_attn", "w_in", "w_conv", "alog", "dtb", "ered_g", "eexp_g", "g_q_a", "w_qb", "gq", "ered_q",
                "eexp_q", "g_kv_a", "gkr", "ered_r", "eexp_r", "wk", "gk", "ered_k", "eexp_k")


def _mixer_rows(xs, c, conv_fn, ropes, q_scale):
    n = range(len(xs))
    w_in = c["w_in"]
    xn = [_rmsnorm(x, c["g_attn"][...]).astype(BF16) for x in xs]

    conv_in = [_dot(xn[i], w_in[:, OFF_CONV:OFF_GB]) for i in n]
    y = [_silu(v) for v in conv_fn(conv_in)]
    qk = [y[i][:, :2 * GDN_QK_DIM] for i in n]
    qk = [qk[i] * _seg_rsqrt(qk[i], c["ered_g"][...], c["eexp_g"][...], 2 * GDN_HEADS) for i in n]

    rest = [_dot(xn[i], w_in[:, OFF_GB:IN_PAD]) for i in n]
    part = lambda i, lo, hi: rest[i][:, lo - OFF_GB:hi - OFF_GB]
    ab = [part(i, OFF_GB, OFF_Z) for i in n]
    gb = []
    for i in n:
        lane = lax.broadcasted_iota(jnp.int32, ab[i].shape, 1)
        g_log = -jnp.exp(c["alog"][...]) * _softplus(ab[i] + c["dtb"][...])
        gb.append(jnp.where(lane < BETA_LANE0, g_log, _sigmoid(ab[i])))

    z = [part(i, OFF_Z, OFF_QA) for i in n]

    qa = [part(i, OFF_QA, OFF_CKV) for i in n]
    qs = [_dot(_rmsnorm(qa[i], c["g_q_a"][...]).astype(BF16), c["w_qb"][...]) for i in n]
    qs = [qs[i] * _seg_rsqrt(qs[i], c["ered_q"][...], c["eexp_q"][...], 2 * MLA_HEADS) * c["gq"][...] for i in n]
    q_mla = [_rope(qs[i], *[_tile_lanes(t, MLA_HEADS) for t in ropes[i]]) * q_scale for i in n]

    ckv = [_rmsnorm(part(i, OFF_CKV, OFF_KR), c["g_kv_a"][...]) for i in n]
    ckv_bf = [v.astype(BF16) for v in ckv]
    kr = [part(i, OFF_KR, IN_PAD) for i in n]
    kr = [kr[i] * _seg_rsqrt(kr[i], c["ered_r"][...], c["eexp_r"][...], 1) * c["gkr"][...] for i in n]
    kr = [_rope(kr[i], *ropes[i]) for i in n]
    kk = [_dot(ckv_bf[i], c["wk"][...]) for i in n]
    kk = [kk[i] * _seg_rsqrt(kk[i], c["ered_k"][...], c["eexp_k"][...], MLA_HEADS) * c["gk"][...] for i in n]
    return [dict(conv_in=conv_in[i], q_g=qk[i][:, :GDN_QK_DIM] * (GDN_DK ** -0.5), k_g=qk[i][:, GDN_QK_DIM:],
                 v_g=y[i][:, 2 * GDN_QK_DIM:], gb=gb[i], z=z[i], q_mla=q_mla[i],
                 k_mla=kk[i] + _tile_lanes(kr[i], MLA_HEADS), ckv=ckv[i], ckv_bf=ckv_bf[i], kr=kr[i]) for i in n]


def _prompt_mixer_kernel(tiles_per_seq, tm, x_ref, rc_ref, rsu_ref, rsd_ref, *refs):
    nc = len(MIXER_CONSTS)
    c = dict(zip(MIXER_CONSTS, refs[:nc]))
    wvt_ref = refs[nc]
    (qg_ref, kg_ref, vg_ref, gb_ref, z_ref, qm_ref, km_ref, vt_ref, ckv_ref, kro_ref, cs_ref, ext_ref) = refs[nc + 1:]
    i = pl.program_id(0)
    hm = tm // MIXER_BLOCKS
    blocks = [slice(j * hm, (j + 1) * hm) for j in range(MIXER_BLOCKS)]

    @pl.when(i % tiles_per_seq == 0)
    def _():
        ext_ref[5:8, :] = jnp.zeros((3, CONV_DIM), F32)

    def conv_fn(conv_ins):
        w = c["w_conv"]
        for sl, v in zip(blocks, conv_ins):
            ext_ref[8 + sl.start:8 + sl.stop, :] = v
        return [v * w[3:4, :] + ext_ref[7 + sl.start:7 + sl.stop, :] * w[2:3, :]
                + ext_ref[6 + sl.start:6 + sl.stop, :] * w[1:2, :] + ext_ref[5 + sl.start:5 + sl.stop, :] * w[0:1, :]
                for sl, v in zip(blocks, conv_ins)]

    res = _mixer_rows([x_ref[sl, :] for sl in blocks], c, conv_fn,
                      [(rc_ref[sl, :], rsu_ref[sl, :], rsd_ref[sl, :]) for sl in blocks], ATTN_SCALE * LOG2E)
    last3 = ext_ref[tm + 5:tm + 8, :]
    ext_ref[5:8, :] = last3
    cs_ref[0] = last3
    for sl, r in zip(blocks, res):
        qg_ref[sl, :] = r["q_g"]
        kg_ref[sl, :] = r["k_g"]
        vg_ref[sl, :] = r["v_g"]
        gb_ref[sl, :] = r["gb"]
        z_ref[sl, :] = r["z"]
        qm_ref[sl, :] = r["q_mla"].astype(BF16)
        km_ref[sl, :] = r["k_mla"].astype(BF16)
        vt_ref[:, sl] = _dot_nt(wvt_ref[...], r["ckv_bf"]).astype(BF16)
        ckv_ref[sl, :] = r["ckv"]
        kro_ref[sl, :] = r["kr"][:, ROPE_LO:ROPE_LO + MLA_ROPE]


def _sample_mixer_kernel(x_ref, rc_ref, rsu_ref, rsd_ref, hist_ref, *refs):
    nc = len(MIXER_CONSTS)
    c = dict(zip(MIXER_CONSTS, refs[:nc]))
    (qg_ref, kg_ref, vg_ref, gb_ref, z_ref, qm_ref, km_ref, ckv_ref, kro_ref, cin_ref) = refs[nc:]

    def conv_fn(conv_ins):
        w = c["w_conv"]
        return [v * w[3:4, :] + hist_ref[2] * w[2:3, :] + hist_ref[1] * w[1:2, :] + hist_ref[0] * w[0:1, :] for v in conv_ins]

    (r,) = _mixer_rows([x_ref[...]], c, conv_fn, [(rc_ref[...], rsu_ref[...], rsd_ref[...])], ATTN_SCALE)
    qg_ref[...] = r["q_g"]
    kg_ref[...] = r["k_g"]
    vg_ref[...] = r["v_g"]
    gb_ref[...] = r["gb"]
    z_ref[...] = r["z"]
    qm_ref[...] = r["q_mla"]
    km_ref[...] = r["k_mla"]
    ckv_ref[...] = r["ckv"]
    kro_ref[...] = r["kr"][:, ROPE_LO:ROPE_LO + MLA_ROPE]
    cin_ref[...] = r["conv_in"]


GROUP_HEADS = 4
GROUP_W = GROUP_HEADS * GDN_DK


def _cumsum_rows(x, period):
    row = lax.broadcasted_iota(jnp.int32, x.shape, 0) % period
    s = 1
    while s < period:
        x = x + jnp.where(row >= s, pltpu.roll(x, s, axis=0), 0.0)
        s *= 2
    return x


GDN_ITER_CHUNKS = 4


def _bd(y, bd_mask):
    return jnp.where(bd_mask, jnp.concatenate([y] * GROUP_HEADS, axis=0), 0.0)


def _gdn_solve_stages(chains, masks, sols):
    bd_mask, tril_cat, strict_cat, eye_cat = masks
    c = GDN_CHUNK
    n = range(len(chains))
    bd = lambda y: _bd(y, bd_mask)
    kb = [ch["k"] * ch["b_r"] for ch in chains]
    aq = [_dot_nt(jnp.concatenate([kb[i], chains[i]["q"]], axis=0), bd(chains[i]["k"])) for i in n]
    yield
    decay = []
    for ch in chains:
        g_col = jnp.sum(jnp.where(eye_cat, ch["g_r"], 0.0), axis=0, keepdims=True)
        decay.append(jnp.exp(jnp.where(tril_cat, ch["g_r"] - g_col, -jnp.inf)))
    qk = [aq[i][c:] * decay[i] for i in n]
    p = [-jnp.where(strict_cat, aq[i][:c] * decay[i], 0.0) for i in n]
    s_inv = [jnp.where(eye_cat, 1.0, 0.0) + p[i] for i in n]
    p = [_dot(p[i], bd(p[i])) for i in n]
    yield
    for lvl in range(5):
        s_next = [s_inv[i] + _dot(p[i], bd(s_inv[i])) for i in n]
        yield
        if lvl < 4:
            p = [_dot(p[i], bd(p[i])) for i in n]
            yield
        s_inv = s_next
    eg = [jnp.exp(ch["g_r"]) for ch in chains]
    u = [_dot(s_inv[i], bd(chains[i]["v"] * chains[i]["b_r"])) for i in n]
    yield
    w = [_dot(s_inv[i], bd(kb[i] * eg[i])) for i in n]
    for i in n:
        g_r = chains[i]["g_r"]
        g_last = g_r[c - 1:c, :]
        sols.append(dict(u=u[i], wq=jnp.concatenate([w[i], chains[i]["q"] * eg[i]], axis=0), qk=qk[i],
                         kd=chains[i]["k"] * jnp.exp(g_last - g_r), e_last=jnp.exp(g_last)))


def _gdn_apply_stages(sols, states, bd_mask, box):
    c = GDN_CHUNK
    ngrp = len(states)
    n = range(ngrp)
    outs = []
    for t in range(len(sols) // ngrp):
        sl = sols[t * ngrp:(t + 1) * ngrp]
        ws = [_dot(sl[i]["wq"], states[i]) for i in n]
        yield
        v_new = [sl[i]["u"] - ws[i][:c] for i in n]
        o = [ws[i][c:] + _dot(sl[i]["qk"], _bd(v_new[i], bd_mask)) for i in n]
        upd = [_dot_tn(sl[i]["kd"], v_new[i]) for i in n]
        yield
        states = [states[i] * sl[i]["e_last"] + jnp.where(bd_mask, upd[i], 0.0) for i in n]
        outs.append(jnp.concatenate(o, axis=1))
    box["o"] = jnp.concatenate(outs, axis=0)
    box["states"] = states


def _interleave(stage_generators):
    live = list(stage_generators)
    while live:
        for g in list(live):
            try:
                next(g)
            except StopIteration:
                live.remove(g)


def _gdn_kernel(nblk, tb, q_ref, k_ref, v_ref, gb_ref, z_ref, e6_ref, esum_ref, gout_ref, o_ref, sfin_ref, s_ref):
    j = pl.program_id(1)

    @pl.when(j == 0)
    def _():
        s_ref[...] = jnp.zeros(s_ref.shape, F32)

    c = GDN_CHUNK
    ngrp = GDN_HEADS // GROUP_HEADS
    rows_it = GDN_ITER_CHUNKS * c
    r_bd = lax.broadcasted_iota(jnp.int32, (GROUP_W, GROUP_W), 0) // c
    c_bd = lax.broadcasted_iota(jnp.int32, (GROUP_W, GROUP_W), 1) // c
    bd_mask = r_bd == c_bd
    ri = lax.broadcasted_iota(jnp.int32, (c, GROUP_W), 0)
    cj = lax.broadcasted_iota(jnp.int32, (c, GROUP_W), 1) % c
    masks = (bd_mask, ri >= cj, ri > cj, ri == cj)
    lane = lax.broadcasted_iota(jnp.int32, (rows_it, LANES), 1)
    copy_id = (lane % BETA_LANE0) // GDN_HEADS

    def chains_of(rows):
        gb = gb_ref[rows, :]
        sc = jnp.where(lane < BETA_LANE0, _cumsum_rows(gb, c), gb)
        p0, p1, p2 = _split3(sc)
        piece = jnp.where(copy_id == 0, p0, jnp.where(copy_id == 1, p1, p2))
        ex = _dot(piece, e6_ref[...])
        q = q_ref[rows, :]
        k = k_ref[rows, :]
        v = v_ref[rows, :]
        chains = []
        for t in range(GDN_ITER_CHUNKS):
            rs = slice(t * c, (t + 1) * c)
            for grp in range(ngrp):
                sl = slice(grp * GROUP_W, (grp + 1) * GROUP_W)
                sb = slice(GDN_WIDTH + grp * GROUP_W, GDN_WIDTH + (grp + 1) * GROUP_W)
                chains.append(dict(q=q[rs, sl], k=k[rs, sl], v=v[rs, sl], g_r=ex[rs, sl], b_r=ex[rs, sb]))
        return chains

    def finish(o, rows):
        hi, lo = _split2(o * o)
        ms = _dot(hi, esum_ref[...]) + _dot(lo, esum_ref[...])
        o_ref[rows, :] = (o * lax.rsqrt(ms + EPS) * gout_ref[...] * _silu(z_ref[rows, :])).astype(o_ref.dtype)

    states = [s_ref[grp] for grp in range(ngrp)]
    pending = None
    for gi in range(tb // rows_it):
        rows = slice(gi * rows_it, (gi + 1) * rows_it)
        sols, box = [], {}
        stages = [_gdn_solve_stages(chains_of(rows), masks, sols)]
        if pending is not None:
            stages.append(_gdn_apply_stages(pending[0], states, bd_mask, box))
        _interleave(stages)
        if pending is not None:
            states = box["states"]
            finish(box["o"], pending[1])
        pending = (sols, rows)
    box = {}
    _interleave([_gdn_apply_stages(pending[0], states, bd_mask, box)])
    finish(box["o"], pending[1])
    for grp in range(ngrp):
        s_ref[grp] = box["states"][grp]

    @pl.when(j == nblk - 1)
    def _():
        for grp in range(ngrp):
            for h in range(GROUP_HEADS):
                sfin_ref[0, grp * GROUP_HEADS + h] = s_ref[grp, h * c:(h + 1) * c, h * c:(h + 1) * c]


ATT_T = 256
ATT_HEADS = 8
ATT_SUM_ROWS = 16
ATT_WIDE = 2


def _attn_kernel(q_ref, k_ref, vt_ref, o_ref):
    qi = pl.program_id(2)
    t = ATT_T
    qs = [q_ref[:, hh * SLAB:(hh + 1) * SLAB] for hh in range(ATT_HEADS)]

    def score_tile(k0, tk):
        return tuple(_dot_nt(k_ref[pl.ds(k0, tk), hh * SLAB:(hh + 1) * SLAB], qs[hh]) for hh in range(ATT_HEADS))

    def consume(k0, tk, carry, masked):
        scores = score_tile(k0, tk)
        stats = []
        for hh in range(ATT_HEADS):
            m, _ = carry[hh]
            s = scores[hh]
            if masked:
                kp = lax.broadcasted_iota(jnp.int32, (tk, t), 0)
                qp = lax.broadcasted_iota(jnp.int32, (tk, t), 1)
                s = jnp.where(qp >= kp, s, -jnp.inf)
            m_new = jnp.maximum(m, jnp.max(s, axis=0, keepdims=True))
            stats.append((m_new, jnp.exp2(m - m_new), jnp.exp2(s - m_new).astype(BF16)))
        out = []
        for hh in range(ATT_HEADS):
            m_new, alpha, p = stats[hh]
            vt = vt_ref[hh * MLA_VDIM:(hh + 1) * MLA_VDIM, pl.ds(k0, tk)]
            ones = jnp.ones((ATT_SUM_ROWS, tk), BF16)
            out.append((m_new, carry[hh][1] * alpha + _dot(jnp.concatenate([vt, ones], axis=0), p)))
        return tuple(out)

    init = tuple((jnp.full((1, t), -jnp.inf, F32), jnp.zeros((MLA_VDIM + ATT_SUM_ROWS, t), F32)) for _ in range(ATT_HEADS))
    wide = ATT_WIDE * t
    nwide = qi // ATT_WIDE
    carry = lax.fori_loop(0, nwide, lambda j, c: consume(pl.multiple_of(j * wide, wide), wide, c, False), init)
    carry = lax.fori_loop(nwide * ATT_WIDE, qi, lambda j, c: consume(pl.multiple_of(j * t, t), t, c, False), carry)
    carry = consume(pl.multiple_of(qi * t, t), t, carry, True)
    outs = [acc[:MLA_VDIM] / acc[MLA_VDIM:MLA_VDIM + 1] for (_, acc) in carry]
    o_ref[...] = jnp.concatenate(outs, axis=0).T.astype(o_ref.dtype)


def _tail_kernel(x_ref, og_ref, om_ref, p_ref, wo_ref, gffn_ref, wg_ref, wu_ref, wd_ref, gple_ref, wpg_ref, wpp_ref, y_ref):
    h = x_ref[...] + _dot(og_ref[...], wo_ref[:GDN_WIDTH, :]) + _dot(om_ref[...], wo_ref[GDN_WIDTH:, :])
    u = _rmsnorm(h, gffn_ref[...]).astype(BF16)
    act = (_silu(_dot(u, wg_ref[...])) * _dot(u, wu_ref[...])).astype(BF16)
    acc = h + _dot(act, wd_ref[...])
    gate = _sigmoid(_dot(_rmsnorm(acc, gple_ref[...]).astype(BF16), wpg_ref[...]))
    y_ref[...] = acc + _dot(p_ref[...].astype(BF16), wpp_ref[...]) * gate


PAGES_PER_STEP = 32
PAGES_PER_SUB = 8
DECODE_SLOTS = 4
DECODE_AHEAD = 2


def _sample_prep_kernel(qm_ref, km_ref, gk_ref, wk_ref, qabs_ref, qr_ref, sself_ref):
    qm = qm_ref[...]
    prod = qm * km_ref[...]
    qg = (qm * gk_ref[...]).astype(BF16)
    for h in range(MLA_HEADS):
        sl = slice(h * SLAB, (h + 1) * SLAB)
        qabs_ref[h] = _dot_nt(qg[:, sl], wk_ref[:, sl])
        qr_ref[h] = qm[:, h * SLAB + ROPE_LO:h * SLAB + ROPE_LO + MLA_ROPE]
        sself_ref[h] = jnp.sum(prod[:, sl], axis=1, keepdims=True)


def _paged_kernel(nsteps, page, pt_ref, qabs_ref, qr_ref, sself_ref, ckvn_ref, wkt_ref, ckv_hbm, krt_hbm, o_ref,
                  cbuf, kbuf, sems):
    pp = PAGES_PER_STEP
    b = pl.program_id(0)
    nb = pl.num_programs(0)

    def copies(row, g):
        slot = g % DECODE_SLOTS
        out = []
        for j in range(pp):
            pg = pt_ref[row, g * pp + j]
            out.append(pltpu.make_async_copy(ckv_hbm.at[pg], cbuf.at[slot, pl.ds(j * page, page), :], sems.at[slot, 0]))
            out.append(pltpu.make_async_copy(krt_hbm.at[pg], kbuf.at[slot, :, pl.ds(j * page, page)], sems.at[slot, 1]))
        return out

    @pl.when(b == 0)
    def _():
        for g in range(DECODE_AHEAD):
            for cp in copies(0, g):
                cp.start()

    qa = qabs_ref[...].astype(BF16)
    lhs = jnp.concatenate([wkt_ref[...], qa, jnp.zeros_like(qa)], axis=0)
    nk = MLA_HEADS * MLA_NOPE
    qr = qr_ref[...].astype(BF16)
    m = sself_ref[...]
    l = jnp.ones(m.shape, F32)
    acc = jnp.broadcast_to(ckvn_ref[...], (MLA_HEADS, KV_LORA))
    sub_tok = PAGES_PER_SUB * page

    def update(s, cs, m, l, acc):
        m_new = jnp.maximum(m, jnp.max(s, axis=1, keepdims=True))
        alpha = jnp.exp(m - m_new)
        p = jnp.exp(s - m_new)
        l = l * alpha + jnp.sum(p, axis=1, keepdims=True)
        return m_new, l, acc * alpha + _dot(p.astype(BF16), jnp.concatenate(cs, axis=0))

    pending = None
    for g in range(nsteps):
        slot = g % DECODE_SLOTS
        for cp in copies(b, g):
            cp.wait()
        ng = g + DECODE_AHEAD
        nrow, ng = (b, ng) if ng < nsteps else (jnp.minimum(b + 1, nb - 1), ng - nsteps)
        for cp in copies(nrow, ng):
            cp.start()
        cs, ss_list = [], []
        for i in range(pp // PAGES_PER_SUB):
            c = cbuf[slot, i * sub_tok:(i + 1) * sub_tok, :].astype(BF16)
            krt = kbuf[slot, :, i * sub_tok:(i + 1) * sub_tok].astype(BF16)
            kvq = _dot_nt(lhs, c)
            kv = kvq[:nk]
            ss = jnp.sum((kv * kv).reshape(MLA_HEADS, MLA_NOPE, kv.shape[1]), axis=1)
            ss_list.append(kvq[nk:nk + MLA_HEADS] * lax.rsqrt(ss * (1.0 / MLA_NOPE) + EPS) + _dot(qr, krt))
            cs.append(c)
        if pending is not None:
            m, l, acc = update(*pending, m, l, acc)
        pending = (jnp.concatenate(ss_list, axis=1), cs)
    m, l, acc = update(*pending, m, l, acc)
    o_ref[...] = acc / l

    @pl.when(b == nb - 1)
    def _():
        for g in range(DECODE_AHEAD):
            for cp in copies(b, g):
                cp.wait()


SAMPLE_GDN_ROWS = 4


def _sample_gdn_kernel(q_ref, k_ref, v_ref, gb_ref, s_ref, snew_ref, o_ref):
    hk = GDN_HEADS * GDN_DK
    hrow = lax.broadcasted_iota(jnp.int32, (GDN_HEADS, hk), 0)
    hlane = lax.broadcasted_iota(jnp.int32, (GDN_HEADS, hk), 1) // GDN_DK
    hm = hrow == hlane
    r8 = lax.broadcasted_iota(jnp.int32, (GDN_HEADS, LANES), 0)
    l8 = lax.broadcasted_iota(jnp.int32, (GDN_HEADS, LANES), 1)
    ones = jnp.where(hm, 1.0, 0.0).astype(BF16)
    rows = range(q_ref.shape[0])
    kmask = [jnp.where(hm, jnp.broadcast_to(k_ref[r], (GDN_HEADS, hk)), 0.0) for r in rows]
    qmask = [jnp.where(hm, jnp.broadcast_to(q_ref[r], (GDN_HEADS, hk)), 0.0) for r in rows]
    gbb = [jnp.broadcast_to(gb_ref[r], (GDN_HEADS, LANES)) for r in rows]
    g_col = [jnp.sum(jnp.where(l8 == r8, gbb[r], 0.0), axis=1, keepdims=True) for r in rows]
    b_col = [jnp.sum(jnp.where(l8 == r8 + BETA_LANE0, gbb[r], 0.0), axis=1, keepdims=True) for r in rows]

    def expand_rows(col):
        e0, e1, e2 = _split3(jnp.broadcast_to(col, (GDN_HEADS, GDN_DV)))
        return _dot_tn(jnp.concatenate([ones, ones, ones], axis=0), jnp.concatenate([e0, e1, e2], axis=0))

    def head_dot(xmask, mat):
        xh, xl = _split2(xmask)
        mh, ml = _split2(mat)
        return _dot(jnp.concatenate([xh, xh, xl], axis=1), jnp.concatenate([mh, ml, mh], axis=0))

    def outer(kmask_r, delta_r):
        kh, kl = _split2(kmask_r)
        dh, dl = _split2(delta_r)
        return _dot_tn(jnp.concatenate([kh, kh, kl], axis=0), jnp.concatenate([dh, dl, dh], axis=0))

    decay = [expand_rows(jnp.exp(g_col[r])) for r in rows]
    state = [s_ref[r] * decay[r] for r in rows]
    ks = [head_dot(kmask[r], state[r]) for r in rows]
    delta = [(v_ref[r] - ks[r]) * b_col[r] for r in rows]
    upd = [outer(kmask[r], delta[r]) for r in rows]
    state = [state[r] + upd[r] for r in rows]
    outs = [head_dot(qmask[r], state[r]) for r in rows]
    for r in rows:
        snew_ref[r] = state[r]
        o_ref[r] = outs[r]


def _sample_mix_kernel(o_ref, z_ref, olat_ref, wvbd_ref, esum_ref, gout_ref, og_ref, om_ref):
    o = o_ref[...]
    hi, lo = _split2(o * o)
    ms = _dot(hi, esum_ref[...]) + _dot(lo, esum_ref[...])
    og_ref[...] = (o * lax.rsqrt(ms + EPS) * gout_ref[...] * _silu(z_ref[...])).astype(og_ref.dtype)
    om_ref[...] = _dot(olat_ref[...].astype(BF16), wvbd_ref[...]).astype(om_ref.dtype)


def _seg_tables(width, segs, mean):
    n = len(segs)
    ered = np.zeros((width, LANES), np.float32)
    eexp = np.zeros((LANES, width), np.float32)
    for copy in range(3):
        for s, (a, b) in enumerate(segs):
            ered[a:b, copy * n + s] = 1.0 / (b - a) if mean else 1.0
            eexp[copy * n + s, a:b] = 1.0
    return jnp.asarray(ered, BF16), jnp.asarray(eexp, BF16)


def _rope_slabs(pos):
    inv_freq = ROPE_THETA ** (-jnp.arange(HALF, dtype=F32) / HALF)
    ang = pos.astype(F32)[:, None] * inv_freq[None, :]
    cos, sin = jnp.cos(ang), jnp.sin(ang)
    n = pos.shape[0]
    zeros = lambda w: jnp.zeros((n, w), F32)
    rc = jnp.concatenate([jnp.ones((n, MLA_NOPE), F32), cos, cos, zeros(SLAB - MLA_QK)], axis=1)
    rsu = jnp.concatenate([zeros(MLA_NOPE), -sin, zeros(SLAB - ROPE_HI)], axis=1)
    rsd = jnp.concatenate([zeros(ROPE_HI), sin, zeros(SLAB - MLA_QK)], axis=1)
    return rc, rsu, rsd


def _prepare_weights(g_attn, w_in, w_conv, gdn_a_log, gdn_dt_bias, g_gdn_out, g_q_a, w_q_b, g_q_nope, g_q_rope,
                     g_kv_a, g_k_rope, w_kv_b, g_k_nope):
    row = lambda v: v.reshape(1, -1).astype(F32)
    a = w_in[:, CONV_DIM:CONV_DIM + GDN_HEADS]
    b = w_in[:, CONV_DIM + GDN_HEADS:CONV_DIM + 2 * GDN_HEADS]
    o_z = CONV_DIM + 2 * GDN_HEADS
    o_qa = o_z + GDN_WIDTH
    o_kv = o_qa + Q_LORA
    zc = lambda w: jnp.zeros((D_MODEL, w), F32)
    gb_slab = jnp.concatenate([a, a, a, zc(BETA_LANE0 - 3 * GDN_HEADS), b, b, b, zc(LANES - BETA_LANE0 - 3 * GDN_HEADS)], axis=1)
    kr_slab = jnp.concatenate([zc(ROPE_LO), w_in[:, o_kv + KV_LORA:], zc(SLAB - MLA_QK)], axis=1)
    w_in_p = jnp.concatenate([w_in[:, :CONV_DIM], gb_slab, w_in[:, o_z:o_qa], w_in[:, o_qa:o_kv],
                              w_in[:, o_kv:o_kv + KV_LORA], kr_slab], axis=1).astype(BF16)

    def scalar_slab(v):
        z8 = jnp.zeros((BETA_LANE0 - 3 * GDN_HEADS,), F32)
        return jnp.concatenate([v, v, v, z8, jnp.zeros((LANES - BETA_LANE0,), F32)]).reshape(1, LANES)

    wq = w_q_b.reshape(Q_LORA, MLA_HEADS, MLA_QK)
    w_qb = jnp.concatenate([wq, jnp.zeros((Q_LORA, MLA_HEADS, SLAB - MLA_QK), F32)], axis=2).reshape(Q_LORA, QK_SLAB_W).astype(BF16)
    gq = jnp.tile(jnp.concatenate([g_q_nope, g_q_rope, jnp.zeros((SLAB - MLA_QK,), F32)]), MLA_HEADS).reshape(1, QK_SLAB_W)
    wkv = w_kv_b.reshape(KV_LORA, MLA_HEADS, MLA_NOPE + MLA_VDIM)
    wk_part, wv_part = wkv[:, :, :MLA_NOPE], wkv[:, :, MLA_NOPE:]
    wk = jnp.concatenate([wk_part, jnp.zeros((KV_LORA, MLA_HEADS, SLAB - MLA_NOPE), F32)], axis=2).reshape(KV_LORA, QK_SLAB_W).astype(BF16)
    gk = jnp.tile(jnp.concatenate([g_k_nope, jnp.zeros((SLAB - MLA_NOPE,), F32)]), MLA_HEADS).reshape(1, QK_SLAB_W)
    gkr = jnp.concatenate([jnp.zeros((ROPE_LO,), F32), g_k_rope, jnp.zeros((SLAB - MLA_QK,), F32)]).reshape(1, SLAB)
    wvt = wv_part.reshape(KV_LORA, MLA_HEADS * MLA_VDIM).T.astype(BF16)
    wkt = wk_part.reshape(KV_LORA, MLA_HEADS * MLA_NOPE).T.astype(BF16)
    wv_bd = jnp.zeros((MLA_HEADS, KV_LORA, MLA_HEADS, MLA_VDIM), F32)
    wv_bd = wv_bd.at[jnp.arange(MLA_HEADS), :, jnp.arange(MLA_HEADS), :].set(jnp.moveaxis(wv_part, 1, 0))
    wv_bd = wv_bd.reshape(MLA_HEADS * KV_LORA, MLA_HEADS * MLA_VDIM).astype(BF16)

    ered_g, eexp_g = _seg_tables(2 * GDN_QK_DIM, [(h * GDN_DK, (h + 1) * GDN_DK) for h in range(2 * GDN_HEADS)], False)
    q_segs = ([(h * SLAB, h * SLAB + MLA_NOPE) for h in range(MLA_HEADS)]
              + [(h * SLAB + ROPE_LO, h * SLAB + MLA_QK) for h in range(MLA_HEADS)])
    ered_q, eexp_q = _seg_tables(QK_SLAB_W, q_segs, True)
    ered_k, eexp_k = _seg_tables(QK_SLAB_W, q_segs[:MLA_HEADS], True)
    ered_r, eexp_r = _seg_tables(SLAB, [(ROPE_LO, MLA_QK)], True)
    e6 = np.zeros((LANES, 2 * GDN_WIDTH), np.float32)
    for copy in range(3):
        for h in range(GDN_HEADS):
            e6[copy * GDN_HEADS + h, h * GDN_DV:(h + 1) * GDN_DV] = 1.0
            e6[BETA_LANE0 + copy * GDN_HEADS + h, GDN_WIDTH + h * GDN_DV:GDN_WIDTH + (h + 1) * GDN_DV] = 1.0
    esum = np.kron(np.eye(GDN_HEADS, dtype=np.float32), np.full((GDN_DV, GDN_DV), 1.0 / GDN_DV, np.float32))

    mixer = dict(g_attn=row(g_attn), w_in=w_in_p, w_conv=w_conv.astype(F32), alog=scalar_slab(gdn_a_log),
                 dtb=scalar_slab(gdn_dt_bias), ered_g=ered_g, eexp_g=eexp_g, g_q_a=row(g_q_a), w_qb=w_qb, gq=gq,
                 ered_q=ered_q, eexp_q=eexp_q, g_kv_a=row(g_kv_a), gkr=gkr, ered_r=ered_r, eexp_r=eexp_r, wk=wk, gk=gk,
                 ered_k=ered_k, eexp_k=eexp_k)
    extra = dict(wvt=wvt, wkt=wkt, wv_bd=wv_bd, e6=jnp.asarray(e6, BF16), esum=jnp.asarray(esum, BF16),
                 gout=jnp.tile(g_gdn_out, GDN_HEADS).reshape(1, GDN_WIDTH).astype(F32))
    return mixer, extra


def _params(sem=None):
    return pltpu.CompilerParams(dimension_semantics=sem, vmem_limit_bytes=VMEM_LIMIT)


def _prompt_mixer(x2, slabs, mixer, wvt, bsz, seq, tm):
    n = x2.shape[0]
    tps = seq // tm
    consts = [mixer[k] for k in MIXER_CONSTS] + [wvt]
    row_spec = lambda w: pl.BlockSpec((tm, w), lambda i: (i, 0))
    rope_spec = pl.BlockSpec((tm, SLAB), lambda i: (i % tps, 0))
    out_shapes = [
        jax.ShapeDtypeStruct((n, GDN_QK_DIM), F32), jax.ShapeDtypeStruct((n, GDN_QK_DIM), F32),
        jax.ShapeDtypeStruct((n, GDN_WIDTH), F32), jax.ShapeDtypeStruct((n, LANES), F32),
        jax.ShapeDtypeStruct((n, GDN_WIDTH), F32), jax.ShapeDtypeStruct((n, QK_SLAB_W), BF16),
        jax.ShapeDtypeStruct((n, QK_SLAB_W), BF16), jax.ShapeDtypeStruct((MLA_HEADS * MLA_VDIM, n), BF16),
        jax.ShapeDtypeStruct((n, KV_LORA), F32), jax.ShapeDtypeStruct((n, MLA_ROPE), F32),
        jax.ShapeDtypeStruct((bsz, CONV_WIDTH - 1, CONV_DIM), F32)]
    out_specs = [row_spec(GDN_QK_DIM), row_spec(GDN_QK_DIM), row_spec(GDN_WIDTH), row_spec(LANES), row_spec(GDN_WIDTH),
                 row_spec(QK_SLAB_W), row_spec(QK_SLAB_W), pl.BlockSpec((MLA_HEADS * MLA_VDIM, tm), lambda i: (0, i)),
                 row_spec(KV_LORA), row_spec(MLA_ROPE),
                 pl.BlockSpec((1, CONV_WIDTH - 1, CONV_DIM), lambda i: (i // tps, 0, 0))]
    return pl.pallas_call(
        functools.partial(_prompt_mixer_kernel, tps, tm),
        grid=(n // tm,),
        in_specs=[row_spec(D_MODEL), rope_spec, rope_spec, rope_spec] + [_const_spec(a, 1) for a in consts],
        out_specs=out_specs, out_shape=out_shapes,
        scratch_shapes=[pltpu.VMEM((tm + 8, CONV_DIM), F32)],
        compiler_params=_params(("arbitrary",)), name="prompt_mixer",
    )(x2, *slabs, *consts)


def _sample_mixer(x2, slabs, hist, mixer):
    n = x2.shape[0]
    consts = [mixer[k] for k in MIXER_CONSTS]
    sd = lambda w: jax.ShapeDtypeStruct((n, w), F32)
    return pl.pallas_call(
        _sample_mixer_kernel,
        out_shape=[sd(GDN_QK_DIM), sd(GDN_QK_DIM), sd(GDN_WIDTH), sd(LANES), sd(GDN_WIDTH), sd(QK_SLAB_W), sd(QK_SLAB_W),
                   sd(KV_LORA), sd(MLA_ROPE), sd(CONV_DIM)],
        compiler_params=_params(), name="sample_mixer",
    )(x2, *slabs, hist, *consts)


def _gdn_prompt(qg, kg, vg, gb, z, extra, bsz, seq, tb):
    n = qg.shape[0]
    nblk = seq // tb
    row_spec = lambda w: pl.BlockSpec((tb, w), lambda b, j: (b * nblk + j, 0))
    consts = [extra["e6"], extra["esum"], extra["gout"]]
    return pl.pallas_call(
        functools.partial(_gdn_kernel, nblk, tb),
        grid=(bsz, nblk),
        in_specs=[row_spec(GDN_QK_DIM), row_spec(GDN_QK_DIM), row_spec(GDN_WIDTH), row_spec(LANES), row_spec(GDN_WIDTH)]
        + [_const_spec(a, 2) for a in consts],
        out_specs=[row_spec(GDN_WIDTH), pl.BlockSpec((1, GDN_HEADS, GDN_DK, GDN_DV), lambda b, j: (b, 0, 0, 0))],
        out_shape=[jax.ShapeDtypeStruct((n, GDN_WIDTH), BF16), jax.ShapeDtypeStruct((bsz, GDN_HEADS, GDN_DK, GDN_DV), F32)],
        scratch_shapes=[pltpu.VMEM((GDN_HEADS // GROUP_HEADS, GROUP_W, GROUP_W), F32)],
        compiler_params=_params(("arbitrary", "arbitrary")), name="gdn_chunked",
    )(qg, kg, vg, gb, z, *consts)


def _attn_prompt(qm, km, vt, bsz, seq):
    n = qm.shape[0]
    t = ATT_T
    nq = seq // t
    return pl.pallas_call(
        _attn_kernel,
        grid=(bsz, MLA_HEADS // ATT_HEADS, nq),
        in_specs=[pl.BlockSpec((t, ATT_HEADS * SLAB), lambda b, hp, qi: (b * nq + qi, hp)),
                  pl.BlockSpec((seq, ATT_HEADS * SLAB), lambda b, hp, qi: (b, hp)),
                  pl.BlockSpec((ATT_HEADS * MLA_VDIM, seq), lambda b, hp, qi: (hp, b))],
        out_specs=pl.BlockSpec((t, ATT_HEADS * MLA_VDIM), lambda b, hp, qi: (b * nq + qi, hp)),
        out_shape=jax.ShapeDtypeStruct((n, MLA_HEADS * MLA_VDIM), BF16),
        compiler_params=_params(("arbitrary", "arbitrary", "arbitrary")), name="mla_prompt_attention",
    )(qm, km, vt)


def _tail(x2, og, om, p2, tailw, tm):
    n = x2.shape[0]
    row_spec = lambda w: pl.BlockSpec((tm, w), lambda i: (i, 0))
    return pl.pallas_call(
        _tail_kernel,
        grid=(n // tm,),
        in_specs=[row_spec(D_MODEL), row_spec(GDN_WIDTH), row_spec(GDN_WIDTH), row_spec(PLE_DIM)]
        + [_const_spec(a, 1) for a in tailw],
        out_specs=row_spec(D_MODEL), out_shape=jax.ShapeDtypeStruct((n, D_MODEL), F32),
        compiler_params=_params(("arbitrary",)), name="layer_tail",
    )(x2, og, om, p2, *tailw)


def _sample_prep(qm, km, gk, wk):
    n = qm.shape[0]
    return pl.pallas_call(
        _sample_prep_kernel,
        out_shape=[jax.ShapeDtypeStruct((MLA_HEADS, n, KV_LORA), F32), jax.ShapeDtypeStruct((MLA_HEADS, n, MLA_ROPE), F32),
                   jax.ShapeDtypeStruct((MLA_HEADS, n, 1), F32)],
        compiler_params=_params(), name="sample_prep",
    )(qm, km, gk, wk)


def _paged_attention(page_table, pool_ckv, pool_kr, qabs, qr, sself, ckv_new, wkt):
    bs, n_pages = page_table.shape
    page = pool_ckv.shape[1]
    pp = PAGES_PER_STEP
    nsteps = n_pages // pp
    assert n_pages % pp == 0 and nsteps % DECODE_SLOTS == 0 and DECODE_AHEAD < DECODE_SLOTS, "page groups must fill whole buffer rings"
    per_b = lambda w: pl.BlockSpec((None, MLA_HEADS, w), lambda b, pt: (b, 0, 0))
    grid_spec = pltpu.PrefetchScalarGridSpec(
        num_scalar_prefetch=1, grid=(bs,),
        in_specs=[per_b(KV_LORA), per_b(MLA_ROPE), per_b(1),
                  pl.BlockSpec((None, 1, KV_LORA), lambda b, pt: (b, 0, 0)),
                  pl.BlockSpec(wkt.shape, lambda b, pt: (0, 0)),
                  pl.BlockSpec(memory_space=pl.ANY), pl.BlockSpec(memory_space=pl.ANY)],
        out_specs=pl.BlockSpec((None, MLA_HEADS, KV_LORA), lambda b, pt: (b, 0, 0)),
        scratch_shapes=[pltpu.VMEM((DECODE_SLOTS, pp * page, KV_LORA), F32), pltpu.VMEM((DECODE_SLOTS, MLA_ROPE, pp * page), F32),
                        pltpu.SemaphoreType.DMA((DECODE_SLOTS, 2))])
    return pl.pallas_call(
        functools.partial(_paged_kernel, nsteps, page), grid_spec=grid_spec,
        out_shape=jax.ShapeDtypeStruct((bs, MLA_HEADS, KV_LORA), F32),
        compiler_params=_params(("arbitrary",)), name="mla_paged_decode",
    )(page_table, qabs, qr, sself, ckv_new, wkt, pool_ckv, pool_kr)


def _sample_gdn(q3, k3, v3, gb3, state):
    bs = q3.shape[0]
    hk = GDN_HEADS * GDN_DK
    rb = _pick(bs, (SAMPLE_GDN_ROWS, 2, 1))
    b3 = lambda r, w: pl.BlockSpec((rb, r, w), lambda b: (b, 0, 0))
    return pl.pallas_call(
        _sample_gdn_kernel, grid=(bs // rb,),
        in_specs=[b3(1, hk), b3(1, hk), b3(GDN_HEADS, GDN_DV), b3(1, LANES), b3(hk, GDN_DV)],
        out_specs=[b3(hk, GDN_DV), b3(GDN_HEADS, GDN_DV)],
        out_shape=[jax.ShapeDtypeStruct((bs, hk, GDN_DV), F32), jax.ShapeDtypeStruct((bs, GDN_HEADS, GDN_DV), F32)],
        compiler_params=_params(("arbitrary",)), name="gdn_recurrent_step",
    )(q3, k3, v3, gb3, state)


def _sample_mix(o2, z, olat2, extra):
    n = o2.shape[0]
    return pl.pallas_call(
        _sample_mix_kernel,
        out_shape=[jax.ShapeDtypeStruct((n, GDN_WIDTH), BF16), jax.ShapeDtypeStruct((n, MLA_HEADS * MLA_VDIM), BF16)],
        compiler_params=_params(), name="sample_mix",
    )(o2, z, olat2, extra["wv_bd"], extra["esum"], extra["gout"])


def _pick(n, prefs):
    for t in prefs:
        if n % t == 0:
            return t
    return n


def kernel(x_prompt, x_sample, cache_ckv, cache_krope, state_gdn, state_conv, page_table, p_prompt, p_sample, g_attn, w_in, w_conv, gdn_a_log, gdn_dt_bias, g_gdn_out, g_q_a, w_q_b, g_q_nope, g_q_rope, g_kv_a, g_k_rope, w_kv_b, g_k_nope, w_o, g_ffn, w_ffn_gate, w_ffn_up, w_ffn_down, g_ple, w_ple_gate, w_ple_proj):
    depth = g_attn.shape[0]
    assert depth == 1 and x_sample.shape[1] == 1, "single layer, single new token per sample row"
    bp, seq, _ = x_prompt.shape
    bs = x_sample.shape[0]
    past = page_table.shape[1] * cache_ckv.shape[2]
    li = 0
    mixer, extra = _prepare_weights(g_attn[li], w_in[li], w_conv[li], gdn_a_log[li], gdn_dt_bias[li], g_gdn_out[li],
                                    g_q_a[li], w_q_b[li], g_q_nope[li], g_q_rope[li], g_kv_a[li], g_k_rope[li],
                                    w_kv_b[li], g_k_nope[li])
    row = lambda v: v.reshape(1, -1).astype(F32)
    tailw = [w_o[li].astype(BF16), row(g_ffn[li]), w_ffn_gate[li].astype(BF16), w_ffn_up[li].astype(BF16),
             w_ffn_down[li].astype(BF16), row(g_ple[li]), w_ple_gate[li].astype(BF16), w_ple_proj[li].astype(BF16)]

    n = bp * seq
    xp2 = x_prompt.reshape(n, D_MODEL)
    tm = _pick(seq, (256, 128, 64, 32, 16, 8))
    (qg, kg, vg, gb, z, qm, km, vt, ckv_p, kr_p, conv_p) = _prompt_mixer(
        xp2, _rope_slabs(jnp.arange(seq)), mixer, extra["wvt"], bp, seq, _pick(seq, (512, 256, 128)))
    tb = _pick(seq, (2048, 1024, 512, 256))
    og, gdn_p = _gdn_prompt(qg, kg, vg, gb, z, extra, bp, seq, tb)
    om = _attn_prompt(qm, km, vt, bp, seq)
    y_prompt = _tail(xp2, og, om, p_prompt[li].reshape(n, PLE_DIM), tailw, tm).reshape(bp, seq, D_MODEL)

    xs2 = x_sample.reshape(bs, D_MODEL)
    hist = jnp.moveaxis(state_conv[li], 1, 0)
    (qg_s, kg_s, vg_s, gb_s, z_s, qm_s, km_s, ckv_s, kr_s, cin_s) = _sample_mixer(
        xs2, _rope_slabs(past + jnp.arange(1)), hist, mixer)
    qabs, qr, sself = _sample_prep(qm_s, km_s, mixer["gk"], mixer["wk"])
    per_row = lambda t: jnp.swapaxes(t, 0, 1)
    pool_krt = jnp.swapaxes(cache_krope[li], 1, 2)
    olat = _paged_attention(page_table, cache_ckv[li], pool_krt, per_row(qabs), per_row(qr), per_row(sself),
                            ckv_s.reshape(bs, 1, KV_LORA), extra["wkt"])
    hk = GDN_HEADS * GDN_DK
    s_new, o_s = _sample_gdn(qg_s.reshape(bs, 1, hk), kg_s.reshape(bs, 1, hk), vg_s.reshape(bs, GDN_HEADS, GDN_DV),
                             gb_s.reshape(bs, 1, LANES), state_gdn[li].reshape(bs, hk, GDN_DV))
    og_s, om_s = _sample_mix(o_s.reshape(bs, GDN_WIDTH), z_s, olat.reshape(bs, MLA_HEADS * KV_LORA), extra)
    y_sample = _tail(xs2, og_s, om_s, p_sample[li].reshape(bs, PLE_DIM), tailw, bs).reshape(bs, 1, D_MODEL)
    conv_s = jnp.concatenate([state_conv[li][:, 1:], cin_s[:, None, :]], axis=1)

    return (y_prompt, y_sample,
            ckv_p.reshape(1, bp, seq, KV_LORA), kr_p.reshape(1, bp, seq, MLA_ROPE),
            gdn_p[None], conv_p[None],
            ckv_s.reshape(1, bs, 1, KV_LORA), kr_s.reshape(1, bs, 1, MLA_ROPE),
            s_new.reshape(1, bs, GDN_HEADS, GDN_DK, GDN_DV), conv_s[None])
```

```python
import functools
import math

import numpy as np
import jax
import jax.numpy as jnp
from jax import lax
from jax.experimental import pallas as pl
from jax.experimental.pallas import tpu as pltpu

F32 = jnp.float32
BF16 = jnp.bfloat16

D_MODEL = 1024
PLE_DIM = 256
GDN_HEADS = 8
GDN_DK = 64
GDN_DV = 64
GDN_WIDTH = GDN_HEADS * GDN_DV
GDN_QK_DIM = GDN_HEADS * GDN_DK
CONV_WIDTH = 4
CONV_DIM = 2 * GDN_QK_DIM + GDN_WIDTH
GDN_CHUNK = 64
MLA_HEADS = 8
MLA_NOPE = 64
MLA_ROPE = 32
MLA_VDIM = 64
MLA_QK = MLA_NOPE + MLA_ROPE
Q_LORA = 384
KV_LORA = 256
ROPE_THETA = 10000.0
ATTN_SCALE = MLA_QK ** -0.5
LOG2E = math.log2(math.e)
D_FF = 2816
EPS = 1e-6

LANES = 128
SLAB = 128
HALF = MLA_ROPE // 2
ROPE_LO = MLA_NOPE
ROPE_HI = MLA_NOPE + HALF
QK_SLAB_W = MLA_HEADS * SLAB

OFF_CONV = 0
OFF_GB = OFF_CONV + CONV_DIM
OFF_Z = OFF_GB + LANES
OFF_QA = OFF_Z + GDN_WIDTH
OFF_CKV = OFF_QA + Q_LORA
OFF_KR = OFF_CKV + KV_LORA
IN_PAD = OFF_KR + LANES
BETA_LANE0 = 32

VMEM_LIMIT = 56 * 1024 * 1024


def _dot(a, b):
    return jnp.dot(a, b, preferred_element_type=F32)


def _dot_nt(a, b):
    return lax.dot_general(a, b, (((1,), (1,)), ((), ())), preferred_element_type=F32)


def _dot_tn(a, b):
    return lax.dot_general(a, b, (((0,), (0,)), ((), ())), preferred_element_type=F32)


def _split2(x):
    hi = x.astype(BF16)
    lo = (x - hi.astype(F32)).astype(BF16)
    return hi, lo


def _split3(x):
    hi = x.astype(BF16)
    r1 = x - hi.astype(F32)
    mid = r1.astype(BF16)
    lo = (r1 - mid.astype(F32)).astype(BF16)
    return hi, mid, lo


def _sigmoid(x):
    return 1.0 / (1.0 + jnp.exp(-x))


def _silu(x):
    return x * _sigmoid(x)


def _softplus(x):
    return jnp.maximum(x, 0.0) + jnp.log1p(jnp.exp(-jnp.abs(x)))


def _rmsnorm(x, g):
    return x * lax.rsqrt(jnp.mean(x * x, axis=-1, keepdims=True) + EPS) * g


def _seg_rsqrt(x, ered, eexp, nseg):
    red = _dot((x * x).astype(BF16), ered)
    r = lax.rsqrt(red + EPS)
    p0, p1, p2 = _split3(r)
    lane = lax.broadcasted_iota(jnp.int32, r.shape, 1)
    piece = jnp.where(lane < nseg, p0, jnp.where(lane < 2 * nseg, p1, p2))
    return _dot(piece, eexp)


def _rope(x, c, s_up, s_dn):
    w = x.shape[-1]
    up = pltpu.roll(x, w - HALF, axis=1)
    dn = pltpu.roll(x, HALF, axis=1)
    return x * c + up * s_up + dn * s_dn


def _tile_lanes(x, n):
    return jnp.concatenate([x] * n, axis=1)


def _const_spec(arr, ngrid):
    nd = arr.ndim
    return pl.BlockSpec(arr.shape, lambda *a, _nd=nd: (0,) * _nd, pipeline_mode=pl.Buffered(1))


MIXER_BLOCKS = 2
MIXER_CONSTS = ("g_attn", "w_in", "w_conv", "alog", "dtb", "ered_g", "eexp_g", "g_q_a", "w_qb", "gq", "ered_q",
                "eexp_q", "g_kv_a", "gkr", "ered_r", "eexp_r", "wk", "gk", "ered_k", "eexp_k")


def _mixer_rows(xs, c, conv_fn, ropes, q_scale):
    n = range(len(xs))
    w_in = c["w_in"]
    xn = [_rmsnorm(x, c["g_attn"][...]).astype(BF16) for x in xs]

    conv_in = [_dot(xn[i], w_in[:, OFF_CONV:OFF_GB]) for i in n]
    y = [_silu(v) for v in conv_fn(conv_in)]
    qk = [y[i][:, :2 * GDN_QK_DIM] for i in n]
    qk = [qk[i] * _seg_rsqrt(qk[i], c["ered_g"][...], c["eexp_g"][...], 2 * GDN_HEADS) for i in n]

    rest = [_dot(xn[i], w_in[:, OFF_GB:IN_PAD]) for i in n]
    part = lambda i, lo, hi: rest[i][:, lo - OFF_GB:hi - OFF_GB]
    ab = [part(i, OFF_GB, OFF_Z) for i in n]
    gb = []
    for i in n:
        lane = lax.broadcasted_iota(jnp.int32, ab[i].shape, 1)
        g_log = -jnp.exp(c["alog"][...]) * _softplus(ab[i] + c["dtb"][...])
        gb.append(jnp.where(lane < BETA_LANE0, g_log, _sigmoid(ab[i])))

    z = [part(i, OFF_Z, OFF_QA) for i in n]

    qa = [part(i, OFF_QA, OFF_CKV) for i in n]
    qs = [_dot(_rmsnorm(qa[i], c["g_q_a"][...]).astype(BF16), c["w_qb"][...]) for i in n]
    qs = [qs[i] * _seg_rsqrt(qs[i], c["ered_q"][...], c["eexp_q"][...], 2 * MLA_HEADS) * c["gq"][...] for i in n]
    q_mla = [_rope(qs[i], *[_tile_lanes(t, MLA_HEADS) for t in ropes[i]]) * q_scale for i in n]

    ckv = [_rmsnorm(part(i, OFF_CKV, OFF_KR), c["g_kv_a"][...]) for i in n]
    ckv_bf = [v.astype(BF16) for v in ckv]
    kr = [part(i, OFF_KR, IN_PAD) for i in n]
    kr = [kr[i] * _seg_rsqrt(kr[i], c["ered_r"][...], c["eexp_r"][...], 1) * c["gkr"][...] for i in n]
    kr = [_rope(kr[i], *ropes[i]) for i in n]
    kk = [_dot(ckv_bf[i], c["wk"][...]) for i in n]
    kk = [kk[i] * _seg_rsqrt(kk[i], c["ered_k"][...], c["eexp_k"][...], MLA_HEADS) * c["gk"][...] for i in n]
    return [dict(conv_in=conv_in[i], q_g=qk[i][:, :GDN_QK_DIM] * (GDN_DK ** -0.5), k_g=qk[i][:, GDN_QK_DIM:],
                 v_g=y[i][:, 2 * GDN_QK_DIM:], gb=gb[i], z=z[i], q_mla=q_mla[i],
                 k_mla=kk[i] + _tile_lanes(kr[i], MLA_HEADS), ckv=ckv[i], ckv_bf=ckv_bf[i], kr=kr[i]) for i in n]


def _prompt_mixer_kernel(tiles_per_seq, tm, x_ref, rc_ref, rsu_ref, rsd_ref, *refs):
    nc = len(MIXER_CONSTS)
    c = dict(zip(MIXER_CONSTS, refs[:nc]))
    wvt_ref = refs[nc]
    (qg_ref, kg_ref, vg_ref, gb_ref, z_ref, qm_ref, km_ref, vt_ref, ckv_ref, kro_ref, cs_ref, ext_ref) = refs[nc + 1:]
    i = pl.program_id(0)
    hm = tm // MIXER_BLOCKS
    blocks = [slice(j * hm, (j + 1) * hm) for j in range(MIXER_BLOCKS)]

    @pl.when(i % tiles_per_seq == 0)
    def _():
        ext_ref[5:8, :] = jnp.zeros((3, CONV_DIM), F32)

    def conv_fn(conv_ins):
        w = c["w_conv"]
        for sl, v in zip(blocks, conv_ins):
            ext_ref[8 + sl.start:8 + sl.stop, :] = v
        return [v * w[3:4, :] + ext_ref[7 + sl.start:7 + sl.stop, :] * w[2:3, :]
                + ext_ref[6 + sl.start:6 + sl.stop, :] * w[1:2, :] + ext_ref[5 + sl.start:5 + sl.stop, :] * w[0:1, :]
                for sl, v in zip(blocks, conv_ins)]

    res = _mixer_rows([x_ref[sl, :] for sl in blocks], c, conv_fn,
                      [(rc_ref[sl, :], rsu_ref[sl, :], rsd_ref[sl, :]) for sl in blocks], ATTN_SCALE * LOG2E)
    last3 = ext_ref[tm + 5:tm + 8, :]
    ext_ref[5:8, :] = last3
    cs_ref[0] = last3
    for sl, r in zip(blocks, res):
        qg_ref[sl, :] = r["q_g"]
        kg_ref[sl, :] = r["k_g"]
        vg_ref[sl, :] = r["v_g"]
        gb_ref[sl, :] = r["gb"]
        z_ref[sl, :] = r["z"]
        qm_ref[sl, :] = r["q_mla"].astype(BF16)
        km_ref[sl, :] = r["k_mla"].astype(BF16)
        vt_ref[:, sl] = _dot_nt(wvt_ref[...], r["ckv_bf"]).astype(BF16)
        ckv_ref[sl, :] = r["ckv"]
        kro_ref[sl, :] = r["kr"][:, ROPE_LO:ROPE_LO + MLA_ROPE]


def _sample_mixer_kernel(x_ref, rc_ref, rsu_ref, rsd_ref, hist_ref, *refs):
    nc = len(MIXER_CONSTS)
    c = dict(zip(MIXER_CONSTS, refs[:nc]))
    (qg_ref, kg_ref, vg_ref, gb_ref, z_ref, qm_ref, km_ref, ckv_ref, kro_ref, cin_ref) = refs[nc:]

    def conv_fn(conv_ins):
        w = c["w_conv"]
        return [v * w[3:4, :] + hist_ref[2] * w[2:3, :] + hist_ref[1] * w[1:2, :] + hist_ref[0] * w[0:1, :] for v in conv_ins]

    (r,) = _mixer_rows([x_ref[...]], c, conv_fn, [(rc_ref[...], rsu_ref[...], rsd_ref[...])], ATTN_SCALE)
    qg_ref[...] = r["q_g"]
    kg_ref[...] = r["k_g"]
    vg_ref[...] = r["v_g"]
    gb_ref[...] = r["gb"]
    z_ref[...] = r["z"]
    qm_ref[...] = r["q_mla"]
    km_ref[...] = r["k_mla"]
    ckv_ref[...] = r["ckv"]
    kro_ref[...] = r["kr"][:, ROPE_LO:ROPE_LO + MLA_ROPE]
    cin_ref[...] = r["conv_in"]


GROUP_HEADS = 4
GROUP_W = GROUP_HEADS * GDN_DK


def _cumsum_rows(x, period):
    row = lax.broadcasted_iota(jnp.int32, x.shape, 0) % period
    s = 1
    while s < period:
        x = x + jnp.where(row >= s, pltpu.roll(x, s, axis=0), 0.0)
        s *= 2
    return x


GDN_ITER_CHUNKS = 4


def _bd(y, bd_mask):
    return jnp.where(bd_mask, jnp.concatenate([y] * GROUP_HEADS, axis=0), 0.0)


def _gdn_solve_stages(chains, masks, sols):
    bd_mask, tril_cat, strict_cat, eye_cat = masks
    c = GDN_CHUNK
    n = range(len(chains))
    bd = lambda y: _bd(y, bd_mask)
    kb = [ch["k"] * ch["b_r"] for ch in chains]
    aq = [_dot_nt(jnp.concatenate([kb[i], chains[i]["q"]], axis=0), bd(chains[i]["k"])) for i in n]
    yield
    decay = []
    for ch in chains:
        g_col = jnp.sum(jnp.where(eye_cat, ch["g_r"], 0.0), axis=0, keepdims=True)
        decay.append(jnp.exp(jnp.where(tril_cat, ch["g_r"] - g_col, -jnp.inf)))
    qk = [aq[i][c:] * decay[i] for i in n]
    p = [-jnp.where(strict_cat, aq[i][:c] * decay[i], 0.0) for i in n]
    s_inv = [jnp.where(eye_cat, 1.0, 0.0) + p[i] for i in n]
    p = [_dot(p[i], bd(p[i])) for i in n]
    yield
    for lvl in range(5):
        s_next = [s_inv[i] + _dot(p[i], bd(s_inv[i])) for i in n]
        yield
        if lvl < 4:
            p = [_dot(p[i], bd(p[i])) for i in n]
            yield
        s_inv = s_next
    eg = [jnp.exp(ch["g_r"]) for ch in chains]
    u = [_dot(s_inv[i], bd(chains[i]["v"] * chains[i]["b_r"])) for i in n]
    yield
    w = [_dot(s_inv[i], bd(kb[i] * eg[i])) for i in n]
    for i in n:
        g_r = chains[i]["g_r"]
        g_last = g_r[c - 1:c, :]
        sols.append(dict(u=u[i], wq=jnp.concatenate([w[i], chains[i]["q"] * eg[i]], axis=0), qk=qk[i],
                         kd=chains[i]["k"] * jnp.exp(g_last - g_r), e_last=jnp.exp(g_last)))


def _gdn_apply_stages(sols, states, bd_mask, box):
    c = GDN_CHUNK
    ngrp = len(states)
    n = range(ngrp)
    outs = []
    for t in range(len(sols) // ngrp):
        sl = sols[t * ngrp:(t + 1) * ngrp]
        ws = [_dot(sl[i]["wq"], states[i]) for i in n]
        yield
        v_new = [sl[i]["u"] - ws[i][:c] for i in n]
        o = [ws[i][c:] + _dot(sl[i]["qk"], _bd(v_new[i], bd_mask)) for i in n]
        upd = [_dot_tn(sl[i]["kd"], v_new[i]) for i in n]
        yield
        states = [states[i] * sl[i]["e_last"] + jnp.where(bd_mask, upd[i], 0.0) for i in n]
        outs.append(jnp.concatenate(o, axis=1))
    box["o"] = jnp.concatenate(outs, axis=0)
    box["states"] = states


def _interleave(stage_generators):
    live = list(stage_generators)
    while live:
        for g in list(live):
            try:
                next(g)
            except StopIteration:
                live.remove(g)


def _gdn_kernel(nblk, tb, q_ref, k_ref, v_ref, gb_ref, z_ref, e6_ref, esum_ref, gout_ref, o_ref, sfin_ref, s_ref):
    j = pl.program_id(1)

    @pl.when(j == 0)
    def _():
        s_ref[...] = jnp.zeros(s_ref.shape, F32)

    c = GDN_CHUNK
    ngrp = GDN_HEADS // GROUP_HEADS
    rows_it = GDN_ITER_CHUNKS * c
    r_bd = lax.broadcasted_iota(jnp.int32, (GROUP_W, GROUP_W), 0) // c
    c_bd = lax.broadcasted_iota(jnp.int32, (GROUP_W, GROUP_W), 1) // c
    bd_mask = r_bd == c_bd
    ri = lax.broadcasted_iota(jnp.int32, (c, GROUP_W), 0)
    cj = lax.broadcasted_iota(jnp.int32, (c, GROUP_W), 1) % c
    masks = (bd_mask, ri >= cj, ri > cj, ri == cj)
    lane = lax.broadcasted_iota(jnp.int32, (rows_it, LANES), 1)
    copy_id = (lane % BETA_LANE0) // GDN_HEADS

    def chains_of(rows):
        gb = gb_ref[rows, :]
        sc = jnp.where(lane < BETA_LANE0, _cumsum_rows(gb, c), gb)
        p0, p1, p2 = _split3(sc)
        piece = jnp.where(copy_id == 0, p0, jnp.where(copy_id == 1, p1, p2))
        ex = _dot(piece, e6_ref[...])
        q = q_ref[rows, :]
        k = k_ref[rows, :]
        v = v_ref[rows, :]
        chains = []
        for t in range(GDN_ITER_CHUNKS):
            rs = slice(t * c, (t + 1) * c)
            for grp in range(ngrp):
                sl = slice(grp * GROUP_W, (grp + 1) * GROUP_W)
                sb = slice(GDN_WIDTH + grp * GROUP_W, GDN_WIDTH + (grp + 1) * GROUP_W)
                chains.append(dict(q=q[rs, sl], k=k[rs, sl], v=v[rs, sl], g_r=ex[rs, sl], b_r=ex[rs, sb]))
        return chains

    def finish(o, rows):
        hi, lo = _split2(o * o)
        ms = _dot(hi, esum_ref[...]) + _dot(lo, esum_ref[...])
        o_ref[rows, :] = (o * lax.rsqrt(ms + EPS) * gout_ref[...] * _silu(z_ref[rows, :])).astype(o_ref.dtype)

    states = [s_ref[grp] for grp in range(ngrp)]
    pending = None
    for gi in range(tb // rows_it):
        rows = slice(gi * rows_it, (gi + 1) * rows_it)
        sols, box = [], {}
        stages = [_gdn_solve_stages(chains_of(rows), masks, sols)]
        if pending is not None:
            stages.append(_gdn_apply_stages(pending[0], states, bd_mask, box))
        _interleave(stages)
        if pending is not None:
            states = box["states"]
            finish(box["o"], pending[1])
        pending = (sols, rows)
    box = {}
    _interleave([_gdn_apply_stages(pending[0], states, bd_mask, box)])
    finish(box["o"], pending[1])
    for grp in range(ngrp):
        s_ref[grp] = box["states"][grp]

    @pl.when(j == nblk - 1)
    def _():
        for grp in range(ngrp):
            for h in range(GROUP_HEADS):
                sfin_ref[0, grp * GROUP_HEADS + h] = s_ref[grp, h * c:(h + 1) * c, h * c:(h + 1) * c]


ATT_T = 256
ATT_HEADS = 8
ATT_SUM_ROWS = 16
ATT_WIDE = 2


def _attn_kernel(q_ref, k_ref, vt_ref, o_ref):
    qi = pl.program_id(2)
    t = ATT_T
    qs = [q_ref[:, hh * SLAB:(hh + 1) * SLAB] for hh in range(ATT_HEADS)]

    def score_tile(k0, tk):
        return tuple(_dot_nt(k_ref[pl.ds(k0, tk), hh * SLAB:(hh + 1) * SLAB], qs[hh]) for hh in range(ATT_HEADS))

    def consume(k0, tk, carry, masked):
        scores = score_tile(k0, tk)
        stats = []
        for hh in range(ATT_HEADS):
            m, _ = carry[hh]
            s = scores[hh]
            if masked:
                kp = lax.broadcasted_iota(jnp.int32, (tk, t), 0)
                qp = lax.broadcasted_iota(jnp.int32, (tk, t), 1)
                s = jnp.where(qp >= kp, s, -jnp.inf)
            m_new = jnp.maximum(m, jnp.max(s, axis=0, keepdims=True))
            stats.append((m_new, jnp.exp2(m - m_new), jnp.exp2(s - m_new).astype(BF16)))
        out = []
        for hh in range(ATT_HEADS):
            m_new, alpha, p = stats[hh]
            vt = vt_ref[hh * MLA_VDIM:(hh + 1) * MLA_VDIM, pl.ds(k0, tk)]
            ones = jnp.ones((ATT_SUM_ROWS, tk), BF16)
            out.append((m_new, carry[hh][1] * alpha + _dot(jnp.concatenate([vt, ones], axis=0), p)))
        return tuple(out)

    init = tuple((jnp.full((1, t), -jnp.inf, F32), jnp.zeros((MLA_VDIM + ATT_SUM_ROWS, t), F32)) for _ in range(ATT_HEADS))
    wide = ATT_WIDE * t
    nwide = qi // ATT_WIDE
    carry = lax.fori_loop(0, nwide, lambda j, c: consume(pl.multiple_of(j * wide, wide), wide, c, False), init)
    carry = lax.fori_loop(nwide * ATT_WIDE, qi, lambda j, c: consume(pl.multiple_of(j * t, t), t, c, False), carry)
    carry = consume(pl.multiple_of(qi * t, t), t, carry, True)
    outs = [acc[:MLA_VDIM] / acc[MLA_VDIM:MLA_VDIM + 1] for (_, acc) in carry]
    o_ref[...] = jnp.concatenate(outs, axis=0).T.astype(o_ref.dtype)


def _tail_kernel(x_ref, og_ref, om_ref, p_ref, wo_ref, gffn_ref, wg_ref, wu_ref, wd_ref, gple_ref, wpg_ref, wpp_ref, y_ref):
    h = x_ref[...] + _dot(og_ref[...], wo_ref[:GDN_WIDTH, :]) + _dot(om_ref[...], wo_ref[GDN_WIDTH:, :])
    u = _rmsnorm(h, gffn_ref[...]).astype(BF16)
    act = (_silu(_dot(u, wg_ref[...])) * _dot(u, wu_ref[...])).astype(BF16)
    acc = h + _dot(act, wd_ref[...])
    gate = _sigmoid(_dot(_rmsnorm(acc, gple_ref[...]).astype(BF16), wpg_ref[...]))
    y_ref[...] = acc + _dot(p_ref[...].astype(BF16), wpp_ref[...]) * gate


PAGES_PER_STEP = 32
PAGES_PER_SUB = 8
DECODE_SLOTS = 4
DECODE_AHEAD = 2


def _sample_prep_kernel(qm_ref, km_ref, gk_ref, wk_ref, qabs_ref, qr_ref, sself_ref):
    qm = qm_ref[...]
    prod = qm * km_ref[...]
    qg = (qm * gk_ref[...]).astype(BF16)
    for h in range(MLA_HEADS):
        sl = slice(h * SLAB, (h + 1) * SLAB)
        qabs_ref[h] = _dot_nt(qg[:, sl], wk_ref[:, sl])
        qr_ref[h] = qm[:, h * SLAB + ROPE_LO:h * SLAB + ROPE_LO + MLA_ROPE]
        sself_ref[h] = jnp.sum(prod[:, sl], axis=1, keepdims=True)


def _paged_kernel(nsteps, page, pt_ref, qabs_ref, qr_ref, sself_ref, ckvn_ref, wkt_ref, ckv_hbm, krt_hbm, o_ref,
                  cbuf, kbuf, sems):
    pp = PAGES_PER_STEP
    b = pl.program_id(0)
    nb = pl.num_programs(0)

    def copies(row, g):
        slot = g % DECODE_SLOTS
        out = []
        for j in range(pp):
            pg = pt_ref[row, g * pp + j]
            out.append(pltpu.make_async_copy(ckv_hbm.at[pg], cbuf.at[slot, pl.ds(j * page, page), :], sems.at[slot, 0]))
            out.append(pltpu.make_async_copy(krt_hbm.at[pg], kbuf.at[slot, :, pl.ds(j * page, page)], sems.at[slot, 1]))
        return out

    @pl.when(b == 0)
    def _():
        for g in range(DECODE_AHEAD):
            for cp in copies(0, g):
                cp.start()

    qa = qabs_ref[...].astype(BF16)
    lhs = jnp.concatenate([wkt_ref[...], qa, jnp.zeros_like(qa)], axis=0)
    nk = MLA_HEADS * MLA_NOPE
    qr = qr_ref[...].astype(BF16)
    m = sself_ref[...]
    l = jnp.ones(m.shape, F32)
    acc = jnp.broadcast_to(ckvn_ref[...], (MLA_HEADS, KV_LORA))
    sub_tok = PAGES_PER_SUB * page

    def update(s, cs, m, l, acc):
        m_new = jnp.maximum(m, jnp.max(s, axis=1, keepdims=True))
        alpha = jnp.exp(m - m_new)
        p = jnp.exp(s - m_new)
        l = l * alpha + jnp.sum(p, axis=1, keepdims=True)
        return m_new, l, acc * alpha + _dot(p.astype(BF16), jnp.concatenate(cs, axis=0))

    pending = None
    for g in range(nsteps):
        slot = g % DECODE_SLOTS
        for cp in copies(b, g):
            cp.wait()
        ng = g + DECODE_AHEAD
        nrow, ng = (b, ng) if ng < nsteps else (jnp.minimum(b + 1, nb - 1), ng - nsteps)
        for cp in copies(nrow, ng):
            cp.start()
        cs, ss_list = [], []
        for i in range(pp // PAGES_PER_SUB):
            c = cbuf[slot, i * sub_tok:(i + 1) * sub_tok, :].astype(BF16)
            krt = kbuf[slot, :, i * sub_tok:(i + 1) * sub_tok].astype(BF16)
            kvq = _dot_nt(lhs, c)
            kv = kvq[:nk]
            ss = jnp.sum((kv * kv).reshape(MLA_HEADS, MLA_NOPE, kv.shape[1]), axis=1)
            ss_list.append(kvq[nk:nk + MLA_HEADS] * lax.rsqrt(ss * (1.0 / MLA_NOPE) + EPS) + _dot(qr, krt))
            cs.append(c)
        if pending is not None:
            m, l, acc = update(*pending, m, l, acc)
        pending = (jnp.concatenate(ss_list, axis=1), cs)
    m, l, acc = update(*pending, m, l, acc)
    o_ref[...] = acc / l

    @pl.when(b == nb - 1)
    def _():
        for g in range(DECODE_AHEAD):
            for cp in copies(b, g):
                cp.wait()


SAMPLE_GDN_ROWS = 4


def _sample_gdn_kernel(q_ref, k_ref, v_ref, gb_ref, s_ref, snew_ref, o_ref):
    hk = GDN_HEADS * GDN_DK
    hrow = lax.broadcasted_iota(jnp.int32, (GDN_HEADS, hk), 0)
    hlane = lax.broadcasted_iota(jnp.int32, (GDN_HEADS, hk), 1) // GDN_DK
    hm = hrow == hlane
    r8 = lax.broadcasted_iota(jnp.int32, (GDN_HEADS, LANES), 0)
    l8 = lax.broadcasted_iota(jnp.int32, (GDN_HEADS, LANES), 1)
    ones = jnp.where(hm, 1.0, 0.0).astype(BF16)
    rows = range(q_ref.shape[0])
    kmask = [jnp.where(hm, jnp.broadcast_to(k_ref[r], (GDN_HEADS, hk)), 0.0) for r in rows]
    qmask = [jnp.where(hm, jnp.broadcast_to(q_ref[r], (GDN_HEADS, hk)), 0.0) for r in rows]
    gbb = [jnp.broadcast_to(gb_ref[r], (GDN_HEADS, LANES)) for r in rows]
    g_col = [jnp.sum(jnp.where(l8 == r8, gbb[r], 0.0), axis=1, keepdims=True) for r in rows]
    b_col = [jnp.sum(jnp.where(l8 == r8 + BETA_LANE0, gbb[r], 0.0), axis=1, keepdims=True) for r in rows]

    def expand_rows(col):
        e0, e1, e2 = _split3(jnp.broadcast_to(col, (GDN_HEADS, GDN_DV)))
        return _dot_tn(jnp.concatenate([ones, ones, ones], axis=0), jnp.concatenate([e0, e1, e2], axis=0))

    def head_dot(xmask, mat):
        xh, xl = _split2(xmask)
        mh, ml = _split2(mat)
        return _dot(jnp.concatenate([xh, xh, xl], axis=1), jnp.concatenate([mh, ml, mh], axis=0))

    def outer(kmask_r, delta_r):
        kh, kl = _split2(kmask_r)
        dh, dl = _split2(delta_r)
        return _dot_tn(jnp.concatenate([kh, kh, kl], axis=0), jnp.concatenate([dh, dl, dh], axis=0))

    decay = [expand_rows(jnp.exp(g_col[r])) for r in rows]
    state = [s_ref[r] * decay[r] for r in rows]
    ks = [head_dot(kmask[r], state[r]) for r in rows]
    delta = [(v_ref[r] - ks[r]) * b_col[r] for r in rows]
    upd = [outer(kmask[r], delta[r]) for r in rows]
    state = [state[r] + upd[r] for r in rows]
    outs = [head_dot(qmask[r], state[r]) for r in rows]
    for r in rows:
        snew_ref[r] = state[r]
        o_ref[r] = outs[r]


def _sample_mix_kernel(o_ref, z_ref, olat_ref, wvbd_ref, esum_ref, gout_ref, og_ref, om_ref):
    o = o_ref[...]
    hi, lo = _split2(o * o)
    ms = _dot(hi, esum_ref[...]) + _dot(lo, esum_ref[...])
    og_ref[...] = (o * lax.rsqrt(ms + EPS) * gout_ref[...] * _silu(z_ref[...])).astype(og_ref.dtype)
    om_ref[...] = _dot(olat_ref[...].astype(BF16), wvbd_ref[...]).astype(om_ref.dtype)


def _seg_tables(width, segs, mean):
    n = len(segs)
    ered = np.zeros((width, LANES), np.float32)
    eexp = np.zeros((LANES, width), np.float32)
    for copy in range(3):
        for s, (a, b) in enumerate(segs):
            ered[a:b, copy * n + s] = 1.0 / (b - a) if mean else 1.0
            eexp[copy * n + s, a:b] = 1.0
    return jnp.asarray(ered, BF16), jnp.asarray(eexp, BF16)


def _rope_slabs(pos):
    inv_freq = ROPE_THETA ** (-jnp.arange(HALF, dtype=F32) / HALF)
    ang = pos.astype(F32)[:, None] * inv_freq[None, :]
    cos, sin = jnp.cos(ang), jnp.sin(ang)
    n = pos.shape[0]
    zeros = lambda w: jnp.zeros((n, w), F32)
    rc = jnp.concatenate([jnp.ones((n, MLA_NOPE), F32), cos, cos, zeros(SLAB - MLA_QK)], axis=1)
    rsu = jnp.concatenate([zeros(MLA_NOPE), -sin, zeros(SLAB - ROPE_HI)], axis=1)
    rsd = jnp.concatenate([zeros(ROPE_HI), sin, zeros(SLAB - MLA_QK)], axis=1)
    return rc, rsu, rsd


def _prepare_weights(g_attn, w_in, w_conv, gdn_a_log, gdn_dt_bias, g_gdn_out, g_q_a, w_q_b, g_q_nope, g_q_rope,
                     g_kv_a, g_k_rope, w_kv_b, g_k_nope):
    row = lambda v: v.reshape(1, -1).astype(F32)
    a = w_in[:, CONV_DIM:CONV_DIM + GDN_HEADS]
    b = w_in[:, CONV_DIM + GDN_HEADS:CONV_DIM + 2 * GDN_HEADS]
    o_z = CONV_DIM + 2 * GDN_HEADS
    o_qa = o_z + GDN_WIDTH
    o_kv = o_qa + Q_LORA
    zc = lambda w: jnp.zeros((D_MODEL, w), F32)
    gb_slab = jnp.concatenate([a, a, a, zc(BETA_LANE0 - 3 * GDN_HEADS), b, b, b, zc(LANES - BETA_LANE0 - 3 * GDN_HEADS)], axis=1)
    kr_slab = jnp.concatenate([zc(ROPE_LO), w_in[:, o_kv + KV_LORA:], zc(SLAB - MLA_QK)], axis=1)
    w_in_p = jnp.concatenate([w_in[:, :CONV_DIM], gb_slab, w_in[:, o_z:o_qa], w_in[:, o_qa:o_kv],
                              w_in[:, o_kv:o_kv + KV_LORA], kr_slab], axis=1).astype(BF16)

    def scalar_slab(v):
        z8 = jnp.zeros((BETA_LANE0 - 3 * GDN_HEADS,), F32)
        return jnp.concatenate([v, v, v, z8, jnp.zeros((LANES - BETA_LANE0,), F32)]).reshape(1, LANES)

    wq = w_q_b.reshape(Q_LORA, MLA_HEADS, MLA_QK)
    w_qb = jnp.concatenate([wq, jnp.zeros((Q_LORA, MLA_HEADS, SLAB - MLA_QK), F32)], axis=2).reshape(Q_LORA, QK_SLAB_W).astype(BF16)
    gq = jnp.tile(jnp.concatenate([g_q_nope, g_q_rope, jnp.zeros((SLAB - MLA_QK,), F32)]), MLA_HEADS).reshape(1, QK_SLAB_W)
    wkv = w_kv_b.reshape(KV_LORA, MLA_HEADS, MLA_NOPE + MLA_VDIM)
    wk_part, wv_part = wkv[:, :, :MLA_NOPE], wkv[:, :, MLA_NOPE:]
    wk = jnp.concatenate([wk_part, jnp.zeros((KV_LORA, MLA_HEADS, SLAB - MLA_NOPE), F32)], axis=2).reshape(KV_LORA, QK_SLAB_W).astype(BF16)
    gk = jnp.tile(jnp.concatenate([g_k_nope, jnp.zeros((SLAB - MLA_NOPE,), F32)]), MLA_HEADS).reshape(1, QK_SLAB_W)
    gkr = jnp.concatenate([jnp.zeros((ROPE_LO,), F32), g_k_rope, jnp.zeros((SLAB - MLA_QK,), F32)]).reshape(1, SLAB)
    wvt = wv_part.reshape(KV_LORA, MLA_HEADS * MLA_VDIM).T.astype(BF16)
    wkt = wk_part.reshape(KV_LORA, MLA_HEADS * MLA_NOPE).T.astype(BF16)
    wv_bd = jnp.zeros((MLA_HEADS, KV_LORA, MLA_HEADS, MLA_VDIM), F32)
    wv_bd = wv_bd.at[jnp.arange(MLA_HEADS), :, jnp.arange(MLA_HEADS), :].set(jnp.moveaxis(wv_part, 1, 0))
    wv_bd = wv_bd.reshape(MLA_HEADS * KV_LORA, MLA_HEADS * MLA_VDIM).astype(BF16)

    ered_g, eexp_g = _seg_tables(2 * GDN_QK_DIM, [(h * GDN_DK, (h + 1) * GDN_DK) for h in range(2 * GDN_HEADS)], False)
    q_segs = ([(h * SLAB, h * SLAB + MLA_NOPE) for h in range(MLA_HEADS)]
              + [(h * SLAB + ROPE_LO, h * SLAB + MLA_QK) for h in range(MLA_HEADS)])
    ered_q, eexp_q = _seg_tables(QK_SLAB_W, q_segs, True)
    ered_k, eexp_k = _seg_tables(QK_SLAB_W, q_segs[:MLA_HEADS], True)
    ered_r, eexp_r = _seg_tables(SLAB, [(ROPE_LO, MLA_QK)], True)
    e6 = np.zeros((LANES, 2 * GDN_WIDTH), np.float32)
    for copy in range(3):
        for h in range(GDN_HEADS):
            e6[copy * GDN_HEADS + h, h * GDN_DV:(h + 1) * GDN_DV] = 1.0
            e6[BETA_LANE0 + copy * GDN_HEADS + h, GDN_WIDTH + h * GDN_DV:GDN_WIDTH + (h + 1) * GDN_DV] = 1.0
    esum = np.kron(np.eye(GDN_HEADS, dtype=np.float32), np.full((GDN_DV, GDN_DV), 1.0 / GDN_DV, np.float32))

    mixer = dict(g_attn=row(g_attn), w_in=w_in_p, w_conv=w_conv.astype(F32), alog=scalar_slab(gdn_a_log),
                 dtb=scalar_slab(gdn_dt_bias), ered_g=ered_g, eexp_g=eexp_g, g_q_a=row(g_q_a), w_qb=w_qb, gq=gq,
                 ered_q=ered_q, eexp_q=eexp_q, g_kv_a=row(g_kv_a), gkr=gkr, ered_r=ered_r, eexp_r=eexp_r, wk=wk, gk=gk,
                 ered_k=ered_k, eexp_k=eexp_k)
    extra = dict(wvt=wvt, wkt=wkt, wv_bd=wv_bd, e6=jnp.asarray(e6, BF16), esum=jnp.asarray(esum, BF16),
                 gout=jnp.tile(g_gdn_out, GDN_HEADS).reshape(1, GDN_WIDTH).astype(F32))
    return mixer, extra


def _params(sem=None):
    return pltpu.CompilerParams(dimension_semantics=sem, vmem_limit_bytes=VMEM_LIMIT)


def _prompt_mixer(x2, slabs, mixer, wvt, bsz, seq, tm):
    n = x2.shape[0]
    tps = seq // tm
    consts = [mixer[k] for k in MIXER_CONSTS] + [wvt]
    row_spec = lambda w: pl.BlockSpec((tm, w), lambda i: (i, 0))
    rope_spec = pl.BlockSpec((tm, SLAB), lambda i: (i % tps, 0))
    out_shapes = [
        jax.ShapeDtypeStruct((n, GDN_QK_DIM), F32), jax.ShapeDtypeStruct((n, GDN_QK_DIM), F32),
        jax.ShapeDtypeStruct((n, GDN_WIDTH), F32), jax.ShapeDtypeStruct((n, LANES), F32),
        jax.ShapeDtypeStruct((n, GDN_WIDTH), F32), jax.ShapeDtypeStruct((n, QK_SLAB_W), BF16),
        jax.ShapeDtypeStruct((n, QK_SLAB_W), BF16), jax.ShapeDtypeStruct((MLA_HEADS * MLA_VDIM, n), BF16),
        jax.ShapeDtypeStruct((n, KV_LORA), F32), jax.ShapeDtypeStruct((n, MLA_ROPE), F32),
        jax.ShapeDtypeStruct((bsz, CONV_WIDTH - 1, CONV_DIM), F32)]
    out_specs = [row_spec(GDN_QK_DIM), row_spec(GDN_QK_DIM), row_spec(GDN_WIDTH), row_spec(LANES), row_spec(GDN_WIDTH),
                 row_spec(QK_SLAB_W), row_spec(QK_SLAB_W), pl.BlockSpec((MLA_HEADS * MLA_VDIM, tm), lambda i: (0, i)),
                 row_spec(KV_LORA), row_spec(MLA_ROPE),
                 pl.BlockSpec((1, CONV_WIDTH - 1, CONV_DIM), lambda i: (i // tps, 0, 0))]
    return pl.pallas_call(
        functools.partial(_prompt_mixer_kernel, tps, tm),
        grid=(n // tm,),
        in_specs=[row_spec(D_MODEL), rope_spec, rope_spec, rope_spec] + [_const_spec(a, 1) for a in consts],
        out_specs=out_specs, out_shape=out_shapes,
        scratch_shapes=[pltpu.VMEM((tm + 8, CONV_DIM), F32)],
        compiler_params=_params(("arbitrary",)), name="prompt_mixer",
    )(x2, *slabs, *consts)


def _sample_mixer(x2, slabs, hist, mixer):
    n = x2.shape[0]
    consts = [mixer[k] for k in MIXER_CONSTS]
    sd = lambda w: jax.ShapeDtypeStruct((n, w), F32)
    return pl.pallas_call(
        _sample_mixer_kernel,
        out_shape=[sd(GDN_QK_DIM), sd(GDN_QK_DIM), sd(GDN_WIDTH), sd(LANES), sd(GDN_WIDTH), sd(QK_SLAB_W), sd(QK_SLAB_W),
                   sd(KV_LORA), sd(MLA_ROPE), sd(CONV_DIM)],
        compiler_params=_params(), name="sample_mixer",
    )(x2, *slabs, hist, *consts)


def _gdn_prompt(qg, kg, vg, gb, z, extra, bsz, seq, tb):
    n = qg.shape[0]
    nblk = seq // tb
    row_spec = lambda w: pl.BlockSpec((tb, w), lambda b, j: (b * nblk + j, 0))
    consts = [extra["e6"], extra["esum"], extra["gout"]]
    return pl.pallas_call(
        functools.partial(_gdn_kernel, nblk, tb),
        grid=(bsz, nblk),
        in_specs=[row_spec(GDN_QK_DIM), row_spec(GDN_QK_DIM), row_spec(GDN_WIDTH), row_spec(LANES), row_spec(GDN_WIDTH)]
        + [_const_spec(a, 2) for a in consts],
        out_specs=[row_spec(GDN_WIDTH), pl.BlockSpec((1, GDN_HEADS, GDN_DK, GDN_DV), lambda b, j: (b, 0, 0, 0))],
        out_shape=[jax.ShapeDtypeStruct((n, GDN_WIDTH), BF16), jax.ShapeDtypeStruct((bsz, GDN_HEADS, GDN_DK, GDN_DV), F32)],
        scratch_shapes=[pltpu.VMEM((GDN_HEADS // GROUP_HEADS, GROUP_W, GROUP_W), F32)],
        compiler_params=_params(("arbitrary", "arbitrary")), name="gdn_chunked",
    )(qg, kg, vg, gb, z, *consts)


def _attn_prompt(qm, km, vt, bsz, seq):
    n = qm.shape[0]
    t = ATT_T
    nq = seq // t
    return pl.pallas_call(
        _attn_kernel,
        grid=(bsz, MLA_HEADS // ATT_HEADS, nq),
        in_specs=[pl.BlockSpec((t, ATT_HEADS * SLAB), lambda b, hp, qi: (b * nq + qi, hp)),
                  pl.BlockSpec((seq, ATT_HEADS * SLAB), lambda b, hp, qi: (b, hp)),
                  pl.BlockSpec((ATT_HEADS * MLA_VDIM, seq), lambda b, hp, qi: (hp, b))],
        out_specs=pl.BlockSpec((t, ATT_HEADS * MLA_VDIM), lambda b, hp, qi: (b * nq + qi, hp)),
        out_shape=jax.ShapeDtypeStruct((n, MLA_HEADS * MLA_VDIM), BF16),
        compiler_params=_params(("arbitrary", "arbitrary", "arbitrary")), name="mla_prompt_attention",
    )(qm, km, vt)


def _tail(x2, og, om, p2, tailw, tm):
    n = x2.shape[0]
    row_spec = lambda w: pl.BlockSpec((tm, w), lambda i: (i, 0))
    return pl.pallas_call(
        _tail_kernel,
        grid=(n // tm,),
        in_specs=[row_spec(D_MODEL), row_spec(GDN_WIDTH), row_spec(GDN_WIDTH), row_spec(PLE_DIM)]
        + [_const_spec(a, 1) for a in tailw],
        out_specs=row_spec(D_MODEL), out_shape=jax.ShapeDtypeStruct((n, D_MODEL), F32),
        compiler_params=_params(("arbitrary",)), name="layer_tail",
    )(x2, og, om, p2, *tailw)


def _sample_prep(qm, km, gk, wk):
    n = qm.shape[0]
    return pl.pallas_call(
        _sample_prep_kernel,
        out_shape=[jax.ShapeDtypeStruct((MLA_HEADS, n, KV_LORA), F32), jax.ShapeDtypeStruct((MLA_HEADS, n, MLA_ROPE), F32),
                   jax.ShapeDtypeStruct((MLA_HEADS, n, 1), F32)],
        compiler_params=_params(), name="sample_prep",
    )(qm, km, gk, wk)


def _paged_attention(page_table, pool_ckv, pool_kr, qabs, qr, sself, ckv_new, wkt):
    bs, n_pages = page_table.shape
    page = pool_ckv.shape[1]
    pp = PAGES_PER_STEP
    nsteps = n_pages // pp
    assert n_pages % pp == 0 and nsteps % DECODE_SLOTS == 0 and DECODE_AHEAD < DECODE_SLOTS, "page groups must fill whole buffer rings"
    per_b = lambda w: pl.BlockSpec((None, MLA_HEADS, w), lambda b, pt: (b, 0, 0))
    grid_spec = pltpu.PrefetchScalarGridSpec(
        num_scalar_prefetch=1, grid=(bs,),
        in_specs=[per_b(KV_LORA), per_b(MLA_ROPE), per_b(1),
                  pl.BlockSpec((None, 1, KV_LORA), lambda b, pt: (b, 0, 0)),
                  pl.BlockSpec(wkt.shape, lambda b, pt: (0, 0)),
                  pl.BlockSpec(memory_space=pl.ANY), pl.BlockSpec(memory_space=pl.ANY)],
        out_specs=pl.BlockSpec((None, MLA_HEADS, KV_LORA), lambda b, pt: (b, 0, 0)),
        scratch_shapes=[pltpu.VMEM((DECODE_SLOTS, pp * page, KV_LORA), F32), pltpu.VMEM((DECODE_SLOTS, MLA_ROPE, pp * page), F32),
                        pltpu.SemaphoreType.DMA((DECODE_SLOTS, 2))])
    return pl.pallas_call(
        functools.partial(_paged_kernel, nsteps, page), grid_spec=grid_spec,
        out_shape=jax.ShapeDtypeStruct((bs, MLA_HEADS, KV_LORA), F32),
        compiler_params=_params(("arbitrary",)), name="mla_paged_decode",
    )(page_table, qabs, qr, sself, ckv_new, wkt, pool_ckv, pool_kr)


def _sample_gdn(q3, k3, v3, gb3, state):
    bs = q3.shape[0]
    hk = GDN_HEADS * GDN_DK
    rb = _pick(bs, (SAMPLE_GDN_ROWS, 2, 1))
    b3 = lambda r, w: pl.BlockSpec((rb, r, w), lambda b: (b, 0, 0))
    return pl.pallas_call(
        _sample_gdn_kernel, grid=(bs // rb,),
        in_specs=[b3(1, hk), b3(1, hk), b3(GDN_HEADS, GDN_DV), b3(1, LANES), b3(hk, GDN_DV)],
        out_specs=[b3(hk, GDN_DV), b3(GDN_HEADS, GDN_DV)],
        out_shape=[jax.ShapeDtypeStruct((bs, hk, GDN_DV), F32), jax.ShapeDtypeStruct((bs, GDN_HEADS, GDN_DV), F32)],
        compiler_params=_params(("arbitrary",)), name="gdn_recurrent_step",
    )(q3, k3, v3, gb3, state)


def _sample_mix(o2, z, olat2, extra):
    n = o2.shape[0]
    return pl.pallas_call(
        _sample_mix_kernel,
        out_shape=[jax.ShapeDtypeStruct((n, GDN_WIDTH), BF16), jax.ShapeDtypeStruct((n, MLA_HEADS * MLA_VDIM), BF16)],
        compiler_params=_params(), name="sample_mix",
    )(o2, z, olat2, extra["wv_bd"], extra["esum"], extra["gout"])


def _pick(n, prefs):
    for t in prefs:
        if n % t == 0:
            return t
    return n


def kernel(x_prompt, x_sample, cache_ckv, cache_krope, state_gdn, state_conv, page_table, p_prompt, p_sample, g_attn, w_in, w_conv, gdn_a_log, gdn_dt_bias, g_gdn_out, g_q_a, w_q_b, g_q_nope, g_q_rope, g_kv_a, g_k_rope, w_kv_b, g_k_nope, w_o, g_ffn, w_ffn_gate, w_ffn_up, w_ffn_down, g_ple, w_ple_gate, w_ple_proj):
    depth = g_attn.shape[0]
    assert depth == 1 and x_sample.shape[1] == 1, "single layer, single new token per sample row"
    bp, seq, _ = x_prompt.shape
    bs = x_sample.shape[0]
    past = page_table.shape[1] * cache_ckv.shape[2]
    li = 0
    mixer, extra = _prepare_weights(g_attn[li], w_in[li], w_conv[li], gdn_a_log[li], gdn_dt_bias[li], g_gdn_out[li],
                                    g_q_a[li], w_q_b[li], g_q_nope[li], g_q_rope[li], g_kv_a[li], g_k_rope[li],
                                    w_kv_b[li], g_k_nope[li])
    row = lambda v: v.reshape(1, -1).astype(F32)
    tailw = [w_o[li].astype(BF16), row(g_ffn[li]), w_ffn_gate[li].astype(BF16), w_ffn_up[li].astype(BF16),
             w_ffn_down[li].astype(BF16), row(g_ple[li]), w_ple_gate[li].astype(BF16), w_ple_proj[li].astype(BF16)]

    n = bp * seq
    xp2 = x_prompt.reshape(n, D_MODEL)
    tm = _pick(seq, (256, 128, 64, 32, 16, 8))
    (qg, kg, vg, gb, z, qm, km, vt, ckv_p, kr_p, conv_p) = _prompt_mixer(
        xp2, _rope_slabs(jnp.arange(seq)), mixer, extra["wvt"], bp, seq, _pick(seq, (512, 256, 128)))
    tb = _pick(seq, (1024, 512, 256))
    og, gdn_p = _gdn_prompt(qg, kg, vg, gb, z, extra, bp, seq, tb)
    om = _attn_prompt(qm, km, vt, bp, seq)
    y_prompt = _tail(xp2, og, om, p_prompt[li].reshape(n, PLE_DIM), tailw, tm).reshape(bp, seq, D_MODEL)

    xs2 = x_sample.reshape(bs, D_MODEL)
    hist = jnp.moveaxis(state_conv[li], 1, 0)
    (qg_s, kg_s, vg_s, gb_s, z_s, qm_s, km_s, ckv_s, kr_s, cin_s) = _sample_mixer(
        xs2, _rope_slabs(past + jnp.arange(1)), hist, mixer)
    qabs, qr, sself = _sample_prep(qm_s, km_s, mixer["gk"], mixer["wk"])
    per_row = lambda t: jnp.swapaxes(t, 0, 1)
    pool_krt = jnp.swapaxes(cache_krope[li], 1, 2)
    olat = _paged_attention(page_table, cache_ckv[li], pool_krt, per_row(qabs), per_row(qr), per_row(sself),
                            ckv_s.reshape(bs, 1, KV_LORA), extra["wkt"])
    hk = GDN_HEADS * GDN_DK
    s_new, o_s = _sample_gdn(qg_s.reshape(bs, 1, hk), kg_s.reshape(bs, 1, hk), vg_s.reshape(bs, GDN_HEADS, GDN_DV),
                             gb_s.reshape(bs, 1, LANES), state_gdn[li].reshape(bs, hk, GDN_DV))
    og_s, om_s = _sample_mix(o_s.reshape(bs, GDN_WIDTH), z_s, olat.reshape(bs, MLA_HEADS * KV_LORA), extra)
    y_sample = _tail(xs2, og_s, om_s, p_sample[li].reshape(bs, PLE_DIM), tailw, bs).reshape(bs, 1, D_MODEL)
    conv_s = jnp.concatenate([state_conv[li][:, 1:], cin_s[:, None, :]], axis=1)

    return (y_prompt, y_sample,
            ckv_p.reshape(1, bp, seq, KV_LORA), kr_p.reshape(1, bp, seq, MLA_ROPE),
            gdn_p[None], conv_p[None],
            ckv_s.reshape(1, bs, 1, KV_LORA), kr_s.reshape(1, bs, 1, MLA_ROPE),
            s_new.reshape(1, bs, GDN_HEADS, GDN_DK, GDN_DV), conv_s[None])
```

```python
import functools
import math

import numpy as np
import jax
import jax.numpy as jnp
from jax import lax
from jax.experimental import pallas as pl
from jax.experimental.pallas import tpu as pltpu

F32 = jnp.float32
BF16 = jnp.bfloat16

D_MODEL = 1024
PLE_DIM = 256
GDN_HEADS = 8
GDN_DK = 64
GDN_DV = 64
GDN_WIDTH = GDN_HEADS * GDN_DV
GDN_QK_DIM = GDN_HEADS * GDN_DK
CONV_WIDTH = 4
CONV_DIM = 2 * GDN_QK_DIM + GDN_WIDTH
GDN_CHUNK = 64
MLA_HEADS = 8
MLA_NOPE = 64
MLA_ROPE = 32
MLA_VDIM = 64
MLA_QK = MLA_NOPE + MLA_ROPE
Q_LORA = 384
KV_LORA = 256
ROPE_THETA = 10000.0
ATTN_SCALE = MLA_QK ** -0.5
LOG2E = math.log2(math.e)
D_FF = 2816
EPS = 1e-6

LANES = 128
SLAB = 128
HALF = MLA_ROPE // 2
ROPE_LO = MLA_NOPE
ROPE_HI = MLA_NOPE + HALF
QK_SLAB_W = MLA_HEADS * SLAB

OFF_CONV = 0
OFF_GB = OFF_CONV + CONV_DIM
OFF_Z = OFF_GB + LANES
OFF_QA = OFF_Z + GDN_WIDTH
OFF_CKV = OFF_QA + Q_LORA
OFF_KR = OFF_CKV + KV_LORA
IN_PAD = OFF_KR + LANES
BETA_LANE0 = 32

VMEM_LIMIT = 56 * 1024 * 1024


def _dot(a, b):
    return jnp.dot(a, b, preferred_element_type=F32)


def _dot_nt(a, b):
    return lax.dot_general(a, b, (((1,), (1,)), ((), ())), preferred_element_type=F32)


def _dot_tn(a, b):
    return lax.dot_general(a, b, (((0,), (0,)), ((), ())), preferred_element_type=F32)


def _split2(x):
    hi = x.astype(BF16)
    lo = (x - hi.astype(F32)).astype(BF16)
    return hi, lo


def _split3(x):
    hi = x.astype(BF16)
    r1 = x - hi.astype(F32)
    mid = r1.astype(BF16)
    lo = (r1 - mid.astype(F32)).astype(BF16)
    return hi, mid, lo


def _sigmoid(x):
    return 1.0 / (1.0 + jnp.exp(-x))


def _silu(x):
    return x * _sigmoid(x)


def _softplus(x):
    return jnp.maximum(x, 0.0) + jnp.log1p(jnp.exp(-jnp.abs(x)))


def _rmsnorm(x, g):
    return x * lax.rsqrt(jnp.mean(x * x, axis=-1, keepdims=True) + EPS) * g


def _seg_rsqrt(x, ered, eexp, nseg):
    red = _dot((x * x).astype(BF16), ered)
    r = lax.rsqrt(red + EPS)
    p0, p1, p2 = _split3(r)
    lane = lax.broadcasted_iota(jnp.int32, r.shape, 1)
    piece = jnp.where(lane < nseg, p0, jnp.where(lane < 2 * nseg, p1, p2))
    return _dot(piece, eexp)


def _rope(x, c, s_up, s_dn):
    w = x.shape[-1]
    up = pltpu.roll(x, w - HALF, axis=1)
    dn = pltpu.roll(x, HALF, axis=1)
    return x * c + up * s_up + dn * s_dn


def _tile_lanes(x, n):
    return jnp.concatenate([x] * n, axis=1)


def _const_spec(arr):
    nd = arr.ndim
    return pl.BlockSpec(arr.shape, lambda *a, _nd=nd: (0,) * _nd, pipeline_mode=pl.Buffered(1))


MIXER_BLOCKS = 2
MIXER_CONSTS = ("g_attn", "w_in", "w_conv", "alog", "dtb", "ered_g", "eexp_g", "g_q_a", "w_qb", "gq", "ered_q",
                "eexp_q", "g_kv_a", "gkr", "ered_r", "eexp_r", "wk", "gk", "ered_k", "eexp_k")


def _mixer_rows(xs, c, conv_fn, ropes, q_scale):
    n = range(len(xs))
    w_in = c["w_in"]
    xn = [_rmsnorm(x, c["g_attn"][...]).astype(BF16) for x in xs]

    conv_in = [_dot(xn[i], w_in[:, OFF_CONV:OFF_GB]) for i in n]
    rest = [_dot(xn[i], w_in[:, OFF_GB:IN_PAD]) for i in n]
    part = lambda i, lo, hi: rest[i][:, lo - OFF_GB:hi - OFF_GB]
    qs = [_dot(_rmsnorm(part(i, OFF_QA, OFF_CKV), c["g_q_a"][...]).astype(BF16), c["w_qb"][...]) for i in n]
    ckv = [_rmsnorm(part(i, OFF_CKV, OFF_KR), c["g_kv_a"][...]) for i in n]
    ckv_bf = [v.astype(BF16) for v in ckv]
    kk = [_dot(ckv_bf[i], c["wk"][...]) for i in n]

    y = [_silu(v) for v in conv_fn(conv_in)]
    qk = [y[i][:, :2 * GDN_QK_DIM] for i in n]
    qk = [qk[i] * _seg_rsqrt(qk[i], c["ered_g"][...], c["eexp_g"][...], 2 * GDN_HEADS) for i in n]

    gb = []
    for i in n:
        ab = part(i, OFF_GB, OFF_Z)
        lane = lax.broadcasted_iota(jnp.int32, ab.shape, 1)
        g_log = -jnp.exp(c["alog"][...]) * _softplus(ab + c["dtb"][...])
        gb.append(jnp.where(lane < BETA_LANE0, g_log, _sigmoid(ab)))
    z = [part(i, OFF_Z, OFF_QA) for i in n]

    qs = [qs[i] * _seg_rsqrt(qs[i], c["ered_q"][...], c["eexp_q"][...], 2 * MLA_HEADS) * c["gq"][...] for i in n]
    q_mla = [_rope(qs[i], *[_tile_lanes(t, MLA_HEADS) for t in ropes[i]]) * q_scale for i in n]

    kr = [part(i, OFF_KR, IN_PAD) for i in n]
    kr = [kr[i] * _seg_rsqrt(kr[i], c["ered_r"][...], c["eexp_r"][...], 1) * c["gkr"][...] for i in n]
    kr = [_rope(kr[i], *ropes[i]) for i in n]
    kk = [kk[i] * _seg_rsqrt(kk[i], c["ered_k"][...], c["eexp_k"][...], MLA_HEADS) * c["gk"][...] for i in n]
    return [dict(conv_in=conv_in[i], q_g=qk[i][:, :GDN_QK_DIM] * (GDN_DK ** -0.5), k_g=qk[i][:, GDN_QK_DIM:],
                 v_g=y[i][:, 2 * GDN_QK_DIM:], gb=gb[i], z=z[i], q_mla=q_mla[i],
                 k_mla=kk[i] + _tile_lanes(kr[i], MLA_HEADS), ckv=ckv[i], ckv_bf=ckv_bf[i], kr=kr[i]) for i in n]


def _prompt_mixer_kernel(tiles_per_seq, tm, x_ref, rc_ref, rsu_ref, rsd_ref, *refs):
    nc = len(MIXER_CONSTS)
    c = dict(zip(MIXER_CONSTS, refs[:nc]))
    wvt_ref = refs[nc]
    (qg_ref, kg_ref, vg_ref, gb_ref, z_ref, qm_ref, km_ref, vt_ref, ckv_ref, kro_ref, cs_ref, ext_ref) = refs[nc + 1:]
    i = pl.program_id(0)
    hm = tm // MIXER_BLOCKS
    blocks = [slice(j * hm, (j + 1) * hm) for j in range(MIXER_BLOCKS)]

    @pl.when(i % tiles_per_seq == 0)
    def _():
        ext_ref[5:8, :] = jnp.zeros((3, CONV_DIM), F32)

    def conv_fn(conv_ins):
        w = c["w_conv"]
        for sl, v in zip(blocks, conv_ins):
            ext_ref[8 + sl.start:8 + sl.stop, :] = v
        return [v * w[3:4, :] + ext_ref[7 + sl.start:7 + sl.stop, :] * w[2:3, :]
                + ext_ref[6 + sl.start:6 + sl.stop, :] * w[1:2, :] + ext_ref[5 + sl.start:5 + sl.stop, :] * w[0:1, :]
                for sl, v in zip(blocks, conv_ins)]

    res = _mixer_rows([x_ref[sl, :] for sl in blocks], c, conv_fn,
                      [(rc_ref[sl, :], rsu_ref[sl, :], rsd_ref[sl, :]) for sl in blocks], ATTN_SCALE * LOG2E)
    last3 = ext_ref[tm + 5:tm + 8, :]
    ext_ref[5:8, :] = last3
    cs_ref[0] = last3
    for sl, r in zip(blocks, res):
        qg_ref[sl, :] = r["q_g"]
        kg_ref[sl, :] = r["k_g"]
        vg_ref[sl, :] = r["v_g"]
        gb_ref[sl, :] = r["gb"]
        z_ref[sl, :] = r["z"]
        qm_ref[sl, :] = r["q_mla"].astype(BF16)
        km_ref[sl, :] = r["k_mla"].astype(BF16)
        vt_ref[:, sl] = _dot_nt(wvt_ref[...], r["ckv_bf"]).astype(BF16)
        ckv_ref[sl, :] = r["ckv"]
        kro_ref[sl, :] = r["kr"][:, ROPE_LO:ROPE_LO + MLA_ROPE]


def _sample_mixer_kernel(x_ref, rc_ref, rsu_ref, rsd_ref, hist_ref, *refs):
    nc = len(MIXER_CONSTS)
    c = dict(zip(MIXER_CONSTS, refs[:nc]))
    (qg_ref, kg_ref, vg_ref, gb_ref, z_ref, qm_ref, km_ref, ckv_ref, kro_ref, cin_ref) = refs[nc:]

    def conv_fn(conv_ins):
        w = c["w_conv"]
        return [v * w[3:4, :] + hist_ref[2] * w[2:3, :] + hist_ref[1] * w[1:2, :] + hist_ref[0] * w[0:1, :] for v in conv_ins]

    (r,) = _mixer_rows([x_ref[...]], c, conv_fn, [(rc_ref[...], rsu_ref[...], rsd_ref[...])], ATTN_SCALE)
    qg_ref[...] = r["q_g"]
    kg_ref[...] = r["k_g"]
    vg_ref[...] = r["v_g"]
    gb_ref[...] = r["gb"]
    z_ref[...] = r["z"]
    qm_ref[...] = r["q_mla"]
    km_ref[...] = r["k_mla"]
    ckv_ref[...] = r["ckv"]
    kro_ref[...] = r["kr"][:, ROPE_LO:ROPE_LO + MLA_ROPE]
    cin_ref[...] = r["conv_in"]


GROUP_HEADS = 4
GROUP_W = GROUP_HEADS * GDN_DK


def _cumsum_rows(x, period):
    row = lax.broadcasted_iota(jnp.int32, x.shape, 0) % period
    s = 1
    while s < period:
        x = x + jnp.where(row >= s, pltpu.roll(x, s, axis=0), 0.0)
        s *= 2
    return x


GDN_ITER_CHUNKS = 4


def _bd(y, bd_mask):
    return jnp.where(bd_mask, jnp.concatenate([y] * GROUP_HEADS, axis=0), 0.0)


def _gdn_solve_stages(chains, masks, sols):
    bd_mask, tril_cat, strict_cat, eye_cat = masks
    c = GDN_CHUNK
    n = range(len(chains))
    bd = lambda y: _bd(y, bd_mask)
    kb = [ch["k"] * ch["b_r"] for ch in chains]
    aq = [_dot_nt(jnp.concatenate([kb[i], chains[i]["q"]], axis=0), bd(chains[i]["k"])) for i in n]
    yield
    decay = []
    for ch in chains:
        g_col = jnp.sum(jnp.where(eye_cat, ch["g_r"], 0.0), axis=0, keepdims=True)
        decay.append(jnp.exp(jnp.where(tril_cat, ch["g_r"] - g_col, -jnp.inf)))
    qk = [aq[i][c:] * decay[i] for i in n]
    p = [-jnp.where(strict_cat, aq[i][:c] * decay[i], 0.0) for i in n]
    s_inv = [jnp.where(eye_cat, 1.0, 0.0) + p[i] for i in n]
    p = [_dot(p[i], bd(p[i])) for i in n]
    yield
    for lvl in range(5):
        s_next = [s_inv[i] + _dot(p[i], bd(s_inv[i])) for i in n]
        yield
        if lvl < 4:
            p = [_dot(p[i], bd(p[i])) for i in n]
            yield
        s_inv = s_next
    eg = [jnp.exp(ch["g_r"]) for ch in chains]
    u = [_dot(s_inv[i], bd(chains[i]["v"] * chains[i]["b_r"])) for i in n]
    yield
    w = [_dot(s_inv[i], bd(kb[i] * eg[i])) for i in n]
    for i in n:
        g_r = chains[i]["g_r"]
        g_last = g_r[c - 1:c, :]
        sols.append(dict(u=u[i], wq=jnp.concatenate([w[i], chains[i]["q"] * eg[i]], axis=0), qk=qk[i],
                         kd=chains[i]["k"] * jnp.exp(g_last - g_r), e_last=jnp.exp(g_last)))


def _gdn_apply_stages(sols, states, bd_mask, box):
    c = GDN_CHUNK
    ngrp = len(states)
    n = range(ngrp)
    outs = []
    for t in range(len(sols) // ngrp):
        sl = sols[t * ngrp:(t + 1) * ngrp]
        ws = [_dot(sl[i]["wq"], states[i]) for i in n]
        yield
        v_new = [sl[i]["u"] - ws[i][:c] for i in n]
        o = [ws[i][c:] + _dot(sl[i]["qk"], _bd(v_new[i], bd_mask)) for i in n]
        upd = [_dot_tn(sl[i]["kd"], v_new[i]) for i in n]
        yield
        states = [states[i] * sl[i]["e_last"] + jnp.where(bd_mask, upd[i], 0.0) for i in n]
        outs.append(jnp.concatenate(o, axis=1))
    box["o"] = jnp.concatenate(outs, axis=0)
    box["states"] = states


def _interleave(stage_generators):
    live = list(stage_generators)
    while live:
        for g in list(live):
            try:
                next(g)
            except StopIteration:
                live.remove(g)


def _gdn_kernel(nblk, tb, q_ref, k_ref, v_ref, gb_ref, z_ref, e6_ref, esum_ref, gout_ref, o_ref, sfin_ref, s_ref):
    j = pl.program_id(1)

    @pl.when(j == 0)
    def _():
        s_ref[...] = jnp.zeros(s_ref.shape, F32)

    c = GDN_CHUNK
    ngrp = GDN_HEADS // GROUP_HEADS
    rows_it = GDN_ITER_CHUNKS * c
    r_bd = lax.broadcasted_iota(jnp.int32, (GROUP_W, GROUP_W), 0) // c
    c_bd = lax.broadcasted_iota(jnp.int32, (GROUP_W, GROUP_W), 1) // c
    bd_mask = r_bd == c_bd
    ri = lax.broadcasted_iota(jnp.int32, (c, GROUP_W), 0)
    cj = lax.broadcasted_iota(jnp.int32, (c, GROUP_W), 1) % c
    masks = (bd_mask, ri >= cj, ri > cj, ri == cj)
    lane = lax.broadcasted_iota(jnp.int32, (rows_it, LANES), 1)
    copy_id = (lane % BETA_LANE0) // GDN_HEADS

    def chains_of(rows):
        gb = gb_ref[rows, :]
        sc = jnp.where(lane < BETA_LANE0, _cumsum_rows(gb, c), gb)
        p0, p1, p2 = _split3(sc)
        piece = jnp.where(copy_id == 0, p0, jnp.where(copy_id == 1, p1, p2))
        ex = _dot(piece, e6_ref[...])
        q = q_ref[rows, :]
        k = k_ref[rows, :]
        v = v_ref[rows, :]
        chains = []
        for t in range(GDN_ITER_CHUNKS):
            rs = slice(t * c, (t + 1) * c)
            for grp in range(ngrp):
                sl = slice(grp * GROUP_W, (grp + 1) * GROUP_W)
                sb = slice(GDN_WIDTH + grp * GROUP_W, GDN_WIDTH + (grp + 1) * GROUP_W)
                chains.append(dict(q=q[rs, sl], k=k[rs, sl], v=v[rs, sl], g_r=ex[rs, sl], b_r=ex[rs, sb]))
        return chains

    def finish(o, rows):
        hi, lo = _split2(o * o)
        ms = _dot(hi, esum_ref[...]) + _dot(lo, esum_ref[...])
        o_ref[rows, :] = (o * lax.rsqrt(ms + EPS) * gout_ref[...] * _silu(z_ref[rows, :])).astype(o_ref.dtype)

    states = [s_ref[grp] for grp in range(ngrp)]
    pending = None
    for gi in range(tb // rows_it):
        rows = slice(gi * rows_it, (gi + 1) * rows_it)
        sols, box = [], {}
        stages = [_gdn_solve_stages(chains_of(rows), masks, sols)]
        if pending is not None:
            stages.append(_gdn_apply_stages(pending[0], states, bd_mask, box))
        _interleave(stages)
        if pending is not None:
            states = box["states"]
            finish(box["o"], pending[1])
        pending = (sols, rows)
    box = {}
    _interleave([_gdn_apply_stages(pending[0], states, bd_mask, box)])
    finish(box["o"], pending[1])
    for grp in range(ngrp):
        s_ref[grp] = box["states"][grp]

    @pl.when(j == nblk - 1)
    def _():
        for grp in range(ngrp):
            for h in range(GROUP_HEADS):
                sfin_ref[0, grp * GROUP_HEADS + h] = s_ref[grp, h * c:(h + 1) * c, h * c:(h + 1) * c]


ATT_T = 256
ATT_HEADS = 8
ATT_SUM_ROWS = 16
ATT_WIDE = 2


def _attn_kernel(q_ref, k_ref, vt_ref, o_ref):
    qi = pl.program_id(2)
    t = ATT_T
    qs = [q_ref[:, hh * SLAB:(hh + 1) * SLAB] for hh in range(ATT_HEADS)]

    def score_tile(k0, tk):
        return tuple(_dot_nt(k_ref[pl.ds(k0, tk), hh * SLAB:(hh + 1) * SLAB], qs[hh]) for hh in range(ATT_HEADS))

    def consume(k0, tk, carry, masked):
        scores = score_tile(k0, tk)
        stats = []
        for hh in range(ATT_HEADS):
            m, _ = carry[hh]
            s = scores[hh]
            if masked:
                kp = lax.broadcasted_iota(jnp.int32, (tk, t), 0)
                qp = lax.broadcasted_iota(jnp.int32, (tk, t), 1)
                s = jnp.where(qp >= kp, s, -jnp.inf)
            m_new = jnp.maximum(m, jnp.max(s, axis=0, keepdims=True))
            stats.append((m_new, jnp.exp2(m - m_new), jnp.exp2(s - m_new).astype(BF16)))
        out = []
        for hh in range(ATT_HEADS):
            m_new, alpha, p = stats[hh]
            vt = vt_ref[hh * MLA_VDIM:(hh + 1) * MLA_VDIM, pl.ds(k0, tk)]
            ones = jnp.ones((ATT_SUM_ROWS, tk), BF16)
            out.append((m_new, carry[hh][1] * alpha + _dot(jnp.concatenate([vt, ones], axis=0), p)))
        return tuple(out)

    init = tuple((jnp.full((1, t), -jnp.inf, F32), jnp.zeros((MLA_VDIM + ATT_SUM_ROWS, t), F32)) for _ in range(ATT_HEADS))
    wide = ATT_WIDE * t
    nwide = qi // ATT_WIDE
    carry = lax.fori_loop(0, nwide, lambda j, c: consume(pl.multiple_of(j * wide, wide), wide, c, False), init)
    carry = lax.fori_loop(nwide * ATT_WIDE, qi, lambda j, c: consume(pl.multiple_of(j * t, t), t, c, False), carry)
    carry = consume(pl.multiple_of(qi * t, t), t, carry, True)
    outs = [acc[:MLA_VDIM] / acc[MLA_VDIM:MLA_VDIM + 1] for (_, acc) in carry]
    o_ref[...] = jnp.concatenate(outs, axis=0).T.astype(o_ref.dtype)


def _tail_kernel(x_ref, og_ref, om_ref, p_ref, wo_ref, gffn_ref, wg_ref, wu_ref, wd_ref, gple_ref, wpg_ref, wpp_ref, y_ref):
    h = x_ref[...] + _dot(og_ref[...], wo_ref[:GDN_WIDTH, :]) + _dot(om_ref[...], wo_ref[GDN_WIDTH:, :])
    u = _rmsnorm(h, gffn_ref[...]).astype(BF16)
    act = (_silu(_dot(u, wg_ref[...])) * _dot(u, wu_ref[...])).astype(BF16)
    acc = h + _dot(act, wd_ref[...])
    gate = _sigmoid(_dot(_rmsnorm(acc, gple_ref[...]).astype(BF16), wpg_ref[...]))
    y_ref[...] = acc + _dot(p_ref[...].astype(BF16), wpp_ref[...]) * gate


PAGES_PER_STEP = 32
PAGES_PER_SUB = 32
DECODE_SLOTS = 4
DECODE_AHEAD = 2


def _sample_prep_kernel(qm_ref, km_ref, gk_ref, wk_ref, qabs_ref, qr_ref, sself_ref):
    qm = qm_ref[...]
    prod = qm * km_ref[...]
    qg = (qm * gk_ref[...]).astype(BF16)
    for h in range(MLA_HEADS):
        sl = slice(h * SLAB, (h + 1) * SLAB)
        qabs_ref[h] = _dot_nt(qg[:, sl], wk_ref[:, sl])
        qr_ref[h] = qm[:, h * SLAB + ROPE_LO:h * SLAB + ROPE_LO + MLA_ROPE]
        sself_ref[h] = jnp.sum(prod[:, sl], axis=1, keepdims=True)


def _paged_kernel(nsteps, page, pt_ref, qabs_ref, qr_ref, sself_ref, ckvn_ref, wkt_ref, ckv_hbm, krt_hbm, o_ref,
                  cbuf, kbuf, sems):
    pp = PAGES_PER_STEP
    b = pl.program_id(0)
    nb = pl.num_programs(0)

    def copies(row, g):
        slot = g % DECODE_SLOTS
        out = []
        for j in range(pp):
            pg = pt_ref[row, g * pp + j]
            out.append(pltpu.make_async_copy(ckv_hbm.at[pg], cbuf.at[slot, pl.ds(j * page, page), :], sems.at[slot, 0]))
            out.append(pltpu.make_async_copy(krt_hbm.at[pg], kbuf.at[slot, :, pl.ds(j * page, page)], sems.at[slot, 1]))
        return out

    @pl.when(b == 0)
    def _():
        for g in range(DECODE_AHEAD):
            for cp in copies(0, g):
                cp.start()

    qa = qabs_ref[...].astype(BF16)
    lhs = jnp.concatenate([wkt_ref[...], qa, jnp.zeros_like(qa)], axis=0)
    nk = MLA_HEADS * MLA_NOPE
    qr = qr_ref[...].astype(BF16)
    m = sself_ref[...]
    l = jnp.ones(m.shape, F32)
    acc = jnp.broadcast_to(ckvn_ref[...], (MLA_HEADS, KV_LORA))
    sub_tok = PAGES_PER_SUB * page

    def update(s, cs, m, l, acc):
        m_new = jnp.maximum(m, jnp.max(s, axis=1, keepdims=True))
        alpha = jnp.exp(m - m_new)
        p = jnp.exp(s - m_new)
        l = l * alpha + jnp.sum(p, axis=1, keepdims=True)
        return m_new, l, acc * alpha + _dot(p.astype(BF16), jnp.concatenate(cs, axis=0))

    pending = None
    for g in range(nsteps):
        slot = g % DECODE_SLOTS
        for cp in copies(b, g):
            cp.wait()
        ng = g + DECODE_AHEAD
        nrow, ng = (b, ng) if ng < nsteps else (jnp.minimum(b + 1, nb - 1), ng - nsteps)
        for cp in copies(nrow, ng):
            cp.start()
        cs, ss_list = [], []
        for i in range(pp // PAGES_PER_SUB):
            c = cbuf[slot, i * sub_tok:(i + 1) * sub_tok, :].astype(BF16)
            krt = kbuf[slot, :, i * sub_tok:(i + 1) * sub_tok].astype(BF16)
            kvq = _dot_nt(lhs, c)
            kv = kvq[:nk]
            ss = jnp.sum((kv * kv).reshape(MLA_HEADS, MLA_NOPE, kv.shape[1]), axis=1)
            ss_list.append(kvq[nk:nk + MLA_HEADS] * lax.rsqrt(ss * (1.0 / MLA_NOPE) + EPS) + _dot(qr, krt))
            cs.append(c)
        if pending is not None:
            m, l, acc = update(*pending, m, l, acc)
        pending = (jnp.concatenate(ss_list, axis=1), cs)
    m, l, acc = update(*pending, m, l, acc)
    o_ref[...] = acc / l

    @pl.when(b == nb - 1)
    def _():
        for g in range(DECODE_AHEAD):
            for cp in copies(b, g):
                cp.wait()


SAMPLE_GDN_ROWS = 4


def _sample_gdn_kernel(q_ref, k_ref, v_ref, gb_ref, s_ref, snew_ref, o_ref):
    hk = GDN_HEADS * GDN_DK
    hrow = lax.broadcasted_iota(jnp.int32, (GDN_HEADS, hk), 0)
    hlane = lax.broadcasted_iota(jnp.int32, (GDN_HEADS, hk), 1) // GDN_DK
    hm = hrow == hlane
    r8 = lax.broadcasted_iota(jnp.int32, (GDN_HEADS, LANES), 0)
    l8 = lax.broadcasted_iota(jnp.int32, (GDN_HEADS, LANES), 1)
    ones = jnp.where(hm, 1.0, 0.0).astype(BF16)
    rows = range(q_ref.shape[0])
    kmask = [jnp.where(hm, jnp.broadcast_to(k_ref[r], (GDN_HEADS, hk)), 0.0) for r in rows]
    qmask = [jnp.where(hm, jnp.broadcast_to(q_ref[r], (GDN_HEADS, hk)), 0.0) for r in rows]
    gbb = [jnp.broadcast_to(gb_ref[r], (GDN_HEADS, LANES)) for r in rows]
    g_col = [jnp.sum(jnp.where(l8 == r8, gbb[r], 0.0), axis=1, keepdims=True) for r in rows]
    b_col = [jnp.sum(jnp.where(l8 == r8 + BETA_LANE0, gbb[r], 0.0), axis=1, keepdims=True) for r in rows]

    def expand_rows(col):
        e0, e1, e2 = _split3(jnp.broadcast_to(col, (GDN_HEADS, GDN_DV)))
        return _dot_tn(jnp.concatenate([ones, ones, ones], axis=0), jnp.concatenate([e0, e1, e2], axis=0))

    def head_dot(xmask, mat):
        xh, xl = _split2(xmask)
        mh, ml = _split2(mat)
        return _dot(jnp.concatenate([xh, xh, xl], axis=1), jnp.concatenate([mh, ml, mh], axis=0))

    def outer(kmask_r, delta_r):
        kh, kl = _split2(kmask_r)
        dh, dl = _split2(delta_r)
        return _dot_tn(jnp.concatenate([kh, kh, kl], axis=0), jnp.concatenate([dh, dl, dh], axis=0))

    decay = [expand_rows(jnp.exp(g_col[r])) for r in rows]
    state = [s_ref[r] * decay[r] for r in rows]
    ks = [head_dot(kmask[r], state[r]) for r in rows]
    delta = [(v_ref[r] - ks[r]) * b_col[r] for r in rows]
    upd = [outer(kmask[r], delta[r]) for r in rows]
    state = [state[r] + upd[r] for r in rows]
    outs = [head_dot(qmask[r], state[r]) for r in rows]
    for r in rows:
        snew_ref[r] = state[r]
        o_ref[r] = outs[r]


def _sample_mix_kernel(o_ref, z_ref, olat_ref, wvbd_ref, esum_ref, gout_ref, og_ref, om_ref):
    o = o_ref[...]
    hi, lo = _split2(o * o)
    ms = _dot(hi, esum_ref[...]) + _dot(lo, esum_ref[...])
    og_ref[...] = (o * lax.rsqrt(ms + EPS) * gout_ref[...] * _silu(z_ref[...])).astype(og_ref.dtype)
    om_ref[...] = _dot(olat_ref[...].astype(BF16), wvbd_ref[...]).astype(om_ref.dtype)


def _seg_tables(width, segs, mean):
    n = len(segs)
    ered = np.zeros((width, LANES), np.float32)
    eexp = np.zeros((LANES, width), np.float32)
    for copy in range(3):
        for s, (a, b) in enumerate(segs):
            ered[a:b, copy * n + s] = 1.0 / (b - a) if mean else 1.0
            eexp[copy * n + s, a:b] = 1.0
    return jnp.asarray(ered, BF16), jnp.asarray(eexp, BF16)


def _rope_slabs(pos):
    inv_freq = ROPE_THETA ** (-jnp.arange(HALF, dtype=F32) / HALF)
    ang = pos.astype(F32)[:, None] * inv_freq[None, :]
    cos, sin = jnp.cos(ang), jnp.sin(ang)
    n = pos.shape[0]
    zeros = lambda w: jnp.zeros((n, w), F32)
    rc = jnp.concatenate([jnp.ones((n, MLA_NOPE), F32), cos, cos, zeros(SLAB - MLA_QK)], axis=1)
    rsu = jnp.concatenate([zeros(MLA_NOPE), -sin, zeros(SLAB - ROPE_HI)], axis=1)
    rsd = jnp.concatenate([zeros(ROPE_HI), sin, zeros(SLAB - MLA_QK)], axis=1)
    return rc, rsu, rsd


def _prepare_weights(g_attn, w_in, w_conv, gdn_a_log, gdn_dt_bias, g_gdn_out, g_q_a, w_q_b, g_q_nope, g_q_rope,
                     g_kv_a, g_k_rope, w_kv_b, g_k_nope):
    row = lambda v: v.reshape(1, -1).astype(F32)
    a = w_in[:, CONV_DIM:CONV_DIM + GDN_HEADS]
    b = w_in[:, CONV_DIM + GDN_HEADS:CONV_DIM + 2 * GDN_HEADS]
    o_z = CONV_DIM + 2 * GDN_HEADS
    o_qa = o_z + GDN_WIDTH
    o_kv = o_qa + Q_LORA
    zc = lambda w: jnp.zeros((D_MODEL, w), F32)
    gb_slab = jnp.concatenate([a, a, a, zc(BETA_LANE0 - 3 * GDN_HEADS), b, b, b, zc(LANES - BETA_LANE0 - 3 * GDN_HEADS)], axis=1)
    kr_slab = jnp.concatenate([zc(ROPE_LO), w_in[:, o_kv + KV_LORA:], zc(SLAB - MLA_QK)], axis=1)
    w_in_p = jnp.concatenate([w_in[:, :CONV_DIM], gb_slab, w_in[:, o_z:o_qa], w_in[:, o_qa:o_kv],
                              w_in[:, o_kv:o_kv + KV_LORA], kr_slab], axis=1).astype(BF16)

    def scalar_slab(v):
        z8 = jnp.zeros((BETA_LANE0 - 3 * GDN_HEADS,), F32)
        return jnp.concatenate([v, v, v, z8, jnp.zeros((LANES - BETA_LANE0,), F32)]).reshape(1, LANES)

    wq = w_q_b.reshape(Q_LORA, MLA_HEADS, MLA_QK)
    w_qb = jnp.concatenate([wq, jnp.zeros((Q_LORA, MLA_HEADS, SLAB - MLA_QK), F32)], axis=2).reshape(Q_LORA, QK_SLAB_W).astype(BF16)
    gq = jnp.tile(jnp.concatenate([g_q_nope, g_q_rope, jnp.zeros((SLAB - MLA_QK,), F32)]), MLA_HEADS).reshape(1, QK_SLAB_W)
    wkv = w_kv_b.reshape(KV_LORA, MLA_HEADS, MLA_NOPE + MLA_VDIM)
    wk_part, wv_part = wkv[:, :, :MLA_NOPE], wkv[:, :, MLA_NOPE:]
    wk = jnp.concatenate([wk_part, jnp.zeros((KV_LORA, MLA_HEADS, SLAB - MLA_NOPE), F32)], axis=2).reshape(KV_LORA, QK_SLAB_W).astype(BF16)
    gk = jnp.tile(jnp.concatenate([g_k_nope, jnp.zeros((SLAB - MLA_NOPE,), F32)]), MLA_HEADS).reshape(1, QK_SLAB_W)
    gkr = jnp.concatenate([jnp.zeros((ROPE_LO,), F32), g_k_rope, jnp.zeros((SLAB - MLA_QK,), F32)]).reshape(1, SLAB)
    wvt = wv_part.reshape(KV_LORA, MLA_HEADS * MLA_VDIM).T.astype(BF16)
    wkt = wk_part.reshape(KV_LORA, MLA_HEADS * MLA_NOPE).T.astype(BF16)
    wv_bd = jnp.zeros((MLA_HEADS, KV_LORA, MLA_HEADS, MLA_VDIM), F32)
    wv_bd = wv_bd.at[jnp.arange(MLA_HEADS), :, jnp.arange(MLA_HEADS), :].set(jnp.moveaxis(wv_part, 1, 0))
    wv_bd = wv_bd.reshape(MLA_HEADS * KV_LORA, MLA_HEADS * MLA_VDIM).astype(BF16)

    ered_g, eexp_g = _seg_tables(2 * GDN_QK_DIM, [(h * GDN_DK, (h + 1) * GDN_DK) for h in range(2 * GDN_HEADS)], False)
    q_segs = ([(h * SLAB, h * SLAB + MLA_NOPE) for h in range(MLA_HEADS)]
              + [(h * SLAB + ROPE_LO, h * SLAB + MLA_QK) for h in range(MLA_HEADS)])
    ered_q, eexp_q = _seg_tables(QK_SLAB_W, q_segs, True)
    ered_k, eexp_k = _seg_tables(QK_SLAB_W, q_segs[:MLA_HEADS], True)
    ered_r, eexp_r = _seg_tables(SLAB, [(ROPE_LO, MLA_QK)], True)
    e6 = np.zeros((LANES, 2 * GDN_WIDTH), np.float32)
    for copy in range(3):
        for h in range(GDN_HEADS):
            e6[copy * GDN_HEADS + h, h * GDN_DV:(h + 1) * GDN_DV] = 1.0
            e6[BETA_LANE0 + copy * GDN_HEADS + h, GDN_WIDTH + h * GDN_DV:GDN_WIDTH + (h + 1) * GDN_DV] = 1.0
    esum = np.kron(np.eye(GDN_HEADS, dtype=np.float32), np.full((GDN_DV, GDN_DV), 1.0 / GDN_DV, np.float32))

    mixer = dict(g_attn=row(g_attn), w_in=w_in_p, w_conv=w_conv.astype(F32), alog=scalar_slab(gdn_a_log),
                 dtb=scalar_slab(gdn_dt_bias), ered_g=ered_g, eexp_g=eexp_g, g_q_a=row(g_q_a), w_qb=w_qb, gq=gq,
                 ered_q=ered_q, eexp_q=eexp_q, g_kv_a=row(g_kv_a), gkr=gkr, ered_r=ered_r, eexp_r=eexp_r, wk=wk, gk=gk,
                 ered_k=ered_k, eexp_k=eexp_k)
    extra = dict(wvt=wvt, wkt=wkt, wv_bd=wv_bd, e6=jnp.asarray(e6, BF16), esum=jnp.asarray(esum, BF16),
                 gout=jnp.tile(g_gdn_out, GDN_HEADS).reshape(1, GDN_WIDTH).astype(F32))
    return mixer, extra


def _params(sem=None):
    return pltpu.CompilerParams(dimension_semantics=sem, vmem_limit_bytes=VMEM_LIMIT)


def _prompt_mixer(x2, slabs, mixer, wvt, bsz, seq, tm):
    n = x2.shape[0]
    tps = seq // tm
    consts = [mixer[k] for k in MIXER_CONSTS] + [wvt]
    row_spec = lambda w: pl.BlockSpec((tm, w), lambda i: (i, 0))
    rope_spec = pl.BlockSpec((tm, SLAB), lambda i: (i % tps, 0))
    out_shapes = [
        jax.ShapeDtypeStruct((n, GDN_QK_DIM), F32), jax.ShapeDtypeStruct((n, GDN_QK_DIM), F32),
        jax.ShapeDtypeStruct((n, GDN_WIDTH), F32), jax.ShapeDtypeStruct((n, LANES), F32),
        jax.ShapeDtypeStruct((n, GDN_WIDTH), F32), jax.ShapeDtypeStruct((n, QK_SLAB_W), BF16),
        jax.ShapeDtypeStruct((n, QK_SLAB_W), BF16), jax.ShapeDtypeStruct((MLA_HEADS * MLA_VDIM, n), BF16),
        jax.ShapeDtypeStruct((n, KV_LORA), F32), jax.ShapeDtypeStruct((n, MLA_ROPE), F32),
        jax.ShapeDtypeStruct((bsz, CONV_WIDTH - 1, CONV_DIM), F32)]
    out_specs = [row_spec(GDN_QK_DIM), row_spec(GDN_QK_DIM), row_spec(GDN_WIDTH), row_spec(LANES), row_spec(GDN_WIDTH),
                 row_spec(QK_SLAB_W), row_spec(QK_SLAB_W), pl.BlockSpec((MLA_HEADS * MLA_VDIM, tm), lambda i: (0, i)),
                 row_spec(KV_LORA), row_spec(MLA_ROPE),
                 pl.BlockSpec((1, CONV_WIDTH - 1, CONV_DIM), lambda i: (i // tps, 0, 0))]
    return pl.pallas_call(
        functools.partial(_prompt_mixer_kernel, tps, tm),
        grid=(n // tm,),
        in_specs=[row_spec(D_MODEL), rope_spec, rope_spec, rope_spec] + [_const_spec(a) for a in consts],
        out_specs=out_specs, out_shape=out_shapes,
        scratch_shapes=[pltpu.VMEM((tm + 8, CONV_DIM), F32)],
        compiler_params=_params(("arbitrary",)), name="prompt_mixer",
    )(x2, *slabs, *consts)


def _sample_mixer(x2, slabs, hist, mixer):
    n = x2.shape[0]
    consts = [mixer[k] for k in MIXER_CONSTS]
    sd = lambda w: jax.ShapeDtypeStruct((n, w), F32)
    return pl.pallas_call(
        _sample_mixer_kernel,
        out_shape=[sd(GDN_QK_DIM), sd(GDN_QK_DIM), sd(GDN_WIDTH), sd(LANES), sd(GDN_WIDTH), sd(QK_SLAB_W), sd(QK_SLAB_W),
                   sd(KV_LORA), sd(MLA_ROPE), sd(CONV_DIM)],
        compiler_params=_params(), name="sample_mixer",
    )(x2, *slabs, hist, *consts)


def _gdn_prompt(qg, kg, vg, gb, z, extra, bsz, seq, tb):
    n = qg.shape[0]
    nblk = seq // tb
    row_spec = lambda w: pl.BlockSpec((tb, w), lambda b, j: (b * nblk + j, 0))
    consts = [extra["e6"], extra["esum"], extra["gout"]]
    return pl.pallas_call(
        functools.partial(_gdn_kernel, nblk, tb),
        grid=(bsz, nblk),
        in_specs=[row_spec(GDN_QK_DIM), row_spec(GDN_QK_DIM), row_spec(GDN_WIDTH), row_spec(LANES), row_spec(GDN_WIDTH)]
        + [_const_spec(a) for a in consts],
        out_specs=[row_spec(GDN_WIDTH), pl.BlockSpec((1, GDN_HEADS, GDN_DK, GDN_DV), lambda b, j: (b, 0, 0, 0))],
        out_shape=[jax.ShapeDtypeStruct((n, GDN_WIDTH), BF16), jax.ShapeDtypeStruct((bsz, GDN_HEADS, GDN_DK, GDN_DV), F32)],
        scratch_shapes=[pltpu.VMEM((GDN_HEADS // GROUP_HEADS, GROUP_W, GROUP_W), F32)],
        compiler_params=_params(("arbitrary", "arbitrary")), name="gdn_chunked",
    )(qg, kg, vg, gb, z, *consts)


def _attn_prompt(qm, km, vt, bsz, seq):
    n = qm.shape[0]
    t = ATT_T
    nq = seq // t
    return pl.pallas_call(
        _attn_kernel,
        grid=(bsz, MLA_HEADS // ATT_HEADS, nq),
        in_specs=[pl.BlockSpec((t, ATT_HEADS * SLAB), lambda b, hp, qi: (b * nq + qi, hp)),
                  pl.BlockSpec((seq, ATT_HEADS * SLAB), lambda b, hp, qi: (b, hp)),
                  pl.BlockSpec((ATT_HEADS * MLA_VDIM, seq), lambda b, hp, qi: (hp, b))],
        out_specs=pl.BlockSpec((t, ATT_HEADS * MLA_VDIM), lambda b, hp, qi: (b * nq + qi, hp)),
        out_shape=jax.ShapeDtypeStruct((n, MLA_HEADS * MLA_VDIM), BF16),
        compiler_params=_params(("arbitrary", "arbitrary", "arbitrary")), name="mla_prompt_attention",
    )(qm, km, vt)


def _tail(x2, og, om, p2, tailw, tm):
    n = x2.shape[0]
    row_spec = lambda w: pl.BlockSpec((tm, w), lambda i: (i, 0))
    return pl.pallas_call(
        _tail_kernel,
        grid=(n // tm,),
        in_specs=[row_spec(D_MODEL), row_spec(GDN_WIDTH), row_spec(GDN_WIDTH), row_spec(PLE_DIM)]
        + [_const_spec(a) for a in tailw],
        out_specs=row_spec(D_MODEL), out_shape=jax.ShapeDtypeStruct((n, D_MODEL), F32),
        compiler_params=_params(("arbitrary",)), name="layer_tail",
    )(x2, og, om, p2, *tailw)


def _sample_prep(qm, km, gk, wk):
    n = qm.shape[0]
    return pl.pallas_call(
        _sample_prep_kernel,
        out_shape=[jax.ShapeDtypeStruct((MLA_HEADS, n, KV_LORA), F32), jax.ShapeDtypeStruct((MLA_HEADS, n, MLA_ROPE), F32),
                   jax.ShapeDtypeStruct((MLA_HEADS, n, 1), F32)],
        compiler_params=_params(), name="sample_prep",
    )(qm, km, gk, wk)


def _paged_attention(page_table, pool_ckv, pool_kr, qabs, qr, sself, ckv_new, wkt):
    bs, n_pages = page_table.shape
    page = pool_ckv.shape[1]
    pp = PAGES_PER_STEP
    nsteps = n_pages // pp
    assert n_pages % pp == 0 and nsteps % DECODE_SLOTS == 0 and DECODE_AHEAD < DECODE_SLOTS, "page groups must fill whole buffer rings"
    per_b = lambda w: pl.BlockSpec((None, MLA_HEADS, w), lambda b, pt: (b, 0, 0))
    grid_spec = pltpu.PrefetchScalarGridSpec(
        num_scalar_prefetch=1, grid=(bs,),
        in_specs=[per_b(KV_LORA), per_b(MLA_ROPE), per_b(1),
                  pl.BlockSpec((None, 1, KV_LORA), lambda b, pt: (b, 0, 0)),
                  pl.BlockSpec(wkt.shape, lambda b, pt: (0, 0)),
                  pl.BlockSpec(memory_space=pl.ANY), pl.BlockSpec(memory_space=pl.ANY)],
        out_specs=pl.BlockSpec((None, MLA_HEADS, KV_LORA), lambda b, pt: (b, 0, 0)),
        scratch_shapes=[pltpu.VMEM((DECODE_SLOTS, pp * page, KV_LORA), F32), pltpu.VMEM((DECODE_SLOTS, MLA_ROPE, pp * page), F32),
                        pltpu.SemaphoreType.DMA((DECODE_SLOTS, 2))])
    return pl.pallas_call(
        functools.partial(_paged_kernel, nsteps, page), grid_spec=grid_spec,
        out_shape=jax.ShapeDtypeStruct((bs, MLA_HEADS, KV_LORA), F32),
        compiler_params=_params(("arbitrary",)), name="mla_paged_decode",
    )(page_table, qabs, qr, sself, ckv_new, wkt, pool_ckv, pool_kr)


def _sample_gdn(q3, k3, v3, gb3, state):
    bs = q3.shape[0]
    hk = GDN_HEADS * GDN_DK
    rb = _pick(bs, (SAMPLE_GDN_ROWS, 2, 1))
    b3 = lambda r, w: pl.BlockSpec((rb, r, w), lambda b: (b, 0, 0))
    return pl.pallas_call(
        _sample_gdn_kernel, grid=(bs // rb,),
        in_specs=[b3(1, hk), b3(1, hk), b3(GDN_HEADS, GDN_DV), b3(1, LANES), b3(hk, GDN_DV)],
        out_specs=[b3(hk, GDN_DV), b3(GDN_HEADS, GDN_DV)],
        out_shape=[jax.ShapeDtypeStruct((bs, hk, GDN_DV), F32), jax.ShapeDtypeStruct((bs, GDN_HEADS, GDN_DV), F32)],
        compiler_params=_params(("arbitrary",)), name="gdn_recurrent_step",
    )(q3, k3, v3, gb3, state)


def _sample_mix(o2, z, olat2, extra):
    n = o2.shape[0]
    return pl.pallas_call(
        _sample_mix_kernel,
        out_shape=[jax.ShapeDtypeStruct((n, GDN_WIDTH), BF16), jax.ShapeDtypeStruct((n, MLA_HEADS * MLA_VDIM), BF16)],
        compiler_params=_params(), name="sample_mix",
    )(o2, z, olat2, extra["wv_bd"], extra["esum"], extra["gout"])


def _pick(n, prefs):
    for t in prefs:
        if n % t == 0:
            return t
    return n


def kernel(x_prompt, x_sample, cache_ckv, cache_krope, state_gdn, state_conv, page_table, p_prompt, p_sample, g_attn, w_in, w_conv, gdn_a_log, gdn_dt_bias, g_gdn_out, g_q_a, w_q_b, g_q_nope, g_q_rope, g_kv_a, g_k_rope, w_kv_b, g_k_nope, w_o, g_ffn, w_ffn_gate, w_ffn_up, w_ffn_down, g_ple, w_ple_gate, w_ple_proj):
    depth = g_attn.shape[0]
    assert depth == 1 and x_sample.shape[1] == 1, "single layer, single new token per sample row"
    bp, seq, _ = x_prompt.shape
    bs = x_sample.shape[0]
    past = page_table.shape[1] * cache_ckv.shape[2]
    li = 0
    mixer, extra = _prepare_weights(g_attn[li], w_in[li], w_conv[li], gdn_a_log[li], gdn_dt_bias[li], g_gdn_out[li],
                                    g_q_a[li], w_q_b[li], g_q_nope[li], g_q_rope[li], g_kv_a[li], g_k_rope[li],
                                    w_kv_b[li], g_k_nope[li])
    row = lambda v: v.reshape(1, -1).astype(F32)
    tailw = [w_o[li].astype(BF16), row(g_ffn[li]), w_ffn_gate[li].astype(BF16), w_ffn_up[li].astype(BF16),
             w_ffn_down[li].astype(BF16), row(g_ple[li]), w_ple_gate[li].astype(BF16), w_ple_proj[li].astype(BF16)]

    n = bp * seq
    xp2 = x_prompt.reshape(n, D_MODEL)
    tm = _pick(seq, (256, 128, 64, 32, 16, 8))
    (qg, kg, vg, gb, z, qm, km, vt, ckv_p, kr_p, conv_p) = _prompt_mixer(
        xp2, _rope_slabs(jnp.arange(seq)), mixer, extra["wvt"], bp, seq, _pick(seq, (512, 256, 128)))
    tb = _pick(seq, (1024, 512, 256))
    og, gdn_p = _gdn_prompt(qg, kg, vg, gb, z, extra, bp, seq, tb)
    om = _attn_prompt(qm, km, vt, bp, seq)
    y_prompt = _tail(xp2, og, om, p_prompt[li].reshape(n, PLE_DIM), tailw, tm).reshape(bp, seq, D_MODEL)

    xs2 = x_sample.reshape(bs, D_MODEL)
    hist = jnp.moveaxis(state_conv[li], 1, 0)
    (qg_s, kg_s, vg_s, gb_s, z_s, qm_s, km_s, ckv_s, kr_s, cin_s) = _sample_mixer(
        xs2, _rope_slabs(past + jnp.arange(1)), hist, mixer)
    qabs, qr, sself = _sample_prep(qm_s, km_s, mixer["gk"], mixer["wk"])
    per_row = lambda t: jnp.swapaxes(t, 0, 1)
    pool_krt = jnp.swapaxes(cache_krope[li], 1, 2)
    olat = _paged_attention(page_table, cache_ckv[li], pool_krt, per_row(qabs), per_row(qr), per_row(sself),
                            ckv_s.reshape(bs, 1, KV_LORA), extra["wkt"])
    hk = GDN_HEADS * GDN_DK
    s_new, o_s = _sample_gdn(qg_s.reshape(bs, 1, hk), kg_s.reshape(bs, 1, hk), vg_s.reshape(bs, GDN_HEADS, GDN_DV),
                             gb_s.reshape(bs, 1, LANES), state_gdn[li].reshape(bs, hk, GDN_DV))
    og_s, om_s = _sample_mix(o_s.reshape(bs, GDN_WIDTH), z_s, olat.reshape(bs, MLA_HEADS * KV_LORA), extra)
    y_sample = _tail(xs2, og_s, om_s, p_sample[li].reshape(bs, PLE_DIM), tailw, bs).reshape(bs, 1, D_MODEL)
    conv_s = jnp.concatenate([state_conv[li][:, 1:], cin_s[:, None, :]], axis=1)

    return (y_prompt, y_sample,
            ckv_p.reshape(1, bp, seq, KV_LORA), kr_p.reshape(1, bp, seq, MLA_ROPE),
            gdn_p[None], conv_p[None],
            ckv_s.reshape(1, bs, 1, KV_LORA), kr_s.reshape(1, bs, 1, MLA_ROPE),
            s_new.reshape(1, bs, GDN_HEADS, GDN_DK, GDN_DV), conv_s[None])
```

```python
import functools
import math

import numpy as np
import jax
import jax.numpy as jnp
from jax import lax
from jax.experimental import pallas as pl
from jax.experimental.pallas import tpu as pltpu

F32 = jnp.float32
BF16 = jnp.bfloat16

D_MODEL = 1024
PLE_DIM = 256
GDN_HEADS = 8
GDN_DK = 64
GDN_DV = 64
GDN_WIDTH = GDN_HEADS * GDN_DV
GDN_QK_DIM = GDN_HEADS * GDN_DK
CONV_WIDTH = 4
CONV_DIM = 2 * GDN_QK_DIM + GDN_WIDTH
GDN_CHUNK = 64
MLA_HEADS = 8
MLA_NOPE = 64
MLA_ROPE = 32
MLA_VDIM = 64
MLA_QK = MLA_NOPE + MLA_ROPE
Q_LORA = 384
KV_LORA = 256
ROPE_THETA = 10000.0
ATTN_SCALE = MLA_QK ** -0.5
LOG2E = math.log2(math.e)
D_FF = 2816
EPS = 1e-6

LANES = 128
SLAB = 128
HALF = MLA_ROPE // 2
ROPE_LO = MLA_NOPE
ROPE_HI = MLA_NOPE + HALF
QK_SLAB_W = MLA_HEADS * SLAB

OFF_CONV = 0
OFF_GB = OFF_CONV + CONV_DIM
OFF_Z = OFF_GB + LANES
OFF_QA = OFF_Z + GDN_WIDTH
OFF_CKV = OFF_QA + Q_LORA
OFF_KR = OFF_CKV + KV_LORA
IN_PAD = OFF_KR + LANES
BETA_LANE0 = 32

VMEM_LIMIT = 56 * 1024 * 1024


def _dot(a, b):
    return jnp.dot(a, b, preferred_element_type=F32)


def _dot_nt(a, b):
    return lax.dot_general(a, b, (((1,), (1,)), ((), ())), preferred_element_type=F32)


def _dot_tn(a, b):
    return lax.dot_general(a, b, (((0,), (0,)), ((), ())), preferred_element_type=F32)


def _split2(x):
    hi = x.astype(BF16)
    lo = (x - hi.astype(F32)).astype(BF16)
    return hi, lo


def _split3(x):
    hi = x.astype(BF16)
    r1 = x - hi.astype(F32)
    mid = r1.astype(BF16)
    lo = (r1 - mid.astype(F32)).astype(BF16)
    return hi, mid, lo


def _sigmoid(x):
    return 1.0 / (1.0 + jnp.exp(-x))


def _silu(x):
    return x * _sigmoid(x)


def _softplus(x):
    return jnp.maximum(x, 0.0) + jnp.log1p(jnp.exp(-jnp.abs(x)))


def _rmsnorm(x, g):
    return x * lax.rsqrt(jnp.mean(x * x, axis=-1, keepdims=True) + EPS) * g


def _seg_rsqrt(x, ered, eexp, nseg):
    red = _dot((x * x).astype(BF16), ered)
    r = lax.rsqrt(red + EPS)
    p0, p1, p2 = _split3(r)
    lane = lax.broadcasted_iota(jnp.int32, r.shape, 1)
    piece = jnp.where(lane < nseg, p0, jnp.where(lane < 2 * nseg, p1, p2))
    return _dot(piece, eexp)


def _rope(x, c, s_up, s_dn):
    w = x.shape[-1]
    up = pltpu.roll(x, w - HALF, axis=1)
    dn = pltpu.roll(x, HALF, axis=1)
    return x * c + up * s_up + dn * s_dn


def _tile_lanes(x, n):
    return jnp.concatenate([x] * n, axis=1)


def _const_spec(arr):
    nd = arr.ndim
    return pl.BlockSpec(arr.shape, lambda *a, _nd=nd: (0,) * _nd, pipeline_mode=pl.Buffered(1))


MIXER_BLOCKS = 2
MIXER_CONSTS = ("g_attn", "w_in", "w_conv", "alog", "dtb", "ered_g", "eexp_g", "g_q_a", "w_qb", "gq", "ered_q",
                "eexp_q", "g_kv_a", "gkr", "ered_r", "eexp_r", "wk", "gk", "ered_k", "eexp_k")


def _mixer_rows(xs, c, conv_fn, ropes, q_scale):
    n = range(len(xs))
    w_in = c["w_in"]
    xn = [_rmsnorm(x, c["g_attn"][...]).astype(BF16) for x in xs]

    conv_in = [_dot(xn[i], w_in[:, OFF_CONV:OFF_GB]) for i in n]
    rest = [_dot(xn[i], w_in[:, OFF_GB:IN_PAD]) for i in n]
    part = lambda i, lo, hi: rest[i][:, lo - OFF_GB:hi - OFF_GB]
    qs = [_dot(_rmsnorm(part(i, OFF_QA, OFF_CKV), c["g_q_a"][...]).astype(BF16), c["w_qb"][...]) for i in n]
    ckv = [_rmsnorm(part(i, OFF_CKV, OFF_KR), c["g_kv_a"][...]) for i in n]
    ckv_bf = [v.astype(BF16) for v in ckv]
    kk = [_dot(ckv_bf[i], c["wk"][...]) for i in n]

    y = [_silu(v) for v in conv_fn(conv_in)]
    qk = [y[i][:, :2 * GDN_QK_DIM] for i in n]
    qk = [qk[i] * _seg_rsqrt(qk[i], c["ered_g"][...], c["eexp_g"][...], 2 * GDN_HEADS) for i in n]

    gb = []
    for i in n:
        ab = part(i, OFF_GB, OFF_Z)
        lane = lax.broadcasted_iota(jnp.int32, ab.shape, 1)
        g_log = -jnp.exp(c["alog"][...]) * _softplus(ab + c["dtb"][...])
        gb.append(jnp.where(lane < BETA_LANE0, g_log, _sigmoid(ab)))
    z = [part(i, OFF_Z, OFF_QA) for i in n]

    qs = [qs[i] * _seg_rsqrt(qs[i], c["ered_q"][...], c["eexp_q"][...], 2 * MLA_HEADS) * c["gq"][...] for i in n]
    q_mla = [_rope(qs[i], *[_tile_lanes(t, MLA_HEADS) for t in ropes[i]]) * q_scale for i in n]

    kr = [part(i, OFF_KR, IN_PAD) for i in n]
    kr = [kr[i] * _seg_rsqrt(kr[i], c["ered_r"][...], c["eexp_r"][...], 1) * c["gkr"][...] for i in n]
    kr = [_rope(kr[i], *ropes[i]) for i in n]
    kk = [kk[i] * _seg_rsqrt(kk[i], c["ered_k"][...], c["eexp_k"][...], MLA_HEADS) * c["gk"][...] for i in n]
    return [dict(conv_in=conv_in[i], q_g=qk[i][:, :GDN_QK_DIM] * (GDN_DK ** -0.5), k_g=qk[i][:, GDN_QK_DIM:],
                 v_g=y[i][:, 2 * GDN_QK_DIM:], gb=gb[i], z=z[i], q_mla=q_mla[i],
                 k_mla=kk[i] + _tile_lanes(kr[i], MLA_HEADS), ckv=ckv[i], ckv_bf=ckv_bf[i], kr=kr[i]) for i in n]


def _prompt_mixer_kernel(tiles_per_seq, tm, x_ref, rc_ref, rsu_ref, rsd_ref, *refs):
    nc = len(MIXER_CONSTS)
    c = dict(zip(MIXER_CONSTS, refs[:nc]))
    wvt_ref = refs[nc]
    (qg_ref, kg_ref, vg_ref, gb_ref, z_ref, qm_ref, km_ref, vt_ref, ckv_ref, kro_ref, cs_ref, ext_ref) = refs[nc + 1:]
    i = pl.program_id(0)
    hm = tm // MIXER_BLOCKS
    blocks = [slice(j * hm, (j + 1) * hm) for j in range(MIXER_BLOCKS)]

    @pl.when(i % tiles_per_seq == 0)
    def _():
        ext_ref[5:8, :] = jnp.zeros((3, CONV_DIM), F32)

    def conv_fn(conv_ins):
        w = c["w_conv"]
        for sl, v in zip(blocks, conv_ins):
            ext_ref[8 + sl.start:8 + sl.stop, :] = v
        return [v * w[3:4, :] + ext_ref[7 + sl.start:7 + sl.stop, :] * w[2:3, :]
                + ext_ref[6 + sl.start:6 + sl.stop, :] * w[1:2, :] + ext_ref[5 + sl.start:5 + sl.stop, :] * w[0:1, :]
                for sl, v in zip(blocks, conv_ins)]

    res = _mixer_rows([x_ref[sl, :] for sl in blocks], c, conv_fn,
                      [(rc_ref[sl, :], rsu_ref[sl, :], rsd_ref[sl, :]) for sl in blocks], ATTN_SCALE * LOG2E)
    last3 = ext_ref[tm + 5:tm + 8, :]
    ext_ref[5:8, :] = last3
    cs_ref[0] = last3
    for sl, r in zip(blocks, res):
        qg_ref[sl, :] = r["q_g"]
        kg_ref[sl, :] = r["k_g"]
        vg_ref[sl, :] = r["v_g"]
        gb_ref[sl, :] = r["gb"]
        z_ref[sl, :] = r["z"]
        qm_ref[sl, :] = r["q_mla"].astype(BF16)
        km_ref[sl, :] = r["k_mla"].astype(BF16)
        vt_ref[:, sl] = _dot_nt(wvt_ref[...], r["ckv_bf"]).astype(BF16)
        ckv_ref[sl, :] = r["ckv"]
        kro_ref[sl, :] = r["kr"][:, ROPE_LO:ROPE_LO + MLA_ROPE]


def _sample_mixer_kernel(x_ref, rc_ref, rsu_ref, rsd_ref, hist_ref, *refs):
    nc = len(MIXER_CONSTS)
    c = dict(zip(MIXER_CONSTS, refs[:nc]))
    (qg_ref, kg_ref, vg_ref, gb_ref, z_ref, qm_ref, km_ref, ckv_ref, kro_ref, cin_ref) = refs[nc:]

    def conv_fn(conv_ins):
        w = c["w_conv"]
        return [v * w[3:4, :] + hist_ref[2] * w[2:3, :] + hist_ref[1] * w[1:2, :] + hist_ref[0] * w[0:1, :] for v in conv_ins]

    (r,) = _mixer_rows([x_ref[...]], c, conv_fn, [(rc_ref[...], rsu_ref[...], rsd_ref[...])], ATTN_SCALE)
    qg_ref[...] = r["q_g"]
    kg_ref[...] = r["k_g"]
    vg_ref[...] = r["v_g"]
    gb_ref[...] = r["gb"]
    z_ref[...] = r["z"]
    qm_ref[...] = r["q_mla"]
    km_ref[...] = r["k_mla"]
    ckv_ref[...] = r["ckv"]
    kro_ref[...] = r["kr"][:, ROPE_LO:ROPE_LO + MLA_ROPE]
    cin_ref[...] = r["conv_in"]


GROUP_HEADS = 4
GROUP_W = GROUP_HEADS * GDN_DK


def _cumsum_rows(x, period):
    row = lax.broadcasted_iota(jnp.int32, x.shape, 0) % period
    s = 1
    while s < period:
        x = x + jnp.where(row >= s, pltpu.roll(x, s, axis=0), 0.0)
        s *= 2
    return x


GDN_ITER_CHUNKS = 4


def _bd(y, bd_mask):
    return jnp.where(bd_mask, jnp.concatenate([y] * GROUP_HEADS, axis=0), 0.0)


def _gdn_solve_stages(chains, masks, sols):
    bd_mask, tril_cat, strict_cat, eye_cat = masks
    c = GDN_CHUNK
    n = range(len(chains))
    bd = lambda y: _bd(y, bd_mask)
    kb = [ch["k"] * ch["b_r"] for ch in chains]
    aq = [_dot_nt(jnp.concatenate([kb[i], chains[i]["q"]], axis=0), bd(chains[i]["k"])) for i in n]
    yield
    decay = []
    for ch in chains:
        g_col = jnp.sum(jnp.where(eye_cat, ch["g_r"], 0.0), axis=0, keepdims=True)
        decay.append(jnp.exp(jnp.where(tril_cat, ch["g_r"] - g_col, -jnp.inf)))
    qk = [aq[i][c:] * decay[i] for i in n]
    p = [-jnp.where(strict_cat, aq[i][:c] * decay[i], 0.0) for i in n]
    s_inv = [jnp.where(eye_cat, 1.0, 0.0) + p[i] for i in n]
    p = [_dot(p[i], bd(p[i])) for i in n]
    yield
    for lvl in range(5):
        s_next = [s_inv[i] + _dot(p[i], bd(s_inv[i])) for i in n]
        yield
        if lvl < 4:
            p = [_dot(p[i], bd(p[i])) for i in n]
            yield
        s_inv = s_next
    eg = [jnp.exp(ch["g_r"]) for ch in chains]
    u = [_dot(s_inv[i], bd(chains[i]["v"] * chains[i]["b_r"])) for i in n]
    yield
    w = [_dot(s_inv[i], bd(kb[i] * eg[i])) for i in n]
    for i in n:
        g_r = chains[i]["g_r"]
        g_last = g_r[c - 1:c, :]
        sols.append(dict(u=u[i], wq=jnp.concatenate([w[i], chains[i]["q"] * eg[i]], axis=0), qk=qk[i],
                         kd=chains[i]["k"] * jnp.exp(g_last - g_r), e_last=jnp.exp(g_last)))


def _gdn_apply_stages(sols, states, bd_mask, box):
    c = GDN_CHUNK
    ngrp = len(states)
    n = range(ngrp)
    outs = []
    for t in range(len(sols) // ngrp):
        sl = sols[t * ngrp:(t + 1) * ngrp]
        ws = [_dot(sl[i]["wq"], states[i]) for i in n]
        yield
        v_new = [sl[i]["u"] - ws[i][:c] for i in n]
        o = [ws[i][c:] + _dot(sl[i]["qk"], _bd(v_new[i], bd_mask)) for i in n]
        upd = [_dot_tn(sl[i]["kd"], v_new[i]) for i in n]
        yield
        states = [states[i] * sl[i]["e_last"] + jnp.where(bd_mask, upd[i], 0.0) for i in n]
        outs.append(jnp.concatenate(o, axis=1))
    box["o"] = jnp.concatenate(outs, axis=0)
    box["states"] = states


def _interleave(stage_generators):
    live = list(stage_generators)
    while live:
        for g in list(live):
            try:
                next(g)
            except StopIteration:
                live.remove(g)


def _gdn_kernel(nblk, tb, q_ref, k_ref, v_ref, gb_ref, z_ref, e6_ref, esum_ref, gout_ref, o_ref, sfin_ref, s_ref):
    j = pl.program_id(1)

    @pl.when(j == 0)
    def _():
        s_ref[...] = jnp.zeros(s_ref.shape, F32)

    c = GDN_CHUNK
    ngrp = GDN_HEADS // GROUP_HEADS
    rows_it = GDN_ITER_CHUNKS * c
    r_bd = lax.broadcasted_iota(jnp.int32, (GROUP_W, GROUP_W), 0) // c
    c_bd = lax.broadcasted_iota(jnp.int32, (GROUP_W, GROUP_W), 1) // c
    bd_mask = r_bd == c_bd
    ri = lax.broadcasted_iota(jnp.int32, (c, GROUP_W), 0)
    cj = lax.broadcasted_iota(jnp.int32, (c, GROUP_W), 1) % c
    masks = (bd_mask, ri >= cj, ri > cj, ri == cj)
    lane = lax.broadcasted_iota(jnp.int32, (rows_it, LANES), 1)
    copy_id = (lane % BETA_LANE0) // GDN_HEADS

    def chains_of(rows):
        gb = gb_ref[rows, :]
        sc = jnp.where(lane < BETA_LANE0, _cumsum_rows(gb, c), gb)
        p0, p1, p2 = _split3(sc)
        piece = jnp.where(copy_id == 0, p0, jnp.where(copy_id == 1, p1, p2))
        ex = _dot(piece, e6_ref[...])
        q = q_ref[rows, :]
        k = k_ref[rows, :]
        v = v_ref[rows, :]
        chains = []
        for t in range(GDN_ITER_CHUNKS):
            rs = slice(t * c, (t + 1) * c)
            for grp in range(ngrp):
                sl = slice(grp * GROUP_W, (grp + 1) * GROUP_W)
                sb = slice(GDN_WIDTH + grp * GROUP_W, GDN_WIDTH + (grp + 1) * GROUP_W)
                chains.append(dict(q=q[rs, sl], k=k[rs, sl], v=v[rs, sl], g_r=ex[rs, sl], b_r=ex[rs, sb]))
        return chains

    def finish(o, rows):
        hi, lo = _split2(o * o)
        ms = _dot(hi, esum_ref[...]) + _dot(lo, esum_ref[...])
        o_ref[rows, :] = (o * lax.rsqrt(ms + EPS) * gout_ref[...] * _silu(z_ref[rows, :])).astype(o_ref.dtype)

    states = [s_ref[grp] for grp in range(ngrp)]
    pending = None
    for gi in range(tb // rows_it):
        rows = slice(gi * rows_it, (gi + 1) * rows_it)
        sols, box = [], {}
        stages = [_gdn_solve_stages(chains_of(rows), masks, sols)]
        if pending is not None:
            stages.append(_gdn_apply_stages(pending[0], states, bd_mask, box))
        _interleave(stages)
        if pending is not None:
            states = box["states"]
            finish(box["o"], pending[1])
        pending = (sols, rows)
    box = {}
    _interleave([_gdn_apply_stages(pending[0], states, bd_mask, box)])
    finish(box["o"], pending[1])
    for grp in range(ngrp):
        s_ref[grp] = box["states"][grp]

    @pl.when(j == nblk - 1)
    def _():
        for grp in range(ngrp):
            for h in range(GROUP_HEADS):
                sfin_ref[0, grp * GROUP_HEADS + h] = s_ref[grp, h * c:(h + 1) * c, h * c:(h + 1) * c]


ATT_T = 256
ATT_HEADS = 8
ATT_SUM_ROWS = 16
ATT_WIDE = 2


def _attn_kernel(q_ref, k_ref, vt_ref, o_ref):
    qi = pl.program_id(2)
    t = ATT_T
    qs = [q_ref[:, hh * SLAB:(hh + 1) * SLAB] for hh in range(ATT_HEADS)]

    def score_tile(k0, tk):
        return tuple(_dot_nt(k_ref[pl.ds(k0, tk), hh * SLAB:(hh + 1) * SLAB], qs[hh]) for hh in range(ATT_HEADS))

    def consume(k0, tk, carry, masked):
        scores = score_tile(k0, tk)
        stats = []
        for hh in range(ATT_HEADS):
            m, _ = carry[hh]
            s = scores[hh]
            if masked:
                kp = lax.broadcasted_iota(jnp.int32, (tk, t), 0)
                qp = lax.broadcasted_iota(jnp.int32, (tk, t), 1)
                s = jnp.where(qp >= kp, s, -jnp.inf)
            m_new = jnp.maximum(m, jnp.max(s, axis=0, keepdims=True))
            stats.append((m_new, jnp.exp2(m - m_new), jnp.exp2(s - m_new).astype(BF16)))
        out = []
        for hh in range(ATT_HEADS):
            m_new, alpha, p = stats[hh]
            vt = vt_ref[hh * MLA_VDIM:(hh + 1) * MLA_VDIM, pl.ds(k0, tk)]
            ones = jnp.ones((ATT_SUM_ROWS, tk), BF16)
            out.append((m_new, carry[hh][1] * alpha + _dot(jnp.concatenate([vt, ones], axis=0), p)))
        return tuple(out)

    init = tuple((jnp.full((1, t), -jnp.inf, F32), jnp.zeros((MLA_VDIM + ATT_SUM_ROWS, t), F32)) for _ in range(ATT_HEADS))
    wide = ATT_WIDE * t
    nwide = qi // ATT_WIDE
    carry = lax.fori_loop(0, nwide, lambda j, c: consume(pl.multiple_of(j * wide, wide), wide, c, False), init)
    carry = lax.fori_loop(nwide * ATT_WIDE, qi, lambda j, c: consume(pl.multiple_of(j * t, t), t, c, False), carry)
    carry = consume(pl.multiple_of(qi * t, t), t, carry, True)
    outs = [acc[:MLA_VDIM] / acc[MLA_VDIM:MLA_VDIM + 1] for (_, acc) in carry]
    o_ref[...] = jnp.concatenate(outs, axis=0).T.astype(o_ref.dtype)


TAIL_BLOCKS = 2

def _tail_kernel(x_ref, og_ref, om_ref, p_ref, wo_ref, gffn_ref, wg_ref, wu_ref, wd_ref, gple_ref, wpg_ref, wpp_ref, y_ref):
    rows = x_ref.shape[0]
    nblk = TAIL_BLOCKS if rows % (TAIL_BLOCKS * LANES) == 0 else 1
    blocks = [slice(j * rows // nblk, (j + 1) * rows // nblk) for j in range(nblk)]
    n = range(nblk)
    ple = [_dot(p_ref[sl, :].astype(BF16), wpp_ref[...]) for sl in blocks]
    h = [x_ref[sl, :] + _dot(og_ref[sl, :], wo_ref[:GDN_WIDTH, :]) + _dot(om_ref[sl, :], wo_ref[GDN_WIDTH:, :])
         for sl in blocks]
    u = [_rmsnorm(h[i], gffn_ref[...]).astype(BF16) for i in n]
    act = [(_silu(_dot(u[i], wg_ref[...])) * _dot(u[i], wu_ref[...])).astype(BF16) for i in n]
    acc = [h[i] + _dot(act[i], wd_ref[...]) for i in n]
    gate = [_sigmoid(_dot(_rmsnorm(acc[i], gple_ref[...]).astype(BF16), wpg_ref[...])) for i in n]
    for i, sl in enumerate(blocks):
        y_ref[sl, :] = acc[i] + ple[i] * gate[i]


PAGES_PER_STEP = 32
PAGES_PER_SUB = 32
DECODE_SLOTS = 4
DECODE_AHEAD = 2


def _sample_prep_kernel(qm_ref, km_ref, gk_ref, wk_ref, qabs_ref, qr_ref, sself_ref):
    qm = qm_ref[...]
    prod = qm * km_ref[...]
    qg = (qm * gk_ref[...]).astype(BF16)
    for h in range(MLA_HEADS):
        sl = slice(h * SLAB, (h + 1) * SLAB)
        qabs_ref[h] = _dot_nt(qg[:, sl], wk_ref[:, sl])
        qr_ref[h] = qm[:, h * SLAB + ROPE_LO:h * SLAB + ROPE_LO + MLA_ROPE]
        sself_ref[h] = jnp.sum(prod[:, sl], axis=1, keepdims=True)


def _paged_kernel(nsteps, page, pt_ref, qabs_ref, qr_ref, sself_ref, ckvn_ref, wkt_ref, ckv_hbm, krt_hbm, o_ref,
                  cbuf, kbuf, sems):
    pp = PAGES_PER_STEP
    b = pl.program_id(0)
    nb = pl.num_programs(0)

    def copies(row, g):
        slot = g % DECODE_SLOTS
        out = []
        for j in range(pp):
            pg = pt_ref[row, g * pp + j]
            out.append(pltpu.make_async_copy(ckv_hbm.at[pg], cbuf.at[slot, pl.ds(j * page, page), :], sems.at[slot, 0]))
            out.append(pltpu.make_async_copy(krt_hbm.at[pg], kbuf.at[slot, :, pl.ds(j * page, page)], sems.at[slot, 1]))
        return out

    @pl.when(b == 0)
    def _():
        for g in range(DECODE_AHEAD):
            for cp in copies(0, g):
                cp.start()

    qa = qabs_ref[...].astype(BF16)
    lhs = jnp.concatenate([wkt_ref[...], qa, jnp.zeros_like(qa)], axis=0)
    nk = MLA_HEADS * MLA_NOPE
    qr = qr_ref[...].astype(BF16)
    m = sself_ref[...]
    l = jnp.ones(m.shape, F32)
    acc = jnp.broadcast_to(ckvn_ref[...], (MLA_HEADS, KV_LORA))
    sub_tok = PAGES_PER_SUB * page

    def update(s, cs, m, l, acc):
        m_new = jnp.maximum(m, jnp.max(s, axis=1, keepdims=True))
        alpha = jnp.exp(m - m_new)
        p = jnp.exp(s - m_new)
        l = l * alpha + jnp.sum(p, axis=1, keepdims=True)
        return m_new, l, acc * alpha + _dot(p.astype(BF16), jnp.concatenate(cs, axis=0))

    pending = None
    for g in range(nsteps):
        slot = g % DECODE_SLOTS
        for cp in copies(b, g):
            cp.wait()
        ng = g + DECODE_AHEAD
        nrow, ng = (b, ng) if ng < nsteps else (jnp.minimum(b + 1, nb - 1), ng - nsteps)
        for cp in copies(nrow, ng):
            cp.start()
        cs, ss_list = [], []
        for i in range(pp // PAGES_PER_SUB):
            c = cbuf[slot, i * sub_tok:(i + 1) * sub_tok, :].astype(BF16)
            krt = kbuf[slot, :, i * sub_tok:(i + 1) * sub_tok].astype(BF16)
            kvq = _dot_nt(lhs, c)
            kv = kvq[:nk]
            ss = jnp.sum((kv * kv).reshape(MLA_HEADS, MLA_NOPE, kv.shape[1]), axis=1)
            ss_list.append(kvq[nk:nk + MLA_HEADS] * lax.rsqrt(ss * (1.0 / MLA_NOPE) + EPS) + _dot(qr, krt))
            cs.append(c)
        if pending is not None:
            m, l, acc = update(*pending, m, l, acc)
        pending = (jnp.concatenate(ss_list, axis=1), cs)
    m, l, acc = update(*pending, m, l, acc)
    o_ref[...] = acc / l

    @pl.when(b == nb - 1)
    def _():
        for g in range(DECODE_AHEAD):
            for cp in copies(b, g):
                cp.wait()


SAMPLE_GDN_ROWS = 4


def _sample_gdn_kernel(q_ref, k_ref, v_ref, gb_ref, s_ref, snew_ref, o_ref):
    hk = GDN_HEADS * GDN_DK
    hrow = lax.broadcasted_iota(jnp.int32, (GDN_HEADS, hk), 0)
    hlane = lax.broadcasted_iota(jnp.int32, (GDN_HEADS, hk), 1) // GDN_DK
    hm = hrow == hlane
    r8 = lax.broadcasted_iota(jnp.int32, (GDN_HEADS, LANES), 0)
    l8 = lax.broadcasted_iota(jnp.int32, (GDN_HEADS, LANES), 1)
    ones = jnp.where(hm, 1.0, 0.0).astype(BF16)
    rows = range(q_ref.shape[0])
    kmask = [jnp.where(hm, jnp.broadcast_to(k_ref[r], (GDN_HEADS, hk)), 0.0) for r in rows]
    qmask = [jnp.where(hm, jnp.broadcast_to(q_ref[r], (GDN_HEADS, hk)), 0.0) for r in rows]
    gbb = [jnp.broadcast_to(gb_ref[r], (GDN_HEADS, LANES)) for r in rows]
    g_col = [jnp.sum(jnp.where(l8 == r8, gbb[r], 0.0), axis=1, keepdims=True) for r in rows]
    b_col = [jnp.sum(jnp.where(l8 == r8 + BETA_LANE0, gbb[r], 0.0), axis=1, keepdims=True) for r in rows]

    def expand_rows(col):
        e0, e1, e2 = _split3(jnp.broadcast_to(col, (GDN_HEADS, GDN_DV)))
        return _dot_tn(jnp.concatenate([ones, ones, ones], axis=0), jnp.concatenate([e0, e1, e2], axis=0))

    def head_dot(xmask, mat):
        xh, xl = _split2(xmask)
        mh, ml = _split2(mat)
        return _dot(jnp.concatenate([xh, xh, xl], axis=1), jnp.concatenate([mh, ml, mh], axis=0))

    def outer(kmask_r, delta_r):
        kh, kl = _split2(kmask_r)
        dh, dl = _split2(delta_r)
        return _dot_tn(jnp.concatenate([kh, kh, kl], axis=0), jnp.concatenate([dh, dl, dh], axis=0))

    decay = [expand_rows(jnp.exp(g_col[r])) for r in rows]
    state = [s_ref[r] * decay[r] for r in rows]
    ks = [head_dot(kmask[r], state[r]) for r in rows]
    delta = [(v_ref[r] - ks[r]) * b_col[r] for r in rows]
    upd = [outer(kmask[r], delta[r]) for r in rows]
    state = [state[r] + upd[r] for r in rows]
    outs = [head_dot(qmask[r], state[r]) for r in rows]
    for r in rows:
        snew_ref[r] = state[r]
        o_ref[r] = outs[r]


def _sample_mix_kernel(o_ref, z_ref, olat_ref, wvbd_ref, esum_ref, gout_ref, og_ref, om_ref):
    o = o_ref[...]
    hi, lo = _split2(o * o)
    ms = _dot(hi, esum_ref[...]) + _dot(lo, esum_ref[...])
    og_ref[...] = (o * lax.rsqrt(ms + EPS) * gout_ref[...] * _silu(z_ref[...])).astype(og_ref.dtype)
    om_ref[...] = _dot(olat_ref[...].astype(BF16), wvbd_ref[...]).astype(om_ref.dtype)


def _seg_tables(width, segs, mean):
    n = len(segs)
    ered = np.zeros((width, LANES), np.float32)
    eexp = np.zeros((LANES, width), np.float32)
    for copy in range(3):
        for s, (a, b) in enumerate(segs):
            ered[a:b, copy * n + s] = 1.0 / (b - a) if mean else 1.0
            eexp[copy * n + s, a:b] = 1.0
    return jnp.asarray(ered, BF16), jnp.asarray(eexp, BF16)


def _rope_slabs(pos):
    inv_freq = ROPE_THETA ** (-jnp.arange(HALF, dtype=F32) / HALF)
    ang = pos.astype(F32)[:, None] * inv_freq[None, :]
    cos, sin = jnp.cos(ang), jnp.sin(ang)
    n = pos.shape[0]
    zeros = lambda w: jnp.zeros((n, w), F32)
    rc = jnp.concatenate([jnp.ones((n, MLA_NOPE), F32), cos, cos, zeros(SLAB - MLA_QK)], axis=1)
    rsu = jnp.concatenate([zeros(MLA_NOPE), -sin, zeros(SLAB - ROPE_HI)], axis=1)
    rsd = jnp.concatenate([zeros(ROPE_HI), sin, zeros(SLAB - MLA_QK)], axis=1)
    return rc, rsu, rsd


def _prepare_weights(g_attn, w_in, w_conv, gdn_a_log, gdn_dt_bias, g_gdn_out, g_q_a, w_q_b, g_q_nope, g_q_rope,
                     g_kv_a, g_k_rope, w_kv_b, g_k_nope):
    row = lambda v: v.reshape(1, -1).astype(F32)
    a = w_in[:, CONV_DIM:CONV_DIM + GDN_HEADS]
    b = w_in[:, CONV_DIM + GDN_HEADS:CONV_DIM + 2 * GDN_HEADS]
    o_z = CONV_DIM + 2 * GDN_HEADS
    o_qa = o_z + GDN_WIDTH
    o_kv = o_qa + Q_LORA
    zc = lambda w: jnp.zeros((D_MODEL, w), F32)
    gb_slab = jnp.concatenate([a, a, a, zc(BETA_LANE0 - 3 * GDN_HEADS), b, b, b, zc(LANES - BETA_LANE0 - 3 * GDN_HEADS)], axis=1)
    kr_slab = jnp.concatenate([zc(ROPE_LO), w_in[:, o_kv + KV_LORA:], zc(SLAB - MLA_QK)], axis=1)
    w_in_p = jnp.concatenate([w_in[:, :CONV_DIM], gb_slab, w_in[:, o_z:o_qa], w_in[:, o_qa:o_kv],
                              w_in[:, o_kv:o_kv + KV_LORA], kr_slab], axis=1).astype(BF16)

    def scalar_slab(v):
        z8 = jnp.zeros((BETA_LANE0 - 3 * GDN_HEADS,), F32)
        return jnp.concatenate([v, v, v, z8, jnp.zeros((LANES - BETA_LANE0,), F32)]).reshape(1, LANES)

    wq = w_q_b.reshape(Q_LORA, MLA_HEADS, MLA_QK)
    w_qb = jnp.concatenate([wq, jnp.zeros((Q_LORA, MLA_HEADS, SLAB - MLA_QK), F32)], axis=2).reshape(Q_LORA, QK_SLAB_W).astype(BF16)
    gq = jnp.tile(jnp.concatenate([g_q_nope, g_q_rope, jnp.zeros((SLAB - MLA_QK,), F32)]), MLA_HEADS).reshape(1, QK_SLAB_W)
    wkv = w_kv_b.reshape(KV_LORA, MLA_HEADS, MLA_NOPE + MLA_VDIM)
    wk_part, wv_part = wkv[:, :, :MLA_NOPE], wkv[:, :, MLA_NOPE:]
    wk = jnp.concatenate([wk_part, jnp.zeros((KV_LORA, MLA_HEADS, SLAB - MLA_NOPE), F32)], axis=2).reshape(KV_LORA, QK_SLAB_W).astype(BF16)
    gk = jnp.tile(jnp.concatenate([g_k_nope, jnp.zeros((SLAB - MLA_NOPE,), F32)]), MLA_HEADS).reshape(1, QK_SLAB_W)
    gkr = jnp.concatenate([jnp.zeros((ROPE_LO,), F32), g_k_rope, jnp.zeros((SLAB - MLA_QK,), F32)]).reshape(1, SLAB)
    wvt = wv_part.reshape(KV_LORA, MLA_HEADS * MLA_VDIM).T.astype(BF16)
    wkt = wk_part.reshape(KV_LORA, MLA_HEADS * MLA_NOPE).T.astype(BF16)
    wv_bd = jnp.zeros((MLA_HEADS, KV_LORA, MLA_HEADS, MLA_VDIM), F32)
    wv_bd = wv_bd.at[jnp.arange(MLA_HEADS), :, jnp.arange(MLA_HEADS), :].set(jnp.moveaxis(wv_part, 1, 0))
    wv_bd = wv_bd.reshape(MLA_HEADS * KV_LORA, MLA_HEADS * MLA_VDIM).astype(BF16)

    ered_g, eexp_g = _seg_tables(2 * GDN_QK_DIM, [(h * GDN_DK, (h + 1) * GDN_DK) for h in range(2 * GDN_HEADS)], False)
    q_segs = ([(h * SLAB, h * SLAB + MLA_NOPE) for h in range(MLA_HEADS)]
              + [(h * SLAB + ROPE_LO, h * SLAB + MLA_QK) for h in range(MLA_HEADS)])
    ered_q, eexp_q = _seg_tables(QK_SLAB_W, q_segs, True)
    ered_k, eexp_k = _seg_tables(QK_SLAB_W, q_segs[:MLA_HEADS], True)
    ered_r, eexp_r = _seg_tables(SLAB, [(ROPE_LO, MLA_QK)], True)
    e6 = np.zeros((LANES, 2 * GDN_WIDTH), np.float32)
    for copy in range(3):
        for h in range(GDN_HEADS):
            e6[copy * GDN_HEADS + h, h * GDN_DV:(h + 1) * GDN_DV] = 1.0
            e6[BETA_LANE0 + copy * GDN_HEADS + h, GDN_WIDTH + h * GDN_DV:GDN_WIDTH + (h + 1) * GDN_DV] = 1.0
    esum = np.kron(np.eye(GDN_HEADS, dtype=np.float32), np.full((GDN_DV, GDN_DV), 1.0 / GDN_DV, np.float32))

    mixer = dict(g_attn=row(g_attn), w_in=w_in_p, w_conv=w_conv.astype(F32), alog=scalar_slab(gdn_a_log),
                 dtb=scalar_slab(gdn_dt_bias), ered_g=ered_g, eexp_g=eexp_g, g_q_a=row(g_q_a), w_qb=w_qb, gq=gq,
                 ered_q=ered_q, eexp_q=eexp_q, g_kv_a=row(g_kv_a), gkr=gkr, ered_r=ered_r, eexp_r=eexp_r, wk=wk, gk=gk,
                 ered_k=ered_k, eexp_k=eexp_k)
    extra = dict(wvt=wvt, wkt=wkt, wv_bd=wv_bd, e6=jnp.asarray(e6, BF16), esum=jnp.asarray(esum, BF16),
                 gout=jnp.tile(g_gdn_out, GDN_HEADS).reshape(1, GDN_WIDTH).astype(F32))
    return mixer, extra


def _params(sem=None):
    return pltpu.CompilerParams(dimension_semantics=sem, vmem_limit_bytes=VMEM_LIMIT)


def _prompt_mixer(x2, slabs, mixer, wvt, bsz, seq, tm):
    n = x2.shape[0]
    tps = seq // tm
    consts = [mixer[k] for k in MIXER_CONSTS] + [wvt]
    row_spec = lambda w: pl.BlockSpec((tm, w), lambda i: (i, 0))
    rope_spec = pl.BlockSpec((tm, SLAB), lambda i: (i % tps, 0))
    out_shapes = [
        jax.ShapeDtypeStruct((n, GDN_QK_DIM), F32), jax.ShapeDtypeStruct((n, GDN_QK_DIM), F32),
        jax.ShapeDtypeStruct((n, GDN_WIDTH), F32), jax.ShapeDtypeStruct((n, LANES), F32),
        jax.ShapeDtypeStruct((n, GDN_WIDTH), F32), jax.ShapeDtypeStruct((n, QK_SLAB_W), BF16),
        jax.ShapeDtypeStruct((n, QK_SLAB_W), BF16), jax.ShapeDtypeStruct((MLA_HEADS * MLA_VDIM, n), BF16),
        jax.ShapeDtypeStruct((n, KV_LORA), F32), jax.ShapeDtypeStruct((n, MLA_ROPE), F32),
        jax.ShapeDtypeStruct((bsz, CONV_WIDTH - 1, CONV_DIM), F32)]
    out_specs = [row_spec(GDN_QK_DIM), row_spec(GDN_QK_DIM), row_spec(GDN_WIDTH), row_spec(LANES), row_spec(GDN_WIDTH),
                 row_spec(QK_SLAB_W), row_spec(QK_SLAB_W), pl.BlockSpec((MLA_HEADS * MLA_VDIM, tm), lambda i: (0, i)),
                 row_spec(KV_LORA), row_spec(MLA_ROPE),
                 pl.BlockSpec((1, CONV_WIDTH - 1, CONV_DIM), lambda i: (i // tps, 0, 0))]
    return pl.pallas_call(
        functools.partial(_prompt_mixer_kernel, tps, tm),
        grid=(n // tm,),
        in_specs=[row_spec(D_MODEL), rope_spec, rope_spec, rope_spec] + [_const_spec(a) for a in consts],
        out_specs=out_specs, out_shape=out_shapes,
        scratch_shapes=[pltpu.VMEM((tm + 8, CONV_DIM), F32)],
        compiler_params=_params(("arbitrary",)), name="prompt_mixer",
    )(x2, *slabs, *consts)


def _sample_mixer(x2, slabs, hist, mixer):
    n = x2.shape[0]
    consts = [mixer[k] for k in MIXER_CONSTS]
    sd = lambda w: jax.ShapeDtypeStruct((n, w), F32)
    return pl.pallas_call(
        _sample_mixer_kernel,
        out_shape=[sd(GDN_QK_DIM), sd(GDN_QK_DIM), sd(GDN_WIDTH), sd(LANES), sd(GDN_WIDTH), sd(QK_SLAB_W), sd(QK_SLAB_W),
                   sd(KV_LORA), sd(MLA_ROPE), sd(CONV_DIM)],
        compiler_params=_params(), name="sample_mixer",
    )(x2, *slabs, hist, *consts)


def _gdn_prompt(qg, kg, vg, gb, z, extra, bsz, seq, tb):
    n = qg.shape[0]
    nblk = seq // tb
    row_spec = lambda w: pl.BlockSpec((tb, w), lambda b, j: (b * nblk + j, 0))
    consts = [extra["e6"], extra["esum"], extra["gout"]]
    return pl.pallas_call(
        functools.partial(_gdn_kernel, nblk, tb),
        grid=(bsz, nblk),
        in_specs=[row_spec(GDN_QK_DIM), row_spec(GDN_QK_DIM), row_spec(GDN_WIDTH), row_spec(LANES), row_spec(GDN_WIDTH)]
        + [_const_spec(a) for a in consts],
        out_specs=[row_spec(GDN_WIDTH), pl.BlockSpec((1, GDN_HEADS, GDN_DK, GDN_DV), lambda b, j: (b, 0, 0, 0))],
        out_shape=[jax.ShapeDtypeStruct((n, GDN_WIDTH), BF16), jax.ShapeDtypeStruct((bsz, GDN_HEADS, GDN_DK, GDN_DV), F32)],
        scratch_shapes=[pltpu.VMEM((GDN_HEADS // GROUP_HEADS, GROUP_W, GROUP_W), F32)],
        compiler_params=_params(("arbitrary", "arbitrary")), name="gdn_chunked",
    )(qg, kg, vg, gb, z, *consts)


def _attn_prompt(qm, km, vt, bsz, seq):
    n = qm.shape[0]
    t = ATT_T
    nq = seq // t
    return pl.pallas_call(
        _attn_kernel,
        grid=(bsz, MLA_HEADS // ATT_HEADS, nq),
        in_specs=[pl.BlockSpec((t, ATT_HEADS * SLAB), lambda b, hp, qi: (b * nq + qi, hp)),
                  pl.BlockSpec((seq, ATT_HEADS * SLAB), lambda b, hp, qi: (b, hp)),
                  pl.BlockSpec((ATT_HEADS * MLA_VDIM, seq), lambda b, hp, qi: (hp, b))],
        out_specs=pl.BlockSpec((t, ATT_HEADS * MLA_VDIM), lambda b, hp, qi: (b * nq + qi, hp)),
        out_shape=jax.ShapeDtypeStruct((n, MLA_HEADS * MLA_VDIM), BF16),
        compiler_params=_params(("arbitrary", "arbitrary", "arbitrary")), name="mla_prompt_attention",
    )(qm, km, vt)


def _tail(x2, og, om, p2, tailw, tm):
    n = x2.shape[0]
    row_spec = lambda w: pl.BlockSpec((tm, w), lambda i: (i, 0))
    return pl.pallas_call(
        _tail_kernel,
        grid=(n // tm,),
        in_specs=[row_spec(D_MODEL), row_spec(GDN_WIDTH), row_spec(GDN_WIDTH), row_spec(PLE_DIM)]
        + [_const_spec(a) for a in tailw],
        out_specs=row_spec(D_MODEL), out_shape=jax.ShapeDtypeStruct((n, D_MODEL), F32),
        compiler_params=_params(("arbitrary",)), name="layer_tail",
    )(x2, og, om, p2, *tailw)


def _sample_prep(qm, km, gk, wk):
    n = qm.shape[0]
    return pl.pallas_call(
        _sample_prep_kernel,
        out_shape=[jax.ShapeDtypeStruct((MLA_HEADS, n, KV_LORA), F32), jax.ShapeDtypeStruct((MLA_HEADS, n, MLA_ROPE), F32),
                   jax.ShapeDtypeStruct((MLA_HEADS, n, 1), F32)],
        compiler_params=_params(), name="sample_prep",
    )(qm, km, gk, wk)


def _paged_attention(page_table, pool_ckv, pool_kr, qabs, qr, sself, ckv_new, wkt):
    bs, n_pages = page_table.shape
    page = pool_ckv.shape[1]
    pp = PAGES_PER_STEP
    nsteps = n_pages // pp
    assert n_pages % pp == 0 and nsteps % DECODE_SLOTS == 0 and DECODE_AHEAD < DECODE_SLOTS, "page groups must fill whole buffer rings"
    per_b = lambda w: pl.BlockSpec((None, MLA_HEADS, w), lambda b, pt: (b, 0, 0))
    grid_spec = pltpu.PrefetchScalarGridSpec(
        num_scalar_prefetch=1, grid=(bs,),
        in_specs=[per_b(KV_LORA), per_b(MLA_ROPE), per_b(1),
                  pl.BlockSpec((None, 1, KV_LORA), lambda b, pt: (b, 0, 0)),
                  pl.BlockSpec(wkt.shape, lambda b, pt: (0, 0)),
                  pl.BlockSpec(memory_space=pl.ANY), pl.BlockSpec(memory_space=pl.ANY)],
        out_specs=pl.BlockSpec((None, MLA_HEADS, KV_LORA), lambda b, pt: (b, 0, 0)),
        scratch_shapes=[pltpu.VMEM((DECODE_SLOTS, pp * page, KV_LORA), F32), pltpu.VMEM((DECODE_SLOTS, MLA_ROPE, pp * page), F32),
                        pltpu.SemaphoreType.DMA((DECODE_SLOTS, 2))])
    return pl.pallas_call(
        functools.partial(_paged_kernel, nsteps, page), grid_spec=grid_spec,
        out_shape=jax.ShapeDtypeStruct((bs, MLA_HEADS, KV_LORA), F32),
        compiler_params=_params(("arbitrary",)), name="mla_paged_decode",
    )(page_table, qabs, qr, sself, ckv_new, wkt, pool_ckv, pool_kr)


def _sample_gdn(q3, k3, v3, gb3, state):
    bs = q3.shape[0]
    hk = GDN_HEADS * GDN_DK
    rb = _pick(bs, (SAMPLE_GDN_ROWS, 2, 1))
    b3 = lambda r, w: pl.BlockSpec((rb, r, w), lambda b: (b, 0, 0))
    return pl.pallas_call(
        _sample_gdn_kernel, grid=(bs // rb,),
        in_specs=[b3(1, hk), b3(1, hk), b3(GDN_HEADS, GDN_DV), b3(1, LANES), b3(hk, GDN_DV)],
        out_specs=[b3(hk, GDN_DV), b3(GDN_HEADS, GDN_DV)],
        out_shape=[jax.ShapeDtypeStruct((bs, hk, GDN_DV), F32), jax.ShapeDtypeStruct((bs, GDN_HEADS, GDN_DV), F32)],
        compiler_params=_params(("arbitrary",)), name="gdn_recurrent_step",
    )(q3, k3, v3, gb3, state)


def _sample_mix(o2, z, olat2, extra):
    n = o2.shape[0]
    return pl.pallas_call(
        _sample_mix_kernel,
        out_shape=[jax.ShapeDtypeStruct((n, GDN_WIDTH), BF16), jax.ShapeDtypeStruct((n, MLA_HEADS * MLA_VDIM), BF16)],
        compiler_params=_params(), name="sample_mix",
    )(o2, z, olat2, extra["wv_bd"], extra["esum"], extra["gout"])


def _pick(n, prefs):
    for t in prefs:
        if n % t == 0:
            return t
    return n


def kernel(x_prompt, x_sample, cache_ckv, cache_krope, state_gdn, state_conv, page_table, p_prompt, p_sample, g_attn, w_in, w_conv, gdn_a_log, gdn_dt_bias, g_gdn_out, g_q_a, w_q_b, g_q_nope, g_q_rope, g_kv_a, g_k_rope, w_kv_b, g_k_nope, w_o, g_ffn, w_ffn_gate, w_ffn_up, w_ffn_down, g_ple, w_ple_gate, w_ple_proj):
    depth = g_attn.shape[0]
    assert depth == 1 and x_sample.shape[1] == 1, "single layer, single new token per sample row"
    bp, seq, _ = x_prompt.shape
    bs = x_sample.shape[0]
    past = page_table.shape[1] * cache_ckv.shape[2]
    li = 0
    mixer, extra = _prepare_weights(g_attn[li], w_in[li], w_conv[li], gdn_a_log[li], gdn_dt_bias[li], g_gdn_out[li],
                                    g_q_a[li], w_q_b[li], g_q_nope[li], g_q_rope[li], g_kv_a[li], g_k_rope[li],
                                    w_kv_b[li], g_k_nope[li])
    row = lambda v: v.reshape(1, -1).astype(F32)
    tailw = [w_o[li].astype(BF16), row(g_ffn[li]), w_ffn_gate[li].astype(BF16), w_ffn_up[li].astype(BF16),
             w_ffn_down[li].astype(BF16), row(g_ple[li]), w_ple_gate[li].astype(BF16), w_ple_proj[li].astype(BF16)]

    n = bp * seq
    xp2 = x_prompt.reshape(n, D_MODEL)
    tm = _pick(seq, (256, 128, 64, 32, 16, 8))
    (qg, kg, vg, gb, z, qm, km, vt, ckv_p, kr_p, conv_p) = _prompt_mixer(
        xp2, _rope_slabs(jnp.arange(seq)), mixer, extra["wvt"], bp, seq, _pick(seq, (512, 256, 128)))
    tb = _pick(seq, (1024, 512, 256))
    og, gdn_p = _gdn_prompt(qg, kg, vg, gb, z, extra, bp, seq, tb)
    om = _attn_prompt(qm, km, vt, bp, seq)
    y_prompt = _tail(xp2, og, om, p_prompt[li].reshape(n, PLE_DIM), tailw, 2 * tm).reshape(bp, seq, D_MODEL)

    xs2 = x_sample.reshape(bs, D_MODEL)
    hist = jnp.moveaxis(state_conv[li], 1, 0)
    (qg_s, kg_s, vg_s, gb_s, z_s, qm_s, km_s, ckv_s, kr_s, cin_s) = _sample_mixer(
        xs2, _rope_slabs(past + jnp.arange(1)), hist, mixer)
    qabs, qr, sself = _sample_prep(qm_s, km_s, mixer["gk"], mixer["wk"])
    per_row = lambda t: jnp.swapaxes(t, 0, 1)
    pool_krt = jnp.swapaxes(cache_krope[li], 1, 2)
    olat = _paged_attention(page_table, cache_ckv[li], pool_krt, per_row(qabs), per_row(qr), per_row(sself),
                            ckv_s.reshape(bs, 1, KV_LORA), extra["wkt"])
    hk = GDN_HEADS * GDN_DK
    s_new, o_s = _sample_gdn(qg_s.reshape(bs, 1, hk), kg_s.reshape(bs, 1, hk), vg_s.reshape(bs, GDN_HEADS, GDN_DV),
                             gb_s.reshape(bs, 1, LANES), state_gdn[li].reshape(bs, hk, GDN_DV))
    og_s, om_s = _sample_mix(o_s.reshape(bs, GDN_WIDTH), z_s, olat.reshape(bs, MLA_HEADS * KV_LORA), extra)
    y_sample = _tail(xs2, og_s, om_s, p_sample[li].reshape(bs, PLE_DIM), tailw, bs).reshape(bs, 1, D_MODEL)
    conv_s = jnp.concatenate([state_conv[li][:, 1:], cin_s[:, None, :]], axis=1)

    return (y_prompt, y_sample,
            ckv_p.reshape(1, bp, seq, KV_LORA), kr_p.reshape(1, bp, seq, MLA_ROPE),
            gdn_p[None], conv_p[None],
            ckv_s.reshape(1, bs, 1, KV_LORA), kr_s.reshape(1, bs, 1, MLA_ROPE),
            s_new.reshape(1, bs, GDN_HEADS, GDN_DK, GDN_DV), conv_s[None])
```

```python
import functools
import math

import numpy as np
import jax
import jax.numpy as jnp
from jax import lax
from jax.experimental import pallas as pl
from jax.experimental.pallas import tpu as pltpu

F32 = jnp.float32
BF16 = jnp.bfloat16

D_MODEL = 1024
PLE_DIM = 256
GDN_HEADS = 8
GDN_DK = 64
GDN_DV = 64
GDN_WIDTH = GDN_HEADS * GDN_DV
GDN_QK_DIM = GDN_HEADS * GDN_DK
CONV_WIDTH = 4
CONV_DIM = 2 * GDN_QK_DIM + GDN_WIDTH
GDN_CHUNK = 64
MLA_HEADS = 8
MLA_NOPE = 64
MLA_ROPE = 32
MLA_VDIM = 64
MLA_QK = MLA_NOPE + MLA_ROPE
Q_LORA = 384
KV_LORA = 256
ROPE_THETA = 10000.0
ATTN_SCALE = MLA_QK ** -0.5
LOG2E = math.log2(math.e)
D_FF = 2816
EPS = 1e-6

LANES = 128
SLAB = 128
HALF = MLA_ROPE // 2
ROPE_LO = MLA_NOPE
ROPE_HI = MLA_NOPE + HALF
QK_SLAB_W = MLA_HEADS * SLAB

OFF_CONV = 0
OFF_GB = OFF_CONV + CONV_DIM
OFF_Z = OFF_GB + LANES
OFF_QA = OFF_Z + GDN_WIDTH
OFF_CKV = OFF_QA + Q_LORA
OFF_KR = OFF_CKV + KV_LORA
IN_PAD = OFF_KR + LANES
BETA_LANE0 = 32

VMEM_LIMIT = 56 * 1024 * 1024


def _dot(a, b):
    return jnp.dot(a, b, preferred_element_type=F32)


def _dot_nt(a, b):
    return lax.dot_general(a, b, (((1,), (1,)), ((), ())), preferred_element_type=F32)


def _dot_tn(a, b):
    return lax.dot_general(a, b, (((0,), (0,)), ((), ())), preferred_element_type=F32)


def _split2(x):
    hi = x.astype(BF16)
    lo = (x - hi.astype(F32)).astype(BF16)
    return hi, lo


def _split3(x):
    hi = x.astype(BF16)
    r1 = x - hi.astype(F32)
    mid = r1.astype(BF16)
    lo = (r1 - mid.astype(F32)).astype(BF16)
    return hi, mid, lo


def _sigmoid(x):
    return 1.0 / (1.0 + jnp.exp(-x))


def _silu(x):
    return x * _sigmoid(x)


def _softplus(x):
    return jnp.maximum(x, 0.0) + jnp.log1p(jnp.exp(-jnp.abs(x)))


def _rmsnorm(x, g):
    return x * lax.rsqrt(jnp.mean(x * x, axis=-1, keepdims=True) + EPS) * g


def _seg_rsqrt(x, ered, eexp, nseg):
    red = _dot((x * x).astype(BF16), ered)
    r = lax.rsqrt(red + EPS)
    p0, p1, p2 = _split3(r)
    lane = lax.broadcasted_iota(jnp.int32, r.shape, 1)
    piece = jnp.where(lane < nseg, p0, jnp.where(lane < 2 * nseg, p1, p2))
    return _dot(piece, eexp)


def _rope(x, c, s_up, s_dn):
    w = x.shape[-1]
    up = pltpu.roll(x, w - HALF, axis=1)
    dn = pltpu.roll(x, HALF, axis=1)
    return x * c + up * s_up + dn * s_dn


def _tile_lanes(x, n):
    return jnp.concatenate([x] * n, axis=1)


def _const_spec(arr):
    nd = arr.ndim
    return pl.BlockSpec(arr.shape, lambda *a, _nd=nd: (0,) * _nd, pipeline_mode=pl.Buffered(1))


MIXER_BLOCKS = 2
MIXER_CONSTS = ("g_attn", "w_in", "w_conv", "alog", "dtb", "ered_g", "eexp_g", "g_q_a", "w_qb", "gq", "ered_q",
                "eexp_q", "g_kv_a", "gkr", "ered_r", "eexp_r", "wk", "gk", "ered_k", "eexp_k")


def _mixer_rows(xs, c, conv_fn, ropes, q_scale):
    n = range(len(xs))
    w_in = c["w_in"]
    xn = [_rmsnorm(x, c["g_attn"][...]).astype(BF16) for x in xs]

    conv_in = [_dot(xn[i], w_in[:, OFF_CONV:OFF_GB]) for i in n]
    rest = [_dot(xn[i], w_in[:, OFF_GB:IN_PAD]) for i in n]
    part = lambda i, lo, hi: rest[i][:, lo - OFF_GB:hi - OFF_GB]
    qs = [_dot(_rmsnorm(part(i, OFF_QA, OFF_CKV), c["g_q_a"][...]).astype(BF16), c["w_qb"][...]) for i in n]
    ckv = [_rmsnorm(part(i, OFF_CKV, OFF_KR), c["g_kv_a"][...]) for i in n]
    ckv_bf = [v.astype(BF16) for v in ckv]
    kk = [_dot(ckv_bf[i], c["wk"][...]) for i in n]

    y = [_silu(v) for v in conv_fn(conv_in)]
    qk = [y[i][:, :2 * GDN_QK_DIM] for i in n]
    qk = [qk[i] * _seg_rsqrt(qk[i], c["ered_g"][...], c["eexp_g"][...], 2 * GDN_HEADS) for i in n]

    gb = []
    for i in n:
        ab = part(i, OFF_GB, OFF_Z)
        lane = lax.broadcasted_iota(jnp.int32, ab.shape, 1)
        g_log = -jnp.exp(c["alog"][...]) * _softplus(ab + c["dtb"][...])
        gb.append(jnp.where(lane < BETA_LANE0, g_log, _sigmoid(ab)))
    z = [part(i, OFF_Z, OFF_QA) for i in n]

    qs = [qs[i] * _seg_rsqrt(qs[i], c["ered_q"][...], c["eexp_q"][...], 2 * MLA_HEADS) * c["gq"][...] for i in n]
    q_mla = [_rope(qs[i], *[_tile_lanes(t, MLA_HEADS) for t in ropes[i]]) * q_scale for i in n]

    kr = [part(i, OFF_KR, IN_PAD) for i in n]
    kr = [kr[i] * _seg_rsqrt(kr[i], c["ered_r"][...], c["eexp_r"][...], 1) * c["gkr"][...] for i in n]
    kr = [_rope(kr[i], *ropes[i]) for i in n]
    kk = [kk[i] * _seg_rsqrt(kk[i], c["ered_k"][...], c["eexp_k"][...], MLA_HEADS) * c["gk"][...] for i in n]
    return [dict(conv_in=conv_in[i], q_g=qk[i][:, :GDN_QK_DIM] * (GDN_DK ** -0.5), k_g=qk[i][:, GDN_QK_DIM:],
                 v_g=y[i][:, 2 * GDN_QK_DIM:], gb=gb[i], z=z[i], q_mla=q_mla[i],
                 k_mla=kk[i] + _tile_lanes(kr[i], MLA_HEADS), ckv=ckv[i], ckv_bf=ckv_bf[i], kr=kr[i]) for i in n]


def _prompt_mixer_kernel(tiles_per_seq, tm, x_ref, rc_ref, rsu_ref, rsd_ref, *refs):
    nc = len(MIXER_CONSTS)
    c = dict(zip(MIXER_CONSTS, refs[:nc]))
    wvt_ref = refs[nc]
    (qg_ref, kg_ref, vg_ref, gb_ref, z_ref, qm_ref, km_ref, vt_ref, ckv_ref, kro_ref, cs_ref, ext_ref) = refs[nc + 1:]
    i = pl.program_id(0)
    hm = tm // MIXER_BLOCKS
    blocks = [slice(j * hm, (j + 1) * hm) for j in range(MIXER_BLOCKS)]

    @pl.when(i % tiles_per_seq == 0)
    def _():
        ext_ref[5:8, :] = jnp.zeros((3, CONV_DIM), F32)

    def conv_fn(conv_ins):
        w = c["w_conv"]
        for sl, v in zip(blocks, conv_ins):
            ext_ref[8 + sl.start:8 + sl.stop, :] = v
        return [v * w[3:4, :] + ext_ref[7 + sl.start:7 + sl.stop, :] * w[2:3, :]
                + ext_ref[6 + sl.start:6 + sl.stop, :] * w[1:2, :] + ext_ref[5 + sl.start:5 + sl.stop, :] * w[0:1, :]
                for sl, v in zip(blocks, conv_ins)]

    res = _mixer_rows([x_ref[sl, :] for sl in blocks], c, conv_fn,
                      [(rc_ref[sl, :], rsu_ref[sl, :], rsd_ref[sl, :]) for sl in blocks], ATTN_SCALE * LOG2E)
    last3 = ext_ref[tm + 5:tm + 8, :]
    ext_ref[5:8, :] = last3
    cs_ref[0] = last3
    for sl, r in zip(blocks, res):
        qg_ref[sl, :] = r["q_g"]
        kg_ref[sl, :] = r["k_g"]
        vg_ref[sl, :] = r["v_g"]
        gb_ref[sl, :] = r["gb"]
        z_ref[sl, :] = r["z"]
        qm_ref[sl, :] = r["q_mla"].astype(BF16)
        km_ref[sl, :] = r["k_mla"].astype(BF16)
        vt_ref[:, sl] = _dot_nt(wvt_ref[...], r["ckv_bf"]).astype(BF16)
        ckv_ref[sl, :] = r["ckv"]
        kro_ref[sl, :] = r["kr"][:, ROPE_LO:ROPE_LO + MLA_ROPE]


def _sample_mixer_kernel(x_ref, rc_ref, rsu_ref, rsd_ref, hist_ref, *refs):
    nc = len(MIXER_CONSTS)
    c = dict(zip(MIXER_CONSTS, refs[:nc]))
    (qg_ref, kg_ref, vg_ref, gb_ref, z_ref, qm_ref, km_ref, ckv_ref, kro_ref, cin_ref) = refs[nc:]

    def conv_fn(conv_ins):
        w = c["w_conv"]
        return [v * w[3:4, :] + hist_ref[2] * w[2:3, :] + hist_ref[1] * w[1:2, :] + hist_ref[0] * w[0:1, :] for v in conv_ins]

    (r,) = _mixer_rows([x_ref[...]], c, conv_fn, [(rc_ref[...], rsu_ref[...], rsd_ref[...])], ATTN_SCALE)
    qg_ref[...] = r["q_g"]
    kg_ref[...] = r["k_g"]
    vg_ref[...] = r["v_g"]
    gb_ref[...] = r["gb"]
    z_ref[...] = r["z"]
    qm_ref[...] = r["q_mla"]
    km_ref[...] = r["k_mla"]
    ckv_ref[...] = r["ckv"]
    kro_ref[...] = r["kr"][:, ROPE_LO:ROPE_LO + MLA_ROPE]
    cin_ref[...] = r["conv_in"]


GROUP_HEADS = 4
GROUP_W = GROUP_HEADS * GDN_DK


def _cumsum_rows(x, period):
    row = lax.broadcasted_iota(jnp.int32, x.shape, 0) % period
    s = 1
    while s < period:
        x = x + jnp.where(row >= s, pltpu.roll(x, s, axis=0), 0.0)
        s *= 2
    return x


GDN_ITER_CHUNKS = 4


def _bd(y, bd_mask):
    return jnp.where(bd_mask, jnp.concatenate([y] * GROUP_HEADS, axis=0), 0.0)


def _gdn_solve_stages(chains, masks, sols):
    bd_mask, tril_cat, strict_cat, eye_cat = masks
    c = GDN_CHUNK
    n = range(len(chains))
    bd = lambda y: _bd(y, bd_mask)
    kb = [ch["k"] * ch["b_r"] for ch in chains]
    aq = [_dot_nt(jnp.concatenate([kb[i], chains[i]["q"]], axis=0), bd(chains[i]["k"])) for i in n]
    yield
    decay = []
    for ch in chains:
        g_col = jnp.sum(jnp.where(eye_cat, ch["g_r"], 0.0), axis=0, keepdims=True)
        decay.append(jnp.exp(jnp.where(tril_cat, ch["g_r"] - g_col, -jnp.inf)))
    qk = [aq[i][c:] * decay[i] for i in n]
    p = [-jnp.where(strict_cat, aq[i][:c] * decay[i], 0.0) for i in n]
    s_inv = [jnp.where(eye_cat, 1.0, 0.0) + p[i] for i in n]
    p = [_dot(p[i], bd(p[i])) for i in n]
    yield
    for lvl in range(5):
        s_next = [s_inv[i] + _dot(p[i], bd(s_inv[i])) for i in n]
        yield
        if lvl < 4:
            p = [_dot(p[i], bd(p[i])) for i in n]
            yield
        s_inv = s_next
    eg = [jnp.exp(ch["g_r"]) for ch in chains]
    u = [_dot(s_inv[i], bd(chains[i]["v"] * chains[i]["b_r"])) for i in n]
    yield
    w = [_dot(s_inv[i], bd(kb[i] * eg[i])) for i in n]
    for i in n:
        g_r = chains[i]["g_r"]
        g_last = g_r[c - 1:c, :]
        sols.append(dict(u=u[i], wq=jnp.concatenate([w[i], chains[i]["q"] * eg[i]], axis=0), qk=qk[i],
                         kd=chains[i]["k"] * jnp.exp(g_last - g_r), e_last=jnp.exp(g_last)))


def _gdn_apply_stages(sols, states, bd_mask, box):
    c = GDN_CHUNK
    ngrp = len(states)
    n = range(ngrp)
    outs = []
    for t in range(len(sols) // ngrp):
        sl = sols[t * ngrp:(t + 1) * ngrp]
        ws = [_dot(sl[i]["wq"], states[i]) for i in n]
        yield
        v_new = [sl[i]["u"] - ws[i][:c] for i in n]
        o = [ws[i][c:] + _dot(sl[i]["qk"], _bd(v_new[i], bd_mask)) for i in n]
        upd = [_dot_tn(sl[i]["kd"], v_new[i]) for i in n]
        yield
        states = [states[i] * sl[i]["e_last"] + jnp.where(bd_mask, upd[i], 0.0) for i in n]
        outs.append(jnp.concatenate(o, axis=1))
    box["o"] = jnp.concatenate(outs, axis=0)
    box["states"] = states


def _interleave(stage_generators):
    live = list(stage_generators)
    while live:
        for g in list(live):
            try:
                next(g)
            except StopIteration:
                live.remove(g)


def _gdn_kernel(nblk, tb, q_ref, k_ref, v_ref, gb_ref, z_ref, e6_ref, esum_ref, gout_ref, o_ref, sfin_ref, s_ref):
    j = pl.program_id(1)

    @pl.when(j == 0)
    def _():
        s_ref[...] = jnp.zeros(s_ref.shape, F32)

    c = GDN_CHUNK
    ngrp = GDN_HEADS // GROUP_HEADS
    rows_it = GDN_ITER_CHUNKS * c
    r_bd = lax.broadcasted_iota(jnp.int32, (GROUP_W, GROUP_W), 0) // c
    c_bd = lax.broadcasted_iota(jnp.int32, (GROUP_W, GROUP_W), 1) // c
    bd_mask = r_bd == c_bd
    ri = lax.broadcasted_iota(jnp.int32, (c, GROUP_W), 0)
    cj = lax.broadcasted_iota(jnp.int32, (c, GROUP_W), 1) % c
    masks = (bd_mask, ri >= cj, ri > cj, ri == cj)
    lane = lax.broadcasted_iota(jnp.int32, (rows_it, LANES), 1)
    copy_id = (lane % BETA_LANE0) // GDN_HEADS

    def chains_of(rows):
        gb = gb_ref[rows, :]
        sc = jnp.where(lane < BETA_LANE0, _cumsum_rows(gb, c), gb)
        p0, p1, p2 = _split3(sc)
        piece = jnp.where(copy_id == 0, p0, jnp.where(copy_id == 1, p1, p2))
        ex = _dot(piece, e6_ref[...])
        q = q_ref[rows, :]
        k = k_ref[rows, :]
        v = v_ref[rows, :]
        chains = []
        for t in range(GDN_ITER_CHUNKS):
            rs = slice(t * c, (t + 1) * c)
            for grp in range(ngrp):
                sl = slice(grp * GROUP_W, (grp + 1) * GROUP_W)
                sb = slice(GDN_WIDTH + grp * GROUP_W, GDN_WIDTH + (grp + 1) * GROUP_W)
                chains.append(dict(q=q[rs, sl], k=k[rs, sl], v=v[rs, sl], g_r=ex[rs, sl], b_r=ex[rs, sb]))
        return chains

    def finish(o, rows):
        hi, lo = _split2(o * o)
        ms = _dot(hi, esum_ref[...]) + _dot(lo, esum_ref[...])
        o_ref[rows, :] = (o * lax.rsqrt(ms + EPS) * gout_ref[...] * _silu(z_ref[rows, :])).astype(o_ref.dtype)

    states = [s_ref[grp] for grp in range(ngrp)]
    pending = None
    for gi in range(tb // rows_it):
        rows = slice(gi * rows_it, (gi + 1) * rows_it)
        sols, box = [], {}
        stages = [_gdn_solve_stages(chains_of(rows), masks, sols)]
        if pending is not None:
            stages.append(_gdn_apply_stages(pending[0], states, bd_mask, box))
        _interleave(stages)
        if pending is not None:
            states = box["states"]
            finish(box["o"], pending[1])
        pending = (sols, rows)
    box = {}
    _interleave([_gdn_apply_stages(pending[0], states, bd_mask, box)])
    finish(box["o"], pending[1])
    for grp in range(ngrp):
        s_ref[grp] = box["states"][grp]

    @pl.when(j == nblk - 1)
    def _():
        for grp in range(ngrp):
            for h in range(GROUP_HEADS):
                sfin_ref[0, grp * GROUP_HEADS + h] = s_ref[grp, h * c:(h + 1) * c, h * c:(h + 1) * c]


ATT_T = 256
ATT_HEADS = 8
ATT_SUM_ROWS = 16
ATT_WIDE = 2


def _attn_kernel(q_ref, k_ref, vt_ref, o_ref):
    qi = pl.program_id(2)
    t = ATT_T
    qs = [q_ref[:, hh * SLAB:(hh + 1) * SLAB] for hh in range(ATT_HEADS)]

    def score_tile(k0, tk):
        return tuple(_dot_nt(k_ref[pl.ds(k0, tk), hh * SLAB:(hh + 1) * SLAB], qs[hh]) for hh in range(ATT_HEADS))

    def consume(k0, tk, carry, masked):
        scores = score_tile(k0, tk)
        stats = []
        for hh in range(ATT_HEADS):
            m, _ = carry[hh]
            s = scores[hh]
            if masked:
                kp = lax.broadcasted_iota(jnp.int32, (tk, t), 0)
                qp = lax.broadcasted_iota(jnp.int32, (tk, t), 1)
                s = jnp.where(qp >= kp, s, -jnp.inf)
            m_new = jnp.maximum(m, jnp.max(s, axis=0, keepdims=True))
            stats.append((m_new, jnp.exp2(m - m_new), jnp.exp2(s - m_new).astype(BF16)))
        out = []
        for hh in range(ATT_HEADS):
            m_new, alpha, p = stats[hh]
            vt = vt_ref[hh * MLA_VDIM:(hh + 1) * MLA_VDIM, pl.ds(k0, tk)]
            ones = jnp.ones((ATT_SUM_ROWS, tk), BF16)
            out.append((m_new, carry[hh][1] * alpha + _dot(jnp.concatenate([vt, ones], axis=0), p)))
        return tuple(out)

    init = tuple((jnp.full((1, t), -jnp.inf, F32), jnp.zeros((MLA_VDIM + ATT_SUM_ROWS, t), F32)) for _ in range(ATT_HEADS))
    wide = ATT_WIDE * t
    nwide = qi // ATT_WIDE
    carry = lax.fori_loop(0, nwide, lambda j, c: consume(pl.multiple_of(j * wide, wide), wide, c, False), init)
    carry = lax.fori_loop(nwide * ATT_WIDE, qi, lambda j, c: consume(pl.multiple_of(j * t, t), t, c, False), carry)
    carry = consume(pl.multiple_of(qi * t, t), t, carry, True)
    outs = [acc[:MLA_VDIM] / acc[MLA_VDIM:MLA_VDIM + 1] for (_, acc) in carry]
    o_ref[...] = jnp.concatenate(outs, axis=0).T.astype(o_ref.dtype)


TAIL_BLOCKS = 2

def _tail_kernel(x_ref, og_ref, om_ref, p_ref, wo_ref, gffn_ref, wg_ref, wu_ref, wd_ref, gple_ref, wpg_ref, wpp_ref, y_ref):
    rows = x_ref.shape[0]
    nblk = TAIL_BLOCKS if rows % (TAIL_BLOCKS * LANES) == 0 else 1
    blocks = [slice(j * rows // nblk, (j + 1) * rows // nblk) for j in range(nblk)]
    n = range(nblk)
    ple = [_dot(p_ref[sl, :].astype(BF16), wpp_ref[...]) for sl in blocks]
    h = [x_ref[sl, :] + _dot(og_ref[sl, :], wo_ref[:GDN_WIDTH, :]) + _dot(om_ref[sl, :], wo_ref[GDN_WIDTH:, :])
         for sl in blocks]
    u = [_rmsnorm(h[i], gffn_ref[...]).astype(BF16) for i in n]
    act = [(_silu(_dot(u[i], wg_ref[...])) * _dot(u[i], wu_ref[...])).astype(BF16) for i in n]
    acc = [h[i] + _dot(act[i], wd_ref[...]) for i in n]
    gate = [_sigmoid(_dot(_rmsnorm(acc[i], gple_ref[...]).astype(BF16), wpg_ref[...])) for i in n]
    for i, sl in enumerate(blocks):
        y_ref[sl, :] = acc[i] + ple[i] * gate[i]


PAGES_PER_STEP = 64
PAGES_PER_SUB = 64
DECODE_SLOTS = 2
DECODE_AHEAD = 1


def _sample_prep_kernel(qm_ref, km_ref, gk_ref, wk_ref, qabs_ref, qr_ref, sself_ref):
    qm = qm_ref[...]
    prod = qm * km_ref[...]
    qg = (qm * gk_ref[...]).astype(BF16)
    for h in range(MLA_HEADS):
        sl = slice(h * SLAB, (h + 1) * SLAB)
        qabs_ref[h] = _dot_nt(qg[:, sl], wk_ref[:, sl])
        qr_ref[h] = qm[:, h * SLAB + ROPE_LO:h * SLAB + ROPE_LO + MLA_ROPE]
        sself_ref[h] = jnp.sum(prod[:, sl], axis=1, keepdims=True)


def _paged_kernel(nsteps, page, pt_ref, qabs_ref, qr_ref, sself_ref, ckvn_ref, wkt_ref, ckv_hbm, krt_hbm, o_ref,
                  cbuf, kbuf, sems):
    pp = PAGES_PER_STEP
    b = pl.program_id(0)
    nb = pl.num_programs(0)

    def copies(row, g):
        slot = g % DECODE_SLOTS
        out = []
        for j in range(pp):
            pg = pt_ref[row, g * pp + j]
            out.append(pltpu.make_async_copy(ckv_hbm.at[pg], cbuf.at[slot, pl.ds(j * page, page), :], sems.at[slot, 0]))
            out.append(pltpu.make_async_copy(krt_hbm.at[pg], kbuf.at[slot, :, pl.ds(j * page, page)], sems.at[slot, 1]))
        return out

    @pl.when(b == 0)
    def _():
        for g in range(DECODE_AHEAD):
            for cp in copies(0, g):
                cp.start()

    qa = qabs_ref[...].astype(BF16)
    lhs = jnp.concatenate([wkt_ref[...], qa, jnp.zeros_like(qa)], axis=0)
    nk = MLA_HEADS * MLA_NOPE
    qr = qr_ref[...].astype(BF16)
    m = sself_ref[...]
    l = jnp.ones(m.shape, F32)
    acc = jnp.broadcast_to(ckvn_ref[...], (MLA_HEADS, KV_LORA))
    sub_tok = PAGES_PER_SUB * page

    def update(s, cs, m, l, acc):
        m_new = jnp.maximum(m, jnp.max(s, axis=1, keepdims=True))
        alpha = jnp.exp(m - m_new)
        p = jnp.exp(s - m_new)
        l = l * alpha + jnp.sum(p, axis=1, keepdims=True)
        return m_new, l, acc * alpha + _dot(p.astype(BF16), jnp.concatenate(cs, axis=0))

    pending = None
    for g in range(nsteps):
        slot = g % DECODE_SLOTS
        for cp in copies(b, g):
            cp.wait()
        ng = g + DECODE_AHEAD
        nrow, ng = (b, ng) if ng < nsteps else (jnp.minimum(b + 1, nb - 1), ng - nsteps)
        for cp in copies(nrow, ng):
            cp.start()
        cs, ss_list = [], []
        for i in range(pp // PAGES_PER_SUB):
            c = cbuf[slot, i * sub_tok:(i + 1) * sub_tok, :].astype(BF16)
            krt = kbuf[slot, :, i * sub_tok:(i + 1) * sub_tok].astype(BF16)
            kvq = _dot_nt(lhs, c)
            kv = kvq[:nk]
            ss = jnp.sum((kv * kv).reshape(MLA_HEADS, MLA_NOPE, kv.shape[1]), axis=1)
            ss_list.append(kvq[nk:nk + MLA_HEADS] * lax.rsqrt(ss * (1.0 / MLA_NOPE) + EPS) + _dot(qr, krt))
            cs.append(c)
        if pending is not None:
            m, l, acc = update(*pending, m, l, acc)
        pending = (jnp.concatenate(ss_list, axis=1), cs)
    m, l, acc = update(*pending, m, l, acc)
    o_ref[...] = acc / l

    @pl.when(b == nb - 1)
    def _():
        for g in range(DECODE_AHEAD):
            for cp in copies(b, g):
                cp.wait()


SAMPLE_GDN_ROWS = 4


def _sample_gdn_kernel(q_ref, k_ref, v_ref, gb_ref, s_ref, snew_ref, o_ref):
    hk = GDN_HEADS * GDN_DK
    hrow = lax.broadcasted_iota(jnp.int32, (GDN_HEADS, hk), 0)
    hlane = lax.broadcasted_iota(jnp.int32, (GDN_HEADS, hk), 1) // GDN_DK
    hm = hrow == hlane
    r8 = lax.broadcasted_iota(jnp.int32, (GDN_HEADS, LANES), 0)
    l8 = lax.broadcasted_iota(jnp.int32, (GDN_HEADS, LANES), 1)
    ones = jnp.where(hm, 1.0, 0.0).astype(BF16)
    rows = range(q_ref.shape[0])
    kmask = [jnp.where(hm, jnp.broadcast_to(k_ref[r], (GDN_HEADS, hk)), 0.0) for r in rows]
    qmask = [jnp.where(hm, jnp.broadcast_to(q_ref[r], (GDN_HEADS, hk)), 0.0) for r in rows]
    gbb = [jnp.broadcast_to(gb_ref[r], (GDN_HEADS, LANES)) for r in rows]
    g_col = [jnp.sum(jnp.where(l8 == r8, gbb[r], 0.0), axis=1, keepdims=True) for r in rows]
    b_col = [jnp.sum(jnp.where(l8 == r8 + BETA_LANE0, gbb[r], 0.0), axis=1, keepdims=True) for r in rows]

    def expand_rows(col):
        e0, e1, e2 = _split3(jnp.broadcast_to(col, (GDN_HEADS, GDN_DV)))
        return _dot_tn(jnp.concatenate([ones, ones, ones], axis=0), jnp.concatenate([e0, e1, e2], axis=0))

    def head_dot(xmask, mat):
        xh, xl = _split2(xmask)
        mh, ml = _split2(mat)
        return _dot(jnp.concatenate([xh, xh, xl], axis=1), jnp.concatenate([mh, ml, mh], axis=0))

    def outer(kmask_r, delta_r):
        kh, kl = _split2(kmask_r)
        dh, dl = _split2(delta_r)
        return _dot_tn(jnp.concatenate([kh, kh, kl], axis=0), jnp.concatenate([dh, dl, dh], axis=0))

    decay = [expand_rows(jnp.exp(g_col[r])) for r in rows]
    state = [s_ref[r] * decay[r] for r in rows]
    ks = [head_dot(kmask[r], state[r]) for r in rows]
    delta = [(v_ref[r] - ks[r]) * b_col[r] for r in rows]
    upd = [outer(kmask[r], delta[r]) for r in rows]
    state = [state[r] + upd[r] for r in rows]
    outs = [head_dot(qmask[r], state[r]) for r in rows]
    for r in rows:
        snew_ref[r] = state[r]
        o_ref[r] = outs[r]


def _sample_mix_kernel(o_ref, z_ref, olat_ref, wvbd_ref, esum_ref, gout_ref, og_ref, om_ref):
    o = o_ref[...]
    hi, lo = _split2(o * o)
    ms = _dot(hi, esum_ref[...]) + _dot(lo, esum_ref[...])
    og_ref[...] = (o * lax.rsqrt(ms + EPS) * gout_ref[...] * _silu(z_ref[...])).astype(og_ref.dtype)
    om_ref[...] = _dot(olat_ref[...].astype(BF16), wvbd_ref[...]).astype(om_ref.dtype)


def _seg_tables(width, segs, mean):
    n = len(segs)
    ered = np.zeros((width, LANES), np.float32)
    eexp = np.zeros((LANES, width), np.float32)
    for copy in range(3):
        for s, (a, b) in enumerate(segs):
            ered[a:b, copy * n + s] = 1.0 / (b - a) if mean else 1.0
            eexp[copy * n + s, a:b] = 1.0
    return jnp.asarray(ered, BF16), jnp.asarray(eexp, BF16)


def _rope_slabs(pos):
    inv_freq = ROPE_THETA ** (-jnp.arange(HALF, dtype=F32) / HALF)
    ang = pos.astype(F32)[:, None] * inv_freq[None, :]
    cos, sin = jnp.cos(ang), jnp.sin(ang)
    n = pos.shape[0]
    zeros = lambda w: jnp.zeros((n, w), F32)
    rc = jnp.concatenate([jnp.ones((n, MLA_NOPE), F32), cos, cos, zeros(SLAB - MLA_QK)], axis=1)
    rsu = jnp.concatenate([zeros(MLA_NOPE), -sin, zeros(SLAB - ROPE_HI)], axis=1)
    rsd = jnp.concatenate([zeros(ROPE_HI), sin, zeros(SLAB - MLA_QK)], axis=1)
    return rc, rsu, rsd


def _prepare_weights(g_attn, w_in, w_conv, gdn_a_log, gdn_dt_bias, g_gdn_out, g_q_a, w_q_b, g_q_nope, g_q_rope,
                     g_kv_a, g_k_rope, w_kv_b, g_k_nope):
    row = lambda v: v.reshape(1, -1).astype(F32)
    a = w_in[:, CONV_DIM:CONV_DIM + GDN_HEADS]
    b = w_in[:, CONV_DIM + GDN_HEADS:CONV_DIM + 2 * GDN_HEADS]
    o_z = CONV_DIM + 2 * GDN_HEADS
    o_qa = o_z + GDN_WIDTH
    o_kv = o_qa + Q_LORA
    zc = lambda w: jnp.zeros((D_MODEL, w), F32)
    gb_slab = jnp.concatenate([a, a, a, zc(BETA_LANE0 - 3 * GDN_HEADS), b, b, b, zc(LANES - BETA_LANE0 - 3 * GDN_HEADS)], axis=1)
    kr_slab = jnp.concatenate([zc(ROPE_LO), w_in[:, o_kv + KV_LORA:], zc(SLAB - MLA_QK)], axis=1)
    w_in_p = jnp.concatenate([w_in[:, :CONV_DIM], gb_slab, w_in[:, o_z:o_qa], w_in[:, o_qa:o_kv],
                              w_in[:, o_kv:o_kv + KV_LORA], kr_slab], axis=1).astype(BF16)

    def scalar_slab(v):
        z8 = jnp.zeros((BETA_LANE0 - 3 * GDN_HEADS,), F32)
        return jnp.concatenate([v, v, v, z8, jnp.zeros((LANES - BETA_LANE0,), F32)]).reshape(1, LANES)

    wq = w_q_b.reshape(Q_LORA, MLA_HEADS, MLA_QK)
    w_qb = jnp.concatenate([wq, jnp.zeros((Q_LORA, MLA_HEADS, SLAB - MLA_QK), F32)], axis=2).reshape(Q_LORA, QK_SLAB_W).astype(BF16)
    gq = jnp.tile(jnp.concatenate([g_q_nope, g_q_rope, jnp.zeros((SLAB - MLA_QK,), F32)]), MLA_HEADS).reshape(1, QK_SLAB_W)
    wkv = w_kv_b.reshape(KV_LORA, MLA_HEADS, MLA_NOPE + MLA_VDIM)
    wk_part, wv_part = wkv[:, :, :MLA_NOPE], wkv[:, :, MLA_NOPE:]
    wk = jnp.concatenate([wk_part, jnp.zeros((KV_LORA, MLA_HEADS, SLAB - MLA_NOPE), F32)], axis=2).reshape(KV_LORA, QK_SLAB_W).astype(BF16)
    gk = jnp.tile(jnp.concatenate([g_k_nope, jnp.zeros((SLAB - MLA_NOPE,), F32)]), MLA_HEADS).reshape(1, QK_SLAB_W)
    gkr = jnp.concatenate([jnp.zeros((ROPE_LO,), F32), g_k_rope, jnp.zeros((SLAB - MLA_QK,), F32)]).reshape(1, SLAB)
    wvt = wv_part.reshape(KV_LORA, MLA_HEADS * MLA_VDIM).T.astype(BF16)
    wkt = wk_part.reshape(KV_LORA, MLA_HEADS * MLA_NOPE).T.astype(BF16)
    wv_bd = jnp.zeros((MLA_HEADS, KV_LORA, MLA_HEADS, MLA_VDIM), F32)
    wv_bd = wv_bd.at[jnp.arange(MLA_HEADS), :, jnp.arange(MLA_HEADS), :].set(jnp.moveaxis(wv_part, 1, 0))
    wv_bd = wv_bd.reshape(MLA_HEADS * KV_LORA, MLA_HEADS * MLA_VDIM).astype(BF16)

    ered_g, eexp_g = _seg_tables(2 * GDN_QK_DIM, [(h * GDN_DK, (h + 1) * GDN_DK) for h in range(2 * GDN_HEADS)], False)
    q_segs = ([(h * SLAB, h * SLAB + MLA_NOPE) for h in range(MLA_HEADS)]
              + [(h * SLAB + ROPE_LO, h * SLAB + MLA_QK) for h in range(MLA_HEADS)])
    ered_q, eexp_q = _seg_tables(QK_SLAB_W, q_segs, True)
    ered_k, eexp_k = _seg_tables(QK_SLAB_W, q_segs[:MLA_HEADS], True)
    ered_r, eexp_r = _seg_tables(SLAB, [(ROPE_LO, MLA_QK)], True)
    e6 = np.zeros((LANES, 2 * GDN_WIDTH), np.float32)
    for copy in range(3):
        for h in range(GDN_HEADS):
            e6[copy * GDN_HEADS + h, h * GDN_DV:(h + 1) * GDN_DV] = 1.0
            e6[BETA_LANE0 + copy * GDN_HEADS + h, GDN_WIDTH + h * GDN_DV:GDN_WIDTH + (h + 1) * GDN_DV] = 1.0
    esum = np.kron(np.eye(GDN_HEADS, dtype=np.float32), np.full((GDN_DV, GDN_DV), 1.0 / GDN_DV, np.float32))

    mixer = dict(g_attn=row(g_attn), w_in=w_in_p, w_conv=w_conv.astype(F32), alog=scalar_slab(gdn_a_log),
                 dtb=scalar_slab(gdn_dt_bias), ered_g=ered_g, eexp_g=eexp_g, g_q_a=row(g_q_a), w_qb=w_qb, gq=gq,
                 ered_q=ered_q, eexp_q=eexp_q, g_kv_a=row(g_kv_a), gkr=gkr, ered_r=ered_r, eexp_r=eexp_r, wk=wk, gk=gk,
                 ered_k=ered_k, eexp_k=eexp_k)
    extra = dict(wvt=wvt, wkt=wkt, wv_bd=wv_bd, e6=jnp.asarray(e6, BF16), esum=jnp.asarray(esum, BF16),
                 gout=jnp.tile(g_gdn_out, GDN_HEADS).reshape(1, GDN_WIDTH).astype(F32))
    return mixer, extra


def _params(sem=None):
    return pltpu.CompilerParams(dimension_semantics=sem, vmem_limit_bytes=VMEM_LIMIT)


def _prompt_mixer(x2, slabs, mixer, wvt, bsz, seq, tm):
    n = x2.shape[0]
    tps = seq // tm
    consts = [mixer[k] for k in MIXER_CONSTS] + [wvt]
    row_spec = lambda w: pl.BlockSpec((tm, w), lambda i: (i, 0))
    rope_spec = pl.BlockSpec((tm, SLAB), lambda i: (i % tps, 0))
    out_shapes = [
        jax.ShapeDtypeStruct((n, GDN_QK_DIM), F32), jax.ShapeDtypeStruct((n, GDN_QK_DIM), F32),
        jax.ShapeDtypeStruct((n, GDN_WIDTH), F32), jax.ShapeDtypeStruct((n, LANES), F32),
        jax.ShapeDtypeStruct((n, GDN_WIDTH), F32), jax.ShapeDtypeStruct((n, QK_SLAB_W), BF16),
        jax.ShapeDtypeStruct((n, QK_SLAB_W), BF16), jax.ShapeDtypeStruct((MLA_HEADS * MLA_VDIM, n), BF16),
        jax.ShapeDtypeStruct((n, KV_LORA), F32), jax.ShapeDtypeStruct((n, MLA_ROPE), F32),
        jax.ShapeDtypeStruct((bsz, CONV_WIDTH - 1, CONV_DIM), F32)]
    out_specs = [row_spec(GDN_QK_DIM), row_spec(GDN_QK_DIM), row_spec(GDN_WIDTH), row_spec(LANES), row_spec(GDN_WIDTH),
                 row_spec(QK_SLAB_W), row_spec(QK_SLAB_W), pl.BlockSpec((MLA_HEADS * MLA_VDIM, tm), lambda i: (0, i)),
                 row_spec(KV_LORA), row_spec(MLA_ROPE),
                 pl.BlockSpec((1, CONV_WIDTH - 1, CONV_DIM), lambda i: (i // tps, 0, 0))]
    return pl.pallas_call(
        functools.partial(_prompt_mixer_kernel, tps, tm),
        grid=(n // tm,),
        in_specs=[row_spec(D_MODEL), rope_spec, rope_spec, rope_spec] + [_const_spec(a) for a in consts],
        out_specs=out_specs, out_shape=out_shapes,
        scratch_shapes=[pltpu.VMEM((tm + 8, CONV_DIM), F32)],
        compiler_params=_params(("arbitrary",)), name="prompt_mixer",
    )(x2, *slabs, *consts)


def _sample_mixer(x2, slabs, hist, mixer):
    n = x2.shape[0]
    consts = [mixer[k] for k in MIXER_CONSTS]
    sd = lambda w: jax.ShapeDtypeStruct((n, w), F32)
    return pl.pallas_call(
        _sample_mixer_kernel,
        out_shape=[sd(GDN_QK_DIM), sd(GDN_QK_DIM), sd(GDN_WIDTH), sd(LANES), sd(GDN_WIDTH), sd(QK_SLAB_W), sd(QK_SLAB_W),
                   sd(KV_LORA), sd(MLA_ROPE), sd(CONV_DIM)],
        compiler_params=_params(), name="sample_mixer",
    )(x2, *slabs, hist, *consts)


def _gdn_prompt(qg, kg, vg, gb, z, extra, bsz, seq, tb):
    n = qg.shape[0]
    nblk = seq // tb
    row_spec = lambda w: pl.BlockSpec((tb, w), lambda b, j: (b * nblk + j, 0))
    consts = [extra["e6"], extra["esum"], extra["gout"]]
    return pl.pallas_call(
        functools.partial(_gdn_kernel, nblk, tb),
        grid=(bsz, nblk),
        in_specs=[row_spec(GDN_QK_DIM), row_spec(GDN_QK_DIM), row_spec(GDN_WIDTH), row_spec(LANES), row_spec(GDN_WIDTH)]
        + [_const_spec(a) for a in consts],
        out_specs=[row_spec(GDN_WIDTH), pl.BlockSpec((1, GDN_HEADS, GDN_DK, GDN_DV), lambda b, j: (b, 0, 0, 0))],
        out_shape=[jax.ShapeDtypeStruct((n, GDN_WIDTH), BF16), jax.ShapeDtypeStruct((bsz, GDN_HEADS, GDN_DK, GDN_DV), F32)],
        scratch_shapes=[pltpu.VMEM((GDN_HEADS // GROUP_HEADS, GROUP_W, GROUP_W), F32)],
        compiler_params=_params(("arbitrary", "arbitrary")), name="gdn_chunked",
    )(qg, kg, vg, gb, z, *consts)


def _attn_prompt(qm, km, vt, bsz, seq):
    n = qm.shape[0]
    t = ATT_T
    nq = seq // t
    return pl.pallas_call(
        _attn_kernel,
        grid=(bsz, MLA_HEADS // ATT_HEADS, nq),
        in_specs=[pl.BlockSpec((t, ATT_HEADS * SLAB), lambda b, hp, qi: (b * nq + qi, hp)),
                  pl.BlockSpec((seq, ATT_HEADS * SLAB), lambda b, hp, qi: (b, hp)),
                  pl.BlockSpec((ATT_HEADS * MLA_VDIM, seq), lambda b, hp, qi: (hp, b))],
        out_specs=pl.BlockSpec((t, ATT_HEADS * MLA_VDIM), lambda b, hp, qi: (b * nq + qi, hp)),
        out_shape=jax.ShapeDtypeStruct((n, MLA_HEADS * MLA_VDIM), BF16),
        compiler_params=_params(("arbitrary", "arbitrary", "arbitrary")), name="mla_prompt_attention",
    )(qm, km, vt)


def _tail(x2, og, om, p2, tailw, tm):
    n = x2.shape[0]
    row_spec = lambda w: pl.BlockSpec((tm, w), lambda i: (i, 0))
    return pl.pallas_call(
        _tail_kernel,
        grid=(n // tm,),
        in_specs=[row_spec(D_MODEL), row_spec(GDN_WIDTH), row_spec(GDN_WIDTH), row_spec(PLE_DIM)]
        + [_const_spec(a) for a in tailw],
        out_specs=row_spec(D_MODEL), out_shape=jax.ShapeDtypeStruct((n, D_MODEL), F32),
        compiler_params=_params(("arbitrary",)), name="layer_tail",
    )(x2, og, om, p2, *tailw)


def _sample_prep(qm, km, gk, wk):
    n = qm.shape[0]
    return pl.pallas_call(
        _sample_prep_kernel,
        out_shape=[jax.ShapeDtypeStruct((MLA_HEADS, n, KV_LORA), F32), jax.ShapeDtypeStruct((MLA_HEADS, n, MLA_ROPE), F32),
                   jax.ShapeDtypeStruct((MLA_HEADS, n, 1), F32)],
        compiler_params=_params(), name="sample_prep",
    )(qm, km, gk, wk)


def _paged_attention(page_table, pool_ckv, pool_kr, qabs, qr, sself, ckv_new, wkt):
    bs, n_pages = page_table.shape
    page = pool_ckv.shape[1]
    pp = PAGES_PER_STEP
    nsteps = n_pages // pp
    assert n_pages % pp == 0 and nsteps % DECODE_SLOTS == 0 and DECODE_AHEAD < DECODE_SLOTS, "page groups must fill whole buffer rings"
    per_b = lambda w: pl.BlockSpec((None, MLA_HEADS, w), lambda b, pt: (b, 0, 0))
    grid_spec = pltpu.PrefetchScalarGridSpec(
        num_scalar_prefetch=1, grid=(bs,),
        in_specs=[per_b(KV_LORA), per_b(MLA_ROPE), per_b(1),
                  pl.BlockSpec((None, 1, KV_LORA), lambda b, pt: (b, 0, 0)),
                  pl.BlockSpec(wkt.shape, lambda b, pt: (0, 0)),
                  pl.BlockSpec(memory_space=pl.ANY), pl.BlockSpec(memory_space=pl.ANY)],
        out_specs=pl.BlockSpec((None, MLA_HEADS, KV_LORA), lambda b, pt: (b, 0, 0)),
        scratch_shapes=[pltpu.VMEM((DECODE_SLOTS, pp * page, KV_LORA), F32), pltpu.VMEM((DECODE_SLOTS, MLA_ROPE, pp * page), F32),
                        pltpu.SemaphoreType.DMA((DECODE_SLOTS, 2))])
    return pl.pallas_call(
        functools.partial(_paged_kernel, nsteps, page), grid_spec=grid_spec,
        out_shape=jax.ShapeDtypeStruct((bs, MLA_HEADS, KV_LORA), F32),
        compiler_params=_params(("arbitrary",)), name="mla_paged_decode",
    )(page_table, qabs, qr, sself, ckv_new, wkt, pool_ckv, pool_kr)


def _sample_gdn(q3, k3, v3, gb3, state):
    bs = q3.shape[0]
    hk = GDN_HEADS * GDN_DK
    rb = _pick(bs, (SAMPLE_GDN_ROWS, 2, 1))
    b3 = lambda r, w: pl.BlockSpec((rb, r, w), lambda b: (b, 0, 0))
    return pl.pallas_call(
        _sample_gdn_kernel, grid=(bs // rb,),
        in_specs=[b3(1, hk), b3(1, hk), b3(GDN_HEADS, GDN_DV), b3(1, LANES), b3(hk, GDN_DV)],
        out_specs=[b3(hk, GDN_DV), b3(GDN_HEADS, GDN_DV)],
        out_shape=[jax.ShapeDtypeStruct((bs, hk, GDN_DV), F32), jax.ShapeDtypeStruct((bs, GDN_HEADS, GDN_DV), F32)],
        compiler_params=_params(("arbitrary",)), name="gdn_recurrent_step",
    )(q3, k3, v3, gb3, state)


def _sample_mix(o2, z, olat2, extra):
    n = o2.shape[0]
    return pl.pallas_call(
        _sample_mix_kernel,
        out_shape=[jax.ShapeDtypeStruct((n, GDN_WIDTH), BF16), jax.ShapeDtypeStruct((n, MLA_HEADS * MLA_VDIM), BF16)],
        compiler_params=_params(), name="sample_mix",
    )(o2, z, olat2, extra["wv_bd"], extra["esum"], extra["gout"])


def _pick(n, prefs):
    for t in prefs:
        if n % t == 0:
            return t
    return n


def kernel(x_prompt, x_sample, cache_ckv, cache_krope, state_gdn, state_conv, page_table, p_prompt, p_sample, g_attn, w_in, w_conv, gdn_a_log, gdn_dt_bias, g_gdn_out, g_q_a, w_q_b, g_q_nope, g_q_rope, g_kv_a, g_k_rope, w_kv_b, g_k_nope, w_o, g_ffn, w_ffn_gate, w_ffn_up, w_ffn_down, g_ple, w_ple_gate, w_ple_proj):
    depth = g_attn.shape[0]
    assert depth == 1 and x_sample.shape[1] == 1, "single layer, single new token per sample row"
    bp, seq, _ = x_prompt.shape
    bs = x_sample.shape[0]
    past = page_table.shape[1] * cache_ckv.shape[2]
    li = 0
    mixer, extra = _prepare_weights(g_attn[li], w_in[li], w_conv[li], gdn_a_log[li], gdn_dt_bias[li], g_gdn_out[li],
                                    g_q_a[li], w_q_b[li], g_q_nope[li], g_q_rope[li], g_kv_a[li], g_k_rope[li],
                                    w_kv_b[li], g_k_nope[li])
    row = lambda v: v.reshape(1, -1).astype(F32)
    tailw = [w_o[li].astype(BF16), row(g_ffn[li]), w_ffn_gate[li].astype(BF16), w_ffn_up[li].astype(BF16),
             w_ffn_down[li].astype(BF16), row(g_ple[li]), w_ple_gate[li].astype(BF16), w_ple_proj[li].astype(BF16)]

    n = bp * seq
    xp2 = x_prompt.reshape(n, D_MODEL)
    tm = _pick(seq, (256, 128, 64, 32, 16, 8))
    (qg, kg, vg, gb, z, qm, km, vt, ckv_p, kr_p, conv_p) = _prompt_mixer(
        xp2, _rope_slabs(jnp.arange(seq)), mixer, extra["wvt"], bp, seq, _pick(seq, (512, 256, 128)))
    tb = _pick(seq, (1024, 512, 256))
    og, gdn_p = _gdn_prompt(qg, kg, vg, gb, z, extra, bp, seq, tb)
    om = _attn_prompt(qm, km, vt, bp, seq)
    y_prompt = _tail(xp2, og, om, p_prompt[li].reshape(n, PLE_DIM), tailw, 2 * tm).reshape(bp, seq, D_MODEL)

    xs2 = x_sample.reshape(bs, D_MODEL)
    hist = jnp.moveaxis(state_conv[li], 1, 0)
    (qg_s, kg_s, vg_s, gb_s, z_s, qm_s, km_s, ckv_s, kr_s, cin_s) = _sample_mixer(
        xs2, _rope_slabs(past + jnp.arange(1)), hist, mixer)
    qabs, qr, sself = _sample_prep(qm_s, km_s, mixer["gk"], mixer["wk"])
    per_row = lambda t: jnp.swapaxes(t, 0, 1)
    pool_krt = jnp.swapaxes(cache_krope[li], 1, 2)
    olat = _paged_attention(page_table, cache_ckv[li], pool_krt, per_row(qabs), per_row(qr), per_row(sself),
                            ckv_s.reshape(bs, 1, KV_LORA), extra["wkt"])
    hk = GDN_HEADS * GDN_DK
    s_new, o_s = _sample_gdn(qg_s.reshape(bs, 1, hk), kg_s.reshape(bs, 1, hk), vg_s.reshape(bs, GDN_HEADS, GDN_DV),
                             gb_s.reshape(bs, 1, LANES), state_gdn[li].reshape(bs, hk, GDN_DV))
    og_s, om_s = _sample_mix(o_s.reshape(bs, GDN_WIDTH), z_s, olat.reshape(bs, MLA_HEADS * KV_LORA), extra)
    y_sample = _tail(xs2, og_s, om_s, p_sample[li].reshape(bs, PLE_DIM), tailw, bs).reshape(bs, 1, D_MODEL)
    conv_s = jnp.concatenate([state_conv[li][:, 1:], cin_s[:, None, :]], axis=1)

    return (y_prompt, y_sample,
            ckv_p.reshape(1, bp, seq, KV_LORA), kr_p.reshape(1, bp, seq, MLA_ROPE),
            gdn_p[None], conv_p[None],
            ckv_s.reshape(1, bs, 1, KV_LORA), kr_s.reshape(1, bs, 1, MLA_ROPE),
            s_new.reshape(1, bs, GDN_HEADS, GDN_DK, GDN_DV), conv_s[None])
```

```python
import functools
import math

import numpy as np
import jax
import jax.numpy as jnp
from jax import lax
from jax.experimental import pallas as pl
from jax.experimental.pallas import tpu as pltpu

F32 = jnp.float32
BF16 = jnp.bfloat16

D_MODEL = 1024
PLE_DIM = 256
GDN_HEADS = 8
GDN_DK = 64
GDN_DV = 64
GDN_WIDTH = GDN_HEADS * GDN_DV
GDN_QK_DIM = GDN_HEADS * GDN_DK
CONV_WIDTH = 4
CONV_DIM = 2 * GDN_QK_DIM + GDN_WIDTH
GDN_CHUNK = 64
MLA_HEADS = 8
MLA_NOPE = 64
MLA_ROPE = 32
MLA_VDIM = 64
MLA_QK = MLA_NOPE + MLA_ROPE
Q_LORA = 384
KV_LORA = 256
ROPE_THETA = 10000.0
ATTN_SCALE = MLA_QK ** -0.5
LOG2E = math.log2(math.e)
D_FF = 2816
EPS = 1e-6

LANES = 128
SLAB = 128
HALF = MLA_ROPE // 2
ROPE_LO = MLA_NOPE
ROPE_HI = MLA_NOPE + HALF
QK_SLAB_W = MLA_HEADS * SLAB

OFF_CONV = 0
OFF_GB = OFF_CONV + CONV_DIM
OFF_Z = OFF_GB + LANES
OFF_QA = OFF_Z + GDN_WIDTH
OFF_CKV = OFF_QA + Q_LORA
OFF_KR = OFF_CKV + KV_LORA
IN_PAD = OFF_KR + LANES
BETA_LANE0 = 32

VMEM_LIMIT = 56 * 1024 * 1024


def _dot(a, b):
    return jnp.dot(a, b, preferred_element_type=F32)


def _dot_nt(a, b):
    return lax.dot_general(a, b, (((1,), (1,)), ((), ())), preferred_element_type=F32)


def _dot_tn(a, b):
    return lax.dot_general(a, b, (((0,), (0,)), ((), ())), preferred_element_type=F32)


def _split2(x):
    hi = x.astype(BF16)
    lo = (x - hi.astype(F32)).astype(BF16)
    return hi, lo


def _split3(x):
    hi = x.astype(BF16)
    r1 = x - hi.astype(F32)
    mid = r1.astype(BF16)
    lo = (r1 - mid.astype(F32)).astype(BF16)
    return hi, mid, lo


def _sigmoid(x):
    return 1.0 / (1.0 + jnp.exp(-x))


def _silu(x):
    return x * _sigmoid(x)


def _softplus(x):
    return jnp.maximum(x, 0.0) + jnp.log1p(jnp.exp(-jnp.abs(x)))


def _rmsnorm(x, g):
    return x * lax.rsqrt(jnp.mean(x * x, axis=-1, keepdims=True) + EPS) * g


def _seg_rsqrt(x, ered, eexp, nseg):
    red = _dot((x * x).astype(BF16), ered)
    r = lax.rsqrt(red + EPS)
    p0, p1, p2 = _split3(r)
    lane = lax.broadcasted_iota(jnp.int32, r.shape, 1)
    piece = jnp.where(lane < nseg, p0, jnp.where(lane < 2 * nseg, p1, p2))
    return _dot(piece, eexp)


def _rope(x, c, s_up, s_dn):
    w = x.shape[-1]
    up = pltpu.roll(x, w - HALF, axis=1)
    dn = pltpu.roll(x, HALF, axis=1)
    return x * c + up * s_up + dn * s_dn


def _tile_lanes(x, n):
    return jnp.concatenate([x] * n, axis=1)


def _const_spec(arr):
    nd = arr.ndim
    return pl.BlockSpec(arr.shape, lambda *a, _nd=nd: (0,) * _nd, pipeline_mode=pl.Buffered(1))


MIXER_BLOCKS = 2
MIXER_CONSTS = ("g_attn", "w_in", "w_conv", "alog", "dtb", "ered_g", "eexp_g", "g_q_a", "w_qb", "gq", "ered_q",
                "eexp_q", "g_kv_a", "gkr", "ered_r", "eexp_r", "wk", "gk", "ered_k", "eexp_k")


def _mixer_rows(xs, c, conv_fn, ropes, q_scale):
    n = range(len(xs))
    w_in = c["w_in"]
    xn = [_rmsnorm(x, c["g_attn"][...]).astype(BF16) for x in xs]

    conv_in = [_dot(xn[i], w_in[:, OFF_CONV:OFF_GB]) for i in n]
    rest = [_dot(xn[i], w_in[:, OFF_GB:IN_PAD]) for i in n]
    part = lambda i, lo, hi: rest[i][:, lo - OFF_GB:hi - OFF_GB]
    qs = [_dot(_rmsnorm(part(i, OFF_QA, OFF_CKV), c["g_q_a"][...]).astype(BF16), c["w_qb"][...]) for i in n]
    ckv = [_rmsnorm(part(i, OFF_CKV, OFF_KR), c["g_kv_a"][...]) for i in n]
    ckv_bf = [v.astype(BF16) for v in ckv]
    kk = [_dot(ckv_bf[i], c["wk"][...]) for i in n]

    y = [_silu(v) for v in conv_fn(conv_in)]
    qk = [y[i][:, :2 * GDN_QK_DIM] for i in n]
    qk = [qk[i] * _seg_rsqrt(qk[i], c["ered_g"][...], c["eexp_g"][...], 2 * GDN_HEADS) for i in n]

    gb = []
    for i in n:
        ab = part(i, OFF_GB, OFF_Z)
        lane = lax.broadcasted_iota(jnp.int32, ab.shape, 1)
        g_log = -jnp.exp(c["alog"][...]) * _softplus(ab + c["dtb"][...])
        gb.append(jnp.where(lane < BETA_LANE0, g_log, _sigmoid(ab)))
    z = [part(i, OFF_Z, OFF_QA) for i in n]

    qs = [qs[i] * _seg_rsqrt(qs[i], c["ered_q"][...], c["eexp_q"][...], 2 * MLA_HEADS) * c["gq"][...] for i in n]
    q_mla = [_rope(qs[i], *[_tile_lanes(t, MLA_HEADS) for t in ropes[i]]) * q_scale for i in n]

    kr = [part(i, OFF_KR, IN_PAD) for i in n]
    kr = [kr[i] * _seg_rsqrt(kr[i], c["ered_r"][...], c["eexp_r"][...], 1) * c["gkr"][...] for i in n]
    kr = [_rope(kr[i], *ropes[i]) for i in n]
    kk = [kk[i] * _seg_rsqrt(kk[i], c["ered_k"][...], c["eexp_k"][...], MLA_HEADS) * c["gk"][...] for i in n]
    return [dict(conv_in=conv_in[i], q_g=qk[i][:, :GDN_QK_DIM] * (GDN_DK ** -0.5), k_g=qk[i][:, GDN_QK_DIM:],
                 v_g=y[i][:, 2 * GDN_QK_DIM:], gb=gb[i], z=z[i], q_mla=q_mla[i],
                 k_mla=kk[i] + _tile_lanes(kr[i], MLA_HEADS), ckv=ckv[i], ckv_bf=ckv_bf[i], kr=kr[i]) for i in n]


def _prompt_mixer_kernel(tiles_per_seq, tm, x_ref, rc_ref, rsu_ref, rsd_ref, *refs):
    nc = len(MIXER_CONSTS)
    c = dict(zip(MIXER_CONSTS, refs[:nc]))
    wvt_ref = refs[nc]
    (qg_ref, kg_ref, vg_ref, gb_ref, z_ref, qm_ref, km_ref, vt_ref, ckv_ref, kro_ref, cs_ref, ext_ref) = refs[nc + 1:]
    i = pl.program_id(0)
    hm = tm // MIXER_BLOCKS
    blocks = [slice(j * hm, (j + 1) * hm) for j in range(MIXER_BLOCKS)]

    @pl.when(i % tiles_per_seq == 0)
    def _():
        ext_ref[5:8, :] = jnp.zeros((3, CONV_DIM), F32)

    def conv_fn(conv_ins):
        w = c["w_conv"]
        for sl, v in zip(blocks, conv_ins):
            ext_ref[8 + sl.start:8 + sl.stop, :] = v
        return [v * w[3:4, :] + ext_ref[7 + sl.start:7 + sl.stop, :] * w[2:3, :]
                + ext_ref[6 + sl.start:6 + sl.stop, :] * w[1:2, :] + ext_ref[5 + sl.start:5 + sl.stop, :] * w[0:1, :]
                for sl, v in zip(blocks, conv_ins)]

    res = _mixer_rows([x_ref[sl, :] for sl in blocks], c, conv_fn,
                      [(rc_ref[sl, :], rsu_ref[sl, :], rsd_ref[sl, :]) for sl in blocks], ATTN_SCALE * LOG2E)
    last3 = ext_ref[tm + 5:tm + 8, :]
    ext_ref[5:8, :] = last3
    cs_ref[0] = last3
    for sl, r in zip(blocks, res):
        qg_ref[sl, :] = r["q_g"]
        kg_ref[sl, :] = r["k_g"]
        vg_ref[sl, :] = r["v_g"]
        gb_ref[sl, :] = r["gb"]
        z_ref[sl, :] = r["z"]
        qm_ref[sl, :] = r["q_mla"].astype(BF16)
        km_ref[sl, :] = r["k_mla"].astype(BF16)
        vt_ref[:, sl] = _dot_nt(wvt_ref[...], r["ckv_bf"]).astype(BF16)
        ckv_ref[sl, :] = r["ckv"]
        kro_ref[sl, :] = r["kr"][:, ROPE_LO:ROPE_LO + MLA_ROPE]


def _sample_mixer_kernel(x_ref, rc_ref, rsu_ref, rsd_ref, hist_ref, *refs):
    nc = len(MIXER_CONSTS)
    c = dict(zip(MIXER_CONSTS, refs[:nc]))
    (qg_ref, kg_ref, vg_ref, gb_ref, z_ref, qm_ref, km_ref, ckv_ref, kro_ref, cin_ref) = refs[nc:]

    def conv_fn(conv_ins):
        w = c["w_conv"]
        return [v * w[3:4, :] + hist_ref[2] * w[2:3, :] + hist_ref[1] * w[1:2, :] + hist_ref[0] * w[0:1, :] for v in conv_ins]

    (r,) = _mixer_rows([x_ref[...]], c, conv_fn, [(rc_ref[...], rsu_ref[...], rsd_ref[...])], ATTN_SCALE)
    qg_ref[...] = r["q_g"]
    kg_ref[...] = r["k_g"]
    vg_ref[...] = r["v_g"]
    gb_ref[...] = r["gb"]
    z_ref[...] = r["z"]
    qm_ref[...] = r["q_mla"]
    km_ref[...] = r["k_mla"]
    ckv_ref[...] = r["ckv"]
    kro_ref[...] = r["kr"][:, ROPE_LO:ROPE_LO + MLA_ROPE]
    cin_ref[...] = r["conv_in"]


GROUP_HEADS = 4
GROUP_W = GROUP_HEADS * GDN_DK


def _cumsum_rows(x, period):
    row = lax.broadcasted_iota(jnp.int32, x.shape, 0) % period
    s = 1
    while s < period:
        x = x + jnp.where(row >= s, pltpu.roll(x, s, axis=0), 0.0)
        s *= 2
    return x


GDN_ITER_CHUNKS = 4


def _bd(y, bd_mask):
    return jnp.where(bd_mask, jnp.concatenate([y] * GROUP_HEADS, axis=0), 0.0)


def _gdn_solve_stages(chains, masks, sols):
    bd_mask, tril_cat, strict_cat, eye_cat = masks
    c = GDN_CHUNK
    n = range(len(chains))
    bd = lambda y: _bd(y, bd_mask)
    kb = [ch["k"] * ch["b_r"] for ch in chains]
    aq = [_dot_nt(jnp.concatenate([kb[i], chains[i]["q"]], axis=0), bd(chains[i]["k"])) for i in n]
    yield
    decay = []
    for ch in chains:
        g_col = jnp.sum(jnp.where(eye_cat, ch["g_r"], 0.0), axis=0, keepdims=True)
        decay.append(jnp.exp(jnp.where(tril_cat, ch["g_r"] - g_col, -jnp.inf)))
    qk = [aq[i][c:] * decay[i] for i in n]
    p = [-jnp.where(strict_cat, aq[i][:c] * decay[i], 0.0) for i in n]
    s_inv = [jnp.where(eye_cat, 1.0, 0.0) + p[i] for i in n]
    p = [_dot(p[i], bd(p[i])) for i in n]
    yield
    for lvl in range(5):
        s_next = [s_inv[i] + _dot(p[i], bd(s_inv[i])) for i in n]
        yield
        if lvl < 4:
            p = [_dot(p[i], bd(p[i])) for i in n]
            yield
        s_inv = s_next
    eg = [jnp.exp(ch["g_r"]) for ch in chains]
    u = [_dot(s_inv[i], bd(chains[i]["v"] * chains[i]["b_r"])) for i in n]
    yield
    w = [_dot(s_inv[i], bd(kb[i] * eg[i])) for i in n]
    for i in n:
        g_r = chains[i]["g_r"]
        g_last = g_r[c - 1:c, :]
        sols.append(dict(u=u[i], wq=jnp.concatenate([w[i], chains[i]["q"] * eg[i]], axis=0), qk=qk[i],
                         kd=chains[i]["k"] * jnp.exp(g_last - g_r), e_last=jnp.exp(g_last)))


def _gdn_apply_stages(sols, states, bd_mask, box):
    c = GDN_CHUNK
    ngrp = len(states)
    n = range(ngrp)
    outs = []
    for t in range(len(sols) // ngrp):
        sl = sols[t * ngrp:(t + 1) * ngrp]
        ws = [_dot(sl[i]["wq"], states[i]) for i in n]
        yield
        v_new = [sl[i]["u"] - ws[i][:c] for i in n]
        o = [ws[i][c:] + _dot(sl[i]["qk"], _bd(v_new[i], bd_mask)) for i in n]
        upd = [_dot_tn(sl[i]["kd"], v_new[i]) for i in n]
        yield
        states = [states[i] * sl[i]["e_last"] + jnp.where(bd_mask, upd[i], 0.0) for i in n]
        outs.append(jnp.concatenate(o, axis=1))
    box["o"] = jnp.concatenate(outs, axis=0)
    box["states"] = states


def _interleave(stage_generators):
    live = list(stage_generators)
    while live:
        for g in list(live):
            try:
                next(g)
            except StopIteration:
                live.remove(g)


def _gdn_kernel(nblk, tb, q_ref, k_ref, v_ref, gb_ref, z_ref, e6_ref, esum_ref, gout_ref, o_ref, sfin_ref, s_ref):
    j = pl.program_id(1)

    @pl.when(j == 0)
    def _():
        s_ref[...] = jnp.zeros(s_ref.shape, F32)

    c = GDN_CHUNK
    ngrp = GDN_HEADS // GROUP_HEADS
    rows_it = GDN_ITER_CHUNKS * c
    r_bd = lax.broadcasted_iota(jnp.int32, (GROUP_W, GROUP_W), 0) // c
    c_bd = lax.broadcasted_iota(jnp.int32, (GROUP_W, GROUP_W), 1) // c
    bd_mask = r_bd == c_bd
    ri = lax.broadcasted_iota(jnp.int32, (c, GROUP_W), 0)
    cj = lax.broadcasted_iota(jnp.int32, (c, GROUP_W), 1) % c
    masks = (bd_mask, ri >= cj, ri > cj, ri == cj)
    lane = lax.broadcasted_iota(jnp.int32, (rows_it, LANES), 1)
    copy_id = (lane % BETA_LANE0) // GDN_HEADS

    def chains_of(rows):
        gb = gb_ref[rows, :]
        sc = jnp.where(lane < BETA_LANE0, _cumsum_rows(gb, c), gb)
        p0, p1, p2 = _split3(sc)
        piece = jnp.where(copy_id == 0, p0, jnp.where(copy_id == 1, p1, p2))
        ex = _dot(piece, e6_ref[...])
        q = q_ref[rows, :]
        k = k_ref[rows, :]
        v = v_ref[rows, :]
        chains = []
        for t in range(GDN_ITER_CHUNKS):
            rs = slice(t * c, (t + 1) * c)
            for grp in range(ngrp):
                sl = slice(grp * GROUP_W, (grp + 1) * GROUP_W)
                sb = slice(GDN_WIDTH + grp * GROUP_W, GDN_WIDTH + (grp + 1) * GROUP_W)
                chains.append(dict(q=q[rs, sl], k=k[rs, sl], v=v[rs, sl], g_r=ex[rs, sl], b_r=ex[rs, sb]))
        return chains

    def finish(o, rows):
        hi, lo = _split2(o * o)
        ms = _dot(hi, esum_ref[...]) + _dot(lo, esum_ref[...])
        o_ref[rows, :] = (o * lax.rsqrt(ms + EPS) * gout_ref[...] * _silu(z_ref[rows, :])).astype(o_ref.dtype)

    states = [s_ref[grp] for grp in range(ngrp)]
    pending = None
    for gi in range(tb // rows_it):
        rows = slice(gi * rows_it, (gi + 1) * rows_it)
        sols, box = [], {}
        stages = [_gdn_solve_stages(chains_of(rows), masks, sols)]
        if pending is not None:
            stages.append(_gdn_apply_stages(pending[0], states, bd_mask, box))
        _interleave(stages)
        if pending is not None:
            states = box["states"]
            finish(box["o"], pending[1])
        pending = (sols, rows)
    box = {}
    _interleave([_gdn_apply_stages(pending[0], states, bd_mask, box)])
    finish(box["o"], pending[1])
    for grp in range(ngrp):
        s_ref[grp] = box["states"][grp]

    @pl.when(j == nblk - 1)
    def _():
        for grp in range(ngrp):
            for h in range(GROUP_HEADS):
                sfin_ref[0, grp * GROUP_HEADS + h] = s_ref[grp, h * c:(h + 1) * c, h * c:(h + 1) * c]


ATT_T = 256
ATT_HEADS = 8
ATT_SUM_ROWS = 16
ATT_WIDE = 2


def _attn_kernel(q_ref, k_ref, vt_ref, o_ref):
    qi = pl.program_id(2)
    t = ATT_T
    qs = [q_ref[:, hh * SLAB:(hh + 1) * SLAB] for hh in range(ATT_HEADS)]

    def score_tile(k0, tk):
        return tuple(_dot_nt(k_ref[pl.ds(k0, tk), hh * SLAB:(hh + 1) * SLAB], qs[hh]) for hh in range(ATT_HEADS))

    def consume(k0, tk, carry, masked):
        scores = score_tile(k0, tk)
        stats = []
        for hh in range(ATT_HEADS):
            m, _ = carry[hh]
            s = scores[hh]
            if masked:
                kp = lax.broadcasted_iota(jnp.int32, (tk, t), 0)
                qp = lax.broadcasted_iota(jnp.int32, (tk, t), 1)
                s = jnp.where(qp >= kp, s, -jnp.inf)
            m_new = jnp.maximum(m, jnp.max(s, axis=0, keepdims=True))
            stats.append((m_new, jnp.exp2(m - m_new), jnp.exp2(s - m_new).astype(BF16)))
        out = []
        for hh in range(ATT_HEADS):
            m_new, alpha, p = stats[hh]
            vt = vt_ref[hh * MLA_VDIM:(hh + 1) * MLA_VDIM, pl.ds(k0, tk)]
            ones = jnp.ones((ATT_SUM_ROWS, tk), BF16)
            out.append((m_new, carry[hh][1] * alpha + _dot(jnp.concatenate([vt, ones], axis=0), p)))
        return tuple(out)

    init = tuple((jnp.full((1, t), -jnp.inf, F32), jnp.zeros((MLA_VDIM + ATT_SUM_ROWS, t), F32)) for _ in range(ATT_HEADS))
    wide = ATT_WIDE * t
    nwide = qi // ATT_WIDE
    carry = lax.fori_loop(0, nwide, lambda j, c: consume(pl.multiple_of(j * wide, wide), wide, c, False), init)
    carry = lax.fori_loop(nwide * ATT_WIDE, qi, lambda j, c: consume(pl.multiple_of(j * t, t), t, c, False), carry)
    carry = consume(pl.multiple_of(qi * t, t), t, carry, True)
    outs = [acc[:MLA_VDIM] / acc[MLA_VDIM:MLA_VDIM + 1] for (_, acc) in carry]
    o_ref[...] = jnp.concatenate(outs, axis=0).T.astype(o_ref.dtype)


TAIL_BLOCKS = 2

def _tail_kernel(x_ref, og_ref, om_ref, p_ref, wo_ref, gffn_ref, wg_ref, wu_ref, wd_ref, gple_ref, wpg_ref, wpp_ref, y_ref):
    rows = x_ref.shape[0]
    nblk = TAIL_BLOCKS if rows % (TAIL_BLOCKS * LANES) == 0 else 1
    blocks = [slice(j * rows // nblk, (j + 1) * rows // nblk) for j in range(nblk)]
    n = range(nblk)
    ple = [_dot(p_ref[sl, :].astype(BF16), wpp_ref[...]) for sl in blocks]
    h = [x_ref[sl, :] + _dot(og_ref[sl, :], wo_ref[:GDN_WIDTH, :]) + _dot(om_ref[sl, :], wo_ref[GDN_WIDTH:, :])
         for sl in blocks]
    u = [_rmsnorm(h[i], gffn_ref[...]).astype(BF16) for i in n]
    act = [(_silu(_dot(u[i], wg_ref[...])) * _dot(u[i], wu_ref[...])).astype(BF16) for i in n]
    acc = [h[i] + _dot(act[i], wd_ref[...]) for i in n]
    gate = [_sigmoid(_dot(_rmsnorm(acc[i], gple_ref[...]).astype(BF16), wpg_ref[...])) for i in n]
    for i, sl in enumerate(blocks):
        y_ref[sl, :] = acc[i] + ple[i] * gate[i]


PAGES_PER_STEP = 32
PAGES_PER_SUB = 32
DECODE_SLOTS = 4
DECODE_AHEAD = 2


def _sample_prep_kernel(qm_ref, km_ref, gk_ref, wk_ref, qabs_ref, qr_ref, sself_ref):
    qm = qm_ref[...]
    prod = qm * km_ref[...]
    qg = (qm * gk_ref[...]).astype(BF16)
    for h in range(MLA_HEADS):
        sl = slice(h * SLAB, (h + 1) * SLAB)
        qabs_ref[h] = _dot_nt(qg[:, sl], wk_ref[:, sl])
        qr_ref[h] = qm[:, h * SLAB + ROPE_LO:h * SLAB + ROPE_LO + MLA_ROPE]
        sself_ref[h] = jnp.sum(prod[:, sl], axis=1, keepdims=True)


def _paged_kernel(nsteps, page, pt_ref, qabs_ref, qr_ref, sself_ref, ckvn_ref, wkt_ref, ckv_hbm, krt_hbm, o_ref,
                  cbuf, kbuf, sems):
    pp = PAGES_PER_STEP
    b = pl.program_id(0)
    nb = pl.num_programs(0)

    def copies(row, g):
        slot = g % DECODE_SLOTS
        out = []
        for j in range(pp):
            pg = pt_ref[row, g * pp + j]
            out.append(pltpu.make_async_copy(ckv_hbm.at[pg], cbuf.at[slot, pl.ds(j * page, page), :], sems.at[slot, 0]))
            out.append(pltpu.make_async_copy(krt_hbm.at[pg], kbuf.at[slot, :, pl.ds(j * page, page)], sems.at[slot, 1]))
        return out

    @pl.when(b == 0)
    def _():
        for g in range(DECODE_AHEAD):
            for i, cp in enumerate(copies(0, g)):
                cp.start(priority=i % 2)

    qa = qabs_ref[...].astype(BF16)
    lhs = jnp.concatenate([wkt_ref[...], qa, jnp.zeros_like(qa)], axis=0)
    nk = MLA_HEADS * MLA_NOPE
    qr = qr_ref[...].astype(BF16)
    m = sself_ref[...]
    l = jnp.ones(m.shape, F32)
    acc = jnp.broadcast_to(ckvn_ref[...], (MLA_HEADS, KV_LORA))
    sub_tok = PAGES_PER_SUB * page

    def update(s, cs, m, l, acc):
        m_new = jnp.maximum(m, jnp.max(s, axis=1, keepdims=True))
        alpha = jnp.exp(m - m_new)
        p = jnp.exp(s - m_new)
        l = l * alpha + jnp.sum(p, axis=1, keepdims=True)
        return m_new, l, acc * alpha + _dot(p.astype(BF16), jnp.concatenate(cs, axis=0))

    pending = None
    for g in range(nsteps):
        slot = g % DECODE_SLOTS
        for cp in copies(b, g):
            cp.wait()
        ng = g + DECODE_AHEAD
        nrow, ng = (b, ng) if ng < nsteps else (jnp.minimum(b + 1, nb - 1), ng - nsteps)
        for i, cp in enumerate(copies(nrow, ng)):
            cp.start(priority=i % 2)
        cs, ss_list = [], []
        for i in range(pp // PAGES_PER_SUB):
            c = cbuf[slot, i * sub_tok:(i + 1) * sub_tok, :].astype(BF16)
            krt = kbuf[slot, :, i * sub_tok:(i + 1) * sub_tok].astype(BF16)
            kvq = _dot_nt(lhs, c)
            kv = kvq[:nk]
            ss = jnp.sum((kv * kv).reshape(MLA_HEADS, MLA_NOPE, kv.shape[1]), axis=1)
            ss_list.append(kvq[nk:nk + MLA_HEADS] * lax.rsqrt(ss * (1.0 / MLA_NOPE) + EPS) + _dot(qr, krt))
            cs.append(c)
        if pending is not None:
            m, l, acc = update(*pending, m, l, acc)
        pending = (jnp.concatenate(ss_list, axis=1), cs)
    m, l, acc = update(*pending, m, l, acc)
    o_ref[...] = acc / l

    @pl.when(b == nb - 1)
    def _():
        for g in range(DECODE_AHEAD):
            for cp in copies(b, g):
                cp.wait()


SAMPLE_GDN_ROWS = 4


def _sample_gdn_kernel(q_ref, k_ref, v_ref, gb_ref, s_ref, snew_ref, o_ref):
    hk = GDN_HEADS * GDN_DK
    hrow = lax.broadcasted_iota(jnp.int32, (GDN_HEADS, hk), 0)
    hlane = lax.broadcasted_iota(jnp.int32, (GDN_HEADS, hk), 1) // GDN_DK
    hm = hrow == hlane
    r8 = lax.broadcasted_iota(jnp.int32, (GDN_HEADS, LANES), 0)
    l8 = lax.broadcasted_iota(jnp.int32, (GDN_HEADS, LANES), 1)
    ones = jnp.where(hm, 1.0, 0.0).astype(BF16)
    rows = range(q_ref.shape[0])
    kmask = [jnp.where(hm, jnp.broadcast_to(k_ref[r], (GDN_HEADS, hk)), 0.0) for r in rows]
    qmask = [jnp.where(hm, jnp.broadcast_to(q_ref[r], (GDN_HEADS, hk)), 0.0) for r in rows]
    gbb = [jnp.broadcast_to(gb_ref[r], (GDN_HEADS, LANES)) for r in rows]
    g_col = [jnp.sum(jnp.where(l8 == r8, gbb[r], 0.0), axis=1, keepdims=True) for r in rows]
    b_col = [jnp.sum(jnp.where(l8 == r8 + BETA_LANE0, gbb[r], 0.0), axis=1, keepdims=True) for r in rows]

    def expand_rows(col):
        e0, e1, e2 = _split3(jnp.broadcast_to(col, (GDN_HEADS, GDN_DV)))
        return _dot_tn(jnp.concatenate([ones, ones, ones], axis=0), jnp.concatenate([e0, e1, e2], axis=0))

    def head_dot(xmask, mat):
        xh, xl = _split2(xmask)
        mh, ml = _split2(mat)
        return _dot(jnp.concatenate([xh, xh, xl], axis=1), jnp.concatenate([mh, ml, mh], axis=0))

    def outer(kmask_r, delta_r):
        kh, kl = _split2(kmask_r)
        dh, dl = _split2(delta_r)
        return _dot_tn(jnp.concatenate([kh, kh, kl], axis=0), jnp.concatenate([dh, dl, dh], axis=0))

    decay = [expand_rows(jnp.exp(g_col[r])) for r in rows]
    state = [s_ref[r] * decay[r] for r in rows]
    ks = [head_dot(kmask[r], state[r]) for r in rows]
    delta = [(v_ref[r] - ks[r]) * b_col[r] for r in rows]
    upd = [outer(kmask[r], delta[r]) for r in rows]
    state = [state[r] + upd[r] for r in rows]
    outs = [head_dot(qmask[r], state[r]) for r in rows]
    for r in rows:
        snew_ref[r] = state[r]
        o_ref[r] = outs[r]


def _sample_mix_kernel(o_ref, z_ref, olat_ref, wvbd_ref, esum_ref, gout_ref, og_ref, om_ref):
    o = o_ref[...]
    hi, lo = _split2(o * o)
    ms = _dot(hi, esum_ref[...]) + _dot(lo, esum_ref[...])
    og_ref[...] = (o * lax.rsqrt(ms + EPS) * gout_ref[...] * _silu(z_ref[...])).astype(og_ref.dtype)
    om_ref[...] = _dot(olat_ref[...].astype(BF16), wvbd_ref[...]).astype(om_ref.dtype)


def _seg_tables(width, segs, mean):
    n = len(segs)
    ered = np.zeros((width, LANES), np.float32)
    eexp = np.zeros((LANES, width), np.float32)
    for copy in range(3):
        for s, (a, b) in enumerate(segs):
            ered[a:b, copy * n + s] = 1.0 / (b - a) if mean else 1.0
            eexp[copy * n + s, a:b] = 1.0
    return jnp.asarray(ered, BF16), jnp.asarray(eexp, BF16)


def _rope_slabs(pos):
    inv_freq = ROPE_THETA ** (-jnp.arange(HALF, dtype=F32) / HALF)
    ang = pos.astype(F32)[:, None] * inv_freq[None, :]
    cos, sin = jnp.cos(ang), jnp.sin(ang)
    n = pos.shape[0]
    zeros = lambda w: jnp.zeros((n, w), F32)
    rc = jnp.concatenate([jnp.ones((n, MLA_NOPE), F32), cos, cos, zeros(SLAB - MLA_QK)], axis=1)
    rsu = jnp.concatenate([zeros(MLA_NOPE), -sin, zeros(SLAB - ROPE_HI)], axis=1)
    rsd = jnp.concatenate([zeros(ROPE_HI), sin, zeros(SLAB - MLA_QK)], axis=1)
    return rc, rsu, rsd


def _prepare_weights(g_attn, w_in, w_conv, gdn_a_log, gdn_dt_bias, g_gdn_out, g_q_a, w_q_b, g_q_nope, g_q_rope,
                     g_kv_a, g_k_rope, w_kv_b, g_k_nope):
    row = lambda v: v.reshape(1, -1).astype(F32)
    a = w_in[:, CONV_DIM:CONV_DIM + GDN_HEADS]
    b = w_in[:, CONV_DIM + GDN_HEADS:CONV_DIM + 2 * GDN_HEADS]
    o_z = CONV_DIM + 2 * GDN_HEADS
    o_qa = o_z + GDN_WIDTH
    o_kv = o_qa + Q_LORA
    zc = lambda w: jnp.zeros((D_MODEL, w), F32)
    gb_slab = jnp.concatenate([a, a, a, zc(BETA_LANE0 - 3 * GDN_HEADS), b, b, b, zc(LANES - BETA_LANE0 - 3 * GDN_HEADS)], axis=1)
    kr_slab = jnp.concatenate([zc(ROPE_LO), w_in[:, o_kv + KV_LORA:], zc(SLAB - MLA_QK)], axis=1)
    w_in_p = jnp.concatenate([w_in[:, :CONV_DIM], gb_slab, w_in[:, o_z:o_qa], w_in[:, o_qa:o_kv],
                              w_in[:, o_kv:o_kv + KV_LORA], kr_slab], axis=1).astype(BF16)

    def scalar_slab(v):
        z8 = jnp.zeros((BETA_LANE0 - 3 * GDN_HEADS,), F32)
        return jnp.concatenate([v, v, v, z8, jnp.zeros((LANES - BETA_LANE0,), F32)]).reshape(1, LANES)

    wq = w_q_b.reshape(Q_LORA, MLA_HEADS, MLA_QK)
    w_qb = jnp.concatenate([wq, jnp.zeros((Q_LORA, MLA_HEADS, SLAB - MLA_QK), F32)], axis=2).reshape(Q_LORA, QK_SLAB_W).astype(BF16)
    gq = jnp.tile(jnp.concatenate([g_q_nope, g_q_rope, jnp.zeros((SLAB - MLA_QK,), F32)]), MLA_HEADS).reshape(1, QK_SLAB_W)
    wkv = w_kv_b.reshape(KV_LORA, MLA_HEADS, MLA_NOPE + MLA_VDIM)
    wk_part, wv_part = wkv[:, :, :MLA_NOPE], wkv[:, :, MLA_NOPE:]
    wk = jnp.concatenate([wk_part, jnp.zeros((KV_LORA, MLA_HEADS, SLAB - MLA_NOPE), F32)], axis=2).reshape(KV_LORA, QK_SLAB_W).astype(BF16)
    gk = jnp.tile(jnp.concatenate([g_k_nope, jnp.zeros((SLAB - MLA_NOPE,), F32)]), MLA_HEADS).reshape(1, QK_SLAB_W)
    gkr = jnp.concatenate([jnp.zeros((ROPE_LO,), F32), g_k_rope, jnp.zeros((SLAB - MLA_QK,), F32)]).reshape(1, SLAB)
    wvt = wv_part.reshape(KV_LORA, MLA_HEADS * MLA_VDIM).T.astype(BF16)
    wkt = wk_part.reshape(KV_LORA, MLA_HEADS * MLA_NOPE).T.astype(BF16)
    wv_bd = jnp.zeros((MLA_HEADS, KV_LORA, MLA_HEADS, MLA_VDIM), F32)
    wv_bd = wv_bd.at[jnp.arange(MLA_HEADS), :, jnp.arange(MLA_HEADS), :].set(jnp.moveaxis(wv_part, 1, 0))
    wv_bd = wv_bd.reshape(MLA_HEADS * KV_LORA, MLA_HEADS * MLA_VDIM).astype(BF16)

    ered_g, eexp_g = _seg_tables(2 * GDN_QK_DIM, [(h * GDN_DK, (h + 1) * GDN_DK) for h in range(2 * GDN_HEADS)], False)
    q_segs = ([(h * SLAB, h * SLAB + MLA_NOPE) for h in range(MLA_HEADS)]
              + [(h * SLAB + ROPE_LO, h * SLAB + MLA_QK) for h in range(MLA_HEADS)])
    ered_q, eexp_q = _seg_tables(QK_SLAB_W, q_segs, True)
    ered_k, eexp_k = _seg_tables(QK_SLAB_W, q_segs[:MLA_HEADS], True)
    ered_r, eexp_r = _seg_tables(SLAB, [(ROPE_LO, MLA_QK)], True)
    e6 = np.zeros((LANES, 2 * GDN_WIDTH), np.float32)
    for copy in range(3):
        for h in range(GDN_HEADS):
            e6[copy * GDN_HEADS + h, h * GDN_DV:(h + 1) * GDN_DV] = 1.0
            e6[BETA_LANE0 + copy * GDN_HEADS + h, GDN_WIDTH + h * GDN_DV:GDN_WIDTH + (h + 1) * GDN_DV] = 1.0
    esum = np.kron(np.eye(GDN_HEADS, dtype=np.float32), np.full((GDN_DV, GDN_DV), 1.0 / GDN_DV, np.float32))

    mixer = dict(g_attn=row(g_attn), w_in=w_in_p, w_conv=w_conv.astype(F32), alog=scalar_slab(gdn_a_log),
                 dtb=scalar_slab(gdn_dt_bias), ered_g=ered_g, eexp_g=eexp_g, g_q_a=row(g_q_a), w_qb=w_qb, gq=gq,
                 ered_q=ered_q, eexp_q=eexp_q, g_kv_a=row(g_kv_a), gkr=gkr, ered_r=ered_r, eexp_r=eexp_r, wk=wk, gk=gk,
                 ered_k=ered_k, eexp_k=eexp_k)
    extra = dict(wvt=wvt, wkt=wkt, wv_bd=wv_bd, e6=jnp.asarray(e6, BF16), esum=jnp.asarray(esum, BF16),
                 gout=jnp.tile(g_gdn_out, GDN_HEADS).reshape(1, GDN_WIDTH).astype(F32))
    return mixer, extra


def _params(sem=None):
    return pltpu.CompilerParams(dimension_semantics=sem, vmem_limit_bytes=VMEM_LIMIT)


def _prompt_mixer(x2, slabs, mixer, wvt, bsz, seq, tm):
    n = x2.shape[0]
    tps = seq // tm
    consts = [mixer[k] for k in MIXER_CONSTS] + [wvt]
    row_spec = lambda w: pl.BlockSpec((tm, w), lambda i: (i, 0))
    rope_spec = pl.BlockSpec((tm, SLAB), lambda i: (i % tps, 0))
    out_shapes = [
        jax.ShapeDtypeStruct((n, GDN_QK_DIM), F32), jax.ShapeDtypeStruct((n, GDN_QK_DIM), F32),
        jax.ShapeDtypeStruct((n, GDN_WIDTH), F32), jax.ShapeDtypeStruct((n, LANES), F32),
        jax.ShapeDtypeStruct((n, GDN_WIDTH), F32), jax.ShapeDtypeStruct((n, QK_SLAB_W), BF16),
        jax.ShapeDtypeStruct((n, QK_SLAB_W), BF16), jax.ShapeDtypeStruct((MLA_HEADS * MLA_VDIM, n), BF16),
        jax.ShapeDtypeStruct((n, KV_LORA), F32), jax.ShapeDtypeStruct((n, MLA_ROPE), F32),
        jax.ShapeDtypeStruct((bsz, CONV_WIDTH - 1, CONV_DIM), F32)]
    out_specs = [row_spec(GDN_QK_DIM), row_spec(GDN_QK_DIM), row_spec(GDN_WIDTH), row_spec(LANES), row_spec(GDN_WIDTH),
                 row_spec(QK_SLAB_W), row_spec(QK_SLAB_W), pl.BlockSpec((MLA_HEADS * MLA_VDIM, tm), lambda i: (0, i)),
                 row_spec(KV_LORA), row_spec(MLA_ROPE),
                 pl.BlockSpec((1, CONV_WIDTH - 1, CONV_DIM), lambda i: (i // tps, 0, 0))]
    return pl.pallas_call(
        functools.partial(_prompt_mixer_kernel, tps, tm),
        grid=(n // tm,),
        in_specs=[row_spec(D_MODEL), rope_spec, rope_spec, rope_spec] + [_const_spec(a) for a in consts],
        out_specs=out_specs, out_shape=out_shapes,
        scratch_shapes=[pltpu.VMEM((tm + 8, CONV_DIM), F32)],
        compiler_params=_params(("arbitrary",)), name="prompt_mixer",
    )(x2, *slabs, *consts)


def _sample_mixer(x2, slabs, hist, mixer):
    n = x2.shape[0]
    consts = [mixer[k] for k in MIXER_CONSTS]
    sd = lambda w: jax.ShapeDtypeStruct((n, w), F32)
    return pl.pallas_call(
        _sample_mixer_kernel,
        out_shape=[sd(GDN_QK_DIM), sd(GDN_QK_DIM), sd(GDN_WIDTH), sd(LANES), sd(GDN_WIDTH), sd(QK_SLAB_W), sd(QK_SLAB_W),
                   sd(KV_LORA), sd(MLA_ROPE), sd(CONV_DIM)],
        compiler_params=_params(), name="sample_mixer",
    )(x2, *slabs, hist, *consts)


def _gdn_prompt(qg, kg, vg, gb, z, extra, bsz, seq, tb):
    n = qg.shape[0]
    nblk = seq // tb
    row_spec = lambda w: pl.BlockSpec((tb, w), lambda b, j: (b * nblk + j, 0))
    consts = [extra["e6"], extra["esum"], extra["gout"]]
    return pl.pallas_call(
        functools.partial(_gdn_kernel, nblk, tb),
        grid=(bsz, nblk),
        in_specs=[row_spec(GDN_QK_DIM), row_spec(GDN_QK_DIM), row_spec(GDN_WIDTH), row_spec(LANES), row_spec(GDN_WIDTH)]
        + [_const_spec(a) for a in consts],
        out_specs=[row_spec(GDN_WIDTH), pl.BlockSpec((1, GDN_HEADS, GDN_DK, GDN_DV), lambda b, j: (b, 0, 0, 0))],
        out_shape=[jax.ShapeDtypeStruct((n, GDN_WIDTH), BF16), jax.ShapeDtypeStruct((bsz, GDN_HEADS, GDN_DK, GDN_DV), F32)],
        scratch_shapes=[pltpu.VMEM((GDN_HEADS // GROUP_HEADS, GROUP_W, GROUP_W), F32)],
        compiler_params=_params(("arbitrary", "arbitrary")), name="gdn_chunked",
    )(qg, kg, vg, gb, z, *consts)


def _attn_prompt(qm, km, vt, bsz, seq):
    n = qm.shape[0]
    t = ATT_T
    nq = seq // t
    return pl.pallas_call(
        _attn_kernel,
        grid=(bsz, MLA_HEADS // ATT_HEADS, nq),
        in_specs=[pl.BlockSpec((t, ATT_HEADS * SLAB), lambda b, hp, qi: (b * nq + qi, hp)),
                  pl.BlockSpec((seq, ATT_HEADS * SLAB), lambda b, hp, qi: (b, hp)),
                  pl.BlockSpec((ATT_HEADS * MLA_VDIM, seq), lambda b, hp, qi: (hp, b))],
        out_specs=pl.BlockSpec((t, ATT_HEADS * MLA_VDIM), lambda b, hp, qi: (b * nq + qi, hp)),
        out_shape=jax.ShapeDtypeStruct((n, MLA_HEADS * MLA_VDIM), BF16),
        compiler_params=_params(("arbitrary", "arbitrary", "arbitrary")), name="mla_prompt_attention",
    )(qm, km, vt)


def _tail(x2, og, om, p2, tailw, tm):
    n = x2.shape[0]
    row_spec = lambda w: pl.BlockSpec((tm, w), lambda i: (i, 0))
    return pl.pallas_call(
        _tail_kernel,
        grid=(n // tm,),
        in_specs=[row_spec(D_MODEL), row_spec(GDN_WIDTH), row_spec(GDN_WIDTH), row_spec(PLE_DIM)]
        + [_const_spec(a) for a in tailw],
        out_specs=row_spec(D_MODEL), out_shape=jax.ShapeDtypeStruct((n, D_MODEL), F32),
        compiler_params=_params(("arbitrary",)), name="layer_tail",
    )(x2, og, om, p2, *tailw)


def _sample_prep(qm, km, gk, wk):
    n = qm.shape[0]
    return pl.pallas_call(
        _sample_prep_kernel,
        out_shape=[jax.ShapeDtypeStruct((MLA_HEADS, n, KV_LORA), F32), jax.ShapeDtypeStruct((MLA_HEADS, n, MLA_ROPE), F32),
                   jax.ShapeDtypeStruct((MLA_HEADS, n, 1), F32)],
        compiler_params=_params(), name="sample_prep",
    )(qm, km, gk, wk)


def _paged_attention(page_table, pool_ckv, pool_kr, qabs, qr, sself, ckv_new, wkt):
    bs, n_pages = page_table.shape
    page = pool_ckv.shape[1]
    pp = PAGES_PER_STEP
    nsteps = n_pages // pp
    assert n_pages % pp == 0 and nsteps % DECODE_SLOTS == 0 and DECODE_AHEAD < DECODE_SLOTS, "page groups must fill whole buffer rings"
    per_b = lambda w: pl.BlockSpec((None, MLA_HEADS, w), lambda b, pt: (b, 0, 0))
    grid_spec = pltpu.PrefetchScalarGridSpec(
        num_scalar_prefetch=1, grid=(bs,),
        in_specs=[per_b(KV_LORA), per_b(MLA_ROPE), per_b(1),
                  pl.BlockSpec((None, 1, KV_LORA), lambda b, pt: (b, 0, 0)),
                  pl.BlockSpec(wkt.shape, lambda b, pt: (0, 0)),
                  pl.BlockSpec(memory_space=pl.ANY), pl.BlockSpec(memory_space=pl.ANY)],
        out_specs=pl.BlockSpec((None, MLA_HEADS, KV_LORA), lambda b, pt: (b, 0, 0)),
        scratch_shapes=[pltpu.VMEM((DECODE_SLOTS, pp * page, KV_LORA), F32), pltpu.VMEM((DECODE_SLOTS, MLA_ROPE, pp * page), F32),
                        pltpu.SemaphoreType.DMA((DECODE_SLOTS, 2))])
    return pl.pallas_call(
        functools.partial(_paged_kernel, nsteps, page), grid_spec=grid_spec,
        out_shape=jax.ShapeDtypeStruct((bs, MLA_HEADS, KV_LORA), F32),
        compiler_params=_params(("arbitrary",)), name="mla_paged_decode",
    )(page_table, qabs, qr, sself, ckv_new, wkt, pool_ckv, pool_kr)


def _sample_gdn(q3, k3, v3, gb3, state):
    bs = q3.shape[0]
    hk = GDN_HEADS * GDN_DK
    rb = _pick(bs, (SAMPLE_GDN_ROWS, 2, 1))
    b3 = lambda r, w: pl.BlockSpec((rb, r, w), lambda b: (b, 0, 0))
    return pl.pallas_call(
        _sample_gdn_kernel, grid=(bs // rb,),
        in_specs=[b3(1, hk), b3(1, hk), b3(GDN_HEADS, GDN_DV), b3(1, LANES), b3(hk, GDN_DV)],
        out_specs=[b3(hk, GDN_DV), b3(GDN_HEADS, GDN_DV)],
        out_shape=[jax.ShapeDtypeStruct((bs, hk, GDN_DV), F32), jax.ShapeDtypeStruct((bs, GDN_HEADS, GDN_DV), F32)],
        compiler_params=_params(("arbitrary",)), name="gdn_recurrent_step",
    )(q3, k3, v3, gb3, state)


def _sample_mix(o2, z, olat2, extra):
    n = o2.shape[0]
    return pl.pallas_call(
        _sample_mix_kernel,
        out_shape=[jax.ShapeDtypeStruct((n, GDN_WIDTH), BF16), jax.ShapeDtypeStruct((n, MLA_HEADS * MLA_VDIM), BF16)],
        compiler_params=_params(), name="sample_mix",
    )(o2, z, olat2, extra["wv_bd"], extra["esum"], extra["gout"])


def _pick(n, prefs):
    for t in prefs:
        if n % t == 0:
            return t
    return n


def kernel(x_prompt, x_sample, cache_ckv, cache_krope, state_gdn, state_conv, page_table, p_prompt, p_sample, g_attn, w_in, w_conv, gdn_a_log, gdn_dt_bias, g_gdn_out, g_q_a, w_q_b, g_q_nope, g_q_rope, g_kv_a, g_k_rope, w_kv_b, g_k_nope, w_o, g_ffn, w_ffn_gate, w_ffn_up, w_ffn_down, g_ple, w_ple_gate, w_ple_proj):
    depth = g_attn.shape[0]
    assert depth == 1 and x_sample.shape[1] == 1, "single layer, single new token per sample row"
    bp, seq, _ = x_prompt.shape
    bs = x_sample.shape[0]
    past = page_table.shape[1] * cache_ckv.shape[2]
    li = 0
    mixer, extra = _prepare_weights(g_attn[li], w_in[li], w_conv[li], gdn_a_log[li], gdn_dt_bias[li], g_gdn_out[li],
                                    g_q_a[li], w_q_b[li], g_q_nope[li], g_q_rope[li], g_kv_a[li], g_k_rope[li],
                                    w_kv_b[li], g_k_nope[li])
    row = lambda v: v.reshape(1, -1).astype(F32)
    tailw = [w_o[li].astype(BF16), row(g_ffn[li]), w_ffn_gate[li].astype(BF16), w_ffn_up[li].astype(BF16),
             w_ffn_down[li].astype(BF16), row(g_ple[li]), w_ple_gate[li].astype(BF16), w_ple_proj[li].astype(BF16)]

    n = bp * seq
    xp2 = x_prompt.reshape(n, D_MODEL)
    tm = _pick(seq, (256, 128, 64, 32, 16, 8))
    (qg, kg, vg, gb, z, qm, km, vt, ckv_p, kr_p, conv_p) = _prompt_mixer(
        xp2, _rope_slabs(jnp.arange(seq)), mixer, extra["wvt"], bp, seq, _pick(seq, (512, 256, 128)))
    tb = _pick(seq, (1024, 512, 256))
    og, gdn_p = _gdn_prompt(qg, kg, vg, gb, z, extra, bp, seq, tb)
    om = _attn_prompt(qm, km, vt, bp, seq)
    y_prompt = _tail(xp2, og, om, p_prompt[li].reshape(n, PLE_DIM), tailw, 2 * tm).reshape(bp, seq, D_MODEL)

    xs2 = x_sample.reshape(bs, D_MODEL)
    hist = jnp.moveaxis(state_conv[li], 1, 0)
    (qg_s, kg_s, vg_s, gb_s, z_s, qm_s, km_s, ckv_s, kr_s, cin_s) = _sample_mixer(
        xs2, _rope_slabs(past + jnp.arange(1)), hist, mixer)
    qabs, qr, sself = _sample_prep(qm_s, km_s, mixer["gk"], mixer["wk"])
    per_row = lambda t: jnp.swapaxes(t, 0, 1)
    pool_krt = jnp.swapaxes(cache_krope[li], 1, 2)
    olat = _paged_attention(page_table, cache_ckv[li], pool_krt, per_row(qabs), per_row(qr), per_row(sself),
                            ckv_s.reshape(bs, 1, KV_LORA), extra["wkt"])
    hk = GDN_HEADS * GDN_DK
    s_new, o_s = _sample_gdn(qg_s.reshape(bs, 1, hk), kg_s.reshape(bs, 1, hk), vg_s.reshape(bs, GDN_HEADS, GDN_DV),
                             gb_s.reshape(bs, 1, LANES), state_gdn[li].reshape(bs, hk, GDN_DV))
    og_s, om_s = _sample_mix(o_s.reshape(bs, GDN_WIDTH), z_s, olat.reshape(bs, MLA_HEADS * KV_LORA), extra)
    y_sample = _tail(xs2, og_s, om_s, p_sample[li].reshape(bs, PLE_DIM), tailw, bs).reshape(bs, 1, D_MODEL)
    conv_s = jnp.concatenate([state_conv[li][:, 1:], cin_s[:, None, :]], axis=1)

    return (y_prompt, y_sample,
            ckv_p.reshape(1, bp, seq, KV_LORA), kr_p.reshape(1, bp, seq, MLA_ROPE),
            gdn_p[None], conv_p[None],
            ckv_s.reshape(1, bs, 1, KV_LORA), kr_s.reshape(1, bs, 1, MLA_ROPE),
            s_new.reshape(1, bs, GDN_HEADS, GDN_DK, GDN_DV), conv_s[None])
```
